```python
import jax
import jax.numpy as jnp
from jax import lax
import numpy as np

D_MODEL = 2048
BATCH = 16
SEQ = 2048
DEPTH = 2

MEM_LEN = 256
N_EVEN = (DEPTH + 1) // 2
N_ODD = DEPTH // 2
ROPE_THETA = 10000.0
LN_EPS = 1e-5
RMS_EPS = 1e-6
DEEPNORM_ALPHA = (2.0 * DEPTH) ** 0.25
DEEPNORM_BETA = (8.0 * DEPTH) ** -0.25
Q_BLOCK = 128
MIX_WIDTH = D_MODEL

MLA_HEADS = 8
MLA_NOPE = 128
MLA_ROPE = 64
MLA_V = 128
MLA_Q_RANK = 512
MLA_KV_RANK = 256
GLA_HEADS = 4
GLA_DK = 128
GLA_DV = 256
GLA_GATE_RANK = 16
GLA_TAU = 16.0
GLA_CHUNK = 64
DIL_HEADS = 8
DIL_HEAD_DIM = 128
DIL_BRANCHES = ((128, 1), (512, 4), (2048, 16))
RWKV_HEADS = 16
RWKV_HEAD_DIM = 64
RWKV_DECAY_RANK = 96
RWKV_A_RANK = 96
RWKV_GATE_RANK = 256
RWKV_GN_EPS = 64e-5
XA_HEADS = 4
XA_HEAD_DIM = D_MODEL // XA_HEADS
D_FF = -(-8 * D_MODEL // (3 * 256)) * 256

MLA_OUT = MLA_HEADS * MLA_V
GLA_OUT = GLA_HEADS * GLA_DV
DIL_WIDTH = DIL_HEADS * DIL_HEAD_DIM
RWKV_WIDTH = RWKV_HEADS * RWKV_HEAD_DIM
EVEN_IN_WIDTHS = (MLA_Q_RANK, MLA_KV_RANK, MLA_ROPE, GLA_HEADS * GLA_DK, GLA_HEADS * GLA_DK, GLA_OUT, GLA_OUT, GLA_GATE_RANK)
RWKV_IN_WIDTHS = (RWKV_WIDTH, RWKV_WIDTH, RWKV_WIDTH, RWKV_DECAY_RANK, RWKV_A_RANK, RWKV_GATE_RANK)
EVEN_IN = sum(EVEN_IN_WIDTHS)
RWKV_IN = sum(RWKV_IN_WIDTHS)
ODD_IN = 3 * DIL_WIDTH + RWKV_IN

kernel_name = "hybrid_mla_gla_dilated_rwkv7_deepnorm"


def _split(t, widths):
    points, acc = [], 0
    for w in widths[:-1]:
        acc += w
        points.append(acc)
    return jnp.split(t, points, axis=-1)


def layer_norm(x, g, b, eps=LN_EPS):
    xf = x.astype(jnp.float32)
    mu = jnp.mean(xf, axis=-1, keepdims=True)
    var = jnp.mean(jnp.square(xf - mu), axis=-1, keepdims=True)
    return ((xf - mu) * lax.rsqrt(var + eps)).astype(x.dtype) * g + b


def rms_norm(x, g):
    xf = x.astype(jnp.float32)
    return (xf * lax.rsqrt(jnp.mean(xf * xf, axis=-1, keepdims=True) + RMS_EPS)).astype(x.dtype) * g


def rope_tables(seq_len, dim):
    inv = ROPE_THETA ** (-jnp.arange(0, dim, 2, dtype=jnp.float32) / dim)
    ang = jnp.arange(seq_len, dtype=jnp.float32)[:, None] * inv[None, :]
    return jnp.cos(ang), jnp.sin(ang)


def apply_rope(x, cos, sin):
    x1, x2 = jnp.split(x, 2, axis=-1)
    c, s = cos.astype(x.dtype), sin.astype(x.dtype)
    return jnp.concatenate([x1 * c - x2 * s, x1 * s + x2 * c], axis=-1)


def token_shift(t, mu):
    prev = jnp.pad(t, ((0, 0), (1, 0), (0, 0)))[:, :-1]
    return t + (prev - t) * mu


def causal_block_attention(q, k, v, scale):
    B, H, S, dk = q.shape
    nb = S // Q_BLOCK
    qb = q.reshape(B, H, nb, Q_BLOCK, dk).transpose(2, 0, 1, 3, 4)
    kpos = jnp.arange(S)

    def one_block(args):
        qi, i = args
        s = jnp.einsum('bhqd,bhkd->bhqk', qi, k).astype(jnp.float32) * scale
        qpos = i * Q_BLOCK + jnp.arange(Q_BLOCK)
        s = jnp.where(kpos[None, :] <= qpos[:, None], s, -jnp.inf)
        p = jax.nn.softmax(s, axis=-1).astype(v.dtype)
        return jnp.einsum('bhqk,bhkd->bhqd', p, v)

    out = lax.map(one_block, (qb, jnp.arange(nb)))
    return out.transpose(1, 2, 0, 3, 4).reshape(B, H, S, v.shape[-1])


def gla(q, k, v, r, gate_lr, w_gate2, b_gate, norm_g, norm_b):
    B, S, _ = q.shape
    H, dk, dv, C = GLA_HEADS, GLA_DK, GLA_DV, GLA_CHUNK
    nc = S // C
    f32 = jnp.float32
    log_a = jax.nn.log_sigmoid((gate_lr @ w_gate2 + b_gate).astype(f32)) / GLA_TAU

    def chunks(t, d):
        return t.astype(f32).reshape(B, nc, C, H, d).transpose(0, 3, 1, 2, 4)

    qc = chunks(q, dk) * (dk ** -0.5)
    kc = chunks(k, dk)
    vc = chunks(v, dv)
    b = jnp.cumsum(chunks(log_a, dk), axis=3)
    b_last = b[:, :, :, -1:, :]
    q_dec = qc * jnp.exp(b)
    k_inv = kc * jnp.exp(-b)
    k_end = kc * jnp.exp(b_last - b)
    causal = jnp.tril(jnp.ones((C, C), dtype=bool))
    att = jnp.where(causal, jnp.einsum('bhnid,bhnjd->bhnij', q_dec, k_inv), 0.0)
    o_intra = jnp.einsum('bhnij,bhnjv->bhniv', att, vc)

    def step(state, xs):
        q_t, k_t, v_t, dec_t = xs
        o_t = jnp.einsum('bhcd,bhdv->bhcv', q_t, state)
        state = state * dec_t[..., None] + jnp.einsum('bhcd,bhcv->bhdv', k_t, v_t)
        return state, o_t

    xs = (jnp.moveaxis(q_dec, 2, 0), jnp.moveaxis(k_end, 2, 0), jnp.moveaxis(vc, 2, 0),
          jnp.moveaxis(jnp.exp(b_last[:, :, :, 0, :]), 2, 0))
    _, o_inter = lax.scan(step, jnp.zeros((B, H, dk, dv), f32), xs)
    o = o_intra + jnp.moveaxis(o_inter, 0, 2)
    o = o.transpose(0, 2, 3, 1, 4).reshape(B, S, H, dv)
    o = layer_norm(o, norm_g, norm_b).reshape(B, S, H * dv)
    return (o * jax.nn.silu(r.astype(f32))).astype(q.dtype)


def dilated_branch(q, k, v, window, dil):
    B, H, S, dh = q.shape
    span = window // dil
    L = S // dil
    nb = -(-L // span)
    Lp = nb * span

    def residues(t):
        t = t.reshape(B, H, L, dil, dh).transpose(0, 1, 3, 2, 4)
        t = jnp.pad(t, ((0, 0), (0, 0), (0, 0), (0, Lp - L), (0, 0)))
        return t.reshape(B, H, dil, nb, span, dh)

    def with_prev(t):
        prev = jnp.pad(t, ((0, 0), (0, 0), (0, 0), (1, 0), (0, 0), (0, 0)))[:, :, :, :-1]
        return jnp.concatenate([prev, t], axis=4)

    qb = residues(q)
    kw, vw = with_prev(residues(k)), with_prev(residues(v))
    s = jnp.einsum('bhrnqd,bhrnkd->bhrnqk', qb, kw).astype(jnp.float32) * (dh ** -0.5)
    qi = jnp.arange(span)[:, None] + span
    kj = jnp.arange(2 * span)[None, :]
    dist = qi - kj
    in_band = (dist >= 0) & (dist <= span)
    has_prev = (jnp.arange(nb) > 0)[:, None, None] | (kj >= span)[None]
    valid = in_band[None] & has_prev
    s = jnp.where(valid, s, -jnp.inf)
    m = jnp.max(s, axis=-1, keepdims=True)
    p = jnp.exp(s - m)
    den = jnp.sum(p, axis=-1, keepdims=True)
    o = jnp.einsum('bhrnqk,bhrnkd->bhrnqd', (p / den).astype(v.dtype), vw)
    lse = (m + jnp.log(den))[..., 0]

    def back(t):
        t = t.reshape((B, H, dil, Lp) + t.shape[5:])[:, :, :, :L]
        return jnp.moveaxis(t, 2, 3).reshape((B, H, S) + t.shape[4:])

    return back(o), back(lse)


def dilated_mixture(q, k, v):
    outs, lses = [], []
    for window, dil in DIL_BRANCHES:
        o, lse = dilated_branch(q, k, v, window, dil)
        outs.append(o)
        lses.append(lse)
    wts = jax.nn.softmax(jnp.stack(lses, axis=0), axis=0)
    return jnp.sum(wts[..., None] * jnp.stack(outs, axis=0).astype(jnp.float32), axis=0)


def rwkv7(r, k, v, w_lr, a_lr, g_lr, w0, w_decay2, a0, w_a2, w_gate2, k_k, k_a, r_k, gn_g, gn_b):
    B, S, _ = r.shape
    H, N = RWKV_HEADS, RWKV_HEAD_DIM
    f32 = jnp.float32
    w = -jax.nn.softplus(-(w0 + jnp.tanh(w_lr) @ w_decay2).astype(f32)) - 0.5
    decay = jnp.exp(-jnp.exp(w))
    a = jax.nn.sigmoid((a0 + a_lr @ w_a2).astype(f32))
    g = jax.nn.sigmoid(g_lr) @ w_gate2

    def heads(t):
        return t.astype(f32).reshape(B, S, H, N)

    kk = heads(k * k_k)
    kk = kk / jnp.maximum(jnp.sqrt(jnp.sum(kk * kk, axis=-1, keepdims=True)), 1e-12)
    ah = a.reshape(B, S, H, N)
    kh = heads(k) * (1.0 + (ah - 1.0) * k_a.astype(f32).reshape(H, N))
    rh, vh, wh = heads(r), heads(v), decay.reshape(B, S, H, N)

    def step(state, xs):
        r_t, w_t, k_t, v_t, kk_t, a_t = xs
        sa = jnp.einsum('bhij,bhj->bhi', state, -kk_t)
        state = (state * w_t[:, :, None, :] + sa[..., None] * (kk_t * a_t)[:, :, None, :]
                 + v_t[..., None] * k_t[:, :, None, :])
        return state, jnp.einsum('bhij,bhj->bhi', state, r_t)

    tm = lambda t: jnp.moveaxis(t, 1, 0)
    _, y = lax.scan(step, jnp.zeros((B, H, N, N), f32), (tm(rh), tm(wh), tm(kh), tm(vh), tm(kk), tm(ah)))
    y = jnp.moveaxis(y, 0, 1)
    mu = jnp.mean(y, axis=-1, keepdims=True)
    var = jnp.mean(jnp.square(y - mu), axis=-1, keepdims=True)
    y = ((y - mu) * lax.rsqrt(var + RWKV_GN_EPS)).reshape(B, S, H * N) * gn_g + gn_b
    bonus = jnp.sum(rh * kh * r_k.astype(f32), axis=-1, keepdims=True) * vh
    y = y + bonus.reshape(B, S, H * N)
    return (y * g).astype(r.dtype)


def even_mixer(x, cos, sin, w_in, q_norm, w_uq, kv_norm, w_ukv, w_gate2, b_gate, norm_g, norm_b, w_out):
    B, S, _ = x.shape
    H = MLA_HEADS
    c_q, c_kv, k_pe, q_g, k_g, v_g, r_g, lr_g = _split(x @ w_in, EVEN_IN_WIDTHS)
    q = (rms_norm(c_q, q_norm) @ w_uq).reshape(B, S, H, MLA_NOPE + MLA_ROPE).transpose(0, 2, 1, 3)
    kv = (rms_norm(c_kv, kv_norm) @ w_ukv).reshape(B, S, H, MLA_NOPE + MLA_V).transpose(0, 2, 1, 3)
    q_pe = apply_rope(q[..., MLA_NOPE:], cos, sin)
    k_pe = jnp.broadcast_to(apply_rope(k_pe[:, None], cos, sin), (B, H, S, MLA_ROPE))
    qf = jnp.concatenate([q[..., :MLA_NOPE], q_pe], axis=-1)
    kf = jnp.concatenate([kv[..., :MLA_NOPE], k_pe], axis=-1)
    a_out = causal_block_attention(qf, kf, kv[..., MLA_NOPE:], (MLA_NOPE + MLA_ROPE) ** -0.5)
    a_out = a_out.transpose(0, 2, 1, 3).reshape(B, S, MLA_OUT)
    b_out = gla(q_g, k_g, v_g, r_g, lr_g, w_gate2, b_gate, norm_g, norm_b)
    return jnp.concatenate([a_out, b_out], axis=-1) @ w_out


def odd_mixer(x, cos, sin, w_in, mu, w0, w_decay2, a0, w_a2, w_gate2, k_k, k_a, r_k, gn_g, gn_b, w_out):
    B, S, _ = x.shape
    h = x @ w_in
    c_in, d_in = h[..., :3 * DIL_WIDTH], h[..., 3 * DIL_WIDTH:]
    q, k, v = [t.reshape(B, S, DIL_HEADS, DIL_HEAD_DIM).transpose(0, 2, 1, 3) for t in jnp.split(c_in, 3, axis=-1)]
    q, k = apply_rope(q, cos, sin), apply_rope(k, cos, sin)
    c_out = dilated_mixture(q, k, v).transpose(0, 2, 1, 3).reshape(B, S, DIL_WIDTH).astype(x.dtype)
    r, kd, vd, w_lr, a_lr, g_lr = _split(token_shift(d_in, mu), RWKV_IN_WIDTHS)
    d_out = rwkv7(r, kd, vd, w_lr, a_lr, g_lr, w0, w_decay2, a0, w_a2, w_gate2, k_k, k_a, r_k, gn_g, gn_b)
    return jnp.concatenate([c_out, d_out], axis=-1) @ w_out


def cross_attention(x, mem, w_q, w_k, w_v, w_o):
    B, S, _ = x.shape
    M = mem.shape[1]
    q = (x @ w_q).reshape(B, S, XA_HEADS, XA_HEAD_DIM)
    k = (mem @ w_k).reshape(B, M, XA_HEADS, XA_HEAD_DIM)
    v = (mem @ w_v).reshape(B, M, XA_HEADS, XA_HEAD_DIM)
    s = jnp.einsum('bqhd,bkhd->bhqk', q, k).astype(jnp.float32) * (XA_HEAD_DIM ** -0.5)
    p = jax.nn.softmax(s, axis=-1).astype(v.dtype)
    o = jnp.einsum('bhqk,bkhd->bqhd', p, v).reshape(B, S, XA_HEADS * XA_HEAD_DIM)
    return o @ w_o


def swiglu(x, w_gate, w_up, w_down):
    return (jax.nn.silu(x @ w_gate) * (x @ w_up)) @ w_down


def _fwd_setup_inputs(seed: int = 0) -> dict:
    key = jax.random.key(seed)
    ks = iter(jax.random.split(key, 48))
    D = D_MODEL
    NE, NO, L = N_EVEN, N_ODD, DEPTH

    def nrm(shape, scale):
        return jax.random.normal(next(ks), shape, jnp.float32) * scale

    def gain(shape):
        return 1.0 + nrm(shape, 0.02)

    def unif(shape, lo, hi):
        return jax.random.uniform(next(ks), shape, jnp.float32, lo, hi)

    inp = {}
    inp['x'] = nrm((BATCH, SEQ, D), 1.0)
    inp['mem'] = nrm((BATCH, MEM_LEN, D), 1.0)
    inp['ev_w_in'] = nrm((NE, D, EVEN_IN), D ** -0.5)
    inp['ev_mla_q_norm'] = gain((NE, MLA_Q_RANK))
    inp['ev_mla_w_uq'] = nrm((NE, MLA_Q_RANK, MLA_HEADS * (MLA_NOPE + MLA_ROPE)), MLA_Q_RANK ** -0.5)
    inp['ev_mla_kv_norm'] = gain((NE, MLA_KV_RANK))
    inp['ev_mla_w_ukv'] = nrm((NE, MLA_KV_RANK, MLA_HEADS * (MLA_NOPE + MLA_V)), MLA_KV_RANK ** -0.5)
    inp['ev_gla_w_gate2'] = nrm((NE, GLA_GATE_RANK, GLA_HEADS * GLA_DK), GLA_GATE_RANK ** -0.5)
    inp['ev_gla_b_gate'] = nrm((NE, GLA_HEADS * GLA_DK), 0.1)
    inp['ev_gla_norm_g'] = gain((NE, GLA_DV))
    inp['ev_gla_norm_b'] = nrm((NE, GLA_DV), 0.02)
    inp['ev_w_out'] = nrm((NE, MIX_WIDTH, D), MIX_WIDTH ** -0.5 * DEEPNORM_BETA)
    inp['od_w_in'] = nrm((NO, D, ODD_IN), D ** -0.5)
    inp['od_rwkv_mu'] = unif((NO, RWKV_IN), 0.0, 1.0)
    inp['od_rwkv_w0'] = unif((NO, RWKV_WIDTH), -6.0, -1.0)
    inp['od_rwkv_w_decay2'] = nrm((NO, RWKV_DECAY_RANK, RWKV_WIDTH), RWKV_DECAY_RANK ** -0.5)
    inp['od_rwkv_a0'] = nrm((NO, RWKV_WIDTH), 0.1)
    inp['od_rwkv_w_a2'] = nrm((NO, RWKV_A_RANK, RWKV_WIDTH), RWKV_A_RANK ** -0.5)
    inp['od_rwkv_w_gate2'] = nrm((NO, RWKV_GATE_RANK, RWKV_WIDTH), RWKV_GATE_RANK ** -0.5)
    inp['od_rwkv_k_k'] = 0.85 + nrm((NO, RWKV_WIDTH), 0.02)
    inp['od_rwkv_k_a'] = gain((NO, RWKV_WIDTH))
    inp['od_rwkv_r_k'] = nrm((NO, RWKV_HEADS, RWKV_HEAD_DIM), 0.1)
    inp['od_rwkv_gn_g'] = gain((NO, RWKV_WIDTH))
    inp['od_rwkv_gn_b'] = nrm((NO, RWKV_WIDTH), 0.02)
    inp['od_w_out'] = nrm((NO, MIX_WIDTH, D), MIX_WIDTH ** -0.5 * DEEPNORM_BETA)
    inp['ln_mix_g'] = gain((L, D))
    inp['ln_mix_b'] = nrm((L, D), 0.02)
    inp['xa_w_q'] = nrm((L, D, D), D ** -0.5)
    inp['xa_w_k'] = nrm((L, D, D), D ** -0.5)
    inp['xa_w_v'] = nrm((L, D, D), D ** -0.5)
    inp['xa_w_o'] = nrm((L, D, D), D ** -0.5 * DEEPNORM_BETA)
    inp['ln_xa_g'] = gain((L, D))
    inp['ln_xa_b'] = nrm((L, D), 0.02)
    inp['ffn_w_gate'] = nrm((L, D, D_FF), D ** -0.5)
    inp['ffn_w_up'] = nrm((L, D, D_FF), D ** -0.5)
    inp['ffn_w_down'] = nrm((L, D_FF, D), D_FF ** -0.5 * DEEPNORM_BETA)
    inp['ln_ffn_g'] = gain((L, D))
    inp['ln_ffn_b'] = nrm((L, D), 0.02)
    return inp


def _fwd_reference(x, mem, ev_w_in, ev_mla_q_norm, ev_mla_w_uq, ev_mla_kv_norm, ev_mla_w_ukv, ev_gla_w_gate2,
              ev_gla_b_gate, ev_gla_norm_g, ev_gla_norm_b, ev_w_out, od_w_in, od_rwkv_mu, od_rwkv_w0,
              od_rwkv_w_decay2, od_rwkv_a0, od_rwkv_w_a2, od_rwkv_w_gate2, od_rwkv_k_k, od_rwkv_k_a, od_rwkv_r_k,
              od_rwkv_gn_g, od_rwkv_gn_b, od_w_out, ln_mix_g, ln_mix_b, xa_w_q, xa_w_k, xa_w_v, xa_w_o,
              ln_xa_g, ln_xa_b, ffn_w_gate, ffn_w_up, ffn_w_down, ln_ffn_g, ln_ffn_b):
    S = x.shape[1]
    cos_pe, sin_pe = rope_tables(S, MLA_ROPE)
    cos_c, sin_c = rope_tables(S, DIL_HEAD_DIM)
    h = x
    for layer in range(DEPTH):
        i = layer // 2
        if layer % 2 == 0:
            mix = even_mixer(h, cos_pe, sin_pe, ev_w_in[i], ev_mla_q_norm[i], ev_mla_w_uq[i], ev_mla_kv_norm[i],
                             ev_mla_w_ukv[i], ev_gla_w_gate2[i], ev_gla_b_gate[i], ev_gla_norm_g[i],
                             ev_gla_norm_b[i], ev_w_out[i])
        else:
            mix = odd_mixer(h, cos_c, sin_c, od_w_in[i], od_rwkv_mu[i], od_rwkv_w0[i], od_rwkv_w_decay2[i],
                            od_rwkv_a0[i], od_rwkv_w_a2[i], od_rwkv_w_gate2[i], od_rwkv_k_k[i], od_rwkv_k_a[i],
                            od_rwkv_r_k[i], od_rwkv_gn_g[i], od_rwkv_gn_b[i], od_w_out[i])
        h = layer_norm(DEEPNORM_ALPHA * h + mix, ln_mix_g[layer], ln_mix_b[layer])
        xa = cross_attention(h, mem, xa_w_q[layer], xa_w_k[layer], xa_w_v[layer], xa_w_o[layer])
        h = layer_norm(DEEPNORM_ALPHA * h + xa, ln_xa_g[layer], ln_xa_b[layer])
        ff = swiglu(h, ffn_w_gate[layer], ffn_w_up[layer], ffn_w_down[layer])
        h = layer_norm(DEEPNORM_ALPHA * h + ff, ln_ffn_g[layer], ln_ffn_b[layer])
    return h


import jax as _jax
import jax.numpy as _jnp

TWIN_FORMAT = 'train_step'
FWD_PARAMS = ['x', 'mem', 'ev_w_in', 'ev_mla_q_norm', 'ev_mla_w_uq', 'ev_mla_kv_norm', 'ev_mla_w_ukv', 'ev_gla_w_gate2', 'ev_gla_b_gate', 'ev_gla_norm_g', 'ev_gla_norm_b', 'ev_w_out', 'od_w_in', 'od_rwkv_mu', 'od_rwkv_w0', 'od_rwkv_w_decay2', 'od_rwkv_a0', 'od_rwkv_w_a2', 'od_rwkv_w_gate2', 'od_rwkv_k_k', 'od_rwkv_k_a', 'od_rwkv_r_k', 'od_rwkv_gn_g', 'od_rwkv_gn_b', 'od_w_out', 'ln_mix_g', 'ln_mix_b', 'xa_w_q', 'xa_w_k', 'xa_w_v', 'xa_w_o', 'ln_xa_g', 'ln_xa_b', 'ffn_w_gate', 'ffn_w_up', 'ffn_w_down', 'ln_ffn_g', 'ln_ffn_b']
TWIN_WEIGHTS = ['ev_w_in', 'ev_mla_q_norm', 'ev_mla_w_uq', 'ev_mla_kv_norm', 'ev_mla_w_ukv', 'ev_gla_w_gate2', 'ev_gla_b_gate', 'ev_gla_norm_g', 'ev_gla_norm_b', 'ev_w_out', 'od_w_in', 'od_rwkv_mu', 'od_rwkv_w0', 'od_rwkv_w_decay2', 'od_rwkv_a0', 'od_rwkv_w_a2', 'od_rwkv_w_gate2', 'od_rwkv_k_k', 'od_rwkv_k_a', 'od_rwkv_r_k', 'od_rwkv_gn_g', 'od_rwkv_gn_b', 'od_w_out', 'ln_mix_g', 'ln_mix_b', 'xa_w_q', 'xa_w_k', 'xa_w_v', 'xa_w_o', 'ln_xa_g', 'ln_xa_b', 'ffn_w_gate', 'ffn_w_up', 'ffn_w_down', 'ln_ffn_g', 'ln_ffn_b']
TWIN_DIFF_INPUT = 'x'
TWIN_INPUTS = ['x', 'mem', 'ev_w_in', 'ev_mla_q_norm', 'ev_mla_w_uq', 'ev_mla_kv_norm', 'ev_mla_w_ukv', 'ev_gla_w_gate2', 'ev_gla_b_gate', 'ev_gla_norm_g', 'ev_gla_norm_b', 'ev_w_out', 'od_w_in', 'od_rwkv_mu', 'od_rwkv_w0', 'od_rwkv_w_decay2', 'od_rwkv_a0', 'od_rwkv_w_a2', 'od_rwkv_w_gate2', 'od_rwkv_k_k', 'od_rwkv_k_a', 'od_rwkv_r_k', 'od_rwkv_gn_g', 'od_rwkv_gn_b', 'od_w_out', 'ln_mix_g', 'ln_mix_b', 'xa_w_q', 'xa_w_k', 'xa_w_v', 'xa_w_o', 'ln_xa_g', 'ln_xa_b', 'ffn_w_gate', 'ffn_w_up', 'ffn_w_down', 'ln_ffn_g', 'ln_ffn_b', 'loss_target', 'm_ev_w_in', 'm_ev_mla_q_norm', 'm_ev_mla_w_uq', 'm_ev_mla_kv_norm', 'm_ev_mla_w_ukv', 'm_ev_gla_w_gate2', 'm_ev_gla_b_gate', 'm_ev_gla_norm_g', 'm_ev_gla_norm_b', 'm_ev_w_out', 'm_od_w_in', 'm_od_rwkv_mu', 'm_od_rwkv_w0', 'm_od_rwkv_w_decay2', 'm_od_rwkv_a0', 'm_od_rwkv_w_a2', 'm_od_rwkv_w_gate2', 'm_od_rwkv_k_k', 'm_od_rwkv_k_a', 'm_od_rwkv_r_k', 'm_od_rwkv_gn_g', 'm_od_rwkv_gn_b', 'm_od_w_out', 'm_ln_mix_g', 'm_ln_mix_b', 'm_xa_w_q', 'm_xa_w_k', 'm_xa_w_v', 'm_xa_w_o', 'm_ln_xa_g', 'm_ln_xa_b', 'm_ffn_w_gate', 'm_ffn_w_up', 'm_ffn_w_down', 'm_ln_ffn_g', 'm_ln_ffn_b', 'v_ev_w_in', 'v_ev_mla_q_norm', 'v_ev_mla_w_uq', 'v_ev_mla_kv_norm', 'v_ev_mla_w_ukv', 'v_ev_gla_w_gate2', 'v_ev_gla_b_gate', 'v_ev_gla_norm_g', 'v_ev_gla_norm_b', 'v_ev_w_out', 'v_od_w_in', 'v_od_rwkv_mu', 'v_od_rwkv_w0', 'v_od_rwkv_w_decay2', 'v_od_rwkv_a0', 'v_od_rwkv_w_a2', 'v_od_rwkv_w_gate2', 'v_od_rwkv_k_k', 'v_od_rwkv_k_a', 'v_od_rwkv_r_k', 'v_od_rwkv_gn_g', 'v_od_rwkv_gn_b', 'v_od_w_out', 'v_ln_mix_g', 'v_ln_mix_b', 'v_xa_w_q', 'v_xa_w_k', 'v_xa_w_v', 'v_xa_w_o', 'v_ln_xa_g', 'v_ln_xa_b', 'v_ffn_w_gate', 'v_ffn_w_up', 'v_ffn_w_down', 'v_ln_ffn_g', 'v_ln_ffn_b']
TWIN_OUTPUTS = ['loss', 'grad_x', 'grad_ev_w_in', 'grad_ev_mla_q_norm', 'grad_ev_mla_w_uq', 'grad_ev_mla_kv_norm', 'grad_ev_mla_w_ukv', 'grad_ev_gla_w_gate2', 'grad_ev_gla_b_gate', 'grad_ev_gla_norm_g', 'grad_ev_gla_norm_b', 'grad_ev_w_out', 'grad_od_w_in', 'grad_od_rwkv_mu', 'grad_od_rwkv_w0', 'grad_od_rwkv_w_decay2', 'grad_od_rwkv_a0', 'grad_od_rwkv_w_a2', 'grad_od_rwkv_w_gate2', 'grad_od_rwkv_k_k', 'grad_od_rwkv_k_a', 'grad_od_rwkv_r_k', 'grad_od_rwkv_gn_g', 'grad_od_rwkv_gn_b', 'grad_od_w_out', 'grad_ln_mix_g', 'grad_ln_mix_b', 'grad_xa_w_q', 'grad_xa_w_k', 'grad_xa_w_v', 'grad_xa_w_o', 'grad_ln_xa_g', 'grad_ln_xa_b', 'grad_ffn_w_gate', 'grad_ffn_w_up', 'grad_ffn_w_down', 'grad_ln_ffn_g', 'grad_ln_ffn_b', 'delta_ev_w_in', 'delta_ev_mla_q_norm', 'delta_ev_mla_w_uq', 'delta_ev_mla_kv_norm', 'delta_ev_mla_w_ukv', 'delta_ev_gla_w_gate2', 'delta_ev_gla_b_gate', 'delta_ev_gla_norm_g', 'delta_ev_gla_norm_b', 'delta_ev_w_out', 'delta_od_w_in', 'delta_od_rwkv_mu', 'delta_od_rwkv_w0', 'delta_od_rwkv_w_decay2', 'delta_od_rwkv_a0', 'delta_od_rwkv_w_a2', 'delta_od_rwkv_w_gate2', 'delta_od_rwkv_k_k', 'delta_od_rwkv_k_a', 'delta_od_rwkv_r_k', 'delta_od_rwkv_gn_g', 'delta_od_rwkv_gn_b', 'delta_od_w_out', 'delta_ln_mix_g', 'delta_ln_mix_b', 'delta_xa_w_q', 'delta_xa_w_k', 'delta_xa_w_v', 'delta_xa_w_o', 'delta_ln_xa_g', 'delta_ln_xa_b', 'delta_ffn_w_gate', 'delta_ffn_w_up', 'delta_ffn_w_down', 'delta_ln_ffn_g', 'delta_ln_ffn_b', 'new_m_ev_w_in', 'new_m_ev_mla_q_norm', 'new_m_ev_mla_w_uq', 'new_m_ev_mla_kv_norm', 'new_m_ev_mla_w_ukv', 'new_m_ev_gla_w_gate2', 'new_m_ev_gla_b_gate', 'new_m_ev_gla_norm_g', 'new_m_ev_gla_norm_b', 'new_m_ev_w_out', 'new_m_od_w_in', 'new_m_od_rwkv_mu', 'new_m_od_rwkv_w0', 'new_m_od_rwkv_w_decay2', 'new_m_od_rwkv_a0', 'new_m_od_rwkv_w_a2', 'new_m_od_rwkv_w_gate2', 'new_m_od_rwkv_k_k', 'new_m_od_rwkv_k_a', 'new_m_od_rwkv_r_k', 'new_m_od_rwkv_gn_g', 'new_m_od_rwkv_gn_b', 'new_m_od_w_out', 'new_m_ln_mix_g', 'new_m_ln_mix_b', 'new_m_xa_w_q', 'new_m_xa_w_k', 'new_m_xa_w_v', 'new_m_xa_w_o', 'new_m_ln_xa_g', 'new_m_ln_xa_b', 'new_m_ffn_w_gate', 'new_m_ffn_w_up', 'new_m_ffn_w_down', 'new_m_ln_ffn_g', 'new_m_ln_ffn_b', 'new_v_ev_w_in', 'new_v_ev_mla_q_norm', 'new_v_ev_mla_w_uq', 'new_v_ev_mla_kv_norm', 'new_v_ev_mla_w_ukv', 'new_v_ev_gla_w_gate2', 'new_v_ev_gla_b_gate', 'new_v_ev_gla_norm_g', 'new_v_ev_gla_norm_b', 'new_v_ev_w_out', 'new_v_od_w_in', 'new_v_od_rwkv_mu', 'new_v_od_rwkv_w0', 'new_v_od_rwkv_w_decay2', 'new_v_od_rwkv_a0', 'new_v_od_rwkv_w_a2', 'new_v_od_rwkv_w_gate2', 'new_v_od_rwkv_k_k', 'new_v_od_rwkv_k_a', 'new_v_od_rwkv_r_k', 'new_v_od_rwkv_gn_g', 'new_v_od_rwkv_gn_b', 'new_v_od_w_out', 'new_v_ln_mix_g', 'new_v_ln_mix_b', 'new_v_xa_w_q', 'new_v_xa_w_k', 'new_v_xa_w_v', 'new_v_xa_w_o', 'new_v_ln_xa_g', 'new_v_ln_xa_b', 'new_v_ffn_w_gate', 'new_v_ffn_w_up', 'new_v_ffn_w_down', 'new_v_ln_ffn_g', 'new_v_ln_ffn_b']
TWIN_LEAF_KINDS = {'loss': 'loss', 'grad_x': 'grad_x', 'grad_ev_w_in': 'grad_w', 'grad_ev_mla_q_norm': 'grad_w', 'grad_ev_mla_w_uq': 'grad_w', 'grad_ev_mla_kv_norm': 'grad_w', 'grad_ev_mla_w_ukv': 'grad_w', 'grad_ev_gla_w_gate2': 'grad_w', 'grad_ev_gla_b_gate': 'grad_w', 'grad_ev_gla_norm_g': 'grad_w', 'grad_ev_gla_norm_b': 'grad_w', 'grad_ev_w_out': 'grad_w', 'grad_od_w_in': 'grad_w', 'grad_od_rwkv_mu': 'grad_w', 'grad_od_rwkv_w0': 'grad_w', 'grad_od_rwkv_w_decay2': 'grad_w', 'grad_od_rwkv_a0': 'grad_w', 'grad_od_rwkv_w_a2': 'grad_w', 'grad_od_rwkv_w_gate2': 'grad_w', 'grad_od_rwkv_k_k': 'grad_w', 'grad_od_rwkv_k_a': 'grad_w', 'grad_od_rwkv_r_k': 'grad_w', 'grad_od_rwkv_gn_g': 'grad_w', 'grad_od_rwkv_gn_b': 'grad_w', 'grad_od_w_out': 'grad_w', 'grad_ln_mix_g': 'grad_w', 'grad_ln_mix_b': 'grad_w', 'grad_xa_w_q': 'grad_w', 'grad_xa_w_k': 'grad_w', 'grad_xa_w_v': 'grad_w', 'grad_xa_w_o': 'grad_w', 'grad_ln_xa_g': 'grad_w', 'grad_ln_xa_b': 'grad_w', 'grad_ffn_w_gate': 'grad_w', 'grad_ffn_w_up': 'grad_w', 'grad_ffn_w_down': 'grad_w', 'grad_ln_ffn_g': 'grad_w', 'grad_ln_ffn_b': 'grad_w', 'delta_ev_w_in': 'delta_w', 'delta_ev_mla_q_norm': 'delta_w', 'delta_ev_mla_w_uq': 'delta_w', 'delta_ev_mla_kv_norm': 'delta_w', 'delta_ev_mla_w_ukv': 'delta_w', 'delta_ev_gla_w_gate2': 'delta_w', 'delta_ev_gla_b_gate': 'delta_w', 'delta_ev_gla_norm_g': 'delta_w', 'delta_ev_gla_norm_b': 'delta_w', 'delta_ev_w_out': 'delta_w', 'delta_od_w_in': 'delta_w', 'delta_od_rwkv_mu': 'delta_w', 'delta_od_rwkv_w0': 'delta_w', 'delta_od_rwkv_w_decay2': 'delta_w', 'delta_od_rwkv_a0': 'delta_w', 'delta_od_rwkv_w_a2': 'delta_w', 'delta_od_rwkv_w_gate2': 'delta_w', 'delta_od_rwkv_k_k': 'delta_w', 'delta_od_rwkv_k_a': 'delta_w', 'delta_od_rwkv_r_k': 'delta_w', 'delta_od_rwkv_gn_g': 'delta_w', 'delta_od_rwkv_gn_b': 'delta_w', 'delta_od_w_out': 'delta_w', 'delta_ln_mix_g': 'delta_w', 'delta_ln_mix_b': 'delta_w', 'delta_xa_w_q': 'delta_w', 'delta_xa_w_k': 'delta_w', 'delta_xa_w_v': 'delta_w', 'delta_xa_w_o': 'delta_w', 'delta_ln_xa_g': 'delta_w', 'delta_ln_xa_b': 'delta_w', 'delta_ffn_w_gate': 'delta_w', 'delta_ffn_w_up': 'delta_w', 'delta_ffn_w_down': 'delta_w', 'delta_ln_ffn_g': 'delta_w', 'delta_ln_ffn_b': 'delta_w', 'new_m_ev_w_in': 'new_m', 'new_m_ev_mla_q_norm': 'new_m', 'new_m_ev_mla_w_uq': 'new_m', 'new_m_ev_mla_kv_norm': 'new_m', 'new_m_ev_mla_w_ukv': 'new_m', 'new_m_ev_gla_w_gate2': 'new_m', 'new_m_ev_gla_b_gate': 'new_m', 'new_m_ev_gla_norm_g': 'new_m', 'new_m_ev_gla_norm_b': 'new_m', 'new_m_ev_w_out': 'new_m', 'new_m_od_w_in': 'new_m', 'new_m_od_rwkv_mu': 'new_m', 'new_m_od_rwkv_w0': 'new_m', 'new_m_od_rwkv_w_decay2': 'new_m', 'new_m_od_rwkv_a0': 'new_m', 'new_m_od_rwkv_w_a2': 'new_m', 'new_m_od_rwkv_w_gate2': 'new_m', 'new_m_od_rwkv_k_k': 'new_m', 'new_m_od_rwkv_k_a': 'new_m', 'new_m_od_rwkv_r_k': 'new_m', 'new_m_od_rwkv_gn_g': 'new_m', 'new_m_od_rwkv_gn_b': 'new_m', 'new_m_od_w_out': 'new_m', 'new_m_ln_mix_g': 'new_m', 'new_m_ln_mix_b': 'new_m', 'new_m_xa_w_q': 'new_m', 'new_m_xa_w_k': 'new_m', 'new_m_xa_w_v': 'new_m', 'new_m_xa_w_o': 'new_m', 'new_m_ln_xa_g': 'new_m', 'new_m_ln_xa_b': 'new_m', 'new_m_ffn_w_gate': 'new_m', 'new_m_ffn_w_up': 'new_m', 'new_m_ffn_w_down': 'new_m', 'new_m_ln_ffn_g': 'new_m', 'new_m_ln_ffn_b': 'new_m', 'new_v_ev_w_in': 'new_v', 'new_v_ev_mla_q_norm': 'new_v', 'new_v_ev_mla_w_uq': 'new_v', 'new_v_ev_mla_kv_norm': 'new_v', 'new_v_ev_mla_w_ukv': 'new_v', 'new_v_ev_gla_w_gate2': 'new_v', 'new_v_ev_gla_b_gate': 'new_v', 'new_v_ev_gla_norm_g': 'new_v', 'new_v_ev_gla_norm_b': 'new_v', 'new_v_ev_w_out': 'new_v', 'new_v_od_w_in': 'new_v', 'new_v_od_rwkv_mu': 'new_v', 'new_v_od_rwkv_w0': 'new_v', 'new_v_od_rwkv_w_decay2': 'new_v', 'new_v_od_rwkv_a0': 'new_v', 'new_v_od_rwkv_w_a2': 'new_v', 'new_v_od_rwkv_w_gate2': 'new_v', 'new_v_od_rwkv_k_k': 'new_v', 'new_v_od_rwkv_k_a': 'new_v', 'new_v_od_rwkv_r_k': 'new_v', 'new_v_od_rwkv_gn_g': 'new_v', 'new_v_od_rwkv_gn_b': 'new_v', 'new_v_od_w_out': 'new_v', 'new_v_ln_mix_g': 'new_v', 'new_v_ln_mix_b': 'new_v', 'new_v_xa_w_q': 'new_v', 'new_v_xa_w_k': 'new_v', 'new_v_xa_w_v': 'new_v', 'new_v_xa_w_o': 'new_v', 'new_v_ln_xa_g': 'new_v', 'new_v_ln_xa_b': 'new_v', 'new_v_ffn_w_gate': 'new_v', 'new_v_ffn_w_up': 'new_v', 'new_v_ffn_w_down': 'new_v', 'new_v_ln_ffn_g': 'new_v', 'new_v_ln_ffn_b': 'new_v'}


def _forward(args):
    return _fwd_reference(*[args[k] for k in FWD_PARAMS])


def _output_shape():
    out = _jax.eval_shape(lambda: _forward(_fwd_setup_inputs(0)))
    return out.shape, out.dtype

N_MICROBATCH = 1
ADAM_LR = 0.001
ADAM_B1 = 0.9
ADAM_B2 = 0.999
ADAM_EPS = 1e-08
ADAM_WD = 0.01
ADAM_STEP = 10
PER_EXAMPLE_BATCH_AXIS = {'x': 0, 'mem': 0, 'loss_target': 0}
SHARED_INPUTS = []
_WEIGHT_DTYPES = {'ev_w_in': _jnp.float32, 'ev_mla_q_norm': _jnp.float32, 'ev_mla_w_uq': _jnp.float32, 'ev_mla_kv_norm': _jnp.float32, 'ev_mla_w_ukv': _jnp.float32, 'ev_gla_w_gate2': _jnp.float32, 'ev_gla_b_gate': _jnp.float32, 'ev_gla_norm_g': _jnp.float32, 'ev_gla_norm_b': _jnp.float32, 'ev_w_out': _jnp.float32, 'od_w_in': _jnp.float32, 'od_rwkv_mu': _jnp.float32, 'od_rwkv_w0': _jnp.float32, 'od_rwkv_w_decay2': _jnp.float32, 'od_rwkv_a0': _jnp.float32, 'od_rwkv_w_a2': _jnp.float32, 'od_rwkv_w_gate2': _jnp.float32, 'od_rwkv_k_k': _jnp.float32, 'od_rwkv_k_a': _jnp.float32, 'od_rwkv_r_k': _jnp.float32, 'od_rwkv_gn_g': _jnp.float32, 'od_rwkv_gn_b': _jnp.float32, 'od_w_out': _jnp.float32, 'ln_mix_g': _jnp.float32, 'ln_mix_b': _jnp.float32, 'xa_w_q': _jnp.float32, 'xa_w_k': _jnp.float32, 'xa_w_v': _jnp.float32, 'xa_w_o': _jnp.float32, 'ln_xa_g': _jnp.float32, 'ln_xa_b': _jnp.float32, 'ffn_w_gate': _jnp.float32, 'ffn_w_up': _jnp.float32, 'ffn_w_down': _jnp.float32, 'ln_ffn_g': _jnp.float32, 'ln_ffn_b': _jnp.float32}
MOMENT_SCALE = {'ev_w_in': 2.137770e-02, 'ev_mla_q_norm': 8.591213e-03, 'ev_mla_w_uq': 4.912381e-03, 'ev_mla_kv_norm': 1.801751e-02, 'ev_mla_w_ukv': 6.386057e-03, 'ev_gla_w_gate2': 3.509262e-03, 'ev_gla_b_gate': 1.541355e-02, 'ev_gla_norm_g': 4.159670e-02, 'ev_gla_norm_b': 5.359997e-02, 'ev_w_out': 2.990307e-02, 'od_w_in': 1.359462e-02, 'od_rwkv_mu': 2.836506e-02, 'od_rwkv_w0': 6.800378e-03, 'od_rwkv_w_decay2': 8.917144e-04, 'od_rwkv_a0': 6.559756e-03, 'od_rwkv_w_a2': 6.129422e-03, 'od_rwkv_w_gate2': 1.774756e-02, 'od_rwkv_k_k': 2.241231e-02, 'od_rwkv_k_a': 1.845103e-02, 'od_rwkv_r_k': 4.065455e-02, 'od_rwkv_gn_g': 1.672684e-02, 'od_rwkv_gn_b': 3.907290e-02, 'od_w_out': 2.645741e-02, 'ln_mix_g': 5.059141e-01, 'ln_mix_b': 2.373598e-01, 'xa_w_q': 3.346861e-03, 'xa_w_k': 3.355568e-03, 'xa_w_v': 3.897427e-03, 'xa_w_o': 7.768425e-03, 'ln_xa_g': 5.075554e-01, 'ln_xa_b': 2.377009e-01, 'ffn_w_gate': 1.164173e-02, 'ffn_w_up': 1.126216e-02, 'ffn_w_down': 3.736659e-02, 'ln_ffn_g': 1.135258e+01, 'ln_ffn_b': 4.756852e-01}


def _to_microbatches(a, axis):
    t = _jnp.moveaxis(a, axis, 0)
    t = t.reshape((N_MICROBATCH, t.shape[0] // N_MICROBATCH) + t.shape[1:])
    return _jnp.moveaxis(t, 1, axis + 1)


def setup_inputs(seed: int = 0) -> dict:
    inp = _fwd_setup_inputs(seed)
    key = _jax.random.fold_in(_jax.random.key(seed), 7919)
    shape, _ = _output_shape()
    out = dict(inp)
    out["loss_target"] = _jax.random.normal(_jax.random.fold_in(key, 0), shape, _jnp.float32)
    for i, name in enumerate(TWIN_WEIGHTS):
        w = inp[name].astype(_jnp.float32)
        if MOMENT_SCALE is None:
            s = _jnp.sqrt(_jnp.mean(_jnp.square(w)) + 1e-30)
        else:
            s = MOMENT_SCALE[name]
        km, kv = _jax.random.split(_jax.random.fold_in(key, i + 1))
        out[name] = w
        out["m_" + name] = s * _jax.random.normal(km, w.shape, _jnp.float32)
        out["v_" + name] = (s * s) * _jax.random.uniform(kv, w.shape, _jnp.float32, 0.5, 1.5)
    if N_MICROBATCH > 1:
        for name, axis in PER_EXAMPLE_BATCH_AXIS.items():
            out[name] = _to_microbatches(out[name], axis)
    return {'x': out['x'], 'mem': out['mem'], 'ev_w_in': out['ev_w_in'], 'ev_mla_q_norm': out['ev_mla_q_norm'], 'ev_mla_w_uq': out['ev_mla_w_uq'], 'ev_mla_kv_norm': out['ev_mla_kv_norm'], 'ev_mla_w_ukv': out['ev_mla_w_ukv'], 'ev_gla_w_gate2': out['ev_gla_w_gate2'], 'ev_gla_b_gate': out['ev_gla_b_gate'], 'ev_gla_norm_g': out['ev_gla_norm_g'], 'ev_gla_norm_b': out['ev_gla_norm_b'], 'ev_w_out': out['ev_w_out'], 'od_w_in': out['od_w_in'], 'od_rwkv_mu': out['od_rwkv_mu'], 'od_rwkv_w0': out['od_rwkv_w0'], 'od_rwkv_w_decay2': out['od_rwkv_w_decay2'], 'od_rwkv_a0': out['od_rwkv_a0'], 'od_rwkv_w_a2': out['od_rwkv_w_a2'], 'od_rwkv_w_gate2': out['od_rwkv_w_gate2'], 'od_rwkv_k_k': out['od_rwkv_k_k'], 'od_rwkv_k_a': out['od_rwkv_k_a'], 'od_rwkv_r_k': out['od_rwkv_r_k'], 'od_rwkv_gn_g': out['od_rwkv_gn_g'], 'od_rwkv_gn_b': out['od_rwkv_gn_b'], 'od_w_out': out['od_w_out'], 'ln_mix_g': out['ln_mix_g'], 'ln_mix_b': out['ln_mix_b'], 'xa_w_q': out['xa_w_q'], 'xa_w_k': out['xa_w_k'], 'xa_w_v': out['xa_w_v'], 'xa_w_o': out['xa_w_o'], 'ln_xa_g': out['ln_xa_g'], 'ln_xa_b': out['ln_xa_b'], 'ffn_w_gate': out['ffn_w_gate'], 'ffn_w_up': out['ffn_w_up'], 'ffn_w_down': out['ffn_w_down'], 'ln_ffn_g': out['ln_ffn_g'], 'ln_ffn_b': out['ln_ffn_b'], 'loss_target': out['loss_target'], 'm_ev_w_in': out['m_ev_w_in'], 'm_ev_mla_q_norm': out['m_ev_mla_q_norm'], 'm_ev_mla_w_uq': out['m_ev_mla_w_uq'], 'm_ev_mla_kv_norm': out['m_ev_mla_kv_norm'], 'm_ev_mla_w_ukv': out['m_ev_mla_w_ukv'], 'm_ev_gla_w_gate2': out['m_ev_gla_w_gate2'], 'm_ev_gla_b_gate': out['m_ev_gla_b_gate'], 'm_ev_gla_norm_g': out['m_ev_gla_norm_g'], 'm_ev_gla_norm_b': out['m_ev_gla_norm_b'], 'm_ev_w_out': out['m_ev_w_out'], 'm_od_w_in': out['m_od_w_in'], 'm_od_rwkv_mu': out['m_od_rwkv_mu'], 'm_od_rwkv_w0': out['m_od_rwkv_w0'], 'm_od_rwkv_w_decay2': out['m_od_rwkv_w_decay2'], 'm_od_rwkv_a0': out['m_od_rwkv_a0'], 'm_od_rwkv_w_a2': out['m_od_rwkv_w_a2'], 'm_od_rwkv_w_gate2': out['m_od_rwkv_w_gate2'], 'm_od_rwkv_k_k': out['m_od_rwkv_k_k'], 'm_od_rwkv_k_a': out['m_od_rwkv_k_a'], 'm_od_rwkv_r_k': out['m_od_rwkv_r_k'], 'm_od_rwkv_gn_g': out['m_od_rwkv_gn_g'], 'm_od_rwkv_gn_b': out['m_od_rwkv_gn_b'], 'm_od_w_out': out['m_od_w_out'], 'm_ln_mix_g': out['m_ln_mix_g'], 'm_ln_mix_b': out['m_ln_mix_b'], 'm_xa_w_q': out['m_xa_w_q'], 'm_xa_w_k': out['m_xa_w_k'], 'm_xa_w_v': out['m_xa_w_v'], 'm_xa_w_o': out['m_xa_w_o'], 'm_ln_xa_g': out['m_ln_xa_g'], 'm_ln_xa_b': out['m_ln_xa_b'], 'm_ffn_w_gate': out['m_ffn_w_gate'], 'm_ffn_w_up': out['m_ffn_w_up'], 'm_ffn_w_down': out['m_ffn_w_down'], 'm_ln_ffn_g': out['m_ln_ffn_g'], 'm_ln_ffn_b': out['m_ln_ffn_b'], 'v_ev_w_in': out['v_ev_w_in'], 'v_ev_mla_q_norm': out['v_ev_mla_q_norm'], 'v_ev_mla_w_uq': out['v_ev_mla_w_uq'], 'v_ev_mla_kv_norm': out['v_ev_mla_kv_norm'], 'v_ev_mla_w_ukv': out['v_ev_mla_w_ukv'], 'v_ev_gla_w_gate2': out['v_ev_gla_w_gate2'], 'v_ev_gla_b_gate': out['v_ev_gla_b_gate'], 'v_ev_gla_norm_g': out['v_ev_gla_norm_g'], 'v_ev_gla_norm_b': out['v_ev_gla_norm_b'], 'v_ev_w_out': out['v_ev_w_out'], 'v_od_w_in': out['v_od_w_in'], 'v_od_rwkv_mu': out['v_od_rwkv_mu'], 'v_od_rwkv_w0': out['v_od_rwkv_w0'], 'v_od_rwkv_w_decay2': out['v_od_rwkv_w_decay2'], 'v_od_rwkv_a0': out['v_od_rwkv_a0'], 'v_od_rwkv_w_a2': out['v_od_rwkv_w_a2'], 'v_od_rwkv_w_gate2': out['v_od_rwkv_w_gate2'], 'v_od_rwkv_k_k': out['v_od_rwkv_k_k'], 'v_od_rwkv_k_a': out['v_od_rwkv_k_a'], 'v_od_rwkv_r_k': out['v_od_rwkv_r_k'], 'v_od_rwkv_gn_g': out['v_od_rwkv_gn_g'], 'v_od_rwkv_gn_b': out['v_od_rwkv_gn_b'], 'v_od_w_out': out['v_od_w_out'], 'v_ln_mix_g': out['v_ln_mix_g'], 'v_ln_mix_b': out['v_ln_mix_b'], 'v_xa_w_q': out['v_xa_w_q'], 'v_xa_w_k': out['v_xa_w_k'], 'v_xa_w_v': out['v_xa_w_v'], 'v_xa_w_o': out['v_xa_w_o'], 'v_ln_xa_g': out['v_ln_xa_g'], 'v_ln_xa_b': out['v_ln_xa_b'], 'v_ffn_w_gate': out['v_ffn_w_gate'], 'v_ffn_w_up': out['v_ffn_w_up'], 'v_ffn_w_down': out['v_ffn_w_down'], 'v_ln_ffn_g': out['v_ln_ffn_g'], 'v_ln_ffn_b': out['v_ln_ffn_b']}


def _loss(weights, diff, rest, loss_target):
    with _jax.named_scope("forward"):
        args = {**rest, TWIN_DIFF_INPUT: diff, **{k: w.astype(_WEIGHT_DTYPES[k]) for k, w in weights.items()}}
        y = _forward(args)
    with _jax.named_scope("loss_head"):
        err = _jnp.square(y.astype(_jnp.float32) - loss_target)
        return 0.5 * _jnp.sum(_jnp.mean(err, axis=-1)) if err.ndim else 0.5 * err


def _adamw(w, g, m, v):
    m = ADAM_B1 * m + (1.0 - ADAM_B1) * g
    v = ADAM_B2 * v + (1.0 - ADAM_B2) * _jnp.square(g)
    m_hat = m / (1.0 - ADAM_B1 ** ADAM_STEP)
    v_hat = v / (1.0 - ADAM_B2 ** ADAM_STEP)
    delta = -ADAM_LR * (m_hat / (_jnp.sqrt(v_hat) + ADAM_EPS) + ADAM_WD * w)
    return delta, m, v


def reference(x, mem, ev_w_in, ev_mla_q_norm, ev_mla_w_uq, ev_mla_kv_norm, ev_mla_w_ukv, ev_gla_w_gate2, ev_gla_b_gate, ev_gla_norm_g, ev_gla_norm_b, ev_w_out, od_w_in, od_rwkv_mu, od_rwkv_w0, od_rwkv_w_decay2, od_rwkv_a0, od_rwkv_w_a2, od_rwkv_w_gate2, od_rwkv_k_k, od_rwkv_k_a, od_rwkv_r_k, od_rwkv_gn_g, od_rwkv_gn_b, od_w_out, ln_mix_g, ln_mix_b, xa_w_q, xa_w_k, xa_w_v, xa_w_o, ln_xa_g, ln_xa_b, ffn_w_gate, ffn_w_up, ffn_w_down, ln_ffn_g, ln_ffn_b, loss_target, m_ev_w_in, m_ev_mla_q_norm, m_ev_mla_w_uq, m_ev_mla_kv_norm, m_ev_mla_w_ukv, m_ev_gla_w_gate2, m_ev_gla_b_gate, m_ev_gla_norm_g, m_ev_gla_norm_b, m_ev_w_out, m_od_w_in, m_od_rwkv_mu, m_od_rwkv_w0, m_od_rwkv_w_decay2, m_od_rwkv_a0, m_od_rwkv_w_a2, m_od_rwkv_w_gate2, m_od_rwkv_k_k, m_od_rwkv_k_a, m_od_rwkv_r_k, m_od_rwkv_gn_g, m_od_rwkv_gn_b, m_od_w_out, m_ln_mix_g, m_ln_mix_b, m_xa_w_q, m_xa_w_k, m_xa_w_v, m_xa_w_o, m_ln_xa_g, m_ln_xa_b, m_ffn_w_gate, m_ffn_w_up, m_ffn_w_down, m_ln_ffn_g, m_ln_ffn_b, v_ev_w_in, v_ev_mla_q_norm, v_ev_mla_w_uq, v_ev_mla_kv_norm, v_ev_mla_w_ukv, v_ev_gla_w_gate2, v_ev_gla_b_gate, v_ev_gla_norm_g, v_ev_gla_norm_b, v_ev_w_out, v_od_w_in, v_od_rwkv_mu, v_od_rwkv_w0, v_od_rwkv_w_decay2, v_od_rwkv_a0, v_od_rwkv_w_a2, v_od_rwkv_w_gate2, v_od_rwkv_k_k, v_od_rwkv_k_a, v_od_rwkv_r_k, v_od_rwkv_gn_g, v_od_rwkv_gn_b, v_od_w_out, v_ln_mix_g, v_ln_mix_b, v_xa_w_q, v_xa_w_k, v_xa_w_v, v_xa_w_o, v_ln_xa_g, v_ln_xa_b, v_ffn_w_gate, v_ffn_w_up, v_ffn_w_down, v_ln_ffn_g, v_ln_ffn_b):
    given = dict(x=x, mem=mem, ev_w_in=ev_w_in, ev_mla_q_norm=ev_mla_q_norm, ev_mla_w_uq=ev_mla_w_uq, ev_mla_kv_norm=ev_mla_kv_norm, ev_mla_w_ukv=ev_mla_w_ukv, ev_gla_w_gate2=ev_gla_w_gate2, ev_gla_b_gate=ev_gla_b_gate, ev_gla_norm_g=ev_gla_norm_g, ev_gla_norm_b=ev_gla_norm_b, ev_w_out=ev_w_out, od_w_in=od_w_in, od_rwkv_mu=od_rwkv_mu, od_rwkv_w0=od_rwkv_w0, od_rwkv_w_decay2=od_rwkv_w_decay2, od_rwkv_a0=od_rwkv_a0, od_rwkv_w_a2=od_rwkv_w_a2, od_rwkv_w_gate2=od_rwkv_w_gate2, od_rwkv_k_k=od_rwkv_k_k, od_rwkv_k_a=od_rwkv_k_a, od_rwkv_r_k=od_rwkv_r_k, od_rwkv_gn_g=od_rwkv_gn_g, od_rwkv_gn_b=od_rwkv_gn_b, od_w_out=od_w_out, ln_mix_g=ln_mix_g, ln_mix_b=ln_mix_b, xa_w_q=xa_w_q, xa_w_k=xa_w_k, xa_w_v=xa_w_v, xa_w_o=xa_w_o, ln_xa_g=ln_xa_g, ln_xa_b=ln_xa_b, ffn_w_gate=ffn_w_gate, ffn_w_up=ffn_w_up, ffn_w_down=ffn_w_down, ln_ffn_g=ln_ffn_g, ln_ffn_b=ln_ffn_b, loss_target=loss_target, m_ev_w_in=m_ev_w_in, m_ev_mla_q_norm=m_ev_mla_q_norm, m_ev_mla_w_uq=m_ev_mla_w_uq, m_ev_mla_kv_norm=m_ev_mla_kv_norm, m_ev_mla_w_ukv=m_ev_mla_w_ukv, m_ev_gla_w_gate2=m_ev_gla_w_gate2, m_ev_gla_b_gate=m_ev_gla_b_gate, m_ev_gla_norm_g=m_ev_gla_norm_g, m_ev_gla_norm_b=m_ev_gla_norm_b, m_ev_w_out=m_ev_w_out, m_od_w_in=m_od_w_in, m_od_rwkv_mu=m_od_rwkv_mu, m_od_rwkv_w0=m_od_rwkv_w0, m_od_rwkv_w_decay2=m_od_rwkv_w_decay2, m_od_rwkv_a0=m_od_rwkv_a0, m_od_rwkv_w_a2=m_od_rwkv_w_a2, m_od_rwkv_w_gate2=m_od_rwkv_w_gate2, m_od_rwkv_k_k=m_od_rwkv_k_k, m_od_rwkv_k_a=m_od_rwkv_k_a, m_od_rwkv_r_k=m_od_rwkv_r_k, m_od_rwkv_gn_g=m_od_rwkv_gn_g, m_od_rwkv_gn_b=m_od_rwkv_gn_b, m_od_w_out=m_od_w_out, m_ln_mix_g=m_ln_mix_g, m_ln_mix_b=m_ln_mix_b, m_xa_w_q=m_xa_w_q, m_xa_w_k=m_xa_w_k, m_xa_w_v=m_xa_w_v, m_xa_w_o=m_xa_w_o, m_ln_xa_g=m_ln_xa_g, m_ln_xa_b=m_ln_xa_b, m_ffn_w_gate=m_ffn_w_gate, m_ffn_w_up=m_ffn_w_up, m_ffn_w_down=m_ffn_w_down, m_ln_ffn_g=m_ln_ffn_g, m_ln_ffn_b=m_ln_ffn_b, v_ev_w_in=v_ev_w_in, v_ev_mla_q_norm=v_ev_mla_q_norm, v_ev_mla_w_uq=v_ev_mla_w_uq, v_ev_mla_kv_norm=v_ev_mla_kv_norm, v_ev_mla_w_ukv=v_ev_mla_w_ukv, v_ev_gla_w_gate2=v_ev_gla_w_gate2, v_ev_gla_b_gate=v_ev_gla_b_gate, v_ev_gla_norm_g=v_ev_gla_norm_g, v_ev_gla_norm_b=v_ev_gla_norm_b, v_ev_w_out=v_ev_w_out, v_od_w_in=v_od_w_in, v_od_rwkv_mu=v_od_rwkv_mu, v_od_rwkv_w0=v_od_rwkv_w0, v_od_rwkv_w_decay2=v_od_rwkv_w_decay2, v_od_rwkv_a0=v_od_rwkv_a0, v_od_rwkv_w_a2=v_od_rwkv_w_a2, v_od_rwkv_w_gate2=v_od_rwkv_w_gate2, v_od_rwkv_k_k=v_od_rwkv_k_k, v_od_rwkv_k_a=v_od_rwkv_k_a, v_od_rwkv_r_k=v_od_rwkv_r_k, v_od_rwkv_gn_g=v_od_rwkv_gn_g, v_od_rwkv_gn_b=v_od_rwkv_gn_b, v_od_w_out=v_od_w_out, v_ln_mix_g=v_ln_mix_g, v_ln_mix_b=v_ln_mix_b, v_xa_w_q=v_xa_w_q, v_xa_w_k=v_xa_w_k, v_xa_w_v=v_xa_w_v, v_xa_w_o=v_xa_w_o, v_ln_xa_g=v_ln_xa_g, v_ln_xa_b=v_ln_xa_b, v_ffn_w_gate=v_ffn_w_gate, v_ffn_w_up=v_ffn_w_up, v_ffn_w_down=v_ffn_w_down, v_ln_ffn_g=v_ln_ffn_g, v_ln_ffn_b=v_ln_ffn_b)
    weights = {n: given[n] for n in TWIN_WEIGHTS}
    shared = {n: given[n] for n in SHARED_INPUTS}
    per_example = {n: given[n] for n in ['x', 'mem']}
    grad_fn = _jax.value_and_grad(_loss, argnums=(0, 1))

    def one_microbatch(ex, loss_target):
        ex = dict(ex)
        diff = ex.pop(TWIN_DIFF_INPUT)
        return grad_fn(weights, diff, {**shared, **ex}, loss_target)

    if N_MICROBATCH == 1:
        loss, (grad_w, grad_x) = one_microbatch(per_example, given["loss_target"])
    else:
        def body(carry, xs):
            loss_sum, grad_sum = carry
            l_k, (gw_k, gx_k) = one_microbatch(xs[0], xs[1])
            with _jax.named_scope("update"):
                return (loss_sum + l_k, _jax.tree.map(_jnp.add, grad_sum, gw_k)), gx_k

        init = (_jnp.zeros((), _jnp.float32), _jax.tree.map(_jnp.zeros_like, weights))
        (loss, grad_w), grad_x = _jax.lax.scan(body, init, (per_example, given["loss_target"]))
    with _jax.named_scope("update"):
        delta_w, new_m, new_v = {}, {}, {}
        for n in TWIN_WEIGHTS:
            delta_w[n], new_m[n], new_v[n] = _adamw(weights[n], grad_w[n], given["m_" + n], given["v_" + n])
    return (loss, grad_x, *[grad_w[n] for n in TWIN_WEIGHTS], *[delta_w[n] for n in TWIN_WEIGHTS],
            *[new_m[n] for n in TWIN_WEIGHTS], *[new_v[n] for n in TWIN_WEIGHTS])
```

```python
import functools

import jax
import jax.numpy as jnp
from jax import lax
from jax.experimental import pallas as pl
from jax.experimental.pallas import tpu as pltpu

F32 = jnp.float32
BF16 = jnp.bfloat16
MESH = pl.DeviceIdType.MESH

ROPE_THETA = 10000.0
LN_EPS = 1e-5
RMS_EPS = 1e-6
DEPTH = 2
DEEPNORM_ALPHA = (2.0 * DEPTH) ** 0.25
MLA_HEADS, MLA_NOPE, MLA_ROPE, MLA_V, MLA_Q_RANK, MLA_KV_RANK = 8, 128, 64, 128, 512, 256
GLA_HEADS, GLA_DK, GLA_DV, GLA_GATE_RANK, GLA_TAU, GLA_CHUNK = 4, 128, 256, 16, 16.0, 64
DIL_HEADS, DIL_HEAD_DIM = 8, 128
DIL_BRANCHES = ((128, 1), (512, 4), (2048, 16))
RWKV_HEADS, RWKV_HEAD_DIM = 16, 64
RWKV_DECAY_RANK, RWKV_A_RANK, RWKV_GATE_RANK = 96, 96, 256
RWKV_GN_EPS = 64e-5
XA_HEADS = 4
DIL_WIDTH = DIL_HEADS * DIL_HEAD_DIM
RWKV_WIDTH = RWKV_HEADS * RWKV_HEAD_DIM
EVEN_IN_WIDTHS = (MLA_Q_RANK, MLA_KV_RANK, MLA_ROPE, GLA_HEADS * GLA_DK, GLA_HEADS * GLA_DK,
                  GLA_HEADS * GLA_DV, GLA_HEADS * GLA_DV, GLA_GATE_RANK)
RWKV_IN_WIDTHS = (RWKV_WIDTH, RWKV_WIDTH, RWKV_WIDTH, RWKV_DECAY_RANK, RWKV_A_RANK, RWKV_GATE_RANK)
ADAM_LR, ADAM_B1, ADAM_B2, ADAM_EPS, ADAM_WD, ADAM_STEP = 0.001, 0.9, 0.999, 1e-08, 0.01, 10

LANES = 128
SUBLANES = 8
VMEM_LIMIT_BYTES = 48 * 1024 * 1024
NEG_BIG = -1e30

PACK_COLS = 1024
PACK_CHUNKS = 4
SMALL_COLS = 128

MATRICES = {
    'ev_w_in': 2, 'ev_mla_w_uq': 2, 'ev_mla_w_ukv': 2, 'ev_gla_w_gate2': 2, 'ev_w_out': 1, 'od_w_in': 2,
    'od_rwkv_w_decay2': 2, 'od_rwkv_w_a2': 2, 'od_rwkv_w_gate2': 2, 'od_w_out': 1,
    'xa_w_q': 1, 'xa_w_k': 1, 'xa_w_v': 1, 'xa_w_o': 1, 'ffn_w_gate': 2, 'ffn_w_up': 2, 'ffn_w_down': 1,
}
SHARDED_VECTORS = {
    'od_rwkv_mu': 1, 'od_rwkv_w0': 1, 'od_rwkv_a0': 1, 'od_rwkv_k_k': 1, 'od_rwkv_k_a': 1,
    'od_rwkv_gn_g': 1, 'od_rwkv_gn_b': 1,
}
REPLICATED = ('ev_mla_q_norm', 'ev_mla_kv_norm', 'ev_gla_b_gate', 'ev_gla_norm_g', 'ev_gla_norm_b', 'od_rwkv_r_k',
              'ln_mix_g', 'ln_mix_b', 'ln_xa_g', 'ln_xa_b', 'ln_ffn_g', 'ln_ffn_b')
WEIGHT_NAMES = ('ev_w_in', 'ev_mla_q_norm', 'ev_mla_w_uq', 'ev_mla_kv_norm', 'ev_mla_w_ukv', 'ev_gla_w_gate2',
                'ev_gla_b_gate', 'ev_gla_norm_g', 'ev_gla_norm_b', 'ev_w_out', 'od_w_in', 'od_rwkv_mu', 'od_rwkv_w0',
                'od_rwkv_w_decay2', 'od_rwkv_a0', 'od_rwkv_w_a2', 'od_rwkv_w_gate2', 'od_rwkv_k_k', 'od_rwkv_k_a',
                'od_rwkv_r_k', 'od_rwkv_gn_g', 'od_rwkv_gn_b', 'od_w_out', 'ln_mix_g', 'ln_mix_b', 'xa_w_q', 'xa_w_k',
                'xa_w_v', 'xa_w_o', 'ln_xa_g', 'ln_xa_b', 'ffn_w_gate', 'ffn_w_up', 'ffn_w_down', 'ln_ffn_g', 'ln_ffn_b')


def _pick(n, cap, mult):
    d = (min(cap, n) // mult) * mult
    while d >= mult:
        if n % d == 0:
            return d
        d -= mult
    return n


def _params(semantics):
    return pltpu.CompilerParams(dimension_semantics=semantics, vmem_limit_bytes=VMEM_LIMIT_BYTES)


_DIMS = {(False, False): (((1,), (0,)), ((), ())), (False, True): (((1,), (1,)), ((), ())),
         (True, False): (((0,), (0,)), ((), ()))}


def _bmm(a, b, ta, tb, out_dtype=F32):
    G = a.shape[0]
    K, M = (a.shape[1], a.shape[2]) if ta else (a.shape[2], a.shape[1])
    N = b.shape[1] if tb else b.shape[2]
    assert (b.shape[2] if tb else b.shape[1]) == K and b.shape[0] == G
    tm, tn, tk = _pick(M, 512, LANES), _pick(N, 512, LANES), _pick(K, 1024, LANES)
    nk = K // tk
    gb = 1
    if tm == M and tn == N and nk == 1:
        per = 4 * (M * K + K * N + M * N)
        gb = _pick(G, max(1, min(8, (2 << 20) // per)), 1)
    dims = _DIMS[(ta, tb)]

    def body(a_ref, b_ref, o_ref, *scratch):
        def prod(i):
            return lax.dot_general(a_ref[i].astype(BF16), b_ref[i].astype(BF16), dims, preferred_element_type=F32)

        if nk == 1:
            for i in range(gb):
                o_ref[i] = prod(i).astype(o_ref.dtype)
        else:
            acc_ref, = scratch
            k = pl.program_id(3)

            @pl.when(k == 0)
            def _():
                acc_ref[...] = jnp.zeros_like(acc_ref)

            for i in range(gb):
                acc_ref[i] += prod(i)

            @pl.when(k == nk - 1)
            def _():
                o_ref[...] = acc_ref[...].astype(o_ref.dtype)

    a_spec = (pl.BlockSpec((gb, tk, tm), lambda g, i, j, k: (g, k, i)) if ta
              else pl.BlockSpec((gb, tm, tk), lambda g, i, j, k: (g, i, k)))
    b_spec = (pl.BlockSpec((gb, tn, tk), lambda g, i, j, k: (g, j, k)) if tb
              else pl.BlockSpec((gb, tk, tn), lambda g, i, j, k: (g, k, j)))
    return pl.pallas_call(
        body, name='bmm_' + ('t' if ta else 'n') + ('t' if tb else 'n'),
        out_shape=jax.ShapeDtypeStruct((G, M, N), out_dtype),
        grid=(G // gb, M // tm, N // tn, nk),
        in_specs=[a_spec, b_spec],
        out_specs=pl.BlockSpec((gb, tm, tn), lambda g, i, j, k: (g, i, j)),
        scratch_shapes=[] if nk == 1 else [pltpu.VMEM((gb, tm, tn), F32)],
        compiler_params=_params(('parallel', 'parallel', 'parallel', 'arbitrary')),
    )(a, b)


@jax.custom_vjp
def bmm_nn(a, b):
    return _bmm(a, b, False, False)


def _bmm_nn_fwd(a, b):
    return _bmm(a, b, False, False), (a, b)


def _bmm_nn_bwd(res, g):
    a, b = res
    return _bmm(g, b, False, True, a.dtype), _bmm(a, g, True, False, b.dtype)


bmm_nn.defvjp(_bmm_nn_fwd, _bmm_nn_bwd)


@jax.custom_vjp
def bmm_nt(a, b):
    return _bmm(a, b, False, True)


def _bmm_nt_fwd(a, b):
    return _bmm(a, b, False, True), (a, b)


def _bmm_nt_bwd(res, g):
    a, b = res
    return _bmm(g, b, False, False, a.dtype), _bmm(g, a, True, False, b.dtype)


bmm_nt.defvjp(_bmm_nt_fwd, _bmm_nt_bwd)


def mm(x, w):
    lead = x.shape[:-1]
    out = bmm_nn(x.reshape(1, -1, x.shape[-1]), w[None])
    return out.reshape(lead + (w.shape[1],))


def _norm_stats(x, center, eps):
    if center:
        xc = x - jnp.mean(x, axis=-1, keepdims=True)
    else:
        xc = x
    rstd = lax.rsqrt(jnp.mean(xc * xc, axis=-1, keepdims=True) + eps)
    return xc * rstd, rstd


def _norm_fwd_call(x, g, b, center, eps):
    R, C = x.shape
    tr = _pick(R, max(8, (1 << 19) // C), 8)

    def body(x_ref, g_ref, b_ref, y_ref):
        xhat, _ = _norm_stats(x_ref[...], center, eps)
        y_ref[...] = xhat * g_ref[...] + b_ref[...]

    row = pl.BlockSpec((tr, C), lambda i: (i, 0))
    vec = pl.BlockSpec((1, C), lambda i: (0, 0))
    return pl.pallas_call(
        body, name='norm_fwd', out_shape=jax.ShapeDtypeStruct((R, C), F32), grid=(R // tr,),
        in_specs=[row, vec, vec], out_specs=row, compiler_params=_params(('parallel',)),
    )(x, g, b)


def _norm_bwd_call(x, g, dy, center, eps):
    R, C = x.shape
    tr = _pick(R, max(8, (1 << 19) // C), 8)

    def body(x_ref, g_ref, dy_ref, dx_ref, dg_ref, db_ref):
        @pl.when(pl.program_id(0) == 0)
        def _():
            dg_ref[...] = jnp.zeros_like(dg_ref)
            db_ref[...] = jnp.zeros_like(db_ref)

        xhat, rstd = _norm_stats(x_ref[...], center, eps)
        dy = dy_ref[...]
        dxh = dy * g_ref[...]
        proj = xhat * jnp.mean(dxh * xhat, axis=-1, keepdims=True)
        if center:
            dx_ref[...] = rstd * (dxh - jnp.mean(dxh, axis=-1, keepdims=True) - proj)
        else:
            dx_ref[...] = rstd * (dxh - proj)
        dg_ref[...] += jnp.sum(dy * xhat, axis=0, keepdims=True)
        db_ref[...] += jnp.sum(dy, axis=0, keepdims=True)

    row = pl.BlockSpec((tr, C), lambda i: (i, 0))
    vec = pl.BlockSpec((1, C), lambda i: (0, 0))
    return pl.pallas_call(
        body, name='norm_bwd',
        out_shape=(jax.ShapeDtypeStruct((R, C), F32), jax.ShapeDtypeStruct((1, C), F32), jax.ShapeDtypeStruct((1, C), F32)),
        grid=(R // tr,), in_specs=[row, vec, row], out_specs=(row, vec, vec), compiler_params=_params(('arbitrary',)),
    )(x, g, dy)


@functools.partial(jax.custom_vjp, nondiff_argnums=(3, 4))
def _norm2d(x, g, b, center, eps):
    return _norm_fwd_call(x, g, b, center, eps)


def _norm2d_fwd(x, g, b, center, eps):
    return _norm_fwd_call(x, g, b, center, eps), (x, g)


def _norm2d_bwd(center, eps, res, dy):
    x, g = res
    return _norm_bwd_call(x, g, dy, center, eps)


_norm2d.defvjp(_norm2d_fwd, _norm2d_bwd)


def layer_norm(x, g, b, eps=LN_EPS):
    C = x.shape[-1]
    return _norm2d(x.reshape(-1, C), g.reshape(1, C), b.reshape(1, C), True, eps).reshape(x.shape)


def rms_norm(x, g):
    C = x.shape[-1]
    return _norm2d(x.reshape(-1, C), g.reshape(1, C), jnp.zeros((1, C), F32), False, RMS_EPS).reshape(x.shape)


def _softmax_fwd_call(s, bias, scale):
    G1, G2, R, C = s.shape
    tr = _pick(R, max(8, (1 << 19) // C), 8)

    def body(s_ref, bias_ref, p_ref, lse_ref):
        z = s_ref[0, 0] * scale + bias_ref[0]
        m = jnp.max(z, axis=-1, keepdims=True)
        e = jnp.exp(z - m)
        den = jnp.sum(e, axis=-1, keepdims=True)
        p_ref[0, 0] = e / den
        lse_ref[0, 0] = m + jnp.log(den)

    blk = pl.BlockSpec((1, 1, tr, C), lambda a, b, r: (a, b, r, 0))
    col = pl.BlockSpec((1, 1, tr, 1), lambda a, b, r: (a, b, r, 0))
    return pl.pallas_call(
        body, name='softmax_fwd',
        out_shape=(jax.ShapeDtypeStruct(s.shape, F32), jax.ShapeDtypeStruct((G1, G2, R, 1), F32)),
        grid=(G1, G2, R // tr), in_specs=[blk, pl.BlockSpec((1, tr, C), lambda a, b, r: (b, r, 0))],
        out_specs=(blk, col), compiler_params=_params(('parallel', 'parallel', 'parallel')),
    )(s, bias)


def _softmax_bwd_call(p, dp, dlse, scale):
    G1, G2, R, C = p.shape
    tr = _pick(R, max(8, (1 << 19) // C), 8)

    def body(p_ref, dp_ref, dlse_ref, ds_ref):
        p = p_ref[0, 0]
        dp = dp_ref[0, 0]
        inner = jnp.sum(dp * p, axis=-1, keepdims=True)
        ds_ref[0, 0] = (p * (dp - inner + dlse_ref[0, 0])) * scale

    blk = pl.BlockSpec((1, 1, tr, C), lambda a, b, r: (a, b, r, 0))
    col = pl.BlockSpec((1, 1, tr, 1), lambda a, b, r: (a, b, r, 0))
    return pl.pallas_call(
        body, name='softmax_bwd', out_shape=jax.ShapeDtypeStruct(p.shape, F32),
        grid=(G1, G2, R // tr), in_specs=[blk, blk, col], out_specs=blk,
        compiler_params=_params(('parallel', 'parallel', 'parallel')),
    )(p, dp, dlse)


@functools.partial(jax.custom_vjp, nondiff_argnums=(2,))
def softmax_lse(s, bias, scale):
    return _softmax_fwd_call(s, bias, scale)


def _softmax_lse_fwd(s, bias, scale):
    p, lse = _softmax_fwd_call(s, bias, scale)
    return (p, lse), (p, bias)


def _softmax_lse_bwd(scale, res, cts):
    p, bias = res
    dp, dlse = cts
    return _softmax_bwd_call(p, dp, dlse, scale), jnp.zeros_like(bias)


softmax_lse.defvjp(_softmax_lse_fwd, _softmax_lse_bwd)


def _dot(a, b, dims):
    return lax.dot_general(a.astype(BF16), b.astype(BF16), dims, preferred_element_type=F32)


_NN, _NT, _TN = _DIMS[(False, False)], _DIMS[(False, True)], _DIMS[(True, False)]


def _gla_fwd_call(q, k, v, dec):
    G, nc, C, dk = q.shape
    dv = v.shape[-1]

    def body(q_ref, k_ref, v_ref, dec_ref, o_ref, st_ref, state):
        @pl.when(pl.program_id(1) == 0)
        def _():
            state[...] = jnp.zeros_like(state)

        s = state[...]
        st_ref[0, 0] = s
        o_ref[0, 0] = _dot(q_ref[0, 0], s, _NN)
        state[...] = s * dec_ref[0, 0] + _dot(k_ref[0, 0], v_ref[0, 0], _TN)

    def spec(r, c):
        return pl.BlockSpec((1, 1, r, c), lambda g, t: (g, t, 0, 0))

    return pl.pallas_call(
        body, name='gla_scan_fwd',
        out_shape=(jax.ShapeDtypeStruct((G, nc, C, dv), F32), jax.ShapeDtypeStruct((G, nc, dk, dv), F32)),
        grid=(G, nc), in_specs=[spec(C, dk), spec(C, dk), spec(C, dv), spec(dk, 1)],
        out_specs=(spec(C, dv), spec(dk, dv)), scratch_shapes=[pltpu.VMEM((dk, dv), F32)],
        compiler_params=_params(('parallel', 'arbitrary')),
    )(q, k, v, dec)


def _gla_bwd_call(q, k, v, dec, states, do):
    G, nc, C, dk = q.shape
    dv = v.shape[-1]

    def body(q_ref, k_ref, v_ref, dec_ref, st_ref, do_ref, dq_ref, dk_ref, dv_ref, ddec_ref, dstate):
        @pl.when(pl.program_id(1) == 0)
        def _():
            dstate[...] = jnp.zeros_like(dstate)

        s = st_ref[0, 0]
        d = dstate[...]
        do = do_ref[0, 0]
        dq_ref[0, 0] = _dot(do, s, _NT)
        dk_ref[0, 0] = _dot(v_ref[0, 0], d, _NT)
        dv_ref[0, 0] = _dot(k_ref[0, 0], d, _NN)
        ddec_ref[0, 0] = jnp.sum(s * d, axis=1, keepdims=True)
        dstate[...] = d * dec_ref[0, 0] + _dot(q_ref[0, 0], do, _TN)

    def spec(r, c):
        return pl.BlockSpec((1, 1, r, c), lambda g, t: (g, nc - 1 - t, 0, 0))

    return pl.pallas_call(
        body, name='gla_scan_bwd',
        out_shape=(jax.ShapeDtypeStruct(q.shape, F32), jax.ShapeDtypeStruct(k.shape, F32),
                   jax.ShapeDtypeStruct(v.shape, F32), jax.ShapeDtypeStruct(dec.shape, F32)),
        grid=(G, nc), in_specs=[spec(C, dk), spec(C, dk), spec(C, dv), spec(dk, 1), spec(dk, dv), spec(C, dv)],
        out_specs=(spec(C, dk), spec(C, dk), spec(C, dv), spec(dk, 1)), scratch_shapes=[pltpu.VMEM((dk, dv), F32)],
        compiler_params=_params(('parallel', 'arbitrary')),
    )(q, k, v, dec, states, do)


@jax.custom_vjp
def gla_scan(q, k, v, dec):
    return _gla_fwd_call(q, k, v, dec)[0]


def _gla_scan_fwd(q, k, v, dec):
    o, states = _gla_fwd_call(q, k, v, dec)
    return o, (q, k, v, dec, states)


def _gla_scan_bwd(res, do):
    return _gla_bwd_call(*res, do)


gla_scan.defvjp(_gla_scan_fwd, _gla_scan_bwd)


RWKV_PAIRS_PER_STEP = 2
RWKV_TIME_BLOCK = 64
RN = RWKV_HEAD_DIM


def _rwkv_consts():
    row = lax.broadcasted_iota(jnp.int32, (RN, LANES), 0)
    lane = lax.broadcasted_iota(jnp.int32, (RN, LANES), 1)
    diag = (lane % RN == row).astype(F32)
    r2 = lax.broadcasted_iota(jnp.int32, (LANES, LANES), 0)
    l2 = lax.broadcasted_iota(jnp.int32, (LANES, LANES), 1)
    seg = (r2 // RN == l2 // RN).astype(BF16)
    return diag, seg


def _seg_sum(p, seg):
    hi = p.astype(BF16)
    r1 = p - hi.astype(F32)
    mid = r1.astype(BF16)
    lo = (r1 - mid.astype(F32)).astype(BF16)

    def d(z):
        return jnp.dot(z, seg, preferred_element_type=F32)

    return d(hi) + d(mid) + d(lo)


def _rwkv_blocks(B, S, C):
    npairs = C // LANES
    pp = RWKV_PAIRS_PER_STEP if npairs % RWKV_PAIRS_PER_STEP == 0 else 1
    T = _pick(S, RWKV_TIME_BLOCK, 8)
    return npairs, pp, T


def _rwkv_fwd_call(r, w, k, v, kk, b):
    B, S, C = r.shape
    npairs, pp, T = _rwkv_blocks(B, S, C)

    def body(r_ref, w_ref, k_ref, v_ref, kk_ref, b_ref, y_ref, sall_ref, state):
        @pl.when(pl.program_id(2) == 0)
        def _():
            state[...] = jnp.zeros_like(state)

        diag, seg = _rwkv_consts()
        rowid = lax.broadcasted_iota(jnp.int32, (SUBLANES, LANES), 0)

        def group(t8, carry):
            rows = pl.ds(pl.multiple_of(t8 * SUBLANES, SUBLANES), SUBLANES)
            out = []
            for p in range(pp):
                sl = slice(p * LANES, (p + 1) * LANES)
                s = carry[p]
                rt, wt, kt = r_ref[0, rows, sl], w_ref[0, rows, sl], k_ref[0, rows, sl]
                vt, kkt, bt = v_ref[0, rows, sl], kk_ref[0, rows, sl], b_ref[0, rows, sl]
                ytile = jnp.zeros((SUBLANES, LANES), F32)
                for j in range(SUBLANES):
                    sall_ref[0, p, t8 * SUBLANES + j] = s
                    sa = -_seg_sum(s * kkt[j:j + 1], seg)
                    vcol = _seg_sum(diag * vt[j:j + 1], seg)
                    s = s * wt[j:j + 1] + sa * bt[j:j + 1] + vcol * kt[j:j + 1]
                    ycol = _seg_sum(s * rt[j:j + 1], seg)
                    ytile = jnp.where(rowid == j, jnp.sum(diag * ycol, axis=0, keepdims=True), ytile)
                y_ref[0, rows, sl] = ytile
                out.append(s)
            return tuple(out)

        final = lax.fori_loop(0, T // SUBLANES, group, tuple(state[p] for p in range(pp)))
        for p in range(pp):
            state[p] = final[p]

    seq = pl.BlockSpec((1, T, pp * LANES), lambda bi, g, t: (bi, t, g))
    return pl.pallas_call(
        body, name='rwkv_scan_fwd',
        out_shape=(jax.ShapeDtypeStruct((B, S, C), F32), jax.ShapeDtypeStruct((B, npairs, S, RN, LANES), F32)),
        grid=(B, npairs // pp, S // T), in_specs=[seq] * 6,
        out_specs=(seq, pl.BlockSpec((1, pp, T, RN, LANES), lambda bi, g, t: (bi, g, t, 0, 0))),
        scratch_shapes=[pltpu.VMEM((pp, RN, LANES), F32)],
        compiler_params=_params(('parallel', 'parallel', 'arbitrary')),
    )(r, w, k, v, kk, b)


def _rwkv_bwd_call(r, w, k, v, kk, b, sall, dy):
    B, S, C = r.shape
    npairs, pp, T = _rwkv_blocks(B, S, C)
    nt = S // T

    def body(r_ref, w_ref, k_ref, v_ref, kk_ref, b_ref, sall_ref, dy_ref,
             dr_ref, dw_ref, dk_ref, dv_ref, dkk_ref, db_ref, dstate):
        @pl.when(pl.program_id(2) == 0)
        def _():
            dstate[...] = jnp.zeros_like(dstate)

        diag, seg = _rwkv_consts()

        def colsum(z):
            return jnp.sum(z, axis=0, keepdims=True)

        rowid = lax.broadcasted_iota(jnp.int32, (SUBLANES, LANES), 0)
        out_refs = (dr_ref, dw_ref, dk_ref, dv_ref, dkk_ref, db_ref)

        def group(i, carry):
            t8 = T // SUBLANES - 1 - i
            rows = pl.ds(pl.multiple_of(t8 * SUBLANES, SUBLANES), SUBLANES)
            out = []
            for p in range(pp):
                sl = slice(p * LANES, (p + 1) * LANES)
                ds = carry[p]
                rt, wt, kt = r_ref[0, rows, sl], w_ref[0, rows, sl], k_ref[0, rows, sl]
                vt, kkt, bt = v_ref[0, rows, sl], kk_ref[0, rows, sl], b_ref[0, rows, sl]
                dyt = dy_ref[0, rows, sl]
                tiles = [jnp.zeros((SUBLANES, LANES), F32) for _ in out_refs]
                for j in reversed(range(SUBLANES)):
                    s = sall_ref[0, p, t8 * SUBLANES + j]
                    rr, ww, kr, kkr, br = rt[j:j + 1], wt[j:j + 1], kt[j:j + 1], kkt[j:j + 1], bt[j:j + 1]
                    sa = -_seg_sum(s * kkr, seg)
                    vcol = _seg_sum(diag * vt[j:j + 1], seg)
                    dycol = _seg_sum(diag * dyt[j:j + 1], seg)
                    s2 = s * ww + sa * br + vcol * kr
                    ds = ds + dycol * rr
                    dsa = _seg_sum(ds * br, seg)
                    dvcol = _seg_sum(ds * kr, seg)
                    vals = (colsum(s2 * dycol), colsum(ds * s), colsum(ds * vcol), colsum(diag * dvcol),
                            -colsum(s * dsa), colsum(ds * sa))
                    tiles = [jnp.where(rowid == j, val, tile) for val, tile in zip(vals, tiles)]
                    ds = ds * ww - dsa * kkr
                for ref, tile in zip(out_refs, tiles):
                    ref[0, rows, sl] = tile
                out.append(ds)
            return tuple(out)

        final = lax.fori_loop(0, T // SUBLANES, group, tuple(dstate[p] for p in range(pp)))
        for p in range(pp):
            dstate[p] = final[p]

    seq = pl.BlockSpec((1, T, pp * LANES), lambda bi, g, t: (bi, nt - 1 - t, g))
    sds = jax.ShapeDtypeStruct((B, S, C), F32)
    return pl.pallas_call(
        body, name='rwkv_scan_bwd', out_shape=(sds,) * 6,
        grid=(B, npairs // pp, nt),
        in_specs=[seq] * 6 + [pl.BlockSpec((1, pp, T, RN, LANES), lambda bi, g, t: (bi, g, nt - 1 - t, 0, 0)), seq],
        out_specs=(seq,) * 6, scratch_shapes=[pltpu.VMEM((pp, RN, LANES), F32)],
        compiler_params=_params(('parallel', 'parallel', 'arbitrary')),
    )(r, w, k, v, kk, b, sall, dy)


@jax.custom_vjp
def rwkv_scan(r, w, k, v, kk, b):
    return _rwkv_fwd_call(r, w, k, v, kk, b)[0]


def _rwkv_scan_fwd(r, w, k, v, kk, b):
    y, sall = _rwkv_fwd_call(r, w, k, v, kk, b)
    return y, (r, w, k, v, kk, b, sall)


def _rwkv_scan_bwd(res, dy):
    return _rwkv_bwd_call(*res, dy)


rwkv_scan.defvjp(_rwkv_scan_fwd, _rwkv_scan_bwd)


def _loss_call(y, target):
    R, D = y.shape
    tr = _pick(R, max(8, (1 << 19) // D), 8)

    def body(y_ref, t_ref, dy_ref, part_ref):
        @pl.when(pl.program_id(0) == 0)
        def _():
            part_ref[...] = jnp.zeros_like(part_ref)

        diff = y_ref[...] - t_ref[...]
        dy_ref[...] = diff / D
        part_ref[...] += jnp.sum(jnp.mean(diff * diff, axis=-1, keepdims=True), axis=0, keepdims=True)

    row = pl.BlockSpec((tr, D), lambda i: (i, 0))
    dy, part = pl.pallas_call(
        body, name='loss_head',
        out_shape=(jax.ShapeDtypeStruct((R, D), F32), jax.ShapeDtypeStruct((1, 1), F32)),
        grid=(R // tr,), in_specs=[row, row], out_specs=(row, pl.BlockSpec((1, 1), lambda i: (0, 0))),
        compiler_params=_params(('arbitrary',)),
    )(y, target)
    return dy, part[0, 0]


def _sum_parts_call(parts):
    P, R, C = parts.shape
    tr = _pick(R, 512, 16)

    def body(p_ref, o_ref):
        acc = p_ref[0].astype(F32)
        for i in range(1, P):
            acc = acc + p_ref[i].astype(F32)
        o_ref[...] = acc

    return pl.pallas_call(
        body, name='sum_parts', out_shape=jax.ShapeDtypeStruct((R, C), F32), grid=(R // tr,),
        in_specs=[pl.BlockSpec((P, tr, C), lambda i: (0, i, 0))], out_specs=pl.BlockSpec((tr, C), lambda i: (i, 0)),
        compiler_params=_params(('parallel',)),
    )(parts)


def _adamw_call(w, g, m, v):
    R, C = w.shape
    tr = _pick(R, 512, 8)

    def body(w_ref, g_ref, m_ref, v_ref, d_ref, nm_ref, nv_ref):
        g = g_ref[...]
        m = ADAM_B1 * m_ref[...] + (1.0 - ADAM_B1) * g
        v = ADAM_B2 * v_ref[...] + (1.0 - ADAM_B2) * (g * g)
        m_hat = m / (1.0 - ADAM_B1 ** ADAM_STEP)
        v_hat = v / (1.0 - ADAM_B2 ** ADAM_STEP)
        d_ref[...] = -ADAM_LR * (m_hat / (jnp.sqrt(v_hat) + ADAM_EPS) + ADAM_WD * w_ref[...])
        nm_ref[...] = m
        nv_ref[...] = v

    row = pl.BlockSpec((tr, C), lambda i: (i, 0))
    sds = jax.ShapeDtypeStruct((R, C), F32)
    return pl.pallas_call(
        body, name='adamw', out_shape=(sds, sds, sds), grid=(R // tr,),
        in_specs=[row] * 4, out_specs=(row,) * 3, compiler_params=_params(('parallel',)),
    )(w, g, m, v)


ANY = pl.BlockSpec(memory_space=pl.ANY)


def _place():
    return lax.axis_index('x'), lax.axis_index('y'), lax.axis_index('c')


def _all_gather_call(pack):
    _, R, C = pack.shape

    def body(pk_ref, out_ref, send_sems, recv_sems, local_sem):
        x, y, c = _place()
        chips = [(1 - x, y), (x, 1 - y), (1 - x, 1 - y)]
        me = 2 * x + y
        mine = pltpu.make_async_copy(pk_ref, out_ref.at[me], local_sem)
        mine.start()

        def copy(k, src, dst, to):
            return pltpu.make_async_remote_copy(src_ref=src, dst_ref=dst, send_sem=send_sems.at[k],
                                                recv_sem=recv_sems.at[k], device_id=to, device_id_type=MESH)

        first = [copy(j, pk_ref.at[c], out_ref.at[me, c], (px, py, c)) for j, (px, py) in enumerate(chips)]
        for cp in first:
            cp.start()
        passed = []
        for j, (px, py) in enumerate(chips):
            landed = out_ref.at[2 * px + py, c]
            copy(j, landed, landed, (px, py, c)).wait_recv()
            fwd = copy(3 + j, landed, landed, (x, y, 1 - c))
            fwd.start()
            passed.append(fwd)
        for j, (px, py) in enumerate(chips):
            other = out_ref.at[2 * px + py, 1 - c]
            copy(3 + j, other, other, (x, y, 1 - c)).wait_recv()
        for cp in first + passed:
            cp.wait_send()
        mine.wait()

    return pl.pallas_call(
        body, name='all_gather', out_shape=jax.ShapeDtypeStruct((4, 2, R, C), pack.dtype),
        in_specs=[ANY], out_specs=ANY,
        scratch_shapes=[pltpu.SemaphoreType.DMA((6,)), pltpu.SemaphoreType.DMA((6,)), pltpu.SemaphoreType.DMA],
    )(pack)


def _scatter_call(src):
    _, _, R, C = src.shape

    def body(src_ref, out_ref, send_sems, recv_sems, local_sem):
        x, y, c = _place()
        me = 4 * x + 2 * y + c
        own = pltpu.make_async_copy(src_ref.at[2 * x + y, c], out_ref.at[me], local_sem)
        own.start()
        peers = []
        for rel in range(1, 8):
            px = 1 - x if rel & 4 else x
            py = 1 - y if rel & 2 else y
            pc = 1 - c if rel & 1 else c
            peers.append((rel - 1, px, py, pc))
        sends = []
        for k, px, py, pc in peers:
            cp = pltpu.make_async_remote_copy(src_ref=src_ref.at[2 * px + py, pc], dst_ref=out_ref.at[me],
                                              send_sem=send_sems.at[k], recv_sem=recv_sems.at[k],
                                              device_id=(px, py, pc), device_id_type=MESH)
            cp.start()
            sends.append(cp)
        for k, px, py, pc in peers:
            slot = out_ref.at[4 * px + 2 * py + pc]
            pltpu.make_async_remote_copy(src_ref=slot, dst_ref=slot, send_sem=send_sems.at[k], recv_sem=recv_sems.at[k],
                                         device_id=(px, py, pc), device_id_type=MESH).wait_recv()
        for cp in sends:
            cp.wait_send()
        own.wait()

    return pl.pallas_call(
        body, name='scatter_parts', out_shape=jax.ShapeDtypeStruct((8, R, C), src.dtype),
        in_specs=[ANY], out_specs=ANY,
        scratch_shapes=[pltpu.SemaphoreType.DMA((7,)), pltpu.SemaphoreType.DMA((7,)), pltpu.SemaphoreType.DMA],
    )(src)


def _sibling_exchange_call(half):
    R, C = half.shape

    def body(h_ref, out_ref, send_sem, recv_sem, local_sem):
        x, y, c = _place()
        own = pltpu.make_async_copy(h_ref, out_ref.at[c], local_sem)
        own.start()
        cp = pltpu.make_async_remote_copy(src_ref=h_ref, dst_ref=out_ref.at[c], send_sem=send_sem, recv_sem=recv_sem,
                                          device_id=(x, y, 1 - c), device_id_type=MESH)
        cp.start()
        other = out_ref.at[1 - c]
        pltpu.make_async_remote_copy(src_ref=other, dst_ref=other, send_sem=send_sem, recv_sem=recv_sem,
                                     device_id=(x, y, 1 - c), device_id_type=MESH).wait_recv()
        cp.wait_send()
        own.wait()

    return pl.pallas_call(
        body, name='sibling_exchange', out_shape=jax.ShapeDtypeStruct((2, R, C), half.dtype),
        in_specs=[ANY], out_specs=ANY,
        scratch_shapes=[pltpu.SemaphoreType.DMA, pltpu.SemaphoreType.DMA, pltpu.SemaphoreType.DMA],
    )(half)


def rope_tables(seq_len, dim):
    inv = ROPE_THETA ** (-jnp.arange(0, dim, 2, dtype=F32) / dim)
    ang = jnp.arange(seq_len, dtype=F32)[:, None] * inv[None, :]
    return jnp.cos(ang), jnp.sin(ang)


def apply_rope(x, cos, sin):
    x1, x2 = jnp.split(x, 2, axis=-1)
    return jnp.concatenate([x1 * cos - x2 * sin, x1 * sin + x2 * cos], axis=-1)


def _pad_to(n):
    return -(-n // LANES) * LANES


def _pad_cols(w, widths):
    parts, at = [], 0
    for n in widths:
        parts.append(jnp.pad(w[..., at:at + n], [(0, 0)] * (w.ndim - 1) + [(0, _pad_to(n) - n)]))
        at += n
    return jnp.concatenate(parts, axis=-1)


def _split_padded(t, widths):
    out, at = [], 0
    for n in widths:
        out.append(t[..., at:at + n])
        at += _pad_to(n)
    return out


def _pad_rows(w, rows):
    return jnp.pad(w, ((0, rows - w.shape[0]), (0, 0)))


def _heads_attention(q, k, v, bias, scale):
    s = bmm_nt(q, k)
    p, _ = softmax_lse(s[:, None], bias[None], scale)
    return bmm_nn(p[:, 0], v)


def gla(q, k, v, r, gate_lr, w_gate2, b_gate, norm_g, norm_b):
    B, S, _ = q.shape
    H, dk, dv, C = GLA_HEADS, GLA_DK, GLA_DV, GLA_CHUNK
    nc = S // C
    log_a = jax.nn.log_sigmoid(mm(gate_lr, w_gate2) + b_gate) / GLA_TAU

    def chunks(t, d):
        return t.reshape(B, nc, C, H, d).transpose(0, 3, 1, 2, 4)

    qc = chunks(q, dk) * (dk ** -0.5)
    kc = chunks(k, dk)
    vc = chunks(v, dv)
    b = jnp.cumsum(chunks(log_a, dk), axis=3)
    b_last = b[:, :, :, -1:, :]
    q_dec = qc * jnp.exp(b)
    k_inv = kc * jnp.exp(-b)
    k_end = kc * jnp.exp(b_last - b)
    causal = jnp.tril(jnp.ones((C, C), dtype=bool))
    G = B * H
    att = bmm_nt(q_dec.reshape(G * nc, C, dk), k_inv.reshape(G * nc, C, dk))
    att = jnp.where(causal, att, 0.0)
    o_intra = bmm_nn(att, vc.reshape(G * nc, C, dv)).reshape(B, H, nc, C, dv)
    dec = jnp.exp(b_last[:, :, :, 0, :]).reshape(G, nc, dk, 1)
    o_inter = gla_scan(q_dec.reshape(G, nc, C, dk), k_end.reshape(G, nc, C, dk), vc.reshape(G, nc, C, dv), dec)
    o = o_intra + o_inter.reshape(B, H, nc, C, dv)
    o = o.transpose(0, 2, 3, 1, 4).reshape(B, S, H, dv)
    o = layer_norm(o, norm_g, norm_b).reshape(B, S, H * dv)
    return o * jax.nn.silu(r)


def even_mixer(x, p):
    B, S, _ = x.shape
    H = MLA_HEADS
    cos, sin = rope_tables(S, MLA_ROPE)
    z = mm(x, _pad_cols(p['ev_w_in'][0], EVEN_IN_WIDTHS))
    c_q, c_kv, k_pe, q_g, k_g, v_g, r_g, _ = _split_padded(z, EVEN_IN_WIDTHS)
    lr_at = sum(_pad_to(n) for n in EVEN_IN_WIDTHS[:-1])
    lr_g = z[..., lr_at:]
    q = mm(rms_norm(c_q, p['ev_mla_q_norm'][0]), p['ev_mla_w_uq'][0])
    q = q.reshape(B, S, H, MLA_NOPE + MLA_ROPE).transpose(0, 2, 1, 3)
    kv = mm(rms_norm(c_kv, p['ev_mla_kv_norm'][0]), p['ev_mla_w_ukv'][0])
    kv = kv.reshape(B, S, H, MLA_NOPE + MLA_V).transpose(0, 2, 1, 3)
    q_pe = apply_rope(q[..., MLA_NOPE:], cos, sin)
    k_pe = jnp.broadcast_to(apply_rope(k_pe[:, None], cos, sin), (B, H, S, MLA_ROPE))
    qf = jnp.concatenate([q[..., :MLA_NOPE], q_pe], axis=-1)
    kf = jnp.concatenate([kv[..., :MLA_NOPE], k_pe], axis=-1)
    pos = jnp.arange(S)
    bias = jnp.where(pos[None, :] <= pos[:, None], 0.0, NEG_BIG).astype(F32)
    a_out = _heads_attention(qf.reshape(B * H, S, -1), kf.reshape(B * H, S, -1),
                             kv[..., MLA_NOPE:].reshape(B * H, S, MLA_V), bias, (MLA_NOPE + MLA_ROPE) ** -0.5)
    a_out = a_out.reshape(B, H, S, MLA_V).transpose(0, 2, 1, 3).reshape(B, S, H * MLA_V)
    w_gate2 = _pad_rows(p['ev_gla_w_gate2'][0], lr_g.shape[-1])
    b_out = gla(q_g, k_g, v_g, r_g, lr_g, w_gate2, p['ev_gla_b_gate'][0], p['ev_gla_norm_g'][0], p['ev_gla_norm_b'][0])
    return mm(jnp.concatenate([a_out, b_out], axis=-1), p['ev_w_out'][0])


def dilated_branch(q, k, v, window, dil):
    B, H, S, dh = q.shape
    span = window // dil
    L = S // dil
    nb = -(-L // span)
    Lp = nb * span

    def residues(t):
        t = t.reshape(B, H, L, dil, dh).transpose(0, 1, 3, 2, 4)
        t = jnp.pad(t, ((0, 0), (0, 0), (0, 0), (0, Lp - L), (0, 0)))
        return t.reshape(B, H, dil, nb, span, dh)

    def with_prev(t):
        prev = jnp.pad(t, ((0, 0), (0, 0), (0, 0), (1, 0), (0, 0), (0, 0)))[:, :, :, :-1]
        return jnp.concatenate([prev, t], axis=4)

    qb = residues(q)
    kw, vw = with_prev(residues(k)), with_prev(residues(v))
    G = B * H * dil * nb
    s = bmm_nt(qb.reshape(G, span, dh), kw.reshape(G, 2 * span, dh))
    qi = jnp.arange(span)[:, None] + span
    kj = jnp.arange(2 * span)[None, :]
    dist = qi - kj
    in_band = (dist >= 0) & (dist <= span)
    has_prev = (jnp.arange(nb) > 0)[:, None, None] | (kj >= span)[None]
    valid = in_band[None] & has_prev
    bias = jnp.where(valid, 0.0, NEG_BIG).astype(F32)
    p, lse = softmax_lse(s.reshape(B * H * dil, nb, span, 2 * span), bias, dh ** -0.5)
    o = bmm_nn(p.reshape(G, span, 2 * span), vw.reshape(G, 2 * span, dh)).reshape(B, H, dil, nb, span, dh)
    lse = lse.reshape(B, H, dil, nb, span)

    def back(t):
        t = t.reshape((B, H, dil, Lp) + t.shape[5:])[:, :, :, :L]
        return jnp.moveaxis(t, 2, 3).reshape((B, H, S) + t.shape[4:])

    return back(o), back(lse)


def dilated_mixture(q, k, v):
    outs, lses = [], []
    for window, dil in DIL_BRANCHES:
        o, lse = dilated_branch(q, k, v, window, dil)
        outs.append(o)
        lses.append(lse)
    wts = jax.nn.softmax(jnp.stack(lses, axis=0), axis=0)
    return jnp.sum(wts[..., None] * jnp.stack(outs, axis=0), axis=0)


def token_shift(t, mu):
    prev = jnp.pad(t, ((0, 0), (1, 0), (0, 0)))[:, :-1]
    return t + (prev - t) * mu


def rwkv7(r, k, v, w_lr, a_lr, g_lr, w0, w_decay2, a0, w_a2, w_gate2, k_k, k_a, r_k, gn_g, gn_b):
    B, S, _ = r.shape
    H, n = RWKV_HEADS, RWKV_HEAD_DIM
    w = -jax.nn.softplus(-(w0 + mm(jnp.tanh(w_lr), w_decay2))) - 0.5
    decay = jnp.exp(-jnp.exp(w))
    a = jax.nn.sigmoid(a0 + mm(a_lr, w_a2))
    g = mm(jax.nn.sigmoid(g_lr), w_gate2)
    kk = (k * k_k).reshape(B, S, H, n)
    kk = kk / jnp.maximum(jnp.sqrt(jnp.sum(kk * kk, axis=-1, keepdims=True)), 1e-12)
    kk = kk.reshape(B, S, H * n)
    kh = k * (1.0 + (a - 1.0) * k_a)
    y = rwkv_scan(r, decay, kh, v, kk, kk * a).reshape(B, S, H, n)
    y = layer_norm(y, jnp.ones((n,), F32), jnp.zeros((n,), F32), RWKV_GN_EPS).reshape(B, S, H * n) * gn_g + gn_b
    bonus = jnp.sum((r * kh).reshape(B, S, H, n) * r_k, axis=-1, keepdims=True) * v.reshape(B, S, H, n)
    y = y + bonus.reshape(B, S, H * n)
    return y * g


def odd_mixer(x, p):
    B, S, _ = x.shape
    cos, sin = rope_tables(S, DIL_HEAD_DIM)
    widths = (3 * DIL_WIDTH,) + RWKV_IN_WIDTHS
    h = mm(x, _pad_cols(p['od_w_in'][0], widths))
    c_in = h[..., :3 * DIL_WIDTH]
    d_in = h[..., 3 * DIL_WIDTH:]
    q, k, v = [t.reshape(B, S, DIL_HEADS, DIL_HEAD_DIM).transpose(0, 2, 1, 3) for t in jnp.split(c_in, 3, axis=-1)]
    q, k = apply_rope(q, cos, sin), apply_rope(k, cos, sin)
    c_out = dilated_mixture(q, k, v).transpose(0, 2, 1, 3).reshape(B, S, DIL_WIDTH)
    mu = _pad_cols(p['od_rwkv_mu'][0], RWKV_IN_WIDTHS)
    sh = token_shift(d_in, mu)
    at = [0]
    for n in RWKV_IN_WIDTHS:
        at.append(at[-1] + _pad_to(n))
    r, kd, vd = [sh[..., at[i]:at[i + 1]] for i in range(3)]
    w_lr, a_lr, g_lr = [sh[..., at[i]:at[i + 1]] for i in range(3, 6)]
    d_out = rwkv7(r, kd, vd, w_lr, a_lr, g_lr, p['od_rwkv_w0'][0], _pad_rows(p['od_rwkv_w_decay2'][0], w_lr.shape[-1]),
                  p['od_rwkv_a0'][0], _pad_rows(p['od_rwkv_w_a2'][0], a_lr.shape[-1]), p['od_rwkv_w_gate2'][0],
                  p['od_rwkv_k_k'][0], p['od_rwkv_k_a'][0], p['od_rwkv_r_k'][0], p['od_rwkv_gn_g'][0], p['od_rwkv_gn_b'][0])
    return mm(jnp.concatenate([c_out, d_out], axis=-1), p['od_w_out'][0])


def cross_attention(x, mem, w_q, w_k, w_v, w_o):
    B, S, D = x.shape
    M = mem.shape[1]
    hd = D // XA_HEADS

    def heads(t, n):
        return t.reshape(B, n, XA_HEADS, hd).transpose(0, 2, 1, 3).reshape(B * XA_HEADS, n, hd)

    q, k, v = heads(mm(x, w_q), S), heads(mm(mem, w_k), M), heads(mm(mem, w_v), M)
    o = _heads_attention(q, k, v, jnp.zeros((S, M), F32), hd ** -0.5)
    o = o.reshape(B, XA_HEADS, S, hd).transpose(0, 2, 1, 3).reshape(B, S, D)
    return mm(o, w_o)


def swiglu(x, w_gate, w_up, w_down):
    return mm(jax.nn.silu(mm(x, w_gate)) * mm(x, w_up), w_down)


def forward(p, x, mem):
    h = x
    for layer in range(DEPTH):
        mix = even_mixer(h, p) if layer % 2 == 0 else odd_mixer(h, p)
        h = layer_norm(DEEPNORM_ALPHA * h + mix, p['ln_mix_g'][layer], p['ln_mix_b'][layer])
        xa = cross_attention(h, mem, p['xa_w_q'][layer], p['xa_w_k'][layer], p['xa_w_v'][layer], p['xa_w_o'][layer])
        h = layer_norm(DEEPNORM_ALPHA * h + xa, p['ln_xa_g'][layer], p['ln_xa_b'][layer])
        ff = swiglu(h, p['ffn_w_gate'][layer], p['ffn_w_up'][layer], p['ffn_w_down'][layer])
        h = layer_norm(DEEPNORM_ALPHA * h + ff, p['ln_ffn_g'][layer], p['ln_ffn_b'][layer])
    return h


def _flat_pack(arrays, length, dtype):
    flat = jnp.concatenate([a.reshape(-1).astype(dtype) for a in arrays])
    return jnp.pad(flat, (0, length - flat.shape[0]))


def _unpack(flat, shapes):
    out, at = [], 0
    for shp in shapes:
        n = 1
        for d in shp:
            n *= d
        out.append(flat[at:at + n].reshape(shp))
        at += n
    return out


def _shard_of(full, axis, s):
    n = full.shape[axis] // 4
    return lax.slice_in_dim(full, s * n, (s + 1) * n, axis=axis)


def kernel(x, mem, ev_w_in, ev_mla_q_norm, ev_mla_w_uq, ev_mla_kv_norm, ev_mla_w_ukv, ev_gla_w_gate2, ev_gla_b_gate, ev_gla_norm_g, ev_gla_norm_b, ev_w_out, od_w_in, od_rwkv_mu, od_rwkv_w0, od_rwkv_w_decay2, od_rwkv_a0, od_rwkv_w_a2, od_rwkv_w_gate2, od_rwkv_k_k, od_rwkv_k_a, od_rwkv_r_k, od_rwkv_gn_g, od_rwkv_gn_b, od_w_out, ln_mix_g, ln_mix_b, xa_w_q, xa_w_k, xa_w_v, xa_w_o, ln_xa_g, ln_xa_b, ffn_w_gate, ffn_w_up, ffn_w_down, ln_ffn_g, ln_ffn_b, loss_target, m_ev_w_in, m_ev_mla_q_norm, m_ev_mla_w_uq, m_ev_mla_kv_norm, m_ev_mla_w_ukv, m_ev_gla_w_gate2, m_ev_gla_b_gate, m_ev_gla_norm_g, m_ev_gla_norm_b, m_ev_w_out, m_od_w_in, m_od_rwkv_mu, m_od_rwkv_w0, m_od_rwkv_w_decay2, m_od_rwkv_a0, m_od_rwkv_w_a2, m_od_rwkv_w_gate2, m_od_rwkv_k_k, m_od_rwkv_k_a, m_od_rwkv_r_k, m_od_rwkv_gn_g, m_od_rwkv_gn_b, m_od_w_out, m_ln_mix_g, m_ln_mix_b, m_xa_w_q, m_xa_w_k, m_xa_w_v, m_xa_w_o, m_ln_xa_g, m_ln_xa_b, m_ffn_w_gate, m_ffn_w_up, m_ffn_w_down, m_ln_ffn_g, m_ln_ffn_b, v_ev_w_in, v_ev_mla_q_norm, v_ev_mla_w_uq, v_ev_mla_kv_norm, v_ev_mla_w_ukv, v_ev_gla_w_gate2, v_ev_gla_b_gate, v_ev_gla_norm_g, v_ev_gla_norm_b, v_ev_w_out, v_od_w_in, v_od_rwkv_mu, v_od_rwkv_w0, v_od_rwkv_w_decay2, v_od_rwkv_a0, v_od_rwkv_w_a2, v_od_rwkv_w_gate2, v_od_rwkv_k_k, v_od_rwkv_k_a, v_od_rwkv_r_k, v_od_rwkv_gn_g, v_od_rwkv_gn_b, v_od_w_out, v_ln_mix_g, v_ln_mix_b, v_xa_w_q, v_xa_w_k, v_xa_w_v, v_xa_w_o, v_ln_xa_g, v_ln_xa_b, v_ffn_w_gate, v_ffn_w_up, v_ffn_w_down, v_ln_ffn_g, v_ln_ffn_b):
    given = dict(locals())
    W = {n: given[n] for n in WEIGHT_NAMES}
    Mo = {n: given['m_' + n] for n in WEIGHT_NAMES}
    Vo = {n: given['v_' + n] for n in WEIGHT_NAMES}
    mat_names = [n for n in WEIGHT_NAMES if n in MATRICES]
    vec_names = [n for n in WEIGHT_NAMES if n in SHARDED_VECTORS]
    rep_names = list(REPLICATED)
    mat_shapes = [W[n].shape for n in mat_names]
    vec_shapes = [W[n].shape for n in vec_names]
    rep_shapes = [W[n].shape for n in rep_names]

    def count(shapes):
        total = 0
        for shp in shapes:
            n = 1
            for d in shp:
                n *= d
            total += n
        return total

    rc = -(-count(mat_shapes) // (PACK_CHUNKS * 2 * PACK_COLS * LANES)) * LANES
    mat_len = PACK_CHUNKS * 2 * rc * PACK_COLS
    rv = -(-count(vec_shapes) // (2 * SMALL_COLS * 8)) * 8
    vec_len = 2 * rv * SMALL_COLS
    rr = -(-count(rep_shapes) // (SMALL_COLS * 8)) * 8
    rep_len = rr * SMALL_COLS

    wmat = _flat_pack([W[n] for n in mat_names], mat_len, BF16).reshape(PACK_CHUNKS, 2, rc, PACK_COLS)
    gathered = jnp.stack([_all_gather_call(wmat[i]) for i in range(PACK_CHUNKS)], axis=1)
    gvec = _all_gather_call(_flat_pack([W[n] for n in vec_names], vec_len, F32).reshape(2, rv, SMALL_COLS))
    full = {}
    mat_parts = [_unpack(gathered[s].reshape(-1), mat_shapes) for s in range(4)]
    for i, n in enumerate(mat_names):
        full[n] = jnp.concatenate([mat_parts[s][i] for s in range(4)], axis=MATRICES[n])
    vec_parts = [_unpack(gvec[s].reshape(-1), vec_shapes) for s in range(4)]
    for i, n in enumerate(vec_names):
        full[n] = jnp.concatenate([vec_parts[s][i] for s in range(4)], axis=SHARDED_VECTORS[n])
    for n in rep_names:
        full[n] = W[n]

    B, S, D = x.shape
    y, vjp = jax.vjp(lambda p, xx: forward(p, xx, mem), full, x)
    dy, part = _loss_call(y.reshape(B * S, D), loss_target.reshape(B * S, D))
    loss = lax.psum(0.5 * part, ('x', 'y', 'c'))
    gfull, grad_x = vjp(dy.reshape(B, S, D))

    gmat = jnp.stack([_flat_pack([_shard_of(gfull[n], MATRICES[n], s) for n in mat_names], mat_len, BF16)
                      for s in range(4)]).reshape(4, PACK_CHUNKS, 2, rc, PACK_COLS)
    halves = [_sum_parts_call(_scatter_call(gmat[:, i])) for i in range(PACK_CHUNKS)]
    gshard = jnp.stack([_sibling_exchange_call(h) for h in halves]).reshape(-1, PACK_COLS)
    grep = _flat_pack([gfull[n] for n in rep_names], rep_len, F32)
    gsmall = jnp.stack([jnp.concatenate([
        _flat_pack([_shard_of(gfull[n], SHARDED_VECTORS[n], s) for n in vec_names], vec_len, F32), grep])
        for s in range(4)]).reshape(4, 1, 2 * rv + rr, SMALL_COLS)
    gsmall = _sum_parts_call(_scatter_call(jnp.concatenate([gsmall, gsmall], axis=1)))

    def mat_pack(src):
        return _flat_pack([src[n] for n in mat_names], mat_len, F32).reshape(-1, PACK_COLS)

    def small_pack(src):
        return jnp.concatenate([_flat_pack([src[n] for n in vec_names], vec_len, F32),
                                _flat_pack([src[n] for n in rep_names], rep_len, F32)]).reshape(-1, SMALL_COLS)

    big = (gshard,) + _adamw_call(mat_pack(W), gshard, mat_pack(Mo), mat_pack(Vo))
    small = (gsmall,) + _adamw_call(small_pack(W), gsmall, small_pack(Mo), small_pack(Vo))
    groups = []
    for bg, sm in zip(big, small):
        vals = dict(zip(mat_names, _unpack(bg.reshape(-1), mat_shapes)))
        sm = sm.reshape(-1)
        vals.update(zip(vec_names, _unpack(sm[:vec_len], vec_shapes)))
        vals.update(zip(rep_names, _unpack(sm[vec_len:], rep_shapes)))
        groups.append([vals[n] for n in WEIGHT_NAMES])
    return (loss, grad_x, *groups[0], *groups[1], *groups[2], *groups[3])
```

```python
import functools

import jax
import jax.numpy as jnp
from jax import lax
from jax.experimental import pallas as pl
from jax.experimental.pallas import tpu as pltpu

F32 = jnp.float32
BF16 = jnp.bfloat16
MESH = pl.DeviceIdType.MESH

ROPE_THETA = 10000.0
LN_EPS = 1e-5
RMS_EPS = 1e-6
DEPTH = 2
DEEPNORM_ALPHA = (2.0 * DEPTH) ** 0.25
MLA_HEADS, MLA_NOPE, MLA_ROPE, MLA_V, MLA_Q_RANK, MLA_KV_RANK = 8, 128, 64, 128, 512, 256
GLA_HEADS, GLA_DK, GLA_DV, GLA_GATE_RANK, GLA_TAU, GLA_CHUNK = 4, 128, 256, 16, 16.0, 64
DIL_HEADS, DIL_HEAD_DIM = 8, 128
DIL_BRANCHES = ((128, 1), (512, 4), (2048, 16))
RWKV_HEADS, RWKV_HEAD_DIM = 16, 64
RWKV_DECAY_RANK, RWKV_A_RANK, RWKV_GATE_RANK = 96, 96, 256
RWKV_GN_EPS = 64e-5
XA_HEADS = 4
DIL_WIDTH = DIL_HEADS * DIL_HEAD_DIM
RWKV_WIDTH = RWKV_HEADS * RWKV_HEAD_DIM
EVEN_IN_WIDTHS = (MLA_Q_RANK, MLA_KV_RANK, MLA_ROPE, GLA_HEADS * GLA_DK, GLA_HEADS * GLA_DK,
                  GLA_HEADS * GLA_DV, GLA_HEADS * GLA_DV, GLA_GATE_RANK)
RWKV_IN_WIDTHS = (RWKV_WIDTH, RWKV_WIDTH, RWKV_WIDTH, RWKV_DECAY_RANK, RWKV_A_RANK, RWKV_GATE_RANK)
ADAM_LR, ADAM_B1, ADAM_B2, ADAM_EPS, ADAM_WD, ADAM_STEP = 0.001, 0.9, 0.999, 1e-08, 0.01, 10

LANES = 128
SUBLANES = 8
VMEM_LIMIT_BYTES = 48 * 1024 * 1024
NEG_BIG = -1e30

PACK_COLS = 1024
PACK_CHUNKS = 4
SMALL_COLS = 128

MATRICES = {
    'ev_w_in': 2, 'ev_mla_w_uq': 2, 'ev_mla_w_ukv': 2, 'ev_gla_w_gate2': 2, 'ev_w_out': 1, 'od_w_in': 2,
    'od_rwkv_w_decay2': 2, 'od_rwkv_w_a2': 2, 'od_rwkv_w_gate2': 2, 'od_w_out': 1,
    'xa_w_q': 1, 'xa_w_k': 1, 'xa_w_v': 1, 'xa_w_o': 1, 'ffn_w_gate': 2, 'ffn_w_up': 2, 'ffn_w_down': 1,
}
SHARDED_VECTORS = {
    'od_rwkv_mu': 1, 'od_rwkv_w0': 1, 'od_rwkv_a0': 1, 'od_rwkv_k_k': 1, 'od_rwkv_k_a': 1,
    'od_rwkv_gn_g': 1, 'od_rwkv_gn_b': 1,
}
REPLICATED = ('ev_mla_q_norm', 'ev_mla_kv_norm', 'ev_gla_b_gate', 'ev_gla_norm_g', 'ev_gla_norm_b', 'od_rwkv_r_k',
              'ln_mix_g', 'ln_mix_b', 'ln_xa_g', 'ln_xa_b', 'ln_ffn_g', 'ln_ffn_b')
WEIGHT_NAMES = ('ev_w_in', 'ev_mla_q_norm', 'ev_mla_w_uq', 'ev_mla_kv_norm', 'ev_mla_w_ukv', 'ev_gla_w_gate2',
                'ev_gla_b_gate', 'ev_gla_norm_g', 'ev_gla_norm_b', 'ev_w_out', 'od_w_in', 'od_rwkv_mu', 'od_rwkv_w0',
                'od_rwkv_w_decay2', 'od_rwkv_a0', 'od_rwkv_w_a2', 'od_rwkv_w_gate2', 'od_rwkv_k_k', 'od_rwkv_k_a',
                'od_rwkv_r_k', 'od_rwkv_gn_g', 'od_rwkv_gn_b', 'od_w_out', 'ln_mix_g', 'ln_mix_b', 'xa_w_q', 'xa_w_k',
                'xa_w_v', 'xa_w_o', 'ln_xa_g', 'ln_xa_b', 'ffn_w_gate', 'ffn_w_up', 'ffn_w_down', 'ln_ffn_g', 'ln_ffn_b')


def _pick(n, cap, mult):
    d = (min(cap, n) // mult) * mult
    while d >= mult:
        if n % d == 0:
            return d
        d -= mult
    return n


def _params(semantics):
    return pltpu.CompilerParams(dimension_semantics=semantics, vmem_limit_bytes=VMEM_LIMIT_BYTES)


_DIMS = {(False, False): (((1,), (0,)), ((), ())), (False, True): (((1,), (1,)), ((), ())),
         (True, False): (((0,), (0,)), ((), ()))}


def _bmm(a, b, ta, tb, out_dtype=F32):
    G = a.shape[0]
    K, M = (a.shape[1], a.shape[2]) if ta else (a.shape[2], a.shape[1])
    N = b.shape[1] if tb else b.shape[2]
    assert (b.shape[2] if tb else b.shape[1]) == K and b.shape[0] == G
    tm, tn = _pick(M, 1024, LANES), _pick(N, 512, LANES)
    tk = K if K <= 2048 else _pick(K, 2048, LANES)
    nk = K // tk
    gb = 1
    if tm == M and tn == N and nk == 1:
        per = 4 * (M * K + K * N + M * N)
        gb = _pick(G, max(1, min(8, (2 << 20) // per)), 1)
    dims = _DIMS[(ta, tb)]

    def body(a_ref, b_ref, o_ref, *scratch):
        def prod(i):
            return lax.dot_general(a_ref[i].astype(BF16), b_ref[i].astype(BF16), dims, preferred_element_type=F32)

        if nk == 1:
            for i in range(gb):
                o_ref[i] = prod(i).astype(o_ref.dtype)
        else:
            acc_ref, = scratch
            k = pl.program_id(3)

            @pl.when(k == 0)
            def _():
                acc_ref[...] = jnp.zeros_like(acc_ref)

            for i in range(gb):
                acc_ref[i] += prod(i)

            @pl.when(k == nk - 1)
            def _():
                o_ref[...] = acc_ref[...].astype(o_ref.dtype)

    a_spec = (pl.BlockSpec((gb, tk, tm), lambda g, i, j, k: (g, k, i)) if ta
              else pl.BlockSpec((gb, tm, tk), lambda g, i, j, k: (g, i, k)))
    b_spec = (pl.BlockSpec((gb, tn, tk), lambda g, i, j, k: (g, j, k)) if tb
              else pl.BlockSpec((gb, tk, tn), lambda g, i, j, k: (g, k, j)))
    return pl.pallas_call(
        body, name='bmm_' + ('t' if ta else 'n') + ('t' if tb else 'n'),
        out_shape=jax.ShapeDtypeStruct((G, M, N), out_dtype),
        grid=(G // gb, M // tm, N // tn, nk),
        in_specs=[a_spec, b_spec],
        out_specs=pl.BlockSpec((gb, tm, tn), lambda g, i, j, k: (g, i, j)),
        scratch_shapes=[] if nk == 1 else [pltpu.VMEM((gb, tm, tn), F32)],
        compiler_params=_params(('parallel', 'parallel', 'parallel', 'arbitrary')),
    )(a, b)


def _like(x):
    return jnp.zeros((), x.dtype)


@jax.custom_vjp
def bmm_nn(a, b):
    return _bmm(a, b, False, False)


def _bmm_nn_fwd(a, b):
    ab, bb = a.astype(BF16), b.astype(BF16)
    return _bmm(ab, bb, False, False), (ab, bb, _like(a), _like(b))


def _bmm_nn_bwd(res, g):
    a, b, la, lb = res
    g = g.astype(BF16)
    return _bmm(g, b, False, True, la.dtype), _bmm(a, g, True, False, lb.dtype)


bmm_nn.defvjp(_bmm_nn_fwd, _bmm_nn_bwd)


@jax.custom_vjp
def bmm_nt(a, b):
    return _bmm(a, b, False, True)


def _bmm_nt_fwd(a, b):
    ab, bb = a.astype(BF16), b.astype(BF16)
    return _bmm(ab, bb, False, True), (ab, bb, _like(a), _like(b))


def _bmm_nt_bwd(res, g):
    a, b, la, lb = res
    g = g.astype(BF16)
    return _bmm(g, b, False, False, la.dtype), _bmm(g, a, True, False, lb.dtype)


bmm_nt.defvjp(_bmm_nt_fwd, _bmm_nt_bwd)


def mm(x, w):
    lead = x.shape[:-1]
    out = bmm_nn(x.reshape(1, -1, x.shape[-1]), w[None])
    return out.reshape(lead + (w.shape[1],))


def _norm_stats(x, center, eps):
    if center:
        xc = x - jnp.mean(x, axis=-1, keepdims=True)
    else:
        xc = x
    rstd = lax.rsqrt(jnp.mean(xc * xc, axis=-1, keepdims=True) + eps)
    return xc * rstd, rstd


def _norm_fwd_call(x, g, b, center, eps):
    R, C = x.shape
    tr = _pick(R, max(8, (1 << 19) // C), 8)

    def body(x_ref, g_ref, b_ref, y_ref):
        xhat, _ = _norm_stats(x_ref[...], center, eps)
        y_ref[...] = xhat * g_ref[...] + b_ref[...]

    row = pl.BlockSpec((tr, C), lambda i: (i, 0))
    vec = pl.BlockSpec((1, C), lambda i: (0, 0))
    return pl.pallas_call(
        body, name='norm_fwd', out_shape=jax.ShapeDtypeStruct((R, C), F32), grid=(R // tr,),
        in_specs=[row, vec, vec], out_specs=row, compiler_params=_params(('parallel',)),
    )(x, g, b)


def _norm_bwd_call(x, g, dy, center, eps):
    R, C = x.shape
    tr = _pick(R, max(8, (1 << 19) // C), 8)

    def body(x_ref, g_ref, dy_ref, dx_ref, dg_ref, db_ref):
        @pl.when(pl.program_id(0) == 0)
        def _():
            dg_ref[...] = jnp.zeros_like(dg_ref)
            db_ref[...] = jnp.zeros_like(db_ref)

        xhat, rstd = _norm_stats(x_ref[...], center, eps)
        dy = dy_ref[...]
        dxh = dy * g_ref[...]
        proj = xhat * jnp.mean(dxh * xhat, axis=-1, keepdims=True)
        if center:
            dx_ref[...] = rstd * (dxh - jnp.mean(dxh, axis=-1, keepdims=True) - proj)
        else:
            dx_ref[...] = rstd * (dxh - proj)
        dg_ref[...] += jnp.sum(dy * xhat, axis=0, keepdims=True)
        db_ref[...] += jnp.sum(dy, axis=0, keepdims=True)

    row = pl.BlockSpec((tr, C), lambda i: (i, 0))
    vec = pl.BlockSpec((1, C), lambda i: (0, 0))
    return pl.pallas_call(
        body, name='norm_bwd',
        out_shape=(jax.ShapeDtypeStruct((R, C), F32), jax.ShapeDtypeStruct((1, C), F32), jax.ShapeDtypeStruct((1, C), F32)),
        grid=(R // tr,), in_specs=[row, vec, row], out_specs=(row, vec, vec), compiler_params=_params(('arbitrary',)),
    )(x, g, dy)


@functools.partial(jax.custom_vjp, nondiff_argnums=(3, 4))
def _norm2d(x, g, b, center, eps):
    return _norm_fwd_call(x, g, b, center, eps)


def _norm2d_fwd(x, g, b, center, eps):
    return _norm_fwd_call(x, g, b, center, eps), (x, g)


def _norm2d_bwd(center, eps, res, dy):
    x, g = res
    return _norm_bwd_call(x, g, dy, center, eps)


_norm2d.defvjp(_norm2d_fwd, _norm2d_bwd)


def layer_norm(x, g, b, eps=LN_EPS):
    C = x.shape[-1]
    return _norm2d(x.reshape(-1, C), g.reshape(1, C), b.reshape(1, C), True, eps).reshape(x.shape)


def rms_norm(x, g):
    C = x.shape[-1]
    return _norm2d(x.reshape(-1, C), g.reshape(1, C), jnp.zeros((1, C), F32), False, RMS_EPS).reshape(x.shape)


def _softmax_fwd_call(s, bias, scale):
    G1, G2, R, C = s.shape
    tr = _pick(R, max(8, (1 << 19) // C), 8)

    def body(s_ref, bias_ref, p_ref, lse_ref):
        z = s_ref[0, 0] * scale + bias_ref[0]
        m = jnp.max(z, axis=-1, keepdims=True)
        e = jnp.exp(z - m)
        den = jnp.sum(e, axis=-1, keepdims=True)
        p_ref[0, 0] = e / den
        lse_ref[0, 0] = m + jnp.log(den)

    blk = pl.BlockSpec((1, 1, tr, C), lambda a, b, r: (a, b, r, 0))
    col = pl.BlockSpec((1, 1, tr, 1), lambda a, b, r: (a, b, r, 0))
    return pl.pallas_call(
        body, name='softmax_fwd',
        out_shape=(jax.ShapeDtypeStruct(s.shape, F32), jax.ShapeDtypeStruct((G1, G2, R, 1), F32)),
        grid=(G1, G2, R // tr), in_specs=[blk, pl.BlockSpec((1, tr, C), lambda a, b, r: (b, r, 0))],
        out_specs=(blk, col), compiler_params=_params(('parallel', 'parallel', 'parallel')),
    )(s, bias)


def _softmax_bwd_call(p, dp, dlse, scale):
    G1, G2, R, C = p.shape
    tr = _pick(R, max(8, (1 << 19) // C), 8)

    def body(p_ref, dp_ref, dlse_ref, ds_ref):
        p = p_ref[0, 0]
        dp = dp_ref[0, 0]
        inner = jnp.sum(dp * p, axis=-1, keepdims=True)
        ds_ref[0, 0] = (p * (dp - inner + dlse_ref[0, 0])) * scale

    blk = pl.BlockSpec((1, 1, tr, C), lambda a, b, r: (a, b, r, 0))
    col = pl.BlockSpec((1, 1, tr, 1), lambda a, b, r: (a, b, r, 0))
    return pl.pallas_call(
        body, name='softmax_bwd', out_shape=jax.ShapeDtypeStruct(p.shape, F32),
        grid=(G1, G2, R // tr), in_specs=[blk, blk, col], out_specs=blk,
        compiler_params=_params(('parallel', 'parallel', 'parallel')),
    )(p, dp, dlse)


@functools.partial(jax.custom_vjp, nondiff_argnums=(2,))
def softmax_lse(s, bias, scale):
    return _softmax_fwd_call(s, bias, scale)


def _softmax_lse_fwd(s, bias, scale):
    p, lse = _softmax_fwd_call(s, bias, scale)
    return (p, lse), (p, bias)


def _softmax_lse_bwd(scale, res, cts):
    p, bias = res
    dp, dlse = cts
    return _softmax_bwd_call(p, dp, dlse, scale), jnp.zeros_like(bias)


softmax_lse.defvjp(_softmax_lse_fwd, _softmax_lse_bwd)


def _dot(a, b, dims):
    return lax.dot_general(a.astype(BF16), b.astype(BF16), dims, preferred_element_type=F32)


_NN, _NT, _TN = _DIMS[(False, False)], _DIMS[(False, True)], _DIMS[(True, False)]


def _gla_fwd_call(q, k, v, dec):
    G, nc, C, dk = q.shape
    dv = v.shape[-1]

    def body(q_ref, k_ref, v_ref, dec_ref, o_ref, st_ref, state):
        @pl.when(pl.program_id(1) == 0)
        def _():
            state[...] = jnp.zeros_like(state)

        s = state[...]
        st_ref[0, 0] = s
        o_ref[0, 0] = _dot(q_ref[0, 0], s, _NN)
        state[...] = s * dec_ref[0, 0] + _dot(k_ref[0, 0], v_ref[0, 0], _TN)

    def spec(r, c):
        return pl.BlockSpec((1, 1, r, c), lambda g, t: (g, t, 0, 0))

    return pl.pallas_call(
        body, name='gla_scan_fwd',
        out_shape=(jax.ShapeDtypeStruct((G, nc, C, dv), F32), jax.ShapeDtypeStruct((G, nc, dk, dv), F32)),
        grid=(G, nc), in_specs=[spec(C, dk), spec(C, dk), spec(C, dv), spec(dk, 1)],
        out_specs=(spec(C, dv), spec(dk, dv)), scratch_shapes=[pltpu.VMEM((dk, dv), F32)],
        compiler_params=_params(('parallel', 'arbitrary')),
    )(q, k, v, dec)


def _gla_bwd_call(q, k, v, dec, states, do):
    G, nc, C, dk = q.shape
    dv = v.shape[-1]

    def body(q_ref, k_ref, v_ref, dec_ref, st_ref, do_ref, dq_ref, dk_ref, dv_ref, ddec_ref, dstate):
        @pl.when(pl.program_id(1) == 0)
        def _():
            dstate[...] = jnp.zeros_like(dstate)

        s = st_ref[0, 0]
        d = dstate[...]
        do = do_ref[0, 0]
        dq_ref[0, 0] = _dot(do, s, _NT)
        dk_ref[0, 0] = _dot(v_ref[0, 0], d, _NT)
        dv_ref[0, 0] = _dot(k_ref[0, 0], d, _NN)
        ddec_ref[0, 0] = jnp.sum(s * d, axis=1, keepdims=True)
        dstate[...] = d * dec_ref[0, 0] + _dot(q_ref[0, 0], do, _TN)

    def spec(r, c):
        return pl.BlockSpec((1, 1, r, c), lambda g, t: (g, nc - 1 - t, 0, 0))

    return pl.pallas_call(
        body, name='gla_scan_bwd',
        out_shape=(jax.ShapeDtypeStruct(q.shape, F32), jax.ShapeDtypeStruct(k.shape, F32),
                   jax.ShapeDtypeStruct(v.shape, F32), jax.ShapeDtypeStruct(dec.shape, F32)),
        grid=(G, nc), in_specs=[spec(C, dk), spec(C, dk), spec(C, dv), spec(dk, 1), spec(dk, dv), spec(C, dv)],
        out_specs=(spec(C, dk), spec(C, dk), spec(C, dv), spec(dk, 1)), scratch_shapes=[pltpu.VMEM((dk, dv), F32)],
        compiler_params=_params(('parallel', 'arbitrary')),
    )(q, k, v, dec, states, do)


@jax.custom_vjp
def gla_scan(q, k, v, dec):
    return _gla_fwd_call(q, k, v, dec)[0]


def _gla_scan_fwd(q, k, v, dec):
    o, states = _gla_fwd_call(q, k, v, dec)
    return o, (q, k, v, dec, states)


def _gla_scan_bwd(res, do):
    return _gla_bwd_call(*res, do)


gla_scan.defvjp(_gla_scan_fwd, _gla_scan_bwd)


RWKV_PAIRS_PER_STEP = 4
RWKV_TIME_BLOCK = 64
RN = RWKV_HEAD_DIM


def _rwkv_consts():
    row = lax.broadcasted_iota(jnp.int32, (RN, LANES), 0)
    lane = lax.broadcasted_iota(jnp.int32, (RN, LANES), 1)
    diag = (lane % RN == row).astype(F32)
    r2 = lax.broadcasted_iota(jnp.int32, (LANES, LANES), 0)
    l2 = lax.broadcasted_iota(jnp.int32, (LANES, LANES), 1)
    seg = (r2 // RN == l2 // RN).astype(BF16)
    return diag, seg


def _seg_sum(p, seg):
    hi = p.astype(BF16)
    mid = (p - hi.astype(F32)).astype(BF16)
    both = jnp.dot(jnp.concatenate([hi, mid], axis=0), seg, preferred_element_type=F32)
    return both[:RN] + both[RN:]


def _rwkv_blocks(B, S, C):
    npairs = C // LANES
    pp = RWKV_PAIRS_PER_STEP if npairs % RWKV_PAIRS_PER_STEP == 0 else 1
    T = _pick(S, RWKV_TIME_BLOCK, 8)
    return npairs, pp, T


def _rwkv_fwd_call(r, w, k, v, kk, b):
    B, S, C = r.shape
    npairs, pp, T = _rwkv_blocks(B, S, C)

    def body(r_ref, w_ref, k_ref, v_ref, kk_ref, b_ref, y_ref, sall_ref, state):
        @pl.when(pl.program_id(2) == 0)
        def _():
            state[...] = jnp.zeros_like(state)

        diag, seg = _rwkv_consts()
        rowid = lax.broadcasted_iota(jnp.int32, (SUBLANES, LANES), 0)

        def group(t8, carry):
            rows = pl.ds(pl.multiple_of(t8 * SUBLANES, SUBLANES), SUBLANES)
            sls = [slice(p * LANES, (p + 1) * LANES) for p in range(pp)]
            ops = [[ref[0, rows, sl] for ref in (r_ref, w_ref, k_ref, v_ref, kk_ref, b_ref)] for sl in sls]
            s = list(carry)
            ytiles = [jnp.zeros((SUBLANES, LANES), F32)] * pp
            for j in range(SUBLANES):
                for p in range(pp):
                    rt, wt, kt, vt, kkt, bt = ops[p]
                    sall_ref[0, p, t8 * SUBLANES + j] = s[p]
                    sa = -_seg_sum(s[p] * kkt[j:j + 1], seg)
                    vcol = _seg_sum(diag * vt[j:j + 1], seg)
                    s[p] = s[p] * wt[j:j + 1] + sa * bt[j:j + 1] + vcol * kt[j:j + 1]
                    ycol = _seg_sum(s[p] * rt[j:j + 1], seg)
                    ytiles[p] = jnp.where(rowid == j, jnp.sum(diag * ycol, axis=0, keepdims=True), ytiles[p])
            for p in range(pp):
                y_ref[0, rows, sls[p]] = ytiles[p]
            return tuple(s)

        final = lax.fori_loop(0, T // SUBLANES, group, tuple(state[p] for p in range(pp)))
        for p in range(pp):
            state[p] = final[p]

    seq = pl.BlockSpec((1, T, pp * LANES), lambda bi, g, t: (bi, t, g))
    return pl.pallas_call(
        body, name='rwkv_scan_fwd',
        out_shape=(jax.ShapeDtypeStruct((B, S, C), F32), jax.ShapeDtypeStruct((B, npairs, S, RN, LANES), F32)),
        grid=(B, npairs // pp, S // T), in_specs=[seq] * 6,
        out_specs=(seq, pl.BlockSpec((1, pp, T, RN, LANES), lambda bi, g, t: (bi, g, t, 0, 0))),
        scratch_shapes=[pltpu.VMEM((pp, RN, LANES), F32)],
        compiler_params=_params(('parallel', 'parallel', 'arbitrary')),
    )(r, w, k, v, kk, b)


def _rwkv_bwd_call(r, w, k, v, kk, b, sall, dy):
    B, S, C = r.shape
    npairs, pp, T = _rwkv_blocks(B, S, C)
    nt = S // T

    def body(r_ref, w_ref, k_ref, v_ref, kk_ref, b_ref, sall_ref, dy_ref,
             dr_ref, dw_ref, dk_ref, dv_ref, dkk_ref, db_ref, dstate):
        @pl.when(pl.program_id(2) == 0)
        def _():
            dstate[...] = jnp.zeros_like(dstate)

        diag, seg = _rwkv_consts()

        def colsum(z):
            return jnp.sum(z, axis=0, keepdims=True)

        rowid = lax.broadcasted_iota(jnp.int32, (SUBLANES, LANES), 0)
        out_refs = (dr_ref, dw_ref, dk_ref, dv_ref, dkk_ref, db_ref)

        def group(i, carry):
            t8 = T // SUBLANES - 1 - i
            rows = pl.ds(pl.multiple_of(t8 * SUBLANES, SUBLANES), SUBLANES)
            sls = [slice(p * LANES, (p + 1) * LANES) for p in range(pp)]
            ops = [[ref[0, rows, sl] for ref in (r_ref, w_ref, k_ref, v_ref, kk_ref, b_ref, dy_ref)] for sl in sls]
            ds = list(carry)
            tiles = [[jnp.zeros((SUBLANES, LANES), F32) for _ in out_refs] for _ in range(pp)]
            for j in reversed(range(SUBLANES)):
                for p in range(pp):
                    rt, wt, kt, vt, kkt, bt, dyt = ops[p]
                    s = sall_ref[0, p, t8 * SUBLANES + j]
                    rr, ww, kr, kkr, br = rt[j:j + 1], wt[j:j + 1], kt[j:j + 1], kkt[j:j + 1], bt[j:j + 1]
                    sa = -_seg_sum(s * kkr, seg)
                    vcol = _seg_sum(diag * vt[j:j + 1], seg)
                    dycol = _seg_sum(diag * dyt[j:j + 1], seg)
                    s2 = s * ww + sa * br + vcol * kr
                    d = ds[p] + dycol * rr
                    dsa = _seg_sum(d * br, seg)
                    dvcol = _seg_sum(d * kr, seg)
                    vals = (colsum(s2 * dycol), colsum(d * s), colsum(d * vcol), colsum(diag * dvcol),
                            -colsum(s * dsa), colsum(d * sa))
                    tiles[p] = [jnp.where(rowid == j, val, tile) for val, tile in zip(vals, tiles[p])]
                    ds[p] = d * ww - dsa * kkr
            for p in range(pp):
                for ref, tile in zip(out_refs, tiles[p]):
                    ref[0, rows, sls[p]] = tile
            return tuple(ds)

        final = lax.fori_loop(0, T // SUBLANES, group, tuple(dstate[p] for p in range(pp)))
        for p in range(pp):
            dstate[p] = final[p]

    seq = pl.BlockSpec((1, T, pp * LANES), lambda bi, g, t: (bi, nt - 1 - t, g))
    sds = jax.ShapeDtypeStruct((B, S, C), F32)
    return pl.pallas_call(
        body, name='rwkv_scan_bwd', out_shape=(sds,) * 6,
        grid=(B, npairs // pp, nt),
        in_specs=[seq] * 6 + [pl.BlockSpec((1, pp, T, RN, LANES), lambda bi, g, t: (bi, g, nt - 1 - t, 0, 0)), seq],
        out_specs=(seq,) * 6, scratch_shapes=[pltpu.VMEM((pp, RN, LANES), F32)],
        compiler_params=_params(('parallel', 'parallel', 'arbitrary')),
    )(r, w, k, v, kk, b, sall, dy)


@jax.custom_vjp
def rwkv_scan(r, w, k, v, kk, b):
    return _rwkv_fwd_call(r, w, k, v, kk, b)[0]


def _rwkv_scan_fwd(r, w, k, v, kk, b):
    y, sall = _rwkv_fwd_call(r, w, k, v, kk, b)
    return y, (r, w, k, v, kk, b, sall)


def _rwkv_scan_bwd(res, dy):
    return _rwkv_bwd_call(*res, dy)


rwkv_scan.defvjp(_rwkv_scan_fwd, _rwkv_scan_bwd)


def _loss_call(y, target):
    R, D = y.shape
    tr = _pick(R, max(8, (1 << 19) // D), 8)

    def body(y_ref, t_ref, dy_ref, part_ref):
        @pl.when(pl.program_id(0) == 0)
        def _():
            part_ref[...] = jnp.zeros_like(part_ref)

        diff = y_ref[...] - t_ref[...]
        dy_ref[...] = diff / D
        part_ref[...] += jnp.sum(jnp.mean(diff * diff, axis=-1, keepdims=True), axis=0, keepdims=True)

    row = pl.BlockSpec((tr, D), lambda i: (i, 0))
    dy, part = pl.pallas_call(
        body, name='loss_head',
        out_shape=(jax.ShapeDtypeStruct((R, D), F32), jax.ShapeDtypeStruct((1, 1), F32)),
        grid=(R // tr,), in_specs=[row, row], out_specs=(row, pl.BlockSpec((1, 1), lambda i: (0, 0))),
        compiler_params=_params(('arbitrary',)),
    )(y, target)
    return dy, part[0, 0]


def _sum_parts_call(parts):
    P, R, C = parts.shape
    tr = _pick(R, 512, 16)

    def body(p_ref, o_ref):
        acc = p_ref[0].astype(F32)
        for i in range(1, P):
            acc = acc + p_ref[i].astype(F32)
        o_ref[...] = acc

    return pl.pallas_call(
        body, name='sum_parts', out_shape=jax.ShapeDtypeStruct((R, C), F32), grid=(R // tr,),
        in_specs=[pl.BlockSpec((P, tr, C), lambda i: (0, i, 0))], out_specs=pl.BlockSpec((tr, C), lambda i: (i, 0)),
        compiler_params=_params(('parallel',)),
    )(parts)


def _adamw_call(w, g, m, v):
    R, C = w.shape
    tr = _pick(R, max(8, (1 << 18) // C), 8)

    def body(w_ref, g_ref, m_ref, v_ref, d_ref, nm_ref, nv_ref):
        g = g_ref[...]
        m = ADAM_B1 * m_ref[...] + (1.0 - ADAM_B1) * g
        v = ADAM_B2 * v_ref[...] + (1.0 - ADAM_B2) * (g * g)
        m_hat = m / (1.0 - ADAM_B1 ** ADAM_STEP)
        v_hat = v / (1.0 - ADAM_B2 ** ADAM_STEP)
        d_ref[...] = -ADAM_LR * (m_hat / (jnp.sqrt(v_hat) + ADAM_EPS) + ADAM_WD * w_ref[...])
        nm_ref[...] = m
        nv_ref[...] = v

    row = pl.BlockSpec((tr, C), lambda i: (i, 0))
    sds = jax.ShapeDtypeStruct((R, C), F32)
    return pl.pallas_call(
        body, name='adamw', out_shape=(sds, sds, sds), grid=(R // tr,),
        in_specs=[row] * 4, out_specs=(row,) * 3, compiler_params=_params(('parallel',)),
    )(w, g, m, v)


ANY = pl.BlockSpec(memory_space=pl.ANY)


def _place():
    return lax.axis_index('x'), lax.axis_index('y'), lax.axis_index('c')


def _all_gather_call(pack):
    _, R, C = pack.shape

    def body(pk_ref, out_ref, send_sems, recv_sems, local_sem):
        x, y, c = _place()
        chips = [(1 - x, y), (x, 1 - y), (1 - x, 1 - y)]
        me = 2 * x + y
        mine = pltpu.make_async_copy(pk_ref, out_ref.at[me], local_sem)
        mine.start()

        def copy(k, src, dst, to):
            return pltpu.make_async_remote_copy(src_ref=src, dst_ref=dst, send_sem=send_sems.at[k],
                                                recv_sem=recv_sems.at[k], device_id=to, device_id_type=MESH)

        first = [copy(j, pk_ref.at[c], out_ref.at[me, c], (px, py, c)) for j, (px, py) in enumerate(chips)]
        for cp in first:
            cp.start()
        passed = []
        for j, (px, py) in enumerate(chips):
            landed = out_ref.at[2 * px + py, c]
            copy(j, landed, landed, (px, py, c)).wait_recv()
            fwd = copy(3 + j, landed, landed, (x, y, 1 - c))
            fwd.start()
            passed.append(fwd)
        for j, (px, py) in enumerate(chips):
            other = out_ref.at[2 * px + py, 1 - c]
            copy(3 + j, other, other, (x, y, 1 - c)).wait_recv()
        for cp in first + passed:
            cp.wait_send()
        mine.wait()

    return pl.pallas_call(
        body, name='all_gather', out_shape=jax.ShapeDtypeStruct((4, 2, R, C), pack.dtype),
        in_specs=[ANY], out_specs=ANY,
        scratch_shapes=[pltpu.SemaphoreType.DMA((6,)), pltpu.SemaphoreType.DMA((6,)), pltpu.SemaphoreType.DMA],
    )(pack)


def _scatter_call(src):
    _, _, R, C = src.shape

    def body(src_ref, out_ref, send_sems, recv_sems, local_sem):
        x, y, c = _place()
        me = 4 * x + 2 * y + c
        own = pltpu.make_async_copy(src_ref.at[2 * x + y, c], out_ref.at[me], local_sem)
        own.start()
        peers = []
        for rel in range(1, 8):
            px = 1 - x if rel & 4 else x
            py = 1 - y if rel & 2 else y
            pc = 1 - c if rel & 1 else c
            peers.append((rel - 1, px, py, pc))
        sends = []
        for k, px, py, pc in peers:
            cp = pltpu.make_async_remote_copy(src_ref=src_ref.at[2 * px + py, pc], dst_ref=out_ref.at[me],
                                              send_sem=send_sems.at[k], recv_sem=recv_sems.at[k],
                                              device_id=(px, py, pc), device_id_type=MESH)
            cp.start()
            sends.append(cp)
        for k, px, py, pc in peers:
            slot = out_ref.at[4 * px + 2 * py + pc]
            pltpu.make_async_remote_copy(src_ref=slot, dst_ref=slot, send_sem=send_sems.at[k], recv_sem=recv_sems.at[k],
                                         device_id=(px, py, pc), device_id_type=MESH).wait_recv()
        for cp in sends:
            cp.wait_send()
        own.wait()

    return pl.pallas_call(
        body, name='scatter_parts', out_shape=jax.ShapeDtypeStruct((8, R, C), src.dtype),
        in_specs=[ANY], out_specs=ANY,
        scratch_shapes=[pltpu.SemaphoreType.DMA((7,)), pltpu.SemaphoreType.DMA((7,)), pltpu.SemaphoreType.DMA],
    )(src)


def _sibling_exchange_call(half):
    R, C = half.shape

    def body(h_ref, out_ref, send_sem, recv_sem, local_sem):
        x, y, c = _place()
        own = pltpu.make_async_copy(h_ref, out_ref.at[c], local_sem)
        own.start()
        cp = pltpu.make_async_remote_copy(src_ref=h_ref, dst_ref=out_ref.at[c], send_sem=send_sem, recv_sem=recv_sem,
                                          device_id=(x, y, 1 - c), device_id_type=MESH)
        cp.start()
        other = out_ref.at[1 - c]
        pltpu.make_async_remote_copy(src_ref=other, dst_ref=other, send_sem=send_sem, recv_sem=recv_sem,
                                     device_id=(x, y, 1 - c), device_id_type=MESH).wait_recv()
        cp.wait_send()
        own.wait()

    return pl.pallas_call(
        body, name='sibling_exchange', out_shape=jax.ShapeDtypeStruct((2, R, C), half.dtype),
        in_specs=[ANY], out_specs=ANY,
        scratch_shapes=[pltpu.SemaphoreType.DMA, pltpu.SemaphoreType.DMA, pltpu.SemaphoreType.DMA],
    )(half)


def rope_tables(seq_len, dim):
    inv = ROPE_THETA ** (-jnp.arange(0, dim, 2, dtype=F32) / dim)
    ang = jnp.arange(seq_len, dtype=F32)[:, None] * inv[None, :]
    return jnp.cos(ang), jnp.sin(ang)


def apply_rope(x, cos, sin):
    x1, x2 = jnp.split(x, 2, axis=-1)
    return jnp.concatenate([x1 * cos - x2 * sin, x1 * sin + x2 * cos], axis=-1)


def _pad_to(n):
    return -(-n // LANES) * LANES


def _pad_cols(w, widths):
    parts, at = [], 0
    for n in widths:
        parts.append(jnp.pad(w[..., at:at + n], [(0, 0)] * (w.ndim - 1) + [(0, _pad_to(n) - n)]))
        at += n
    return jnp.concatenate(parts, axis=-1)


def _split_padded(t, widths):
    out, at = [], 0
    for n in widths:
        out.append(t[..., at:at + n])
        at += _pad_to(n)
    return out


def _pad_rows(w, rows):
    return jnp.pad(w, ((0, rows - w.shape[0]), (0, 0)))


def _heads_attention(q, k, v, bias, scale):
    s = bmm_nt(q, k)
    p, _ = softmax_lse(s[:, None], bias[None], scale)
    return bmm_nn(p[:, 0], v)


def gla(q, k, v, r, gate_lr, w_gate2, b_gate, norm_g, norm_b):
    B, S, _ = q.shape
    H, dk, dv, C = GLA_HEADS, GLA_DK, GLA_DV, GLA_CHUNK
    nc = S // C
    log_a = jax.nn.log_sigmoid(mm(gate_lr, w_gate2) + b_gate) / GLA_TAU

    def chunks(t, d):
        return t.reshape(B, nc, C, H, d).transpose(0, 3, 1, 2, 4)

    qc = chunks(q, dk) * (dk ** -0.5)
    kc = chunks(k, dk)
    vc = chunks(v, dv)
    b = jnp.cumsum(chunks(log_a, dk), axis=3)
    b_last = b[:, :, :, -1:, :]
    q_dec = qc * jnp.exp(b)
    k_inv = kc * jnp.exp(-b)
    k_end = kc * jnp.exp(b_last - b)
    causal = jnp.tril(jnp.ones((C, C), dtype=bool))
    G = B * H
    att = bmm_nt(q_dec.reshape(G * nc, C, dk), k_inv.reshape(G * nc, C, dk))
    att = jnp.where(causal, att, 0.0)
    o_intra = bmm_nn(att, vc.reshape(G * nc, C, dv)).reshape(B, H, nc, C, dv)
    dec = jnp.exp(b_last[:, :, :, 0, :]).reshape(G, nc, dk, 1)
    o_inter = gla_scan(q_dec.reshape(G, nc, C, dk), k_end.reshape(G, nc, C, dk), vc.reshape(G, nc, C, dv), dec)
    o = o_intra + o_inter.reshape(B, H, nc, C, dv)
    o = o.transpose(0, 2, 3, 1, 4).reshape(B, S, H, dv)
    o = layer_norm(o, norm_g, norm_b).reshape(B, S, H * dv)
    return o * jax.nn.silu(r)


def even_mixer(x, p):
    B, S, _ = x.shape
    H = MLA_HEADS
    cos, sin = rope_tables(S, MLA_ROPE)
    z = mm(x, _pad_cols(p['ev_w_in'][0], EVEN_IN_WIDTHS))
    c_q, c_kv, k_pe, q_g, k_g, v_g, r_g, _ = _split_padded(z, EVEN_IN_WIDTHS)
    lr_at = sum(_pad_to(n) for n in EVEN_IN_WIDTHS[:-1])
    lr_g = z[..., lr_at:]
    q = mm(rms_norm(c_q, p['ev_mla_q_norm'][0]), p['ev_mla_w_uq'][0])
    q = q.reshape(B, S, H, MLA_NOPE + MLA_ROPE).transpose(0, 2, 1, 3)
    kv = mm(rms_norm(c_kv, p['ev_mla_kv_norm'][0]), p['ev_mla_w_ukv'][0])
    kv = kv.reshape(B, S, H, MLA_NOPE + MLA_V).transpose(0, 2, 1, 3)
    q_pe = apply_rope(q[..., MLA_NOPE:], cos, sin)
    k_pe = jnp.broadcast_to(apply_rope(k_pe[:, None], cos, sin), (B, H, S, MLA_ROPE))
    qf = jnp.concatenate([q[..., :MLA_NOPE], q_pe], axis=-1)
    kf = jnp.concatenate([kv[..., :MLA_NOPE], k_pe], axis=-1)
    pos = jnp.arange(S)
    bias = jnp.where(pos[None, :] <= pos[:, None], 0.0, NEG_BIG).astype(F32)
    a_out = _heads_attention(qf.reshape(B * H, S, -1), kf.reshape(B * H, S, -1),
                             kv[..., MLA_NOPE:].reshape(B * H, S, MLA_V), bias, (MLA_NOPE + MLA_ROPE) ** -0.5)
    a_out = a_out.reshape(B, H, S, MLA_V).transpose(0, 2, 1, 3).reshape(B, S, H * MLA_V)
    w_gate2 = _pad_rows(p['ev_gla_w_gate2'][0], lr_g.shape[-1])
    b_out = gla(q_g, k_g, v_g, r_g, lr_g, w_gate2, p['ev_gla_b_gate'][0], p['ev_gla_norm_g'][0], p['ev_gla_norm_b'][0])
    return mm(jnp.concatenate([a_out, b_out], axis=-1), p['ev_w_out'][0])


def dilated_branch(q, k, v, window, dil):
    B, H, S, dh = q.shape
    span = window // dil
    L = S // dil
    nb = -(-L // span)
    Lp = nb * span

    def residues(t):
        t = t.reshape(B, H, L, dil, dh).transpose(0, 1, 3, 2, 4)
        t = jnp.pad(t, ((0, 0), (0, 0), (0, 0), (0, Lp - L), (0, 0)))
        return t.reshape(B, H, dil, nb, span, dh)

    def with_prev(t):
        prev = jnp.pad(t, ((0, 0), (0, 0), (0, 0), (1, 0), (0, 0), (0, 0)))[:, :, :, :-1]
        return jnp.concatenate([prev, t], axis=4)

    qb = residues(q)
    kw, vw = with_prev(residues(k)), with_prev(residues(v))
    G = B * H * dil * nb
    s = bmm_nt(qb.reshape(G, span, dh), kw.reshape(G, 2 * span, dh))
    qi = jnp.arange(span)[:, None] + span
    kj = jnp.arange(2 * span)[None, :]
    dist = qi - kj
    in_band = (dist >= 0) & (dist <= span)
    has_prev = (jnp.arange(nb) > 0)[:, None, None] | (kj >= span)[None]
    valid = in_band[None] & has_prev
    bias = jnp.where(valid, 0.0, NEG_BIG).astype(F32)
    p, lse = softmax_lse(s.reshape(B * H * dil, nb, span, 2 * span), bias, dh ** -0.5)
    o = bmm_nn(p.reshape(G, span, 2 * span), vw.reshape(G, 2 * span, dh)).reshape(B, H, dil, nb, span, dh)
    lse = lse.reshape(B, H, dil, nb, span)

    def back(t):
        t = t.reshape((B, H, dil, Lp) + t.shape[5:])[:, :, :, :L]
        return jnp.moveaxis(t, 2, 3).reshape((B, H, S) + t.shape[4:])

    return back(o), back(lse)


def dilated_mixture(q, k, v):
    outs, lses = [], []
    for window, dil in DIL_BRANCHES:
        o, lse = dilated_branch(q, k, v, window, dil)
        outs.append(o)
        lses.append(lse)
    wts = jax.nn.softmax(jnp.stack(lses, axis=0), axis=0)
    return jnp.sum(wts[..., None] * jnp.stack(outs, axis=0), axis=0)


def token_shift(t, mu):
    prev = jnp.pad(t, ((0, 0), (1, 0), (0, 0)))[:, :-1]
    return t + (prev - t) * mu


def rwkv7(r, k, v, w_lr, a_lr, g_lr, w0, w_decay2, a0, w_a2, w_gate2, k_k, k_a, r_k, gn_g, gn_b):
    B, S, _ = r.shape
    H, n = RWKV_HEADS, RWKV_HEAD_DIM
    w = -jax.nn.softplus(-(w0 + mm(jnp.tanh(w_lr), w_decay2))) - 0.5
    decay = jnp.exp(-jnp.exp(w))
    a = jax.nn.sigmoid(a0 + mm(a_lr, w_a2))
    g = mm(jax.nn.sigmoid(g_lr), w_gate2)
    kk = (k * k_k).reshape(B, S, H, n)
    kk = kk / jnp.maximum(jnp.sqrt(jnp.sum(kk * kk, axis=-1, keepdims=True)), 1e-12)
    kk = kk.reshape(B, S, H * n)
    kh = k * (1.0 + (a - 1.0) * k_a)
    y = rwkv_scan(r, decay, kh, v, kk, kk * a).reshape(B, S, H, n)
    y = layer_norm(y, jnp.ones((n,), F32), jnp.zeros((n,), F32), RWKV_GN_EPS).reshape(B, S, H * n) * gn_g + gn_b
    bonus = jnp.sum((r * kh).reshape(B, S, H, n) * r_k, axis=-1, keepdims=True) * v.reshape(B, S, H, n)
    y = y + bonus.reshape(B, S, H * n)
    return y * g


def odd_mixer(x, p):
    B, S, _ = x.shape
    cos, sin = rope_tables(S, DIL_HEAD_DIM)
    widths = (3 * DIL_WIDTH,) + RWKV_IN_WIDTHS
    h = mm(x, _pad_cols(p['od_w_in'][0], widths))
    c_in = h[..., :3 * DIL_WIDTH]
    d_in = h[..., 3 * DIL_WIDTH:]
    q, k, v = [t.reshape(B, S, DIL_HEADS, DIL_HEAD_DIM).transpose(0, 2, 1, 3) for t in jnp.split(c_in, 3, axis=-1)]
    q, k = apply_rope(q, cos, sin), apply_rope(k, cos, sin)
    c_out = dilated_mixture(q, k, v).transpose(0, 2, 1, 3).reshape(B, S, DIL_WIDTH)
    mu = _pad_cols(p['od_rwkv_mu'][0], RWKV_IN_WIDTHS)
    sh = token_shift(d_in, mu)
    at = [0]
    for n in RWKV_IN_WIDTHS:
        at.append(at[-1] + _pad_to(n))
    r, kd, vd = [sh[..., at[i]:at[i + 1]] for i in range(3)]
    w_lr, a_lr, g_lr = [sh[..., at[i]:at[i + 1]] for i in range(3, 6)]
    d_out = rwkv7(r, kd, vd, w_lr, a_lr, g_lr, p['od_rwkv_w0'][0], _pad_rows(p['od_rwkv_w_decay2'][0], w_lr.shape[-1]),
                  p['od_rwkv_a0'][0], _pad_rows(p['od_rwkv_w_a2'][0], a_lr.shape[-1]), p['od_rwkv_w_gate2'][0],
                  p['od_rwkv_k_k'][0], p['od_rwkv_k_a'][0], p['od_rwkv_r_k'][0], p['od_rwkv_gn_g'][0], p['od_rwkv_gn_b'][0])
    return mm(jnp.concatenate([c_out, d_out], axis=-1), p['od_w_out'][0])


def cross_attention(x, mem, w_q, w_k, w_v, w_o):
    B, S, D = x.shape
    M = mem.shape[1]
    hd = D // XA_HEADS

    def heads(t, n):
        return t.reshape(B, n, XA_HEADS, hd).transpose(0, 2, 1, 3).reshape(B * XA_HEADS, n, hd)

    q, k, v = heads(mm(x, w_q), S), heads(mm(mem, w_k), M), heads(mm(mem, w_v), M)
    o = _heads_attention(q, k, v, jnp.zeros((S, M), F32), hd ** -0.5)
    o = o.reshape(B, XA_HEADS, S, hd).transpose(0, 2, 1, 3).reshape(B, S, D)
    return mm(o, w_o)


def swiglu(x, w_gate, w_up, w_down):
    return mm(jax.nn.silu(mm(x, w_gate)) * mm(x, w_up), w_down)


def forward(p, x, mem):
    h = x
    for layer in range(DEPTH):
        mix = even_mixer(h, p) if layer % 2 == 0 else odd_mixer(h, p)
        h = layer_norm(DEEPNORM_ALPHA * h + mix, p['ln_mix_g'][layer], p['ln_mix_b'][layer])
        xa = cross_attention(h, mem, p['xa_w_q'][layer], p['xa_w_k'][layer], p['xa_w_v'][layer], p['xa_w_o'][layer])
        h = layer_norm(DEEPNORM_ALPHA * h + xa, p['ln_xa_g'][layer], p['ln_xa_b'][layer])
        ff = swiglu(h, p['ffn_w_gate'][layer], p['ffn_w_up'][layer], p['ffn_w_down'][layer])
        h = layer_norm(DEEPNORM_ALPHA * h + ff, p['ln_ffn_g'][layer], p['ln_ffn_b'][layer])
    return h


def _flat_pack(arrays, length, dtype):
    flat = jnp.concatenate([a.reshape(-1).astype(dtype) for a in arrays])
    return jnp.pad(flat, (0, length - flat.shape[0]))


def _unpack(flat, shapes):
    out, at = [], 0
    for shp in shapes:
        n = 1
        for d in shp:
            n *= d
        out.append(flat[at:at + n].reshape(shp))
        at += n
    return out


def _shard_of(full, axis, s):
    n = full.shape[axis] // 4
    return lax.slice_in_dim(full, s * n, (s + 1) * n, axis=axis)


def kernel(x, mem, ev_w_in, ev_mla_q_norm, ev_mla_w_uq, ev_mla_kv_norm, ev_mla_w_ukv, ev_gla_w_gate2, ev_gla_b_gate, ev_gla_norm_g, ev_gla_norm_b, ev_w_out, od_w_in, od_rwkv_mu, od_rwkv_w0, od_rwkv_w_decay2, od_rwkv_a0, od_rwkv_w_a2, od_rwkv_w_gate2, od_rwkv_k_k, od_rwkv_k_a, od_rwkv_r_k, od_rwkv_gn_g, od_rwkv_gn_b, od_w_out, ln_mix_g, ln_mix_b, xa_w_q, xa_w_k, xa_w_v, xa_w_o, ln_xa_g, ln_xa_b, ffn_w_gate, ffn_w_up, ffn_w_down, ln_ffn_g, ln_ffn_b, loss_target, m_ev_w_in, m_ev_mla_q_norm, m_ev_mla_w_uq, m_ev_mla_kv_norm, m_ev_mla_w_ukv, m_ev_gla_w_gate2, m_ev_gla_b_gate, m_ev_gla_norm_g, m_ev_gla_norm_b, m_ev_w_out, m_od_w_in, m_od_rwkv_mu, m_od_rwkv_w0, m_od_rwkv_w_decay2, m_od_rwkv_a0, m_od_rwkv_w_a2, m_od_rwkv_w_gate2, m_od_rwkv_k_k, m_od_rwkv_k_a, m_od_rwkv_r_k, m_od_rwkv_gn_g, m_od_rwkv_gn_b, m_od_w_out, m_ln_mix_g, m_ln_mix_b, m_xa_w_q, m_xa_w_k, m_xa_w_v, m_xa_w_o, m_ln_xa_g, m_ln_xa_b, m_ffn_w_gate, m_ffn_w_up, m_ffn_w_down, m_ln_ffn_g, m_ln_ffn_b, v_ev_w_in, v_ev_mla_q_norm, v_ev_mla_w_uq, v_ev_mla_kv_norm, v_ev_mla_w_ukv, v_ev_gla_w_gate2, v_ev_gla_b_gate, v_ev_gla_norm_g, v_ev_gla_norm_b, v_ev_w_out, v_od_w_in, v_od_rwkv_mu, v_od_rwkv_w0, v_od_rwkv_w_decay2, v_od_rwkv_a0, v_od_rwkv_w_a2, v_od_rwkv_w_gate2, v_od_rwkv_k_k, v_od_rwkv_k_a, v_od_rwkv_r_k, v_od_rwkv_gn_g, v_od_rwkv_gn_b, v_od_w_out, v_ln_mix_g, v_ln_mix_b, v_xa_w_q, v_xa_w_k, v_xa_w_v, v_xa_w_o, v_ln_xa_g, v_ln_xa_b, v_ffn_w_gate, v_ffn_w_up, v_ffn_w_down, v_ln_ffn_g, v_ln_ffn_b):
    given = dict(locals())
    W = {n: given[n] for n in WEIGHT_NAMES}
    Mo = {n: given['m_' + n] for n in WEIGHT_NAMES}
    Vo = {n: given['v_' + n] for n in WEIGHT_NAMES}
    mat_names = [n for n in WEIGHT_NAMES if n in MATRICES]
    vec_names = [n for n in WEIGHT_NAMES if n in SHARDED_VECTORS]
    rep_names = list(REPLICATED)
    mat_shapes = [W[n].shape for n in mat_names]
    vec_shapes = [W[n].shape for n in vec_names]
    rep_shapes = [W[n].shape for n in rep_names]

    def count(shapes):
        total = 0
        for shp in shapes:
            n = 1
            for d in shp:
                n *= d
            total += n
        return total

    rc = -(-count(mat_shapes) // (PACK_CHUNKS * 2 * PACK_COLS * LANES)) * LANES
    mat_len = PACK_CHUNKS * 2 * rc * PACK_COLS
    rv = -(-count(vec_shapes) // (2 * SMALL_COLS * 8)) * 8
    vec_len = 2 * rv * SMALL_COLS
    rr = -(-count(rep_shapes) // (SMALL_COLS * 8)) * 8
    rep_len = rr * SMALL_COLS

    wmat = _flat_pack([W[n] for n in mat_names], mat_len, BF16).reshape(PACK_CHUNKS, 2, rc, PACK_COLS)
    gathered = jnp.stack([_all_gather_call(wmat[i]) for i in range(PACK_CHUNKS)], axis=1)
    gvec = _all_gather_call(_flat_pack([W[n] for n in vec_names], vec_len, F32).reshape(2, rv, SMALL_COLS))
    full = {}
    mat_parts = [_unpack(gathered[s].reshape(-1), mat_shapes) for s in range(4)]
    for i, n in enumerate(mat_names):
        full[n] = jnp.concatenate([mat_parts[s][i] for s in range(4)], axis=MATRICES[n])
    vec_parts = [_unpack(gvec[s].reshape(-1), vec_shapes) for s in range(4)]
    for i, n in enumerate(vec_names):
        full[n] = jnp.concatenate([vec_parts[s][i] for s in range(4)], axis=SHARDED_VECTORS[n])
    for n in rep_names:
        full[n] = W[n]

    B, S, D = x.shape
    y, vjp = jax.vjp(lambda p, xx: forward(p, xx, mem), full, x)
    dy, part = _loss_call(y.reshape(B * S, D), loss_target.reshape(B * S, D))
    loss = lax.psum(0.5 * part, ('x', 'y', 'c'))
    gfull, grad_x = vjp(dy.reshape(B, S, D))

    gmat = jnp.stack([_flat_pack([_shard_of(gfull[n], MATRICES[n], s) for n in mat_names], mat_len, BF16)
                      for s in range(4)]).reshape(4, PACK_CHUNKS, 2, rc, PACK_COLS)
    halves = [_sum_parts_call(_scatter_call(gmat[:, i])) for i in range(PACK_CHUNKS)]
    gshard = jnp.stack([_sibling_exchange_call(h) for h in halves]).reshape(-1, PACK_COLS)
    grep = _flat_pack([gfull[n] for n in rep_names], rep_len, F32)
    gsmall = jnp.stack([jnp.concatenate([
        _flat_pack([_shard_of(gfull[n], SHARDED_VECTORS[n], s) for n in vec_names], vec_len, F32), grep])
        for s in range(4)]).reshape(4, 1, 2 * rv + rr, SMALL_COLS)
    gsmall = _sum_parts_call(_scatter_call(jnp.concatenate([gsmall, gsmall], axis=1)))

    def small_pack(src):
        return jnp.concatenate([_flat_pack([src[n] for n in vec_names], vec_len, F32),
                                _flat_pack([src[n] for n in rep_names], rep_len, F32)]).reshape(-1, SMALL_COLS)

    groups = [{}, {}, {}, {}]
    for n, g in zip(mat_names, _unpack(gshard.reshape(-1), mat_shapes)):
        rows = (-1, g.shape[-1])
        outs = _adamw_call(W[n].reshape(rows), g.reshape(rows), Mo[n].reshape(rows), Vo[n].reshape(rows))
        for grp, val in zip(groups, (g,) + outs):
            grp[n] = val.reshape(g.shape)
    small = (gsmall,) + _adamw_call(small_pack(W), gsmall, small_pack(Mo), small_pack(Vo))
    for grp, sm in zip(groups, small):
        sm = sm.reshape(-1)
        grp.update(zip(vec_names, _unpack(sm[:vec_len], vec_shapes)))
        grp.update(zip(rep_names, _unpack(sm[vec_len:], rep_shapes)))
    return (loss, grad_x, *[grp[n] for grp in groups for n in WEIGHT_NAMES])
```

```python
import functools

import jax
import jax.numpy as jnp
from jax import lax
from jax.experimental import pallas as pl
from jax.experimental.pallas import tpu as pltpu

F32 = jnp.float32
BF16 = jnp.bfloat16
MESH = pl.DeviceIdType.MESH

ROPE_THETA = 10000.0
LN_EPS = 1e-5
RMS_EPS = 1e-6
DEPTH = 2
DEEPNORM_ALPHA = (2.0 * DEPTH) ** 0.25
MLA_HEADS, MLA_NOPE, MLA_ROPE, MLA_V, MLA_Q_RANK, MLA_KV_RANK = 8, 128, 64, 128, 512, 256
GLA_HEADS, GLA_DK, GLA_DV, GLA_GATE_RANK, GLA_TAU, GLA_CHUNK = 4, 128, 256, 16, 16.0, 64
DIL_HEADS, DIL_HEAD_DIM = 8, 128
DIL_BRANCHES = ((128, 1), (512, 4), (2048, 16))
RWKV_HEADS, RWKV_HEAD_DIM = 16, 64
RWKV_DECAY_RANK, RWKV_A_RANK, RWKV_GATE_RANK = 96, 96, 256
RWKV_GN_EPS = 64e-5
XA_HEADS = 4
DIL_WIDTH = DIL_HEADS * DIL_HEAD_DIM
RWKV_WIDTH = RWKV_HEADS * RWKV_HEAD_DIM
EVEN_IN_WIDTHS = (MLA_Q_RANK, MLA_KV_RANK, MLA_ROPE, GLA_HEADS * GLA_DK, GLA_HEADS * GLA_DK,
                  GLA_HEADS * GLA_DV, GLA_HEADS * GLA_DV, GLA_GATE_RANK)
RWKV_IN_WIDTHS = (RWKV_WIDTH, RWKV_WIDTH, RWKV_WIDTH, RWKV_DECAY_RANK, RWKV_A_RANK, RWKV_GATE_RANK)
ADAM_LR, ADAM_B1, ADAM_B2, ADAM_EPS, ADAM_WD, ADAM_STEP = 0.001, 0.9, 0.999, 1e-08, 0.01, 10

LANES = 128
SUBLANES = 8
VMEM_LIMIT_BYTES = 48 * 1024 * 1024
NEG_BIG = -1e30

PACK_COLS = 1024
PACK_CHUNKS = 4
SMALL_COLS = 128

MATRICES = {
    'ev_w_in': 2, 'ev_mla_w_uq': 2, 'ev_mla_w_ukv': 2, 'ev_gla_w_gate2': 2, 'ev_w_out': 1, 'od_w_in': 2,
    'od_rwkv_w_decay2': 2, 'od_rwkv_w_a2': 2, 'od_rwkv_w_gate2': 2, 'od_w_out': 1,
    'xa_w_q': 1, 'xa_w_k': 1, 'xa_w_v': 1, 'xa_w_o': 1, 'ffn_w_gate': 2, 'ffn_w_up': 2, 'ffn_w_down': 1,
}
SHARDED_VECTORS = {
    'od_rwkv_mu': 1, 'od_rwkv_w0': 1, 'od_rwkv_a0': 1, 'od_rwkv_k_k': 1, 'od_rwkv_k_a': 1,
    'od_rwkv_gn_g': 1, 'od_rwkv_gn_b': 1,
}
REPLICATED = ('ev_mla_q_norm', 'ev_mla_kv_norm', 'ev_gla_b_gate', 'ev_gla_norm_g', 'ev_gla_norm_b', 'od_rwkv_r_k',
              'ln_mix_g', 'ln_mix_b', 'ln_xa_g', 'ln_xa_b', 'ln_ffn_g', 'ln_ffn_b')
WEIGHT_NAMES = ('ev_w_in', 'ev_mla_q_norm', 'ev_mla_w_uq', 'ev_mla_kv_norm', 'ev_mla_w_ukv', 'ev_gla_w_gate2',
                'ev_gla_b_gate', 'ev_gla_norm_g', 'ev_gla_norm_b', 'ev_w_out', 'od_w_in', 'od_rwkv_mu', 'od_rwkv_w0',
                'od_rwkv_w_decay2', 'od_rwkv_a0', 'od_rwkv_w_a2', 'od_rwkv_w_gate2', 'od_rwkv_k_k', 'od_rwkv_k_a',
                'od_rwkv_r_k', 'od_rwkv_gn_g', 'od_rwkv_gn_b', 'od_w_out', 'ln_mix_g', 'ln_mix_b', 'xa_w_q', 'xa_w_k',
                'xa_w_v', 'xa_w_o', 'ln_xa_g', 'ln_xa_b', 'ffn_w_gate', 'ffn_w_up', 'ffn_w_down', 'ln_ffn_g', 'ln_ffn_b')


def _pick(n, cap, mult):
    d = (min(cap, n) // mult) * mult
    while d >= mult:
        if n % d == 0:
            return d
        d -= mult
    return n


def _params(semantics):
    return pltpu.CompilerParams(dimension_semantics=semantics, vmem_limit_bytes=VMEM_LIMIT_BYTES)


_DIMS = {(False, False): (((1,), (0,)), ((), ())), (False, True): (((1,), (1,)), ((), ())),
         (True, False): (((0,), (0,)), ((), ()))}


def _bmm(a, b, ta, tb, out_dtype=F32):
    G = a.shape[0]
    K, M = (a.shape[1], a.shape[2]) if ta else (a.shape[2], a.shape[1])
    N = b.shape[1] if tb else b.shape[2]
    assert (b.shape[2] if tb else b.shape[1]) == K and b.shape[0] == G
    tm, tn = _pick(M, 1024, LANES), _pick(N, 512, LANES)
    tk = K if K <= 2048 else _pick(K, 2048, LANES)
    nk = K // tk
    gb = 1
    if tm == M and tn == N and nk == 1:
        per = 4 * (M * K + K * N + M * N)
        gb = _pick(G, max(1, min(8, (2 << 20) // per)), 1)
    dims = _DIMS[(ta, tb)]

    def body(a_ref, b_ref, o_ref, *scratch):
        def prod(i):
            return lax.dot_general(a_ref[i].astype(BF16), b_ref[i].astype(BF16), dims, preferred_element_type=F32)

        if nk == 1:
            for i in range(gb):
                o_ref[i] = prod(i).astype(o_ref.dtype)
        else:
            acc_ref, = scratch
            k = pl.program_id(3)

            @pl.when(k == 0)
            def _():
                acc_ref[...] = jnp.zeros_like(acc_ref)

            for i in range(gb):
                acc_ref[i] += prod(i)

            @pl.when(k == nk - 1)
            def _():
                o_ref[...] = acc_ref[...].astype(o_ref.dtype)

    a_spec = (pl.BlockSpec((gb, tk, tm), lambda g, i, j, k: (g, k, i)) if ta
              else pl.BlockSpec((gb, tm, tk), lambda g, i, j, k: (g, i, k)))
    b_spec = (pl.BlockSpec((gb, tn, tk), lambda g, i, j, k: (g, j, k)) if tb
              else pl.BlockSpec((gb, tk, tn), lambda g, i, j, k: (g, k, j)))
    return pl.pallas_call(
        body, name='bmm_' + ('t' if ta else 'n') + ('t' if tb else 'n'),
        out_shape=jax.ShapeDtypeStruct((G, M, N), out_dtype),
        grid=(G // gb, M // tm, N // tn, nk),
        in_specs=[a_spec, b_spec],
        out_specs=pl.BlockSpec((gb, tm, tn), lambda g, i, j, k: (g, i, j)),
        scratch_shapes=[] if nk == 1 else [pltpu.VMEM((gb, tm, tn), F32)],
        compiler_params=_params(('parallel', 'parallel', 'parallel', 'arbitrary')),
    )(a, b)


def _like(x):
    return jnp.zeros((), x.dtype)


@jax.custom_vjp
def bmm_nn(a, b):
    return _bmm(a, b, False, False)


def _bmm_nn_fwd(a, b):
    ab, bb = a.astype(BF16), b.astype(BF16)
    return _bmm(ab, bb, False, False), (ab, bb, _like(a), _like(b))


def _bmm_nn_bwd(res, g):
    a, b, la, lb = res
    g = g.astype(BF16)
    return _bmm(g, b, False, True, la.dtype), _bmm(a, g, True, False, lb.dtype)


bmm_nn.defvjp(_bmm_nn_fwd, _bmm_nn_bwd)


@jax.custom_vjp
def bmm_nt(a, b):
    return _bmm(a, b, False, True)


def _bmm_nt_fwd(a, b):
    ab, bb = a.astype(BF16), b.astype(BF16)
    return _bmm(ab, bb, False, True), (ab, bb, _like(a), _like(b))


def _bmm_nt_bwd(res, g):
    a, b, la, lb = res
    g = g.astype(BF16)
    return _bmm(g, b, False, False, la.dtype), _bmm(g, a, True, False, lb.dtype)


bmm_nt.defvjp(_bmm_nt_fwd, _bmm_nt_bwd)


def mm(x, w):
    lead = x.shape[:-1]
    out = bmm_nn(x.reshape(1, -1, x.shape[-1]), w[None])
    return out.reshape(lead + (w.shape[1],))


def _norm_stats(x, center, eps):
    if center:
        xc = x - jnp.mean(x, axis=-1, keepdims=True)
    else:
        xc = x
    rstd = lax.rsqrt(jnp.mean(xc * xc, axis=-1, keepdims=True) + eps)
    return xc * rstd, rstd


def _norm_fwd_call(x, g, b, center, eps):
    R, C = x.shape
    tr = _pick(R, max(8, (1 << 19) // C), 8)

    def body(x_ref, g_ref, b_ref, y_ref):
        xhat, _ = _norm_stats(x_ref[...], center, eps)
        y_ref[...] = xhat * g_ref[...] + b_ref[...]

    row = pl.BlockSpec((tr, C), lambda i: (i, 0))
    vec = pl.BlockSpec((1, C), lambda i: (0, 0))
    return pl.pallas_call(
        body, name='norm_fwd', out_shape=jax.ShapeDtypeStruct((R, C), F32), grid=(R // tr,),
        in_specs=[row, vec, vec], out_specs=row, compiler_params=_params(('parallel',)),
    )(x, g, b)


def _norm_bwd_call(x, g, dy, center, eps):
    R, C = x.shape
    tr = _pick(R, max(8, (1 << 19) // C), 8)

    def body(x_ref, g_ref, dy_ref, dx_ref, dg_ref, db_ref):
        @pl.when(pl.program_id(0) == 0)
        def _():
            dg_ref[...] = jnp.zeros_like(dg_ref)
            db_ref[...] = jnp.zeros_like(db_ref)

        xhat, rstd = _norm_stats(x_ref[...], center, eps)
        dy = dy_ref[...]
        dxh = dy * g_ref[...]
        proj = xhat * jnp.mean(dxh * xhat, axis=-1, keepdims=True)
        if center:
            dx_ref[...] = rstd * (dxh - jnp.mean(dxh, axis=-1, keepdims=True) - proj)
        else:
            dx_ref[...] = rstd * (dxh - proj)
        dg_ref[...] += jnp.sum(dy * xhat, axis=0, keepdims=True)
        db_ref[...] += jnp.sum(dy, axis=0, keepdims=True)

    row = pl.BlockSpec((tr, C), lambda i: (i, 0))
    vec = pl.BlockSpec((1, C), lambda i: (0, 0))
    return pl.pallas_call(
        body, name='norm_bwd',
        out_shape=(jax.ShapeDtypeStruct((R, C), F32), jax.ShapeDtypeStruct((1, C), F32), jax.ShapeDtypeStruct((1, C), F32)),
        grid=(R // tr,), in_specs=[row, vec, row], out_specs=(row, vec, vec), compiler_params=_params(('arbitrary',)),
    )(x, g, dy)


@functools.partial(jax.custom_vjp, nondiff_argnums=(3, 4))
def _norm2d(x, g, b, center, eps):
    return _norm_fwd_call(x, g, b, center, eps)


def _norm2d_fwd(x, g, b, center, eps):
    return _norm_fwd_call(x, g, b, center, eps), (x, g)


def _norm2d_bwd(center, eps, res, dy):
    x, g = res
    return _norm_bwd_call(x, g, dy, center, eps)


_norm2d.defvjp(_norm2d_fwd, _norm2d_bwd)


def layer_norm(x, g, b, eps=LN_EPS):
    C = x.shape[-1]
    return _norm2d(x.reshape(-1, C), g.reshape(1, C), b.reshape(1, C), True, eps).reshape(x.shape)


def rms_norm(x, g):
    C = x.shape[-1]
    return _norm2d(x.reshape(-1, C), g.reshape(1, C), jnp.zeros((1, C), F32), False, RMS_EPS).reshape(x.shape)


def _softmax_fwd_call(s, bias, scale):
    G1, G2, R, C = s.shape
    tr = _pick(R, max(8, (1 << 19) // C), 8)

    def body(s_ref, bias_ref, p_ref, lse_ref):
        z = s_ref[0, 0] * scale + bias_ref[0]
        m = jnp.max(z, axis=-1, keepdims=True)
        e = jnp.exp(z - m)
        den = jnp.sum(e, axis=-1, keepdims=True)
        p_ref[0, 0] = e / den
        lse_ref[0, 0] = m + jnp.log(den)

    blk = pl.BlockSpec((1, 1, tr, C), lambda a, b, r: (a, b, r, 0))
    col = pl.BlockSpec((1, 1, tr, 1), lambda a, b, r: (a, b, r, 0))
    return pl.pallas_call(
        body, name='softmax_fwd',
        out_shape=(jax.ShapeDtypeStruct(s.shape, F32), jax.ShapeDtypeStruct((G1, G2, R, 1), F32)),
        grid=(G1, G2, R // tr), in_specs=[blk, pl.BlockSpec((1, tr, C), lambda a, b, r: (b, r, 0))],
        out_specs=(blk, col), compiler_params=_params(('parallel', 'parallel', 'parallel')),
    )(s, bias)


def _softmax_bwd_call(p, dp, dlse, scale):
    G1, G2, R, C = p.shape
    tr = _pick(R, max(8, (1 << 19) // C), 8)

    def body(p_ref, dp_ref, dlse_ref, ds_ref):
        p = p_ref[0, 0]
        dp = dp_ref[0, 0]
        inner = jnp.sum(dp * p, axis=-1, keepdims=True)
        ds_ref[0, 0] = (p * (dp - inner + dlse_ref[0, 0])) * scale

    blk = pl.BlockSpec((1, 1, tr, C), lambda a, b, r: (a, b, r, 0))
    col = pl.BlockSpec((1, 1, tr, 1), lambda a, b, r: (a, b, r, 0))
    return pl.pallas_call(
        body, name='softmax_bwd', out_shape=jax.ShapeDtypeStruct(p.shape, F32),
        grid=(G1, G2, R // tr), in_specs=[blk, blk, col], out_specs=blk,
        compiler_params=_params(('parallel', 'parallel', 'parallel')),
    )(p, dp, dlse)


@functools.partial(jax.custom_vjp, nondiff_argnums=(2,))
def softmax_lse(s, bias, scale):
    return _softmax_fwd_call(s, bias, scale)


def _softmax_lse_fwd(s, bias, scale):
    p, lse = _softmax_fwd_call(s, bias, scale)
    return (p, lse), (p, bias)


def _softmax_lse_bwd(scale, res, cts):
    p, bias = res
    dp, dlse = cts
    return _softmax_bwd_call(p, dp, dlse, scale), jnp.zeros_like(bias)


softmax_lse.defvjp(_softmax_lse_fwd, _softmax_lse_bwd)


def _dot(a, b, dims):
    return lax.dot_general(a.astype(BF16), b.astype(BF16), dims, preferred_element_type=F32)


_NN, _NT, _TN = _DIMS[(False, False)], _DIMS[(False, True)], _DIMS[(True, False)]


def _gla_fwd_call(q, k, v, dec):
    G, nc, C, dk = q.shape
    dv = v.shape[-1]

    def body(q_ref, k_ref, v_ref, dec_ref, o_ref, st_ref, state):
        @pl.when(pl.program_id(1) == 0)
        def _():
            state[...] = jnp.zeros_like(state)

        s = state[...]
        st_ref[0, 0] = s
        o_ref[0, 0] = _dot(q_ref[0, 0], s, _NN)
        state[...] = s * dec_ref[0, 0] + _dot(k_ref[0, 0], v_ref[0, 0], _TN)

    def spec(r, c):
        return pl.BlockSpec((1, 1, r, c), lambda g, t: (g, t, 0, 0))

    return pl.pallas_call(
        body, name='gla_scan_fwd',
        out_shape=(jax.ShapeDtypeStruct((G, nc, C, dv), F32), jax.ShapeDtypeStruct((G, nc, dk, dv), F32)),
        grid=(G, nc), in_specs=[spec(C, dk), spec(C, dk), spec(C, dv), spec(dk, 1)],
        out_specs=(spec(C, dv), spec(dk, dv)), scratch_shapes=[pltpu.VMEM((dk, dv), F32)],
        compiler_params=_params(('parallel', 'arbitrary')),
    )(q, k, v, dec)


def _gla_bwd_call(q, k, v, dec, states, do):
    G, nc, C, dk = q.shape
    dv = v.shape[-1]

    def body(q_ref, k_ref, v_ref, dec_ref, st_ref, do_ref, dq_ref, dk_ref, dv_ref, ddec_ref, dstate):
        @pl.when(pl.program_id(1) == 0)
        def _():
            dstate[...] = jnp.zeros_like(dstate)

        s = st_ref[0, 0]
        d = dstate[...]
        do = do_ref[0, 0]
        dq_ref[0, 0] = _dot(do, s, _NT)
        dk_ref[0, 0] = _dot(v_ref[0, 0], d, _NT)
        dv_ref[0, 0] = _dot(k_ref[0, 0], d, _NN)
        ddec_ref[0, 0] = jnp.sum(s * d, axis=1, keepdims=True)
        dstate[...] = d * dec_ref[0, 0] + _dot(q_ref[0, 0], do, _TN)

    def spec(r, c):
        return pl.BlockSpec((1, 1, r, c), lambda g, t: (g, nc - 1 - t, 0, 0))

    return pl.pallas_call(
        body, name='gla_scan_bwd',
        out_shape=(jax.ShapeDtypeStruct(q.shape, F32), jax.ShapeDtypeStruct(k.shape, F32),
                   jax.ShapeDtypeStruct(v.shape, F32), jax.ShapeDtypeStruct(dec.shape, F32)),
        grid=(G, nc), in_specs=[spec(C, dk), spec(C, dk), spec(C, dv), spec(dk, 1), spec(dk, dv), spec(C, dv)],
        out_specs=(spec(C, dk), spec(C, dk), spec(C, dv), spec(dk, 1)), scratch_shapes=[pltpu.VMEM((dk, dv), F32)],
        compiler_params=_params(('parallel', 'arbitrary')),
    )(q, k, v, dec, states, do)


@jax.custom_vjp
def gla_scan(q, k, v, dec):
    return _gla_fwd_call(q, k, v, dec)[0]


def _gla_scan_fwd(q, k, v, dec):
    o, states = _gla_fwd_call(q, k, v, dec)
    return o, (q, k, v, dec, states)


def _gla_scan_bwd(res, do):
    return _gla_bwd_call(*res, do)


gla_scan.defvjp(_gla_scan_fwd, _gla_scan_bwd)


RWKV_PAIRS_PER_STEP = 4
RWKV_TIME_BLOCK = 64
RN = RWKV_HEAD_DIM


def _rwkv_consts():
    row = lax.broadcasted_iota(jnp.int32, (RN, LANES), 0)
    lane = lax.broadcasted_iota(jnp.int32, (RN, LANES), 1)
    diag = (lane % RN == row).astype(F32)
    r2 = lax.broadcasted_iota(jnp.int32, (LANES, LANES), 0)
    l2 = lax.broadcasted_iota(jnp.int32, (LANES, LANES), 1)
    seg = (r2 // RN == l2 // RN).astype(BF16)
    return diag, seg


def _stage(lhs_ref, slot, p):
    hi = p.astype(BF16)
    lhs_ref[pl.ds(slot * LANES, RN), :] = hi
    lhs_ref[pl.ds(slot * LANES + RN, RN), :] = (p - hi.astype(F32)).astype(BF16)


def _seg_sums(lhs_ref, nslots, seg):
    res = jnp.dot(lhs_ref[pl.ds(0, nslots * LANES), :], seg, preferred_element_type=F32)
    return [res[i * LANES:i * LANES + RN] + res[i * LANES + RN:(i + 1) * LANES] for i in range(nslots)]


def _rwkv_blocks(B, S, C):
    npairs = C // LANES
    pp = RWKV_PAIRS_PER_STEP if npairs % RWKV_PAIRS_PER_STEP == 0 else 1
    T = _pick(S, RWKV_TIME_BLOCK, 8)
    return npairs, pp, T


def _rwkv_fwd_call(r, w, k, v, kk, b):
    B, S, C = r.shape
    npairs, pp, T = _rwkv_blocks(B, S, C)
    G = SUBLANES

    def body(r_ref, w_ref, k_ref, v_ref, kk_ref, b_ref, y_ref, sall_ref, state, step_lhs, v_lhs, y_lhs):
        @pl.when(pl.program_id(2) == 0)
        def _():
            state[...] = jnp.zeros_like(state)

        diag, seg = _rwkv_consts()
        rowid = lax.broadcasted_iota(jnp.int32, (SUBLANES, LANES), 0)

        def group(t8, carry):
            rows = pl.ds(pl.multiple_of(t8 * G, G), G)
            sls = [slice(p * LANES, (p + 1) * LANES) for p in range(pp)]
            ops = [[ref[0, rows, sl] for ref in (r_ref, w_ref, k_ref, v_ref, kk_ref, b_ref)] for sl in sls]
            for j in range(G):
                for p in range(pp):
                    _stage(v_lhs, j * pp + p, diag * ops[p][3][j:j + 1])
            vcols = _seg_sums(v_lhs, G * pp, seg)
            s = list(carry)
            for j in range(G):
                for p in range(pp):
                    sall_ref[0, p, t8 * G + j] = s[p]
                    _stage(step_lhs, p, s[p] * ops[p][4][j:j + 1])
                sas = _seg_sums(step_lhs, pp, seg)
                for p in range(pp):
                    rt, wt, kt, _, _, bt = ops[p]
                    s[p] = s[p] * wt[j:j + 1] - sas[p] * bt[j:j + 1] + vcols[j * pp + p] * kt[j:j + 1]
                    _stage(y_lhs, j * pp + p, s[p] * rt[j:j + 1])
            ycols = _seg_sums(y_lhs, G * pp, seg)
            for p in range(pp):
                ytile = jnp.zeros((SUBLANES, LANES), F32)
                for j in range(G):
                    ytile = jnp.where(rowid == j, jnp.sum(diag * ycols[j * pp + p], axis=0, keepdims=True), ytile)
                y_ref[0, rows, sls[p]] = ytile
            return tuple(s)

        final = lax.fori_loop(0, T // G, group, tuple(state[p] for p in range(pp)))
        for p in range(pp):
            state[p] = final[p]

    seq = pl.BlockSpec((1, T, pp * LANES), lambda bi, g, t: (bi, t, g))
    return pl.pallas_call(
        body, name='rwkv_scan_fwd',
        out_shape=(jax.ShapeDtypeStruct((B, S, C), F32), jax.ShapeDtypeStruct((B, npairs, S, RN, LANES), F32)),
        grid=(B, npairs // pp, S // T), in_specs=[seq] * 6,
        out_specs=(seq, pl.BlockSpec((1, pp, T, RN, LANES), lambda bi, g, t: (bi, g, t, 0, 0))),
        scratch_shapes=[pltpu.VMEM((pp, RN, LANES), F32), pltpu.VMEM((pp * LANES, LANES), BF16),
                        pltpu.VMEM((G * pp * LANES, LANES), BF16), pltpu.VMEM((G * pp * LANES, LANES), BF16)],
        compiler_params=_params(('parallel', 'parallel', 'arbitrary')),
    )(r, w, k, v, kk, b)


def _rwkv_bwd_call(r, w, k, v, kk, b, sall, dy):
    B, S, C = r.shape
    npairs, pp, T = _rwkv_blocks(B, S, C)
    nt = S // T
    G = SUBLANES

    def body(r_ref, w_ref, k_ref, v_ref, kk_ref, b_ref, sall_ref, dy_ref,
             dr_ref, dw_ref, dk_ref, dv_ref, dkk_ref, db_ref, dstate, step_lhs, pre_lhs, dv_lhs):
        @pl.when(pl.program_id(2) == 0)
        def _():
            dstate[...] = jnp.zeros_like(dstate)

        diag, seg = _rwkv_consts()
        rowid = lax.broadcasted_iota(jnp.int32, (SUBLANES, LANES), 0)

        def colsum(z):
            return jnp.sum(z, axis=0, keepdims=True)

        def group(i, carry):
            t8 = T // G - 1 - i
            rows = pl.ds(pl.multiple_of(t8 * G, G), G)
            sls = [slice(p * LANES, (p + 1) * LANES) for p in range(pp)]
            ops = [[ref[0, rows, sl] for ref in (r_ref, w_ref, k_ref, v_ref, kk_ref, b_ref, dy_ref)] for sl in sls]
            for j in range(G):
                for p in range(pp):
                    _stage(pre_lhs, j * pp + p, sall_ref[0, p, t8 * G + j] * ops[p][4][j:j + 1])
                    _stage(pre_lhs, (G + j) * pp + p, diag * ops[p][3][j:j + 1])
                    _stage(pre_lhs, (2 * G + j) * pp + p, diag * ops[p][6][j:j + 1])
            pre = _seg_sums(pre_lhs, 3 * G * pp, seg)
            ds = list(carry)
            tiles = [[jnp.zeros((SUBLANES, LANES), F32) for _ in range(5)] for _ in range(pp)]
            for j in reversed(range(G)):
                d = []
                for p in range(pp):
                    rt, _, kt, _, _, bt, _ = ops[p]
                    d.append(ds[p] + pre[(2 * G + j) * pp + p] * rt[j:j + 1])
                    _stage(step_lhs, p, d[p] * bt[j:j + 1])
                    _stage(dv_lhs, j * pp + p, d[p] * kt[j:j + 1])
                dsas = _seg_sums(step_lhs, pp, seg)
                for p in range(pp):
                    rt, wt, kt, _, kkt, bt, _ = ops[p]
                    s = sall_ref[0, p, t8 * G + j]
                    sa, vcol, dycol = -pre[j * pp + p], pre[(G + j) * pp + p], pre[(2 * G + j) * pp + p]
                    s2 = s * wt[j:j + 1] + sa * bt[j:j + 1] + vcol * kt[j:j + 1]
                    vals = (colsum(s2 * dycol), colsum(d[p] * s), colsum(d[p] * vcol), -colsum(s * dsas[p]),
                            colsum(d[p] * sa))
                    tiles[p] = [jnp.where(rowid == j, val, tile) for val, tile in zip(vals, tiles[p])]
                    ds[p] = d[p] * wt[j:j + 1] - dsas[p] * kkt[j:j + 1]
            dvcols = _seg_sums(dv_lhs, G * pp, seg)
            for p in range(pp):
                dvt = jnp.zeros((SUBLANES, LANES), F32)
                for j in range(G):
                    dvt = jnp.where(rowid == j, colsum(diag * dvcols[j * pp + p]), dvt)
                dv_ref[0, rows, sls[p]] = dvt
                for ref, tile in zip((dr_ref, dw_ref, dk_ref, dkk_ref, db_ref), tiles[p]):
                    ref[0, rows, sls[p]] = tile
            return tuple(ds)

        final = lax.fori_loop(0, T // G, group, tuple(dstate[p] for p in range(pp)))
        for p in range(pp):
            dstate[p] = final[p]

    seq = pl.BlockSpec((1, T, pp * LANES), lambda bi, g, t: (bi, nt - 1 - t, g))
    sds = jax.ShapeDtypeStruct((B, S, C), F32)
    return pl.pallas_call(
        body, name='rwkv_scan_bwd', out_shape=(sds,) * 6,
        grid=(B, npairs // pp, nt),
        in_specs=[seq] * 6 + [pl.BlockSpec((1, pp, T, RN, LANES), lambda bi, g, t: (bi, g, nt - 1 - t, 0, 0)), seq],
        out_specs=(seq,) * 6,
        scratch_shapes=[pltpu.VMEM((pp, RN, LANES), F32), pltpu.VMEM((pp * LANES, LANES), BF16),
                        pltpu.VMEM((3 * G * pp * LANES, LANES), BF16), pltpu.VMEM((G * pp * LANES, LANES), BF16)],
        compiler_params=_params(('parallel', 'parallel', 'arbitrary')),
    )(r, w, k, v, kk, b, sall, dy)


@jax.custom_vjp
def rwkv_scan(r, w, k, v, kk, b):
    return _rwkv_fwd_call(r, w, k, v, kk, b)[0]


def _rwkv_scan_fwd(r, w, k, v, kk, b):
    y, sall = _rwkv_fwd_call(r, w, k, v, kk, b)
    return y, (r, w, k, v, kk, b, sall)


def _rwkv_scan_bwd(res, dy):
    return _rwkv_bwd_call(*res, dy)


rwkv_scan.defvjp(_rwkv_scan_fwd, _rwkv_scan_bwd)


def _loss_call(y, target):
    R, D = y.shape
    tr = _pick(R, max(8, (1 << 19) // D), 8)

    def body(y_ref, t_ref, dy_ref, part_ref):
        @pl.when(pl.program_id(0) == 0)
        def _():
            part_ref[...] = jnp.zeros_like(part_ref)

        diff = y_ref[...] - t_ref[...]
        dy_ref[...] = diff / D
        part_ref[...] += jnp.sum(jnp.mean(diff * diff, axis=-1, keepdims=True), axis=0, keepdims=True)

    row = pl.BlockSpec((tr, D), lambda i: (i, 0))
    dy, part = pl.pallas_call(
        body, name='loss_head',
        out_shape=(jax.ShapeDtypeStruct((R, D), F32), jax.ShapeDtypeStruct((1, 1), F32)),
        grid=(R // tr,), in_specs=[row, row], out_specs=(row, pl.BlockSpec((1, 1), lambda i: (0, 0))),
        compiler_params=_params(('arbitrary',)),
    )(y, target)
    return dy, part[0, 0]


def _sum_parts_call(parts):
    P, R, C = parts.shape
    tr = _pick(R, 512, 16)

    def body(p_ref, o_ref):
        acc = p_ref[0].astype(F32)
        for i in range(1, P):
            acc = acc + p_ref[i].astype(F32)
        o_ref[...] = acc

    return pl.pallas_call(
        body, name='sum_parts', out_shape=jax.ShapeDtypeStruct((R, C), F32), grid=(R // tr,),
        in_specs=[pl.BlockSpec((P, tr, C), lambda i: (0, i, 0))], out_specs=pl.BlockSpec((tr, C), lambda i: (i, 0)),
        compiler_params=_params(('parallel',)),
    )(parts)


def _adamw_call(w, g, m, v):
    R, C = w.shape
    tr = _pick(R, max(8, (1 << 18) // C), 8)

    def body(w_ref, g_ref, m_ref, v_ref, d_ref, nm_ref, nv_ref):
        g = g_ref[...]
        m = ADAM_B1 * m_ref[...] + (1.0 - ADAM_B1) * g
        v = ADAM_B2 * v_ref[...] + (1.0 - ADAM_B2) * (g * g)
        m_hat = m / (1.0 - ADAM_B1 ** ADAM_STEP)
        v_hat = v / (1.0 - ADAM_B2 ** ADAM_STEP)
        d_ref[...] = -ADAM_LR * (m_hat / (jnp.sqrt(v_hat) + ADAM_EPS) + ADAM_WD * w_ref[...])
        nm_ref[...] = m
        nv_ref[...] = v

    row = pl.BlockSpec((tr, C), lambda i: (i, 0))
    sds = jax.ShapeDtypeStruct((R, C), F32)
    return pl.pallas_call(
        body, name='adamw', out_shape=(sds, sds, sds), grid=(R // tr,),
        in_specs=[row] * 4, out_specs=(row,) * 3, compiler_params=_params(('parallel',)),
    )(w, g, m, v)


ANY = pl.BlockSpec(memory_space=pl.ANY)


def _place():
    return lax.axis_index('x'), lax.axis_index('y'), lax.axis_index('c')


COPY_SPLIT = 8


def _row_split(rows):
    if rows % (COPY_SPLIT * 16) == 0:
        return COPY_SPLIT, rows // COPY_SPLIT
    return 1, rows


def _all_gather_call(pack):
    _, R, C = pack.shape
    ns, rs = _row_split(R)

    def body(pk_ref, out_ref, send_sems, recv_sems, local_sem):
        x, y, c = _place()
        chips = [(1 - x, y), (x, 1 - y), (1 - x, 1 - y)]
        me = 2 * x + y
        mine = pltpu.make_async_copy(pk_ref, out_ref.at[me], local_sem)
        mine.start()

        def copy(k, i, src, dst, to):
            rows = pl.ds(i * rs, rs)
            return pltpu.make_async_remote_copy(src_ref=src.at[rows], dst_ref=dst.at[rows], send_sem=send_sems.at[k * ns + i],
                                                recv_sem=recv_sems.at[k * ns + i], device_id=to, device_id_type=MESH)

        first = [copy(j, i, pk_ref.at[c], out_ref.at[me, c], (px, py, c))
                 for j, (px, py) in enumerate(chips) for i in range(ns)]
        for cp in first:
            cp.start()
        passed = []
        for i in range(ns):
            for j, (px, py) in enumerate(chips):
                landed = out_ref.at[2 * px + py, c]
                copy(j, i, landed, landed, (px, py, c)).wait_recv()
                fwd = copy(3 + j, i, landed, landed, (x, y, 1 - c))
                fwd.start()
                passed.append(fwd)
        for i in range(ns):
            for j, (px, py) in enumerate(chips):
                other = out_ref.at[2 * px + py, 1 - c]
                copy(3 + j, i, other, other, (x, y, 1 - c)).wait_recv()
        for cp in first + passed:
            cp.wait_send()
        mine.wait()

    return pl.pallas_call(
        body, name='all_gather', out_shape=jax.ShapeDtypeStruct((4, 2, R, C), pack.dtype),
        in_specs=[ANY], out_specs=ANY,
        scratch_shapes=[pltpu.SemaphoreType.DMA((6 * ns,)), pltpu.SemaphoreType.DMA((6 * ns,)), pltpu.SemaphoreType.DMA],
    )(pack)


def _scatter_call(src):
    _, _, R, C = src.shape
    ns, rs = _row_split(R)

    def body(src_ref, out_ref, send_sems, recv_sems, local_sem):
        x, y, c = _place()
        me = 4 * x + 2 * y + c
        own = pltpu.make_async_copy(src_ref.at[2 * x + y, c], out_ref.at[me], local_sem)
        own.start()
        peers = []
        for rel in range(1, 8):
            px = 1 - x if rel & 4 else x
            py = 1 - y if rel & 2 else y
            pc = 1 - c if rel & 1 else c
            peers.append((rel - 1, px, py, pc))

        def copy(k, i, src, dst, to):
            rows = pl.ds(i * rs, rs)
            return pltpu.make_async_remote_copy(src_ref=src.at[rows], dst_ref=dst.at[rows], send_sem=send_sems.at[k * ns + i],
                                                recv_sem=recv_sems.at[k * ns + i], device_id=to, device_id_type=MESH)

        sends = [copy(k, i, src_ref.at[2 * px + py, pc], out_ref.at[me], (px, py, pc))
                 for i in range(ns) for k, px, py, pc in peers]
        for cp in sends:
            cp.start()
        for i in range(ns):
            for k, px, py, pc in peers:
                slot = out_ref.at[4 * px + 2 * py + pc]
                copy(k, i, slot, slot, (px, py, pc)).wait_recv()
        for cp in sends:
            cp.wait_send()
        own.wait()

    return pl.pallas_call(
        body, name='scatter_parts', out_shape=jax.ShapeDtypeStruct((8, R, C), src.dtype),
        in_specs=[ANY], out_specs=ANY,
        scratch_shapes=[pltpu.SemaphoreType.DMA((7 * ns,)), pltpu.SemaphoreType.DMA((7 * ns,)), pltpu.SemaphoreType.DMA],
    )(src)


def _sibling_exchange_call(half):
    R, C = half.shape
    ns, rs = _row_split(R)

    def body(h_ref, out_ref, send_sems, recv_sems, local_sem):
        x, y, c = _place()
        own = pltpu.make_async_copy(h_ref, out_ref.at[c], local_sem)
        own.start()

        def copy(i, src, dst):
            rows = pl.ds(i * rs, rs)
            return pltpu.make_async_remote_copy(src_ref=src.at[rows], dst_ref=dst.at[rows], send_sem=send_sems.at[i],
                                                recv_sem=recv_sems.at[i], device_id=(x, y, 1 - c), device_id_type=MESH)

        sends = [copy(i, h_ref, out_ref.at[c]) for i in range(ns)]
        for cp in sends:
            cp.start()
        other = out_ref.at[1 - c]
        for i in range(ns):
            copy(i, other, other).wait_recv()
        for cp in sends:
            cp.wait_send()
        own.wait()

    return pl.pallas_call(
        body, name='sibling_exchange', out_shape=jax.ShapeDtypeStruct((2, R, C), half.dtype),
        in_specs=[ANY], out_specs=ANY,
        scratch_shapes=[pltpu.SemaphoreType.DMA((ns,)), pltpu.SemaphoreType.DMA((ns,)), pltpu.SemaphoreType.DMA],
    )(half)


def rope_tables(seq_len, dim):
    inv = ROPE_THETA ** (-jnp.arange(0, dim, 2, dtype=F32) / dim)
    ang = jnp.arange(seq_len, dtype=F32)[:, None] * inv[None, :]
    return jnp.cos(ang), jnp.sin(ang)


def apply_rope(x, cos, sin):
    x1, x2 = jnp.split(x, 2, axis=-1)
    return jnp.concatenate([x1 * cos - x2 * sin, x1 * sin + x2 * cos], axis=-1)


def _pad_to(n):
    return -(-n // LANES) * LANES


def _pad_cols(w, widths):
    parts, at = [], 0
    for n in widths:
        parts.append(jnp.pad(w[..., at:at + n], [(0, 0)] * (w.ndim - 1) + [(0, _pad_to(n) - n)]))
        at += n
    return jnp.concatenate(parts, axis=-1)


def _split_padded(t, widths):
    out, at = [], 0
    for n in widths:
        out.append(t[..., at:at + n])
        at += _pad_to(n)
    return out


def _pad_rows(w, rows):
    return jnp.pad(w, ((0, rows - w.shape[0]), (0, 0)))


def _heads_attention(q, k, v, bias, scale):
    s = bmm_nt(q, k)
    p, _ = softmax_lse(s[:, None], bias[None], scale)
    return bmm_nn(p[:, 0], v)


def gla(q, k, v, r, gate_lr, w_gate2, b_gate, norm_g, norm_b):
    B, S, _ = q.shape
    H, dk, dv, C = GLA_HEADS, GLA_DK, GLA_DV, GLA_CHUNK
    nc = S // C
    log_a = jax.nn.log_sigmoid(mm(gate_lr, w_gate2) + b_gate) / GLA_TAU

    def chunks(t, d):
        return t.reshape(B, nc, C, H, d).transpose(0, 3, 1, 2, 4)

    qc = chunks(q, dk) * (dk ** -0.5)
    kc = chunks(k, dk)
    vc = chunks(v, dv)
    b = jnp.cumsum(chunks(log_a, dk), axis=3)
    b_last = b[:, :, :, -1:, :]
    q_dec = qc * jnp.exp(b)
    k_inv = kc * jnp.exp(-b)
    k_end = kc * jnp.exp(b_last - b)
    causal = jnp.tril(jnp.ones((C, C), dtype=bool))
    G = B * H
    att = bmm_nt(q_dec.reshape(G * nc, C, dk), k_inv.reshape(G * nc, C, dk))
    att = jnp.where(causal, att, 0.0)
    o_intra = bmm_nn(att, vc.reshape(G * nc, C, dv)).reshape(B, H, nc, C, dv)
    dec = jnp.exp(b_last[:, :, :, 0, :]).reshape(G, nc, dk, 1)
    o_inter = gla_scan(q_dec.reshape(G, nc, C, dk), k_end.reshape(G, nc, C, dk), vc.reshape(G, nc, C, dv), dec)
    o = o_intra + o_inter.reshape(B, H, nc, C, dv)
    o = o.transpose(0, 2, 3, 1, 4).reshape(B, S, H, dv)
    o = layer_norm(o, norm_g, norm_b).reshape(B, S, H * dv)
    return o * jax.nn.silu(r)


def even_mixer(x, p):
    B, S, _ = x.shape
    H = MLA_HEADS
    cos, sin = rope_tables(S, MLA_ROPE)
    z = mm(x, _pad_cols(p['ev_w_in'][0], EVEN_IN_WIDTHS))
    c_q, c_kv, k_pe, q_g, k_g, v_g, r_g, _ = _split_padded(z, EVEN_IN_WIDTHS)
    lr_at = sum(_pad_to(n) for n in EVEN_IN_WIDTHS[:-1])
    lr_g = z[..., lr_at:]
    q = mm(rms_norm(c_q, p['ev_mla_q_norm'][0]), p['ev_mla_w_uq'][0])
    q = q.reshape(B, S, H, MLA_NOPE + MLA_ROPE).transpose(0, 2, 1, 3)
    kv = mm(rms_norm(c_kv, p['ev_mla_kv_norm'][0]), p['ev_mla_w_ukv'][0])
    kv = kv.reshape(B, S, H, MLA_NOPE + MLA_V).transpose(0, 2, 1, 3)
    q_pe = apply_rope(q[..., MLA_NOPE:], cos, sin)
    k_pe = jnp.broadcast_to(apply_rope(k_pe[:, None], cos, sin), (B, H, S, MLA_ROPE))
    qf = jnp.concatenate([q[..., :MLA_NOPE], q_pe], axis=-1)
    kf = jnp.concatenate([kv[..., :MLA_NOPE], k_pe], axis=-1)
    pos = jnp.arange(S)
    bias = jnp.where(pos[None, :] <= pos[:, None], 0.0, NEG_BIG).astype(F32)
    a_out = _heads_attention(qf.reshape(B * H, S, -1), kf.reshape(B * H, S, -1),
                             kv[..., MLA_NOPE:].reshape(B * H, S, MLA_V), bias, (MLA_NOPE + MLA_ROPE) ** -0.5)
    a_out = a_out.reshape(B, H, S, MLA_V).transpose(0, 2, 1, 3).reshape(B, S, H * MLA_V)
    w_gate2 = _pad_rows(p['ev_gla_w_gate2'][0], lr_g.shape[-1])
    b_out = gla(q_g, k_g, v_g, r_g, lr_g, w_gate2, p['ev_gla_b_gate'][0], p['ev_gla_norm_g'][0], p['ev_gla_norm_b'][0])
    return mm(jnp.concatenate([a_out, b_out], axis=-1), p['ev_w_out'][0])


def dilated_branch(q, k, v, window, dil):
    B, H, S, dh = q.shape
    span = window // dil
    L = S // dil
    nb = -(-L // span)
    Lp = nb * span

    def residues(t):
        t = t.reshape(B, H, L, dil, dh).transpose(0, 1, 3, 2, 4)
        t = jnp.pad(t, ((0, 0), (0, 0), (0, 0), (0, Lp - L), (0, 0)))
        return t.reshape(B, H, dil, nb, span, dh)

    def with_prev(t):
        prev = jnp.pad(t, ((0, 0), (0, 0), (0, 0), (1, 0), (0, 0), (0, 0)))[:, :, :, :-1]
        return jnp.concatenate([prev, t], axis=4)

    qb = residues(q)
    kw, vw = with_prev(residues(k)), with_prev(residues(v))
    G = B * H * dil * nb
    s = bmm_nt(qb.reshape(G, span, dh), kw.reshape(G, 2 * span, dh))
    qi = jnp.arange(span)[:, None] + span
    kj = jnp.arange(2 * span)[None, :]
    dist = qi - kj
    in_band = (dist >= 0) & (dist <= span)
    has_prev = (jnp.arange(nb) > 0)[:, None, None] | (kj >= span)[None]
    valid = in_band[None] & has_prev
    bias = jnp.where(valid, 0.0, NEG_BIG).astype(F32)
    p, lse = softmax_lse(s.reshape(B * H * dil, nb, span, 2 * span), bias, dh ** -0.5)
    o = bmm_nn(p.reshape(G, span, 2 * span), vw.reshape(G, 2 * span, dh)).reshape(B, H, dil, nb, span, dh)
    lse = lse.reshape(B, H, dil, nb, span)

    def back(t):
        t = t.reshape((B, H, dil, Lp) + t.shape[5:])[:, :, :, :L]
        return jnp.moveaxis(t, 2, 3).reshape((B, H, S) + t.shape[4:])

    return back(o), back(lse)


def dilated_mixture(q, k, v):
    outs, lses = [], []
    for window, dil in DIL_BRANCHES:
        o, lse = dilated_branch(q, k, v, window, dil)
        outs.append(o)
        lses.append(lse)
    wts = jax.nn.softmax(jnp.stack(lses, axis=0), axis=0)
    return jnp.sum(wts[..., None] * jnp.stack(outs, axis=0), axis=0)


def token_shift(t, mu):
    prev = jnp.pad(t, ((0, 0), (1, 0), (0, 0)))[:, :-1]
    return t + (prev - t) * mu


def rwkv7(r, k, v, w_lr, a_lr, g_lr, w0, w_decay2, a0, w_a2, w_gate2, k_k, k_a, r_k, gn_g, gn_b):
    B, S, _ = r.shape
    H, n = RWKV_HEADS, RWKV_HEAD_DIM
    w = -jax.nn.softplus(-(w0 + mm(jnp.tanh(w_lr), w_decay2))) - 0.5
    decay = jnp.exp(-jnp.exp(w))
    a = jax.nn.sigmoid(a0 + mm(a_lr, w_a2))
    g = mm(jax.nn.sigmoid(g_lr), w_gate2)
    kk = (k * k_k).reshape(B, S, H, n)
    kk = kk / jnp.maximum(jnp.sqrt(jnp.sum(kk * kk, axis=-1, keepdims=True)), 1e-12)
    kk = kk.reshape(B, S, H * n)
    kh = k * (1.0 + (a - 1.0) * k_a)
    y = rwkv_scan(r, decay, kh, v, kk, kk * a).reshape(B, S, H, n)
    y = layer_norm(y, jnp.ones((n,), F32), jnp.zeros((n,), F32), RWKV_GN_EPS).reshape(B, S, H * n) * gn_g + gn_b
    bonus = jnp.sum((r * kh).reshape(B, S, H, n) * r_k, axis=-1, keepdims=True) * v.reshape(B, S, H, n)
    y = y + bonus.reshape(B, S, H * n)
    return y * g


def odd_mixer(x, p):
    B, S, _ = x.shape
    cos, sin = rope_tables(S, DIL_HEAD_DIM)
    widths = (3 * DIL_WIDTH,) + RWKV_IN_WIDTHS
    h = mm(x, _pad_cols(p['od_w_in'][0], widths))
    c_in = h[..., :3 * DIL_WIDTH]
    d_in = h[..., 3 * DIL_WIDTH:]
    q, k, v = [t.reshape(B, S, DIL_HEADS, DIL_HEAD_DIM).transpose(0, 2, 1, 3) for t in jnp.split(c_in, 3, axis=-1)]
    q, k = apply_rope(q, cos, sin), apply_rope(k, cos, sin)
    c_out = dilated_mixture(q, k, v).transpose(0, 2, 1, 3).reshape(B, S, DIL_WIDTH)
    mu = _pad_cols(p['od_rwkv_mu'][0], RWKV_IN_WIDTHS)
    sh = token_shift(d_in, mu)
    at = [0]
    for n in RWKV_IN_WIDTHS:
        at.append(at[-1] + _pad_to(n))
    r, kd, vd = [sh[..., at[i]:at[i + 1]] for i in range(3)]
    w_lr, a_lr, g_lr = [sh[..., at[i]:at[i + 1]] for i in range(3, 6)]
    d_out = rwkv7(r, kd, vd, w_lr, a_lr, g_lr, p['od_rwkv_w0'][0], _pad_rows(p['od_rwkv_w_decay2'][0], w_lr.shape[-1]),
                  p['od_rwkv_a0'][0], _pad_rows(p['od_rwkv_w_a2'][0], a_lr.shape[-1]), p['od_rwkv_w_gate2'][0],
                  p['od_rwkv_k_k'][0], p['od_rwkv_k_a'][0], p['od_rwkv_r_k'][0], p['od_rwkv_gn_g'][0], p['od_rwkv_gn_b'][0])
    return mm(jnp.concatenate([c_out, d_out], axis=-1), p['od_w_out'][0])


def cross_attention(x, mem, w_q, w_k, w_v, w_o):
    B, S, D = x.shape
    M = mem.shape[1]
    hd = D // XA_HEADS

    def heads(t, n):
        return t.reshape(B, n, XA_HEADS, hd).transpose(0, 2, 1, 3).reshape(B * XA_HEADS, n, hd)

    q, k, v = heads(mm(x, w_q), S), heads(mm(mem, w_k), M), heads(mm(mem, w_v), M)
    o = _heads_attention(q, k, v, jnp.zeros((S, M), F32), hd ** -0.5)
    o = o.reshape(B, XA_HEADS, S, hd).transpose(0, 2, 1, 3).reshape(B, S, D)
    return mm(o, w_o)


def swiglu(x, w_gate, w_up, w_down):
    return mm(jax.nn.silu(mm(x, w_gate)) * mm(x, w_up), w_down)


def forward(p, x, mem):
    h = x
    for layer in range(DEPTH):
        mix = even_mixer(h, p) if layer % 2 == 0 else odd_mixer(h, p)
        h = layer_norm(DEEPNORM_ALPHA * h + mix, p['ln_mix_g'][layer], p['ln_mix_b'][layer])
        xa = cross_attention(h, mem, p['xa_w_q'][layer], p['xa_w_k'][layer], p['xa_w_v'][layer], p['xa_w_o'][layer])
        h = layer_norm(DEEPNORM_ALPHA * h + xa, p['ln_xa_g'][layer], p['ln_xa_b'][layer])
        ff = swiglu(h, p['ffn_w_gate'][layer], p['ffn_w_up'][layer], p['ffn_w_down'][layer])
        h = layer_norm(DEEPNORM_ALPHA * h + ff, p['ln_ffn_g'][layer], p['ln_ffn_b'][layer])
    return h


def _flat_pack(arrays, length, dtype):
    flat = jnp.concatenate([a.reshape(-1).astype(dtype) for a in arrays])
    return jnp.pad(flat, (0, length - flat.shape[0]))


def _unpack(flat, shapes):
    out, at = [], 0
    for shp in shapes:
        n = 1
        for d in shp:
            n *= d
        out.append(flat[at:at + n].reshape(shp))
        at += n
    return out


def _shard_of(full, axis, s):
    n = full.shape[axis] // 4
    return lax.slice_in_dim(full, s * n, (s + 1) * n, axis=axis)


def kernel(x, mem, ev_w_in, ev_mla_q_norm, ev_mla_w_uq, ev_mla_kv_norm, ev_mla_w_ukv, ev_gla_w_gate2, ev_gla_b_gate, ev_gla_norm_g, ev_gla_norm_b, ev_w_out, od_w_in, od_rwkv_mu, od_rwkv_w0, od_rwkv_w_decay2, od_rwkv_a0, od_rwkv_w_a2, od_rwkv_w_gate2, od_rwkv_k_k, od_rwkv_k_a, od_rwkv_r_k, od_rwkv_gn_g, od_rwkv_gn_b, od_w_out, ln_mix_g, ln_mix_b, xa_w_q, xa_w_k, xa_w_v, xa_w_o, ln_xa_g, ln_xa_b, ffn_w_gate, ffn_w_up, ffn_w_down, ln_ffn_g, ln_ffn_b, loss_target, m_ev_w_in, m_ev_mla_q_norm, m_ev_mla_w_uq, m_ev_mla_kv_norm, m_ev_mla_w_ukv, m_ev_gla_w_gate2, m_ev_gla_b_gate, m_ev_gla_norm_g, m_ev_gla_norm_b, m_ev_w_out, m_od_w_in, m_od_rwkv_mu, m_od_rwkv_w0, m_od_rwkv_w_decay2, m_od_rwkv_a0, m_od_rwkv_w_a2, m_od_rwkv_w_gate2, m_od_rwkv_k_k, m_od_rwkv_k_a, m_od_rwkv_r_k, m_od_rwkv_gn_g, m_od_rwkv_gn_b, m_od_w_out, m_ln_mix_g, m_ln_mix_b, m_xa_w_q, m_xa_w_k, m_xa_w_v, m_xa_w_o, m_ln_xa_g, m_ln_xa_b, m_ffn_w_gate, m_ffn_w_up, m_ffn_w_down, m_ln_ffn_g, m_ln_ffn_b, v_ev_w_in, v_ev_mla_q_norm, v_ev_mla_w_uq, v_ev_mla_kv_norm, v_ev_mla_w_ukv, v_ev_gla_w_gate2, v_ev_gla_b_gate, v_ev_gla_norm_g, v_ev_gla_norm_b, v_ev_w_out, v_od_w_in, v_od_rwkv_mu, v_od_rwkv_w0, v_od_rwkv_w_decay2, v_od_rwkv_a0, v_od_rwkv_w_a2, v_od_rwkv_w_gate2, v_od_rwkv_k_k, v_od_rwkv_k_a, v_od_rwkv_r_k, v_od_rwkv_gn_g, v_od_rwkv_gn_b, v_od_w_out, v_ln_mix_g, v_ln_mix_b, v_xa_w_q, v_xa_w_k, v_xa_w_v, v_xa_w_o, v_ln_xa_g, v_ln_xa_b, v_ffn_w_gate, v_ffn_w_up, v_ffn_w_down, v_ln_ffn_g, v_ln_ffn_b):
    given = dict(locals())
    W = {n: given[n] for n in WEIGHT_NAMES}
    Mo = {n: given['m_' + n] for n in WEIGHT_NAMES}
    Vo = {n: given['v_' + n] for n in WEIGHT_NAMES}
    mat_names = [n for n in WEIGHT_NAMES if n in MATRICES]
    vec_names = [n for n in WEIGHT_NAMES if n in SHARDED_VECTORS]
    rep_names = list(REPLICATED)
    mat_shapes = [W[n].shape for n in mat_names]
    vec_shapes = [W[n].shape for n in vec_names]
    rep_shapes = [W[n].shape for n in rep_names]

    def count(shapes):
        total = 0
        for shp in shapes:
            n = 1
            for d in shp:
                n *= d
            total += n
        return total

    rc = -(-count(mat_shapes) // (PACK_CHUNKS * 2 * PACK_COLS * LANES)) * LANES
    mat_len = PACK_CHUNKS * 2 * rc * PACK_COLS
    rv = -(-count(vec_shapes) // (2 * SMALL_COLS * 8)) * 8
    vec_len = 2 * rv * SMALL_COLS
    rr = -(-count(rep_shapes) // (SMALL_COLS * 8)) * 8
    rep_len = rr * SMALL_COLS

    wmat = _flat_pack([W[n] for n in mat_names], mat_len, BF16).reshape(PACK_CHUNKS, 2, rc, PACK_COLS)
    gathered = jnp.stack([_all_gather_call(wmat[i]) for i in range(PACK_CHUNKS)], axis=1)
    gvec = _all_gather_call(_flat_pack([W[n] for n in vec_names], vec_len, F32).reshape(2, rv, SMALL_COLS))
    full = {}
    mat_parts = [_unpack(gathered[s].reshape(-1), mat_shapes) for s in range(4)]
    for i, n in enumerate(mat_names):
        full[n] = jnp.concatenate([mat_parts[s][i] for s in range(4)], axis=MATRICES[n])
    vec_parts = [_unpack(gvec[s].reshape(-1), vec_shapes) for s in range(4)]
    for i, n in enumerate(vec_names):
        full[n] = jnp.concatenate([vec_parts[s][i] for s in range(4)], axis=SHARDED_VECTORS[n])
    for n in rep_names:
        full[n] = W[n]

    B, S, D = x.shape
    y, vjp = jax.vjp(lambda p, xx: forward(p, xx, mem), full, x)
    dy, part = _loss_call(y.reshape(B * S, D), loss_target.reshape(B * S, D))
    loss = lax.psum(0.5 * part, ('x', 'y', 'c'))
    gfull, grad_x = vjp(dy.reshape(B, S, D))

    gmat = jnp.stack([_flat_pack([_shard_of(gfull[n], MATRICES[n], s) for n in mat_names], mat_len, BF16)
                      for s in range(4)]).reshape(4, PACK_CHUNKS, 2, rc, PACK_COLS)
    halves = [_sum_parts_call(_scatter_call(gmat[:, i])) for i in range(PACK_CHUNKS)]
    gshard = jnp.stack([_sibling_exchange_call(h) for h in halves]).reshape(-1, PACK_COLS)
    grep = _flat_pack([gfull[n] for n in rep_names], rep_len, F32)
    gsmall = jnp.stack([jnp.concatenate([
        _flat_pack([_shard_of(gfull[n], SHARDED_VECTORS[n], s) for n in vec_names], vec_len, F32), grep])
        for s in range(4)]).reshape(4, 1, 2 * rv + rr, SMALL_COLS)
    gsmall = _sum_parts_call(_scatter_call(jnp.concatenate([gsmall, gsmall], axis=1)))

    def small_pack(src):
        return jnp.concatenate([_flat_pack([src[n] for n in vec_names], vec_len, F32),
                                _flat_pack([src[n] for n in rep_names], rep_len, F32)]).reshape(-1, SMALL_COLS)

    groups = [{}, {}, {}, {}]
    for n, g in zip(mat_names, _unpack(gshard.reshape(-1), mat_shapes)):
        rows = (-1, g.shape[-1])
        outs = _adamw_call(W[n].reshape(rows), g.reshape(rows), Mo[n].reshape(rows), Vo[n].reshape(rows))
        for grp, val in zip(groups, (g,) + outs):
            grp[n] = val.reshape(g.shape)
    small = (gsmall,) + _adamw_call(small_pack(W), gsmall, small_pack(Mo), small_pack(Vo))
    for grp, sm in zip(groups, small):
        sm = sm.reshape(-1)
        grp.update(zip(vec_names, _unpack(sm[:vec_len], vec_shapes)))
        grp.update(zip(rep_names, _unpack(sm[vec_len:], rep_shapes)))
    return (loss, grad_x, *[grp[n] for grp in groups for n in WEIGHT_NAMES])
```

```python
import functools

import jax
import jax.numpy as jnp
from jax import lax
from jax.experimental import pallas as pl
from jax.experimental.pallas import tpu as pltpu

F32 = jnp.float32
BF16 = jnp.bfloat16
MESH = pl.DeviceIdType.MESH

ROPE_THETA = 10000.0
LN_EPS = 1e-5
RMS_EPS = 1e-6
DEPTH = 2
DEEPNORM_ALPHA = (2.0 * DEPTH) ** 0.25
MLA_HEADS, MLA_NOPE, MLA_ROPE, MLA_V, MLA_Q_RANK, MLA_KV_RANK = 8, 128, 64, 128, 512, 256
GLA_HEADS, GLA_DK, GLA_DV, GLA_GATE_RANK, GLA_TAU, GLA_CHUNK = 4, 128, 256, 16, 16.0, 64
DIL_HEADS, DIL_HEAD_DIM = 8, 128
DIL_BRANCHES = ((128, 1), (512, 4), (2048, 16))
RWKV_HEADS, RWKV_HEAD_DIM = 16, 64
RWKV_DECAY_RANK, RWKV_A_RANK, RWKV_GATE_RANK = 96, 96, 256
RWKV_GN_EPS = 64e-5
XA_HEADS = 4
DIL_WIDTH = DIL_HEADS * DIL_HEAD_DIM
RWKV_WIDTH = RWKV_HEADS * RWKV_HEAD_DIM
EVEN_IN_WIDTHS = (MLA_Q_RANK, MLA_KV_RANK, MLA_ROPE, GLA_HEADS * GLA_DK, GLA_HEADS * GLA_DK,
                  GLA_HEADS * GLA_DV, GLA_HEADS * GLA_DV, GLA_GATE_RANK)
RWKV_IN_WIDTHS = (RWKV_WIDTH, RWKV_WIDTH, RWKV_WIDTH, RWKV_DECAY_RANK, RWKV_A_RANK, RWKV_GATE_RANK)
ADAM_LR, ADAM_B1, ADAM_B2, ADAM_EPS, ADAM_WD, ADAM_STEP = 0.001, 0.9, 0.999, 1e-08, 0.01, 10

LANES = 128
SUBLANES = 8
VMEM_LIMIT_BYTES = 48 * 1024 * 1024
NEG_BIG = -1e30

PACK_COLS = 1024
PACK_CHUNKS = 1
SMALL_COLS = 128

MATRICES = {
    'ev_w_in': 2, 'ev_mla_w_uq': 2, 'ev_mla_w_ukv': 2, 'ev_gla_w_gate2': 2, 'ev_w_out': 1, 'od_w_in': 2,
    'od_rwkv_w_decay2': 2, 'od_rwkv_w_a2': 2, 'od_rwkv_w_gate2': 2, 'od_w_out': 1,
    'xa_w_q': 1, 'xa_w_k': 1, 'xa_w_v': 1, 'xa_w_o': 1, 'ffn_w_gate': 2, 'ffn_w_up': 2, 'ffn_w_down': 1,
}
SHARDED_VECTORS = {
    'od_rwkv_mu': 1, 'od_rwkv_w0': 1, 'od_rwkv_a0': 1, 'od_rwkv_k_k': 1, 'od_rwkv_k_a': 1,
    'od_rwkv_gn_g': 1, 'od_rwkv_gn_b': 1,
}
REPLICATED = ('ev_mla_q_norm', 'ev_mla_kv_norm', 'ev_gla_b_gate', 'ev_gla_norm_g', 'ev_gla_norm_b', 'od_rwkv_r_k',
              'ln_mix_g', 'ln_mix_b', 'ln_xa_g', 'ln_xa_b', 'ln_ffn_g', 'ln_ffn_b')
WEIGHT_NAMES = ('ev_w_in', 'ev_mla_q_norm', 'ev_mla_w_uq', 'ev_mla_kv_norm', 'ev_mla_w_ukv', 'ev_gla_w_gate2',
                'ev_gla_b_gate', 'ev_gla_norm_g', 'ev_gla_norm_b', 'ev_w_out', 'od_w_in', 'od_rwkv_mu', 'od_rwkv_w0',
                'od_rwkv_w_decay2', 'od_rwkv_a0', 'od_rwkv_w_a2', 'od_rwkv_w_gate2', 'od_rwkv_k_k', 'od_rwkv_k_a',
                'od_rwkv_r_k', 'od_rwkv_gn_g', 'od_rwkv_gn_b', 'od_w_out', 'ln_mix_g', 'ln_mix_b', 'xa_w_q', 'xa_w_k',
                'xa_w_v', 'xa_w_o', 'ln_xa_g', 'ln_xa_b', 'ffn_w_gate', 'ffn_w_up', 'ffn_w_down', 'ln_ffn_g', 'ln_ffn_b')


def _pick(n, cap, mult):
    d = (min(cap, n) // mult) * mult
    while d >= mult:
        if n % d == 0:
            return d
        d -= mult
    return n


def _params(semantics):
    return pltpu.CompilerParams(dimension_semantics=semantics, vmem_limit_bytes=VMEM_LIMIT_BYTES)


_DIMS = {(False, False): (((1,), (0,)), ((), ())), (False, True): (((1,), (1,)), ((), ())),
         (True, False): (((0,), (0,)), ((), ()))}


def _bmm(a, b, ta, tb, out_dtype=F32):
    G = a.shape[0]
    K, M = (a.shape[1], a.shape[2]) if ta else (a.shape[2], a.shape[1])
    N = b.shape[1] if tb else b.shape[2]
    assert (b.shape[2] if tb else b.shape[1]) == K and b.shape[0] == G
    tm, tn = _pick(M, 1024, LANES), _pick(N, 512, LANES)
    tk = K if K <= 2048 else _pick(K, 2048, LANES)
    nk = K // tk
    gb = 1
    if tm == M and tn == N and nk == 1:
        per = 4 * (M * K + K * N + M * N)
        gb = _pick(G, max(1, min(8, (2 << 20) // per)), 1)
    dims = _DIMS[(ta, tb)]

    def body(a_ref, b_ref, o_ref, *scratch):
        def prod(i):
            return lax.dot_general(a_ref[i].astype(BF16), b_ref[i].astype(BF16), dims, preferred_element_type=F32)

        if nk == 1:
            for i in range(gb):
                o_ref[i] = prod(i).astype(o_ref.dtype)
        else:
            acc_ref, = scratch
            k = pl.program_id(3)

            @pl.when(k == 0)
            def _():
                acc_ref[...] = jnp.zeros_like(acc_ref)

            for i in range(gb):
                acc_ref[i] += prod(i)

            @pl.when(k == nk - 1)
            def _():
                o_ref[...] = acc_ref[...].astype(o_ref.dtype)

    a_spec = (pl.BlockSpec((gb, tk, tm), lambda g, i, j, k: (g, k, i)) if ta
              else pl.BlockSpec((gb, tm, tk), lambda g, i, j, k: (g, i, k)))
    b_spec = (pl.BlockSpec((gb, tn, tk), lambda g, i, j, k: (g, j, k)) if tb
              else pl.BlockSpec((gb, tk, tn), lambda g, i, j, k: (g, k, j)))
    return pl.pallas_call(
        body, name='bmm_' + ('t' if ta else 'n') + ('t' if tb else 'n'),
        out_shape=jax.ShapeDtypeStruct((G, M, N), out_dtype),
        grid=(G // gb, M // tm, N // tn, nk),
        in_specs=[a_spec, b_spec],
        out_specs=pl.BlockSpec((gb, tm, tn), lambda g, i, j, k: (g, i, j)),
        scratch_shapes=[] if nk == 1 else [pltpu.VMEM((gb, tm, tn), F32)],
        compiler_params=_params(('parallel', 'parallel', 'parallel', 'arbitrary')),
    )(a, b)


def _like(x):
    return jnp.zeros((), x.dtype)


@jax.custom_vjp
def bmm_nn(a, b):
    return _bmm(a, b, False, False)


def _bmm_nn_fwd(a, b):
    ab, bb = a.astype(BF16), b.astype(BF16)
    return _bmm(ab, bb, False, False), (ab, bb, _like(a), _like(b))


def _bmm_nn_bwd(res, g):
    a, b, la, lb = res
    g = g.astype(BF16)
    return _bmm(g, b, False, True, la.dtype), _bmm(a, g, True, False, lb.dtype)


bmm_nn.defvjp(_bmm_nn_fwd, _bmm_nn_bwd)


@jax.custom_vjp
def bmm_nt(a, b):
    return _bmm(a, b, False, True)


def _bmm_nt_fwd(a, b):
    ab, bb = a.astype(BF16), b.astype(BF16)
    return _bmm(ab, bb, False, True), (ab, bb, _like(a), _like(b))


def _bmm_nt_bwd(res, g):
    a, b, la, lb = res
    g = g.astype(BF16)
    return _bmm(g, b, False, False, la.dtype), _bmm(g, a, True, False, lb.dtype)


bmm_nt.defvjp(_bmm_nt_fwd, _bmm_nt_bwd)


def mm(x, w):
    lead = x.shape[:-1]
    out = bmm_nn(x.reshape(1, -1, x.shape[-1]), w[None])
    return out.reshape(lead + (w.shape[1],))


def _norm_stats(x, center, eps):
    if center:
        xc = x - jnp.mean(x, axis=-1, keepdims=True)
    else:
        xc = x
    rstd = lax.rsqrt(jnp.mean(xc * xc, axis=-1, keepdims=True) + eps)
    return xc * rstd, rstd


def _norm_fwd_call(x, g, b, center, eps):
    R, C = x.shape
    tr = _pick(R, max(8, (1 << 19) // C), 8)

    def body(x_ref, g_ref, b_ref, y_ref):
        xhat, _ = _norm_stats(x_ref[...], center, eps)
        y_ref[...] = xhat * g_ref[...] + b_ref[...]

    row = pl.BlockSpec((tr, C), lambda i: (i, 0))
    vec = pl.BlockSpec((1, C), lambda i: (0, 0))
    return pl.pallas_call(
        body, name='norm_fwd', out_shape=jax.ShapeDtypeStruct((R, C), F32), grid=(R // tr,),
        in_specs=[row, vec, vec], out_specs=row, compiler_params=_params(('parallel',)),
    )(x, g, b)


def _norm_bwd_call(x, g, dy, center, eps):
    R, C = x.shape
    tr = _pick(R, max(8, (1 << 19) // C), 8)

    def body(x_ref, g_ref, dy_ref, dx_ref, dg_ref, db_ref):
        @pl.when(pl.program_id(0) == 0)
        def _():
            dg_ref[...] = jnp.zeros_like(dg_ref)
            db_ref[...] = jnp.zeros_like(db_ref)

        xhat, rstd = _norm_stats(x_ref[...], center, eps)
        dy = dy_ref[...]
        dxh = dy * g_ref[...]
        proj = xhat * jnp.mean(dxh * xhat, axis=-1, keepdims=True)
        if center:
            dx_ref[...] = rstd * (dxh - jnp.mean(dxh, axis=-1, keepdims=True) - proj)
        else:
            dx_ref[...] = rstd * (dxh - proj)
        dg_ref[...] += jnp.sum(dy * xhat, axis=0, keepdims=True)
        db_ref[...] += jnp.sum(dy, axis=0, keepdims=True)

    row = pl.BlockSpec((tr, C), lambda i: (i, 0))
    vec = pl.BlockSpec((1, C), lambda i: (0, 0))
    return pl.pallas_call(
        body, name='norm_bwd',
        out_shape=(jax.ShapeDtypeStruct((R, C), F32), jax.ShapeDtypeStruct((1, C), F32), jax.ShapeDtypeStruct((1, C), F32)),
        grid=(R // tr,), in_specs=[row, vec, row], out_specs=(row, vec, vec), compiler_params=_params(('arbitrary',)),
    )(x, g, dy)


@functools.partial(jax.custom_vjp, nondiff_argnums=(3, 4))
def _norm2d(x, g, b, center, eps):
    return _norm_fwd_call(x, g, b, center, eps)


def _norm2d_fwd(x, g, b, center, eps):
    return _norm_fwd_call(x, g, b, center, eps), (x, g)


def _norm2d_bwd(center, eps, res, dy):
    x, g = res
    return _norm_bwd_call(x, g, dy, center, eps)


_norm2d.defvjp(_norm2d_fwd, _norm2d_bwd)


def layer_norm(x, g, b, eps=LN_EPS):
    C = x.shape[-1]
    return _norm2d(x.reshape(-1, C), g.reshape(1, C), b.reshape(1, C), True, eps).reshape(x.shape)


def rms_norm(x, g):
    C = x.shape[-1]
    return _norm2d(x.reshape(-1, C), g.reshape(1, C), jnp.zeros((1, C), F32), False, RMS_EPS).reshape(x.shape)


def _softmax_fwd_call(s, bias, scale):
    G1, G2, R, C = s.shape
    tr = _pick(R, max(8, (1 << 19) // C), 8)

    def body(s_ref, bias_ref, p_ref, lse_ref):
        z = s_ref[0, 0] * scale + bias_ref[0]
        m = jnp.max(z, axis=-1, keepdims=True)
        e = jnp.exp(z - m)
        den = jnp.sum(e, axis=-1, keepdims=True)
        p_ref[0, 0] = e / den
        lse_ref[0, 0] = m + jnp.log(den)

    blk = pl.BlockSpec((1, 1, tr, C), lambda a, b, r: (a, b, r, 0))
    col = pl.BlockSpec((1, 1, tr, 1), lambda a, b, r: (a, b, r, 0))
    return pl.pallas_call(
        body, name='softmax_fwd',
        out_shape=(jax.ShapeDtypeStruct(s.shape, F32), jax.ShapeDtypeStruct((G1, G2, R, 1), F32)),
        grid=(G1, G2, R // tr), in_specs=[blk, pl.BlockSpec((1, tr, C), lambda a, b, r: (b, r, 0))],
        out_specs=(blk, col), compiler_params=_params(('parallel', 'parallel', 'parallel')),
    )(s, bias)


def _softmax_bwd_call(p, dp, dlse, scale):
    G1, G2, R, C = p.shape
    tr = _pick(R, max(8, (1 << 19) // C), 8)

    def body(p_ref, dp_ref, dlse_ref, ds_ref):
        p = p_ref[0, 0]
        dp = dp_ref[0, 0]
        inner = jnp.sum(dp * p, axis=-1, keepdims=True)
        ds_ref[0, 0] = (p * (dp - inner + dlse_ref[0, 0])) * scale

    blk = pl.BlockSpec((1, 1, tr, C), lambda a, b, r: (a, b, r, 0))
    col = pl.BlockSpec((1, 1, tr, 1), lambda a, b, r: (a, b, r, 0))
    return pl.pallas_call(
        body, name='softmax_bwd', out_shape=jax.ShapeDtypeStruct(p.shape, F32),
        grid=(G1, G2, R // tr), in_specs=[blk, blk, col], out_specs=blk,
        compiler_params=_params(('parallel', 'parallel', 'parallel')),
    )(p, dp, dlse)


@functools.partial(jax.custom_vjp, nondiff_argnums=(2,))
def softmax_lse(s, bias, scale):
    return _softmax_fwd_call(s, bias, scale)


def _softmax_lse_fwd(s, bias, scale):
    p, lse = _softmax_fwd_call(s, bias, scale)
    return (p, lse), (p, bias)


def _softmax_lse_bwd(scale, res, cts):
    p, bias = res
    dp, dlse = cts
    return _softmax_bwd_call(p, dp, dlse, scale), jnp.zeros_like(bias)


softmax_lse.defvjp(_softmax_lse_fwd, _softmax_lse_bwd)


def _dot(a, b, dims):
    return lax.dot_general(a.astype(BF16), b.astype(BF16), dims, preferred_element_type=F32)


_NN, _NT, _TN = _DIMS[(False, False)], _DIMS[(False, True)], _DIMS[(True, False)]


def _gla_fwd_call(q, k, v, dec):
    G, nc, C, dk = q.shape
    dv = v.shape[-1]

    def body(q_ref, k_ref, v_ref, dec_ref, o_ref, st_ref, state):
        @pl.when(pl.program_id(1) == 0)
        def _():
            state[...] = jnp.zeros_like(state)

        s = state[...]
        st_ref[0, 0] = s
        o_ref[0, 0] = _dot(q_ref[0, 0], s, _NN)
        state[...] = s * dec_ref[0, 0] + _dot(k_ref[0, 0], v_ref[0, 0], _TN)

    def spec(r, c):
        return pl.BlockSpec((1, 1, r, c), lambda g, t: (g, t, 0, 0))

    return pl.pallas_call(
        body, name='gla_scan_fwd',
        out_shape=(jax.ShapeDtypeStruct((G, nc, C, dv), F32), jax.ShapeDtypeStruct((G, nc, dk, dv), F32)),
        grid=(G, nc), in_specs=[spec(C, dk), spec(C, dk), spec(C, dv), spec(dk, 1)],
        out_specs=(spec(C, dv), spec(dk, dv)), scratch_shapes=[pltpu.VMEM((dk, dv), F32)],
        compiler_params=_params(('parallel', 'arbitrary')),
    )(q, k, v, dec)


def _gla_bwd_call(q, k, v, dec, states, do):
    G, nc, C, dk = q.shape
    dv = v.shape[-1]

    def body(q_ref, k_ref, v_ref, dec_ref, st_ref, do_ref, dq_ref, dk_ref, dv_ref, ddec_ref, dstate):
        @pl.when(pl.program_id(1) == 0)
        def _():
            dstate[...] = jnp.zeros_like(dstate)

        s = st_ref[0, 0]
        d = dstate[...]
        do = do_ref[0, 0]
        dq_ref[0, 0] = _dot(do, s, _NT)
        dk_ref[0, 0] = _dot(v_ref[0, 0], d, _NT)
        dv_ref[0, 0] = _dot(k_ref[0, 0], d, _NN)
        ddec_ref[0, 0] = jnp.sum(s * d, axis=1, keepdims=True)
        dstate[...] = d * dec_ref[0, 0] + _dot(q_ref[0, 0], do, _TN)

    def spec(r, c):
        return pl.BlockSpec((1, 1, r, c), lambda g, t: (g, nc - 1 - t, 0, 0))

    return pl.pallas_call(
        body, name='gla_scan_bwd',
        out_shape=(jax.ShapeDtypeStruct(q.shape, F32), jax.ShapeDtypeStruct(k.shape, F32),
                   jax.ShapeDtypeStruct(v.shape, F32), jax.ShapeDtypeStruct(dec.shape, F32)),
        grid=(G, nc), in_specs=[spec(C, dk), spec(C, dk), spec(C, dv), spec(dk, 1), spec(dk, dv), spec(C, dv)],
        out_specs=(spec(C, dk), spec(C, dk), spec(C, dv), spec(dk, 1)), scratch_shapes=[pltpu.VMEM((dk, dv), F32)],
        compiler_params=_params(('parallel', 'arbitrary')),
    )(q, k, v, dec, states, do)


@jax.custom_vjp
def gla_scan(q, k, v, dec):
    return _gla_fwd_call(q, k, v, dec)[0]


def _gla_scan_fwd(q, k, v, dec):
    o, states = _gla_fwd_call(q, k, v, dec)
    return o, (q, k, v, dec, states)


def _gla_scan_bwd(res, do):
    return _gla_bwd_call(*res, do)


gla_scan.defvjp(_gla_scan_fwd, _gla_scan_bwd)


RWKV_PAIRS_PER_STEP = 4
RWKV_TIME_BLOCK = 64
RN = RWKV_HEAD_DIM


def _rwkv_consts():
    row = lax.broadcasted_iota(jnp.int32, (RN, LANES), 0)
    lane = lax.broadcasted_iota(jnp.int32, (RN, LANES), 1)
    diag = (lane % RN == row).astype(F32)
    r2 = lax.broadcasted_iota(jnp.int32, (LANES, LANES), 0)
    l2 = lax.broadcasted_iota(jnp.int32, (LANES, LANES), 1)
    seg = (r2 // RN == l2 // RN).astype(BF16)
    return diag, seg


def _stage(lhs_ref, slot, p):
    hi = p.astype(BF16)
    lhs_ref[pl.ds(slot * LANES, RN), :] = hi
    lhs_ref[pl.ds(slot * LANES + RN, RN), :] = (p - hi.astype(F32)).astype(BF16)


def _seg_sums(lhs_ref, nslots, seg):
    res = jnp.dot(lhs_ref[pl.ds(0, nslots * LANES), :], seg, preferred_element_type=F32)
    return [res[i * LANES:i * LANES + RN] + res[i * LANES + RN:(i + 1) * LANES] for i in range(nslots)]


def _rwkv_blocks(B, S, C):
    npairs = C // LANES
    pp = RWKV_PAIRS_PER_STEP if npairs % RWKV_PAIRS_PER_STEP == 0 else 1
    T = _pick(S, RWKV_TIME_BLOCK, 8)
    return npairs, pp, T


def _rwkv_fwd_call(r, w, k, v, kk, b):
    B, S, C = r.shape
    npairs, pp, T = _rwkv_blocks(B, S, C)
    G = SUBLANES

    def body(r_ref, w_ref, k_ref, v_ref, kk_ref, b_ref, y_ref, sall_ref, state, step_lhs, v_lhs, y_lhs):
        @pl.when(pl.program_id(2) == 0)
        def _():
            state[...] = jnp.zeros_like(state)

        diag, seg = _rwkv_consts()
        rowid = lax.broadcasted_iota(jnp.int32, (SUBLANES, LANES), 0)

        def group(t8, carry):
            rows = pl.ds(pl.multiple_of(t8 * G, G), G)
            sls = [slice(p * LANES, (p + 1) * LANES) for p in range(pp)]
            ops = [[ref[0, rows, sl] for ref in (r_ref, w_ref, k_ref, v_ref, kk_ref, b_ref)] for sl in sls]
            for j in range(G):
                for p in range(pp):
                    _stage(v_lhs, j * pp + p, diag * ops[p][3][j:j + 1])
            vcols = _seg_sums(v_lhs, G * pp, seg)
            s = list(carry)
            for j in range(G):
                for p in range(pp):
                    sall_ref[0, p, t8 * G + j] = s[p]
                    _stage(step_lhs, p, s[p] * ops[p][4][j:j + 1])
                sas = _seg_sums(step_lhs, pp, seg)
                for p in range(pp):
                    rt, wt, kt, _, _, bt = ops[p]
                    s[p] = s[p] * wt[j:j + 1] - sas[p] * bt[j:j + 1] + vcols[j * pp + p] * kt[j:j + 1]
                    _stage(y_lhs, j * pp + p, s[p] * rt[j:j + 1])
            ycols = _seg_sums(y_lhs, G * pp, seg)
            for p in range(pp):
                ytile = jnp.zeros((SUBLANES, LANES), F32)
                for j in range(G):
                    ytile = jnp.where(rowid == j, jnp.sum(diag * ycols[j * pp + p], axis=0, keepdims=True), ytile)
                y_ref[0, rows, sls[p]] = ytile
            return tuple(s)

        final = lax.fori_loop(0, T // G, group, tuple(state[p] for p in range(pp)))
        for p in range(pp):
            state[p] = final[p]

    seq = pl.BlockSpec((1, T, pp * LANES), lambda bi, g, t: (bi, t, g))
    return pl.pallas_call(
        body, name='rwkv_scan_fwd',
        out_shape=(jax.ShapeDtypeStruct((B, S, C), F32), jax.ShapeDtypeStruct((B, npairs, S, RN, LANES), F32)),
        grid=(B, npairs // pp, S // T), in_specs=[seq] * 6,
        out_specs=(seq, pl.BlockSpec((1, pp, T, RN, LANES), lambda bi, g, t: (bi, g, t, 0, 0))),
        scratch_shapes=[pltpu.VMEM((pp, RN, LANES), F32), pltpu.VMEM((pp * LANES, LANES), BF16),
                        pltpu.VMEM((G * pp * LANES, LANES), BF16), pltpu.VMEM((G * pp * LANES, LANES), BF16)],
        compiler_params=_params(('parallel', 'parallel', 'arbitrary')),
    )(r, w, k, v, kk, b)


def _rwkv_bwd_call(r, w, k, v, kk, b, sall, dy):
    B, S, C = r.shape
    npairs, pp, T = _rwkv_blocks(B, S, C)
    nt = S // T
    G = SUBLANES

    def body(r_ref, w_ref, k_ref, v_ref, kk_ref, b_ref, sall_ref, dy_ref,
             dr_ref, dw_ref, dk_ref, dv_ref, dkk_ref, db_ref, dstate, step_lhs, pre_lhs, dv_lhs):
        @pl.when(pl.program_id(2) == 0)
        def _():
            dstate[...] = jnp.zeros_like(dstate)

        diag, seg = _rwkv_consts()
        rowid = lax.broadcasted_iota(jnp.int32, (SUBLANES, LANES), 0)

        def colsum(z):
            return jnp.sum(z, axis=0, keepdims=True)

        def group(i, carry):
            t8 = T // G - 1 - i
            rows = pl.ds(pl.multiple_of(t8 * G, G), G)
            sls = [slice(p * LANES, (p + 1) * LANES) for p in range(pp)]
            ops = [[ref[0, rows, sl] for ref in (r_ref, w_ref, k_ref, v_ref, kk_ref, b_ref, dy_ref)] for sl in sls]
            for j in range(G):
                for p in range(pp):
                    _stage(pre_lhs, j * pp + p, sall_ref[0, p, t8 * G + j] * ops[p][4][j:j + 1])
                    _stage(pre_lhs, (G + j) * pp + p, diag * ops[p][3][j:j + 1])
                    _stage(pre_lhs, (2 * G + j) * pp + p, diag * ops[p][6][j:j + 1])
            pre = _seg_sums(pre_lhs, 3 * G * pp, seg)
            ds = list(carry)
            tiles = [[jnp.zeros((SUBLANES, LANES), F32) for _ in range(5)] for _ in range(pp)]
            for j in reversed(range(G)):
                d = []
                for p in range(pp):
                    rt, _, kt, _, _, bt, _ = ops[p]
                    d.append(ds[p] + pre[(2 * G + j) * pp + p] * rt[j:j + 1])
                    _stage(step_lhs, p, d[p] * bt[j:j + 1])
                    _stage(dv_lhs, j * pp + p, d[p] * kt[j:j + 1])
                dsas = _seg_sums(step_lhs, pp, seg)
                for p in range(pp):
                    rt, wt, kt, _, kkt, bt, _ = ops[p]
                    s = sall_ref[0, p, t8 * G + j]
                    sa, vcol, dycol = -pre[j * pp + p], pre[(G + j) * pp + p], pre[(2 * G + j) * pp + p]
                    s2 = s * wt[j:j + 1] + sa * bt[j:j + 1] + vcol * kt[j:j + 1]
                    vals = (colsum(s2 * dycol), colsum(d[p] * s), colsum(d[p] * vcol), -colsum(s * dsas[p]),
                            colsum(d[p] * sa))
                    tiles[p] = [jnp.where(rowid == j, val, tile) for val, tile in zip(vals, tiles[p])]
                    ds[p] = d[p] * wt[j:j + 1] - dsas[p] * kkt[j:j + 1]
            dvcols = _seg_sums(dv_lhs, G * pp, seg)
            for p in range(pp):
                dvt = jnp.zeros((SUBLANES, LANES), F32)
                for j in range(G):
                    dvt = jnp.where(rowid == j, colsum(diag * dvcols[j * pp + p]), dvt)
                dv_ref[0, rows, sls[p]] = dvt
                for ref, tile in zip((dr_ref, dw_ref, dk_ref, dkk_ref, db_ref), tiles[p]):
                    ref[0, rows, sls[p]] = tile
            return tuple(ds)

        final = lax.fori_loop(0, T // G, group, tuple(dstate[p] for p in range(pp)))
        for p in range(pp):
            dstate[p] = final[p]

    seq = pl.BlockSpec((1, T, pp * LANES), lambda bi, g, t: (bi, nt - 1 - t, g))
    sds = jax.ShapeDtypeStruct((B, S, C), F32)
    return pl.pallas_call(
        body, name='rwkv_scan_bwd', out_shape=(sds,) * 6,
        grid=(B, npairs // pp, nt),
        in_specs=[seq] * 6 + [pl.BlockSpec((1, pp, T, RN, LANES), lambda bi, g, t: (bi, g, nt - 1 - t, 0, 0)), seq],
        out_specs=(seq,) * 6,
        scratch_shapes=[pltpu.VMEM((pp, RN, LANES), F32), pltpu.VMEM((pp * LANES, LANES), BF16),
                        pltpu.VMEM((3 * G * pp * LANES, LANES), BF16), pltpu.VMEM((G * pp * LANES, LANES), BF16)],
        compiler_params=_params(('parallel', 'parallel', 'arbitrary')),
    )(r, w, k, v, kk, b, sall, dy)


@jax.custom_vjp
def rwkv_scan(r, w, k, v, kk, b):
    return _rwkv_fwd_call(r, w, k, v, kk, b)[0]


def _rwkv_scan_fwd(r, w, k, v, kk, b):
    y, sall = _rwkv_fwd_call(r, w, k, v, kk, b)
    return y, (r, w, k, v, kk, b, sall)


def _rwkv_scan_bwd(res, dy):
    return _rwkv_bwd_call(*res, dy)


rwkv_scan.defvjp(_rwkv_scan_fwd, _rwkv_scan_bwd)


def _loss_call(y, target):
    R, D = y.shape
    tr = _pick(R, max(8, (1 << 19) // D), 8)

    def body(y_ref, t_ref, dy_ref, part_ref):
        @pl.when(pl.program_id(0) == 0)
        def _():
            part_ref[...] = jnp.zeros_like(part_ref)

        diff = y_ref[...] - t_ref[...]
        dy_ref[...] = diff / D
        part_ref[...] += jnp.sum(jnp.mean(diff * diff, axis=-1, keepdims=True), axis=0, keepdims=True)

    row = pl.BlockSpec((tr, D), lambda i: (i, 0))
    dy, part = pl.pallas_call(
        body, name='loss_head',
        out_shape=(jax.ShapeDtypeStruct((R, D), F32), jax.ShapeDtypeStruct((1, 1), F32)),
        grid=(R // tr,), in_specs=[row, row], out_specs=(row, pl.BlockSpec((1, 1), lambda i: (0, 0))),
        compiler_params=_params(('arbitrary',)),
    )(y, target)
    return dy, part[0, 0]


def _sum_parts_call(parts):
    P, R, C = parts.shape
    tr = _pick(R, 512, 16)

    def body(p_ref, o_ref):
        acc = p_ref[0].astype(F32)
        for i in range(1, P):
            acc = acc + p_ref[i].astype(F32)
        o_ref[...] = acc

    return pl.pallas_call(
        body, name='sum_parts', out_shape=jax.ShapeDtypeStruct((R, C), F32), grid=(R // tr,),
        in_specs=[pl.BlockSpec((P, tr, C), lambda i: (0, i, 0))], out_specs=pl.BlockSpec((tr, C), lambda i: (i, 0)),
        compiler_params=_params(('parallel',)),
    )(parts)


def _adamw_call(w, g, m, v):
    R, C = w.shape
    tr = _pick(R, max(8, (1 << 18) // C), 8)

    def body(w_ref, g_ref, m_ref, v_ref, d_ref, nm_ref, nv_ref):
        g = g_ref[...]
        m = ADAM_B1 * m_ref[...] + (1.0 - ADAM_B1) * g
        v = ADAM_B2 * v_ref[...] + (1.0 - ADAM_B2) * (g * g)
        m_hat = m / (1.0 - ADAM_B1 ** ADAM_STEP)
        v_hat = v / (1.0 - ADAM_B2 ** ADAM_STEP)
        d_ref[...] = -ADAM_LR * (m_hat / (jnp.sqrt(v_hat) + ADAM_EPS) + ADAM_WD * w_ref[...])
        nm_ref[...] = m
        nv_ref[...] = v

    row = pl.BlockSpec((tr, C), lambda i: (i, 0))
    sds = jax.ShapeDtypeStruct((R, C), F32)
    return pl.pallas_call(
        body, name='adamw', out_shape=(sds, sds, sds), grid=(R // tr,),
        in_specs=[row] * 4, out_specs=(row,) * 3, compiler_params=_params(('parallel',)),
    )(w, g, m, v)


ANY = pl.BlockSpec(memory_space=pl.ANY)


def _place():
    return lax.axis_index('x'), lax.axis_index('y'), lax.axis_index('c')


COPY_SPLIT = 8
LOCAL_SPLIT = 16
BF16_TILE_ROWS = 16
PACK_ROW_ALIGN = LOCAL_SPLIT * BF16_TILE_ROWS


def _row_split(rows, parts=COPY_SPLIT):
    if rows % (parts * BF16_TILE_ROWS) == 0:
        return parts, rows // parts
    return 1, rows


def _local_copies(src, dst, sems, base, rows_total):
    ns, rs = _row_split(rows_total, LOCAL_SPLIT)
    cps = []
    for i in range(ns):
        rows = pl.ds(i * rs, rs)
        cps.append(pltpu.make_async_copy(src.at[rows], dst.at[rows], sems.at[base + i]))
        cps[-1].start()
    return cps


def _all_gather_call(pack):
    _, R, C = pack.shape
    ns, rs = _row_split(R)

    def body(pk_ref, out_ref, send_sems, recv_sems, local_sems):
        x, y, c = _place()
        chips = [(1 - x, y), (x, 1 - y), (1 - x, 1 - y)]
        me = 2 * x + y
        mine = (_local_copies(pk_ref.at[0], out_ref.at[me, 0], local_sems, 0, R)
                + _local_copies(pk_ref.at[1], out_ref.at[me, 1], local_sems, LOCAL_SPLIT, R))

        def copy(k, i, src, dst, to):
            rows = pl.ds(i * rs, rs)
            return pltpu.make_async_remote_copy(src_ref=src.at[rows], dst_ref=dst.at[rows], send_sem=send_sems.at[k * ns + i],
                                                recv_sem=recv_sems.at[k * ns + i], device_id=to, device_id_type=MESH)

        first = [copy(j, i, pk_ref.at[c], out_ref.at[me, c], (px, py, c))
                 for j, (px, py) in enumerate(chips) for i in range(ns)]
        for cp in first:
            cp.start()
        passed = []
        for i in range(ns):
            for j, (px, py) in enumerate(chips):
                landed = out_ref.at[2 * px + py, c]
                copy(j, i, landed, landed, (px, py, c)).wait_recv()
                fwd = copy(3 + j, i, landed, landed, (x, y, 1 - c))
                fwd.start()
                passed.append(fwd)
        for i in range(ns):
            for j, (px, py) in enumerate(chips):
                other = out_ref.at[2 * px + py, 1 - c]
                copy(3 + j, i, other, other, (x, y, 1 - c)).wait_recv()
        for cp in first + passed:
            cp.wait_send()
        for cp in mine:
            cp.wait()

    return pl.pallas_call(
        body, name='all_gather', out_shape=jax.ShapeDtypeStruct((4, 2, R, C), pack.dtype),
        in_specs=[ANY], out_specs=ANY,
        scratch_shapes=[pltpu.SemaphoreType.DMA((6 * ns,)), pltpu.SemaphoreType.DMA((6 * ns,)),
                        pltpu.SemaphoreType.DMA((2 * LOCAL_SPLIT,))],
    )(pack)


def _scatter_call(src):
    _, _, R, C = src.shape
    ns, rs = _row_split(R)

    def body(src_ref, out_ref, send_sems, recv_sems, local_sems):
        x, y, c = _place()
        me = 4 * x + 2 * y + c
        own = _local_copies(src_ref.at[2 * x + y, c], out_ref.at[me], local_sems, 0, R)
        peers = []
        for rel in range(1, 8):
            px = 1 - x if rel & 4 else x
            py = 1 - y if rel & 2 else y
            pc = 1 - c if rel & 1 else c
            peers.append((rel - 1, px, py, pc))

        def copy(k, i, src, dst, to):
            rows = pl.ds(i * rs, rs)
            return pltpu.make_async_remote_copy(src_ref=src.at[rows], dst_ref=dst.at[rows], send_sem=send_sems.at[k * ns + i],
                                                recv_sem=recv_sems.at[k * ns + i], device_id=to, device_id_type=MESH)

        sends = [copy(k, i, src_ref.at[2 * px + py, pc], out_ref.at[me], (px, py, pc))
                 for i in range(ns) for k, px, py, pc in peers]
        for cp in sends:
            cp.start()
        for i in range(ns):
            for k, px, py, pc in peers:
                slot = out_ref.at[4 * px + 2 * py + pc]
                copy(k, i, slot, slot, (px, py, pc)).wait_recv()
        for cp in sends:
            cp.wait_send()
        for cp in own:
            cp.wait()

    return pl.pallas_call(
        body, name='scatter_parts', out_shape=jax.ShapeDtypeStruct((8, R, C), src.dtype),
        in_specs=[ANY], out_specs=ANY,
        scratch_shapes=[pltpu.SemaphoreType.DMA((7 * ns,)), pltpu.SemaphoreType.DMA((7 * ns,)),
                        pltpu.SemaphoreType.DMA((LOCAL_SPLIT,))],
    )(src)


def _sibling_exchange_call(half):
    R, C = half.shape
    ns, rs = _row_split(R)

    def body(h_ref, out_ref, send_sems, recv_sems, local_sems):
        x, y, c = _place()
        own = _local_copies(h_ref, out_ref.at[c], local_sems, 0, R)

        def copy(i, src, dst):
            rows = pl.ds(i * rs, rs)
            return pltpu.make_async_remote_copy(src_ref=src.at[rows], dst_ref=dst.at[rows], send_sem=send_sems.at[i],
                                                recv_sem=recv_sems.at[i], device_id=(x, y, 1 - c), device_id_type=MESH)

        sends = [copy(i, h_ref, out_ref.at[c]) for i in range(ns)]
        for cp in sends:
            cp.start()
        other = out_ref.at[1 - c]
        for i in range(ns):
            copy(i, other, other).wait_recv()
        for cp in sends:
            cp.wait_send()
        for cp in own:
            cp.wait()

    return pl.pallas_call(
        body, name='sibling_exchange', out_shape=jax.ShapeDtypeStruct((2, R, C), half.dtype),
        in_specs=[ANY], out_specs=ANY,
        scratch_shapes=[pltpu.SemaphoreType.DMA((ns,)), pltpu.SemaphoreType.DMA((ns,)),
                        pltpu.SemaphoreType.DMA((LOCAL_SPLIT,))],
    )(half)


def rope_tables(seq_len, dim):
    inv = ROPE_THETA ** (-jnp.arange(0, dim, 2, dtype=F32) / dim)
    ang = jnp.arange(seq_len, dtype=F32)[:, None] * inv[None, :]
    return jnp.cos(ang), jnp.sin(ang)


def apply_rope(x, cos, sin):
    x1, x2 = jnp.split(x, 2, axis=-1)
    return jnp.concatenate([x1 * cos - x2 * sin, x1 * sin + x2 * cos], axis=-1)


def _pad_to(n):
    return -(-n // LANES) * LANES


def _pad_cols(w, widths):
    parts, at = [], 0
    for n in widths:
        parts.append(jnp.pad(w[..., at:at + n], [(0, 0)] * (w.ndim - 1) + [(0, _pad_to(n) - n)]))
        at += n
    return jnp.concatenate(parts, axis=-1)


def _split_padded(t, widths):
    out, at = [], 0
    for n in widths:
        out.append(t[..., at:at + n])
        at += _pad_to(n)
    return out


def _pad_rows(w, rows):
    return jnp.pad(w, ((0, rows - w.shape[0]), (0, 0)))


def _heads_attention(q, k, v, bias, scale):
    s = bmm_nt(q, k)
    p, _ = softmax_lse(s[:, None], bias[None], scale)
    return bmm_nn(p[:, 0], v)


def gla(q, k, v, r, gate_lr, w_gate2, b_gate, norm_g, norm_b):
    B, S, _ = q.shape
    H, dk, dv, C = GLA_HEADS, GLA_DK, GLA_DV, GLA_CHUNK
    nc = S // C
    log_a = jax.nn.log_sigmoid(mm(gate_lr, w_gate2) + b_gate) / GLA_TAU

    def chunks(t, d):
        return t.reshape(B, nc, C, H, d).transpose(0, 3, 1, 2, 4)

    qc = chunks(q, dk) * (dk ** -0.5)
    kc = chunks(k, dk)
    vc = chunks(v, dv)
    b = jnp.cumsum(chunks(log_a, dk), axis=3)
    b_last = b[:, :, :, -1:, :]
    q_dec = qc * jnp.exp(b)
    k_inv = kc * jnp.exp(-b)
    k_end = kc * jnp.exp(b_last - b)
    causal = jnp.tril(jnp.ones((C, C), dtype=bool))
    G = B * H
    att = bmm_nt(q_dec.reshape(G * nc, C, dk), k_inv.reshape(G * nc, C, dk))
    att = jnp.where(causal, att, 0.0)
    o_intra = bmm_nn(att, vc.reshape(G * nc, C, dv)).reshape(B, H, nc, C, dv)
    dec = jnp.exp(b_last[:, :, :, 0, :]).reshape(G, nc, dk, 1)
    o_inter = gla_scan(q_dec.reshape(G, nc, C, dk), k_end.reshape(G, nc, C, dk), vc.reshape(G, nc, C, dv), dec)
    o = o_intra + o_inter.reshape(B, H, nc, C, dv)
    o = o.transpose(0, 2, 3, 1, 4).reshape(B, S, H, dv)
    o = layer_norm(o, norm_g, norm_b).reshape(B, S, H * dv)
    return o * jax.nn.silu(r)


def even_mixer(x, p):
    B, S, _ = x.shape
    H = MLA_HEADS
    cos, sin = rope_tables(S, MLA_ROPE)
    z = mm(x, _pad_cols(p['ev_w_in'][0], EVEN_IN_WIDTHS))
    c_q, c_kv, k_pe, q_g, k_g, v_g, r_g, _ = _split_padded(z, EVEN_IN_WIDTHS)
    lr_at = sum(_pad_to(n) for n in EVEN_IN_WIDTHS[:-1])
    lr_g = z[..., lr_at:]
    q = mm(rms_norm(c_q, p['ev_mla_q_norm'][0]), p['ev_mla_w_uq'][0])
    q = q.reshape(B, S, H, MLA_NOPE + MLA_ROPE).transpose(0, 2, 1, 3)
    kv = mm(rms_norm(c_kv, p['ev_mla_kv_norm'][0]), p['ev_mla_w_ukv'][0])
    kv = kv.reshape(B, S, H, MLA_NOPE + MLA_V).transpose(0, 2, 1, 3)
    q_pe = apply_rope(q[..., MLA_NOPE:], cos, sin)
    k_pe = jnp.broadcast_to(apply_rope(k_pe[:, None], cos, sin), (B, H, S, MLA_ROPE))
    qf = jnp.concatenate([q[..., :MLA_NOPE], q_pe], axis=-1)
    kf = jnp.concatenate([kv[..., :MLA_NOPE], k_pe], axis=-1)
    pos = jnp.arange(S)
    bias = jnp.where(pos[None, :] <= pos[:, None], 0.0, NEG_BIG).astype(F32)
    a_out = _heads_attention(qf.reshape(B * H, S, -1), kf.reshape(B * H, S, -1),
                             kv[..., MLA_NOPE:].reshape(B * H, S, MLA_V), bias, (MLA_NOPE + MLA_ROPE) ** -0.5)
    a_out = a_out.reshape(B, H, S, MLA_V).transpose(0, 2, 1, 3).reshape(B, S, H * MLA_V)
    w_gate2 = _pad_rows(p['ev_gla_w_gate2'][0], lr_g.shape[-1])
    b_out = gla(q_g, k_g, v_g, r_g, lr_g, w_gate2, p['ev_gla_b_gate'][0], p['ev_gla_norm_g'][0], p['ev_gla_norm_b'][0])
    return mm(jnp.concatenate([a_out, b_out], axis=-1), p['ev_w_out'][0])


def dilated_branch(q, k, v, window, dil):
    B, H, S, dh = q.shape
    span = window // dil
    L = S // dil
    nb = -(-L // span)
    Lp = nb * span

    def residues(t):
        t = t.reshape(B, H, L, dil, dh).transpose(0, 1, 3, 2, 4)
        t = jnp.pad(t, ((0, 0), (0, 0), (0, 0), (0, Lp - L), (0, 0)))
        return t.reshape(B, H, dil, nb, span, dh)

    def with_prev(t):
        prev = jnp.pad(t, ((0, 0), (0, 0), (0, 0), (1, 0), (0, 0), (0, 0)))[:, :, :, :-1]
        return jnp.concatenate([prev, t], axis=4)

    qb = residues(q)
    kw, vw = with_prev(residues(k)), with_prev(residues(v))
    G = B * H * dil * nb
    s = bmm_nt(qb.reshape(G, span, dh), kw.reshape(G, 2 * span, dh))
    qi = jnp.arange(span)[:, None] + span
    kj = jnp.arange(2 * span)[None, :]
    dist = qi - kj
    in_band = (dist >= 0) & (dist <= span)
    has_prev = (jnp.arange(nb) > 0)[:, None, None] | (kj >= span)[None]
    valid = in_band[None] & has_prev
    bias = jnp.where(valid, 0.0, NEG_BIG).astype(F32)
    p, lse = softmax_lse(s.reshape(B * H * dil, nb, span, 2 * span), bias, dh ** -0.5)
    o = bmm_nn(p.reshape(G, span, 2 * span), vw.reshape(G, 2 * span, dh)).reshape(B, H, dil, nb, span, dh)
    lse = lse.reshape(B, H, dil, nb, span)

    def back(t):
        t = t.reshape((B, H, dil, Lp) + t.shape[5:])[:, :, :, :L]
        return jnp.moveaxis(t, 2, 3).reshape((B, H, S) + t.shape[4:])

    return back(o), back(lse)


def dilated_mixture(q, k, v):
    outs, lses = [], []
    for window, dil in DIL_BRANCHES:
        o, lse = dilated_branch(q, k, v, window, dil)
        outs.append(o)
        lses.append(lse)
    wts = jax.nn.softmax(jnp.stack(lses, axis=0), axis=0)
    return jnp.sum(wts[..., None] * jnp.stack(outs, axis=0), axis=0)


def token_shift(t, mu):
    prev = jnp.pad(t, ((0, 0), (1, 0), (0, 0)))[:, :-1]
    return t + (prev - t) * mu


def rwkv7(r, k, v, w_lr, a_lr, g_lr, w0, w_decay2, a0, w_a2, w_gate2, k_k, k_a, r_k, gn_g, gn_b):
    B, S, _ = r.shape
    H, n = RWKV_HEADS, RWKV_HEAD_DIM
    w = -jax.nn.softplus(-(w0 + mm(jnp.tanh(w_lr), w_decay2))) - 0.5
    decay = jnp.exp(-jnp.exp(w))
    a = jax.nn.sigmoid(a0 + mm(a_lr, w_a2))
    g = mm(jax.nn.sigmoid(g_lr), w_gate2)
    kk = (k * k_k).reshape(B, S, H, n)
    kk = kk / jnp.maximum(jnp.sqrt(jnp.sum(kk * kk, axis=-1, keepdims=True)), 1e-12)
    kk = kk.reshape(B, S, H * n)
    kh = k * (1.0 + (a - 1.0) * k_a)
    y = rwkv_scan(r, decay, kh, v, kk, kk * a).reshape(B, S, H, n)
    y = layer_norm(y, jnp.ones((n,), F32), jnp.zeros((n,), F32), RWKV_GN_EPS).reshape(B, S, H * n) * gn_g + gn_b
    bonus = jnp.sum((r * kh).reshape(B, S, H, n) * r_k, axis=-1, keepdims=True) * v.reshape(B, S, H, n)
    y = y + bonus.reshape(B, S, H * n)
    return y * g


def odd_mixer(x, p):
    B, S, _ = x.shape
    cos, sin = rope_tables(S, DIL_HEAD_DIM)
    widths = (3 * DIL_WIDTH,) + RWKV_IN_WIDTHS
    h = mm(x, _pad_cols(p['od_w_in'][0], widths))
    c_in = h[..., :3 * DIL_WIDTH]
    d_in = h[..., 3 * DIL_WIDTH:]
    q, k, v = [t.reshape(B, S, DIL_HEADS, DIL_HEAD_DIM).transpose(0, 2, 1, 3) for t in jnp.split(c_in, 3, axis=-1)]
    q, k = apply_rope(q, cos, sin), apply_rope(k, cos, sin)
    c_out = dilated_mixture(q, k, v).transpose(0, 2, 1, 3).reshape(B, S, DIL_WIDTH)
    mu = _pad_cols(p['od_rwkv_mu'][0], RWKV_IN_WIDTHS)
    sh = token_shift(d_in, mu)
    at = [0]
    for n in RWKV_IN_WIDTHS:
        at.append(at[-1] + _pad_to(n))
    r, kd, vd = [sh[..., at[i]:at[i + 1]] for i in range(3)]
    w_lr, a_lr, g_lr = [sh[..., at[i]:at[i + 1]] for i in range(3, 6)]
    d_out = rwkv7(r, kd, vd, w_lr, a_lr, g_lr, p['od_rwkv_w0'][0], _pad_rows(p['od_rwkv_w_decay2'][0], w_lr.shape[-1]),
                  p['od_rwkv_a0'][0], _pad_rows(p['od_rwkv_w_a2'][0], a_lr.shape[-1]), p['od_rwkv_w_gate2'][0],
                  p['od_rwkv_k_k'][0], p['od_rwkv_k_a'][0], p['od_rwkv_r_k'][0], p['od_rwkv_gn_g'][0], p['od_rwkv_gn_b'][0])
    return mm(jnp.concatenate([c_out, d_out], axis=-1), p['od_w_out'][0])


def cross_attention(x, mem, w_q, w_k, w_v, w_o):
    B, S, D = x.shape
    M = mem.shape[1]
    hd = D // XA_HEADS

    def heads(t, n):
        return t.reshape(B, n, XA_HEADS, hd).transpose(0, 2, 1, 3).reshape(B * XA_HEADS, n, hd)

    q, k, v = heads(mm(x, w_q), S), heads(mm(mem, w_k), M), heads(mm(mem, w_v), M)
    o = _heads_attention(q, k, v, jnp.zeros((S, M), F32), hd ** -0.5)
    o = o.reshape(B, XA_HEADS, S, hd).transpose(0, 2, 1, 3).reshape(B, S, D)
    return mm(o, w_o)


def swiglu(x, w_gate, w_up, w_down):
    return mm(jax.nn.silu(mm(x, w_gate)) * mm(x, w_up), w_down)


def forward(p, x, mem):
    h = x
    for layer in range(DEPTH):
        mix = even_mixer(h, p) if layer % 2 == 0 else odd_mixer(h, p)
        h = layer_norm(DEEPNORM_ALPHA * h + mix, p['ln_mix_g'][layer], p['ln_mix_b'][layer])
        xa = cross_attention(h, mem, p['xa_w_q'][layer], p['xa_w_k'][layer], p['xa_w_v'][layer], p['xa_w_o'][layer])
        h = layer_norm(DEEPNORM_ALPHA * h + xa, p['ln_xa_g'][layer], p['ln_xa_b'][layer])
        ff = swiglu(h, p['ffn_w_gate'][layer], p['ffn_w_up'][layer], p['ffn_w_down'][layer])
        h = layer_norm(DEEPNORM_ALPHA * h + ff, p['ln_ffn_g'][layer], p['ln_ffn_b'][layer])
    return h


def _flat_pack(arrays, length, dtype):
    flat = jnp.concatenate([a.reshape(-1).astype(dtype) for a in arrays])
    return jnp.pad(flat, (0, length - flat.shape[0]))


def _unpack(flat, shapes):
    out, at = [], 0
    for shp in shapes:
        n = 1
        for d in shp:
            n *= d
        out.append(flat[at:at + n].reshape(shp))
        at += n
    return out


def _shard_of(full, axis, s):
    n = full.shape[axis] // 4
    return lax.slice_in_dim(full, s * n, (s + 1) * n, axis=axis)


def kernel(x, mem, ev_w_in, ev_mla_q_norm, ev_mla_w_uq, ev_mla_kv_norm, ev_mla_w_ukv, ev_gla_w_gate2, ev_gla_b_gate, ev_gla_norm_g, ev_gla_norm_b, ev_w_out, od_w_in, od_rwkv_mu, od_rwkv_w0, od_rwkv_w_decay2, od_rwkv_a0, od_rwkv_w_a2, od_rwkv_w_gate2, od_rwkv_k_k, od_rwkv_k_a, od_rwkv_r_k, od_rwkv_gn_g, od_rwkv_gn_b, od_w_out, ln_mix_g, ln_mix_b, xa_w_q, xa_w_k, xa_w_v, xa_w_o, ln_xa_g, ln_xa_b, ffn_w_gate, ffn_w_up, ffn_w_down, ln_ffn_g, ln_ffn_b, loss_target, m_ev_w_in, m_ev_mla_q_norm, m_ev_mla_w_uq, m_ev_mla_kv_norm, m_ev_mla_w_ukv, m_ev_gla_w_gate2, m_ev_gla_b_gate, m_ev_gla_norm_g, m_ev_gla_norm_b, m_ev_w_out, m_od_w_in, m_od_rwkv_mu, m_od_rwkv_w0, m_od_rwkv_w_decay2, m_od_rwkv_a0, m_od_rwkv_w_a2, m_od_rwkv_w_gate2, m_od_rwkv_k_k, m_od_rwkv_k_a, m_od_rwkv_r_k, m_od_rwkv_gn_g, m_od_rwkv_gn_b, m_od_w_out, m_ln_mix_g, m_ln_mix_b, m_xa_w_q, m_xa_w_k, m_xa_w_v, m_xa_w_o, m_ln_xa_g, m_ln_xa_b, m_ffn_w_gate, m_ffn_w_up, m_ffn_w_down, m_ln_ffn_g, m_ln_ffn_b, v_ev_w_in, v_ev_mla_q_norm, v_ev_mla_w_uq, v_ev_mla_kv_norm, v_ev_mla_w_ukv, v_ev_gla_w_gate2, v_ev_gla_b_gate, v_ev_gla_norm_g, v_ev_gla_norm_b, v_ev_w_out, v_od_w_in, v_od_rwkv_mu, v_od_rwkv_w0, v_od_rwkv_w_decay2, v_od_rwkv_a0, v_od_rwkv_w_a2, v_od_rwkv_w_gate2, v_od_rwkv_k_k, v_od_rwkv_k_a, v_od_rwkv_r_k, v_od_rwkv_gn_g, v_od_rwkv_gn_b, v_od_w_out, v_ln_mix_g, v_ln_mix_b, v_xa_w_q, v_xa_w_k, v_xa_w_v, v_xa_w_o, v_ln_xa_g, v_ln_xa_b, v_ffn_w_gate, v_ffn_w_up, v_ffn_w_down, v_ln_ffn_g, v_ln_ffn_b):
    given = dict(locals())
    W = {n: given[n] for n in WEIGHT_NAMES}
    Mo = {n: given['m_' + n] for n in WEIGHT_NAMES}
    Vo = {n: given['v_' + n] for n in WEIGHT_NAMES}
    mat_names = [n for n in WEIGHT_NAMES if n in MATRICES]
    vec_names = [n for n in WEIGHT_NAMES if n in SHARDED_VECTORS]
    rep_names = list(REPLICATED)
    mat_shapes = [W[n].shape for n in mat_names]
    vec_shapes = [W[n].shape for n in vec_names]
    rep_shapes = [W[n].shape for n in rep_names]

    def count(shapes):
        total = 0
        for shp in shapes:
            n = 1
            for d in shp:
                n *= d
            total += n
        return total

    rc = -(-count(mat_shapes) // (PACK_CHUNKS * 2 * PACK_COLS * PACK_ROW_ALIGN)) * PACK_ROW_ALIGN
    mat_len = PACK_CHUNKS * 2 * rc * PACK_COLS
    rv = -(-count(vec_shapes) // (2 * SMALL_COLS * 8)) * 8
    vec_len = 2 * rv * SMALL_COLS
    rr = -(-count(rep_shapes) // (SMALL_COLS * 8)) * 8
    rep_len = rr * SMALL_COLS

    wmat = _flat_pack([W[n] for n in mat_names], mat_len, BF16).reshape(PACK_CHUNKS, 2, rc, PACK_COLS)
    gathered = jnp.stack([_all_gather_call(wmat[i]) for i in range(PACK_CHUNKS)], axis=1)
    gvec = _all_gather_call(_flat_pack([W[n] for n in vec_names], vec_len, F32).reshape(2, rv, SMALL_COLS))
    full = {}
    mat_parts = [_unpack(gathered[s].reshape(-1), mat_shapes) for s in range(4)]
    for i, n in enumerate(mat_names):
        full[n] = jnp.concatenate([mat_parts[s][i] for s in range(4)], axis=MATRICES[n])
    vec_parts = [_unpack(gvec[s].reshape(-1), vec_shapes) for s in range(4)]
    for i, n in enumerate(vec_names):
        full[n] = jnp.concatenate([vec_parts[s][i] for s in range(4)], axis=SHARDED_VECTORS[n])
    for n in rep_names:
        full[n] = W[n]

    B, S, D = x.shape
    y, vjp = jax.vjp(lambda p, xx: forward(p, xx, mem), full, x)
    dy, part = _loss_call(y.reshape(B * S, D), loss_target.reshape(B * S, D))
    loss = lax.psum(0.5 * part, ('x', 'y', 'c'))
    gfull, grad_x = vjp(dy.reshape(B, S, D))

    gmat = jnp.stack([_flat_pack([_shard_of(gfull[n], MATRICES[n], s) for n in mat_names], mat_len, BF16)
                      for s in range(4)]).reshape(4, PACK_CHUNKS, 2, rc, PACK_COLS)
    halves = [_sum_parts_call(_scatter_call(gmat[:, i])) for i in range(PACK_CHUNKS)]
    gshard = jnp.stack([_sibling_exchange_call(h) for h in halves]).reshape(-1, PACK_COLS)
    grep = _flat_pack([gfull[n] for n in rep_names], rep_len, F32)
    gsmall = jnp.stack([jnp.concatenate([
        _flat_pack([_shard_of(gfull[n], SHARDED_VECTORS[n], s) for n in vec_names], vec_len, F32), grep])
        for s in range(4)]).reshape(4, 1, 2 * rv + rr, SMALL_COLS)
    gsmall = _sum_parts_call(_scatter_call(jnp.concatenate([gsmall, gsmall], axis=1)))

    def small_pack(src):
        return jnp.concatenate([_flat_pack([src[n] for n in vec_names], vec_len, F32),
                                _flat_pack([src[n] for n in rep_names], rep_len, F32)]).reshape(-1, SMALL_COLS)

    groups = [{}, {}, {}, {}]
    for n, g in zip(mat_names, _unpack(gshard.reshape(-1), mat_shapes)):
        rows = (-1, g.shape[-1])
        outs = _adamw_call(W[n].reshape(rows), g.reshape(rows), Mo[n].reshape(rows), Vo[n].reshape(rows))
        for grp, val in zip(groups, (g,) + outs):
            grp[n] = val.reshape(g.shape)
    small = (gsmall,) + _adamw_call(small_pack(W), gsmall, small_pack(Mo), small_pack(Vo))
    for grp, sm in zip(groups, small):
        sm = sm.reshape(-1)
        grp.update(zip(vec_names, _unpack(sm[:vec_len], vec_shapes)))
        grp.update(zip(rep_names, _unpack(sm[vec_len:], rep_shapes)))
    return (loss, grad_x, *[grp[n] for grp in groups for n in WEIGHT_NAMES])
```

```python
import functools

import jax
import jax.numpy as jnp
from jax import lax
from jax.experimental import pallas as pl
from jax.experimental.pallas import tpu as pltpu

F32 = jnp.float32
BF16 = jnp.bfloat16
MESH = pl.DeviceIdType.MESH

ROPE_THETA = 10000.0
LN_EPS = 1e-5
RMS_EPS = 1e-6
DEPTH = 2
DEEPNORM_ALPHA = (2.0 * DEPTH) ** 0.25
MLA_HEADS, MLA_NOPE, MLA_ROPE, MLA_V, MLA_Q_RANK, MLA_KV_RANK = 8, 128, 64, 128, 512, 256
GLA_HEADS, GLA_DK, GLA_DV, GLA_GATE_RANK, GLA_TAU, GLA_CHUNK = 4, 128, 256, 16, 16.0, 64
DIL_HEADS, DIL_HEAD_DIM = 8, 128
DIL_BRANCHES = ((128, 1), (512, 4), (2048, 16))
RWKV_HEADS, RWKV_HEAD_DIM = 16, 64
RWKV_DECAY_RANK, RWKV_A_RANK, RWKV_GATE_RANK = 96, 96, 256
RWKV_GN_EPS = 64e-5
XA_HEADS = 4
DIL_WIDTH = DIL_HEADS * DIL_HEAD_DIM
RWKV_WIDTH = RWKV_HEADS * RWKV_HEAD_DIM
EVEN_IN_WIDTHS = (MLA_Q_RANK, MLA_KV_RANK, MLA_ROPE, GLA_HEADS * GLA_DK, GLA_HEADS * GLA_DK,
                  GLA_HEADS * GLA_DV, GLA_HEADS * GLA_DV, GLA_GATE_RANK)
RWKV_IN_WIDTHS = (RWKV_WIDTH, RWKV_WIDTH, RWKV_WIDTH, RWKV_DECAY_RANK, RWKV_A_RANK, RWKV_GATE_RANK)
ADAM_LR, ADAM_B1, ADAM_B2, ADAM_EPS, ADAM_WD, ADAM_STEP = 0.001, 0.9, 0.999, 1e-08, 0.01, 10

LANES = 128
SUBLANES = 8
VMEM_LIMIT_BYTES = 48 * 1024 * 1024
NEG_BIG = -1e30

PACK_COLS = 1024
PACK_CHUNKS = 1
SMALL_COLS = 128

MATRICES = {
    'ev_w_in': 2, 'ev_mla_w_uq': 2, 'ev_mla_w_ukv': 2, 'ev_gla_w_gate2': 2, 'ev_w_out': 1, 'od_w_in': 2,
    'od_rwkv_w_decay2': 2, 'od_rwkv_w_a2': 2, 'od_rwkv_w_gate2': 2, 'od_w_out': 1,
    'xa_w_q': 1, 'xa_w_k': 1, 'xa_w_v': 1, 'xa_w_o': 1, 'ffn_w_gate': 2, 'ffn_w_up': 2, 'ffn_w_down': 1,
}
SHARDED_VECTORS = {
    'od_rwkv_mu': 1, 'od_rwkv_w0': 1, 'od_rwkv_a0': 1, 'od_rwkv_k_k': 1, 'od_rwkv_k_a': 1,
    'od_rwkv_gn_g': 1, 'od_rwkv_gn_b': 1,
}
REPLICATED = ('ev_mla_q_norm', 'ev_mla_kv_norm', 'ev_gla_b_gate', 'ev_gla_norm_g', 'ev_gla_norm_b', 'od_rwkv_r_k',
              'ln_mix_g', 'ln_mix_b', 'ln_xa_g', 'ln_xa_b', 'ln_ffn_g', 'ln_ffn_b')
WEIGHT_NAMES = ('ev_w_in', 'ev_mla_q_norm', 'ev_mla_w_uq', 'ev_mla_kv_norm', 'ev_mla_w_ukv', 'ev_gla_w_gate2',
                'ev_gla_b_gate', 'ev_gla_norm_g', 'ev_gla_norm_b', 'ev_w_out', 'od_w_in', 'od_rwkv_mu', 'od_rwkv_w0',
                'od_rwkv_w_decay2', 'od_rwkv_a0', 'od_rwkv_w_a2', 'od_rwkv_w_gate2', 'od_rwkv_k_k', 'od_rwkv_k_a',
                'od_rwkv_r_k', 'od_rwkv_gn_g', 'od_rwkv_gn_b', 'od_w_out', 'ln_mix_g', 'ln_mix_b', 'xa_w_q', 'xa_w_k',
                'xa_w_v', 'xa_w_o', 'ln_xa_g', 'ln_xa_b', 'ffn_w_gate', 'ffn_w_up', 'ffn_w_down', 'ln_ffn_g', 'ln_ffn_b')


def _pick(n, cap, mult):
    d = (min(cap, n) // mult) * mult
    while d >= mult:
        if n % d == 0:
            return d
        d -= mult
    return n


def _params(semantics):
    return pltpu.CompilerParams(dimension_semantics=semantics, vmem_limit_bytes=VMEM_LIMIT_BYTES)


_DIMS = {(False, False): (((1,), (0,)), ((), ())), (False, True): (((1,), (1,)), ((), ())),
         (True, False): (((0,), (0,)), ((), ()))}


def _bmm(a, b, ta, tb, out_dtype=F32):
    G = a.shape[0]
    K, M = (a.shape[1], a.shape[2]) if ta else (a.shape[2], a.shape[1])
    N = b.shape[1] if tb else b.shape[2]
    assert (b.shape[2] if tb else b.shape[1]) == K and b.shape[0] == G
    tm, tn = _pick(M, 1024, LANES), _pick(N, 512, LANES)
    tk = K if K <= 2048 else _pick(K, 2048, LANES)
    nk = K // tk
    gb = 1
    if tm == M and tn == N and nk == 1:
        per = 4 * (M * K + K * N + M * N)
        gb = _pick(G, max(1, min(8, (2 << 20) // per)), 1)
    dims = _DIMS[(ta, tb)]

    def body(a_ref, b_ref, o_ref, *scratch):
        def prod(i):
            return lax.dot_general(a_ref[i].astype(BF16), b_ref[i].astype(BF16), dims, preferred_element_type=F32)

        if nk == 1:
            for i in range(gb):
                o_ref[i] = prod(i).astype(o_ref.dtype)
        else:
            acc_ref, = scratch
            k = pl.program_id(3)

            @pl.when(k == 0)
            def _():
                acc_ref[...] = jnp.zeros_like(acc_ref)

            for i in range(gb):
                acc_ref[i] += prod(i)

            @pl.when(k == nk - 1)
            def _():
                o_ref[...] = acc_ref[...].astype(o_ref.dtype)

    a_spec = (pl.BlockSpec((gb, tk, tm), lambda g, i, j, k: (g, k, i)) if ta
              else pl.BlockSpec((gb, tm, tk), lambda g, i, j, k: (g, i, k)))
    b_spec = (pl.BlockSpec((gb, tn, tk), lambda g, i, j, k: (g, j, k)) if tb
              else pl.BlockSpec((gb, tk, tn), lambda g, i, j, k: (g, k, j)))
    return pl.pallas_call(
        body, name='bmm_' + ('t' if ta else 'n') + ('t' if tb else 'n'),
        out_shape=jax.ShapeDtypeStruct((G, M, N), out_dtype),
        grid=(G // gb, M // tm, N // tn, nk),
        in_specs=[a_spec, b_spec],
        out_specs=pl.BlockSpec((gb, tm, tn), lambda g, i, j, k: (g, i, j)),
        scratch_shapes=[] if nk == 1 else [pltpu.VMEM((gb, tm, tn), F32)],
        compiler_params=_params(('parallel', 'parallel', 'parallel', 'arbitrary')),
    )(a, b)


def _like(x):
    return jnp.zeros((), x.dtype)


@jax.custom_vjp
def bmm_nn(a, b):
    return _bmm(a, b, False, False)


def _bmm_nn_fwd(a, b):
    ab, bb = a.astype(BF16), b.astype(BF16)
    return _bmm(ab, bb, False, False), (ab, bb, _like(a), _like(b))


def _bmm_nn_bwd(res, g):
    a, b, la, lb = res
    g = g.astype(BF16)
    return _bmm(g, b, False, True, la.dtype), _bmm(a, g, True, False, lb.dtype)


bmm_nn.defvjp(_bmm_nn_fwd, _bmm_nn_bwd)


@jax.custom_vjp
def bmm_nt(a, b):
    return _bmm(a, b, False, True)


def _bmm_nt_fwd(a, b):
    ab, bb = a.astype(BF16), b.astype(BF16)
    return _bmm(ab, bb, False, True), (ab, bb, _like(a), _like(b))


def _bmm_nt_bwd(res, g):
    a, b, la, lb = res
    g = g.astype(BF16)
    return _bmm(g, b, False, False, la.dtype), _bmm(g, a, True, False, lb.dtype)


bmm_nt.defvjp(_bmm_nt_fwd, _bmm_nt_bwd)


def mm(x, w):
    lead = x.shape[:-1]
    out = bmm_nn(x.reshape(1, -1, x.shape[-1]), w[None])
    return out.reshape(lead + (w.shape[1],))


def _norm_stats(x, center, eps):
    if center:
        xc = x - jnp.mean(x, axis=-1, keepdims=True)
    else:
        xc = x
    rstd = lax.rsqrt(jnp.mean(xc * xc, axis=-1, keepdims=True) + eps)
    return xc * rstd, rstd


def _norm_fwd_call(x, g, b, center, eps):
    R, C = x.shape
    tr = _pick(R, max(8, (1 << 19) // C), 8)

    def body(x_ref, g_ref, b_ref, y_ref):
        xhat, _ = _norm_stats(x_ref[...], center, eps)
        y_ref[...] = xhat * g_ref[...] + b_ref[...]

    row = pl.BlockSpec((tr, C), lambda i: (i, 0))
    vec = pl.BlockSpec((1, C), lambda i: (0, 0))
    return pl.pallas_call(
        body, name='norm_fwd', out_shape=jax.ShapeDtypeStruct((R, C), F32), grid=(R // tr,),
        in_specs=[row, vec, vec], out_specs=row, compiler_params=_params(('parallel',)),
    )(x, g, b)


def _norm_bwd_call(x, g, dy, center, eps):
    R, C = x.shape
    tr = _pick(R, max(8, (1 << 19) // C), 8)

    def body(x_ref, g_ref, dy_ref, dx_ref, dg_ref, db_ref):
        @pl.when(pl.program_id(0) == 0)
        def _():
            dg_ref[...] = jnp.zeros_like(dg_ref)
            db_ref[...] = jnp.zeros_like(db_ref)

        xhat, rstd = _norm_stats(x_ref[...], center, eps)
        dy = dy_ref[...]
        dxh = dy * g_ref[...]
        proj = xhat * jnp.mean(dxh * xhat, axis=-1, keepdims=True)
        if center:
            dx_ref[...] = rstd * (dxh - jnp.mean(dxh, axis=-1, keepdims=True) - proj)
        else:
            dx_ref[...] = rstd * (dxh - proj)
        dg_ref[...] += jnp.sum(dy * xhat, axis=0, keepdims=True)
        db_ref[...] += jnp.sum(dy, axis=0, keepdims=True)

    row = pl.BlockSpec((tr, C), lambda i: (i, 0))
    vec = pl.BlockSpec((1, C), lambda i: (0, 0))
    return pl.pallas_call(
        body, name='norm_bwd',
        out_shape=(jax.ShapeDtypeStruct((R, C), F32), jax.ShapeDtypeStruct((1, C), F32), jax.ShapeDtypeStruct((1, C), F32)),
        grid=(R // tr,), in_specs=[row, vec, row], out_specs=(row, vec, vec), compiler_params=_params(('arbitrary',)),
    )(x, g, dy)


@functools.partial(jax.custom_vjp, nondiff_argnums=(3, 4))
def _norm2d(x, g, b, center, eps):
    return _norm_fwd_call(x, g, b, center, eps)


def _norm2d_fwd(x, g, b, center, eps):
    return _norm_fwd_call(x, g, b, center, eps), (x, g)


def _norm2d_bwd(center, eps, res, dy):
    x, g = res
    return _norm_bwd_call(x, g, dy, center, eps)


_norm2d.defvjp(_norm2d_fwd, _norm2d_bwd)


def layer_norm(x, g, b, eps=LN_EPS):
    C = x.shape[-1]
    return _norm2d(x.reshape(-1, C), g.reshape(1, C), b.reshape(1, C), True, eps).reshape(x.shape)


def rms_norm(x, g):
    C = x.shape[-1]
    return _norm2d(x.reshape(-1, C), g.reshape(1, C), jnp.zeros((1, C), F32), False, RMS_EPS).reshape(x.shape)


def _softmax_fwd_call(s, bias, scale):
    G1, G2, R, C = s.shape
    tr = _pick(R, max(8, (1 << 19) // C), 8)

    def body(s_ref, bias_ref, p_ref, lse_ref):
        z = s_ref[0, 0] * scale + bias_ref[0]
        m = jnp.max(z, axis=-1, keepdims=True)
        e = jnp.exp(z - m)
        den = jnp.sum(e, axis=-1, keepdims=True)
        p_ref[0, 0] = e / den
        lse_ref[0, 0] = m + jnp.log(den)

    blk = pl.BlockSpec((1, 1, tr, C), lambda a, b, r: (a, b, r, 0))
    col = pl.BlockSpec((1, 1, tr, 1), lambda a, b, r: (a, b, r, 0))
    return pl.pallas_call(
        body, name='softmax_fwd',
        out_shape=(jax.ShapeDtypeStruct(s.shape, F32), jax.ShapeDtypeStruct((G1, G2, R, 1), F32)),
        grid=(G1, G2, R // tr), in_specs=[blk, pl.BlockSpec((1, tr, C), lambda a, b, r: (b, r, 0))],
        out_specs=(blk, col), compiler_params=_params(('parallel', 'parallel', 'parallel')),
    )(s, bias)


def _softmax_bwd_call(p, dp, dlse, scale):
    G1, G2, R, C = p.shape
    tr = _pick(R, max(8, (1 << 19) // C), 8)

    def body(p_ref, dp_ref, dlse_ref, ds_ref):
        p = p_ref[0, 0]
        dp = dp_ref[0, 0]
        inner = jnp.sum(dp * p, axis=-1, keepdims=True)
        ds_ref[0, 0] = (p * (dp - inner + dlse_ref[0, 0])) * scale

    blk = pl.BlockSpec((1, 1, tr, C), lambda a, b, r: (a, b, r, 0))
    col = pl.BlockSpec((1, 1, tr, 1), lambda a, b, r: (a, b, r, 0))
    return pl.pallas_call(
        body, name='softmax_bwd', out_shape=jax.ShapeDtypeStruct(p.shape, F32),
        grid=(G1, G2, R // tr), in_specs=[blk, blk, col], out_specs=blk,
        compiler_params=_params(('parallel', 'parallel', 'parallel')),
    )(p, dp, dlse)


@functools.partial(jax.custom_vjp, nondiff_argnums=(2,))
def softmax_lse(s, bias, scale):
    return _softmax_fwd_call(s, bias, scale)


def _softmax_lse_fwd(s, bias, scale):
    p, lse = _softmax_fwd_call(s, bias, scale)
    return (p, lse), (p, bias)


def _softmax_lse_bwd(scale, res, cts):
    p, bias = res
    dp, dlse = cts
    return _softmax_bwd_call(p, dp, dlse, scale), jnp.zeros_like(bias)


softmax_lse.defvjp(_softmax_lse_fwd, _softmax_lse_bwd)


def _dot(a, b, dims):
    return lax.dot_general(a.astype(BF16), b.astype(BF16), dims, preferred_element_type=F32)


_NN, _NT, _TN = _DIMS[(False, False)], _DIMS[(False, True)], _DIMS[(True, False)]


def _gla_fwd_call(q, k, v, dec):
    G, nc, C, dk = q.shape
    dv = v.shape[-1]

    def body(q_ref, k_ref, v_ref, dec_ref, o_ref, st_ref, state):
        @pl.when(pl.program_id(1) == 0)
        def _():
            state[...] = jnp.zeros_like(state)

        s = state[...]
        st_ref[0, 0] = s
        o_ref[0, 0] = _dot(q_ref[0, 0], s, _NN)
        state[...] = s * dec_ref[0, 0] + _dot(k_ref[0, 0], v_ref[0, 0], _TN)

    def spec(r, c):
        return pl.BlockSpec((1, 1, r, c), lambda g, t: (g, t, 0, 0))

    return pl.pallas_call(
        body, name='gla_scan_fwd',
        out_shape=(jax.ShapeDtypeStruct((G, nc, C, dv), F32), jax.ShapeDtypeStruct((G, nc, dk, dv), F32)),
        grid=(G, nc), in_specs=[spec(C, dk), spec(C, dk), spec(C, dv), spec(dk, 1)],
        out_specs=(spec(C, dv), spec(dk, dv)), scratch_shapes=[pltpu.VMEM((dk, dv), F32)],
        compiler_params=_params(('parallel', 'arbitrary')),
    )(q, k, v, dec)


def _gla_bwd_call(q, k, v, dec, states, do):
    G, nc, C, dk = q.shape
    dv = v.shape[-1]

    def body(q_ref, k_ref, v_ref, dec_ref, st_ref, do_ref, dq_ref, dk_ref, dv_ref, ddec_ref, dstate):
        @pl.when(pl.program_id(1) == 0)
        def _():
            dstate[...] = jnp.zeros_like(dstate)

        s = st_ref[0, 0]
        d = dstate[...]
        do = do_ref[0, 0]
        dq_ref[0, 0] = _dot(do, s, _NT)
        dk_ref[0, 0] = _dot(v_ref[0, 0], d, _NT)
        dv_ref[0, 0] = _dot(k_ref[0, 0], d, _NN)
        ddec_ref[0, 0] = jnp.sum(s * d, axis=1, keepdims=True)
        dstate[...] = d * dec_ref[0, 0] + _dot(q_ref[0, 0], do, _TN)

    def spec(r, c):
        return pl.BlockSpec((1, 1, r, c), lambda g, t: (g, nc - 1 - t, 0, 0))

    return pl.pallas_call(
        body, name='gla_scan_bwd',
        out_shape=(jax.ShapeDtypeStruct(q.shape, F32), jax.ShapeDtypeStruct(k.shape, F32),
                   jax.ShapeDtypeStruct(v.shape, F32), jax.ShapeDtypeStruct(dec.shape, F32)),
        grid=(G, nc), in_specs=[spec(C, dk), spec(C, dk), spec(C, dv), spec(dk, 1), spec(dk, dv), spec(C, dv)],
        out_specs=(spec(C, dk), spec(C, dk), spec(C, dv), spec(dk, 1)), scratch_shapes=[pltpu.VMEM((dk, dv), F32)],
        compiler_params=_params(('parallel', 'arbitrary')),
    )(q, k, v, dec, states, do)


@jax.custom_vjp
def gla_scan(q, k, v, dec):
    return _gla_fwd_call(q, k, v, dec)[0]


def _gla_scan_fwd(q, k, v, dec):
    o, states = _gla_fwd_call(q, k, v, dec)
    return o, (q, k, v, dec, states)


def _gla_scan_bwd(res, do):
    return _gla_bwd_call(*res, do)


gla_scan.defvjp(_gla_scan_fwd, _gla_scan_bwd)


RWKV_PAIRS_PER_STEP = 4
RWKV_TIME_BLOCK = 64
RN = RWKV_HEAD_DIM


def _rwkv_consts():
    row = lax.broadcasted_iota(jnp.int32, (RN, LANES), 0)
    lane = lax.broadcasted_iota(jnp.int32, (RN, LANES), 1)
    diag = (lane % RN == row).astype(F32)
    r2 = lax.broadcasted_iota(jnp.int32, (LANES, LANES), 0)
    l2 = lax.broadcasted_iota(jnp.int32, (LANES, LANES), 1)
    seg = (r2 // RN == l2 // RN).astype(BF16)
    return diag, seg


def _stage(lhs_ref, slot, p):
    hi = p.astype(BF16)
    lhs_ref[pl.ds(slot * LANES, RN), :] = hi
    lhs_ref[pl.ds(slot * LANES + RN, RN), :] = (p - hi.astype(F32)).astype(BF16)


def _seg_sums(lhs_ref, nslots, seg):
    res = jnp.dot(lhs_ref[pl.ds(0, nslots * LANES), :], seg, preferred_element_type=F32)
    return [res[i * LANES:i * LANES + RN] + res[i * LANES + RN:(i + 1) * LANES] for i in range(nslots)]


def _rwkv_blocks(B, S, C):
    npairs = C // LANES
    pp = RWKV_PAIRS_PER_STEP if npairs % RWKV_PAIRS_PER_STEP == 0 else 1
    T = _pick(S, RWKV_TIME_BLOCK, 8)
    return npairs, pp, T


def _rwkv_fwd_call(r, w, k, v, kk, b):
    B, S, C = r.shape
    npairs, pp, T = _rwkv_blocks(B, S, C)
    G = SUBLANES

    def body(r_ref, w_ref, k_ref, v_ref, kk_ref, b_ref, y_ref, sall_ref, state, step_lhs, v_lhs, y_lhs):
        @pl.when(pl.program_id(2) == 0)
        def _():
            state[...] = jnp.zeros_like(state)

        diag, seg = _rwkv_consts()
        rowid = lax.broadcasted_iota(jnp.int32, (SUBLANES, LANES), 0)

        def group(t8, carry):
            rows = pl.ds(pl.multiple_of(t8 * G, G), G)
            sls = [slice(p * LANES, (p + 1) * LANES) for p in range(pp)]
            ops = [[ref[0, rows, sl] for ref in (r_ref, w_ref, k_ref, v_ref, kk_ref, b_ref)] for sl in sls]
            for j in range(G):
                for p in range(pp):
                    _stage(v_lhs, j * pp + p, diag * ops[p][3][j:j + 1])
            vcols = _seg_sums(v_lhs, G * pp, seg)
            s = list(carry)
            for j in range(G):
                for p in range(pp):
                    sall_ref[0, p, t8 * G + j] = s[p]
                    _stage(step_lhs, p, s[p] * ops[p][4][j:j + 1])
                sas = _seg_sums(step_lhs, pp, seg)
                for p in range(pp):
                    rt, wt, kt, _, _, bt = ops[p]
                    s[p] = s[p] * wt[j:j + 1] - sas[p] * bt[j:j + 1] + vcols[j * pp + p] * kt[j:j + 1]
                    _stage(y_lhs, j * pp + p, s[p] * rt[j:j + 1])
            ycols = _seg_sums(y_lhs, G * pp, seg)
            for p in range(pp):
                ytile = jnp.zeros((SUBLANES, LANES), F32)
                for j in range(G):
                    ytile = jnp.where(rowid == j, jnp.sum(diag * ycols[j * pp + p], axis=0, keepdims=True), ytile)
                y_ref[0, rows, sls[p]] = ytile
            return tuple(s)

        final = lax.fori_loop(0, T // G, group, tuple(state[p] for p in range(pp)))
        for p in range(pp):
            state[p] = final[p]

    seq = pl.BlockSpec((1, T, pp * LANES), lambda bi, g, t: (bi, t, g))
    return pl.pallas_call(
        body, name='rwkv_scan_fwd',
        out_shape=(jax.ShapeDtypeStruct((B, S, C), F32), jax.ShapeDtypeStruct((B, npairs, S, RN, LANES), F32)),
        grid=(B, npairs // pp, S // T), in_specs=[seq] * 6,
        out_specs=(seq, pl.BlockSpec((1, pp, T, RN, LANES), lambda bi, g, t: (bi, g, t, 0, 0))),
        scratch_shapes=[pltpu.VMEM((pp, RN, LANES), F32), pltpu.VMEM((pp * LANES, LANES), BF16),
                        pltpu.VMEM((G * pp * LANES, LANES), BF16), pltpu.VMEM((G * pp * LANES, LANES), BF16)],
        compiler_params=_params(('parallel', 'parallel', 'arbitrary')),
    )(r, w, k, v, kk, b)


def _rwkv_bwd_call(r, w, k, v, kk, b, sall, dy):
    B, S, C = r.shape
    npairs, pp, T = _rwkv_blocks(B, S, C)
    nt = S // T
    G = SUBLANES

    def body(r_ref, w_ref, k_ref, v_ref, kk_ref, b_ref, sall_ref, dy_ref,
             dr_ref, dw_ref, dk_ref, dv_ref, dkk_ref, db_ref, dstate, step_lhs, pre_lhs, dv_lhs):
        @pl.when(pl.program_id(2) == 0)
        def _():
            dstate[...] = jnp.zeros_like(dstate)

        diag, seg = _rwkv_consts()
        rowid = lax.broadcasted_iota(jnp.int32, (SUBLANES, LANES), 0)

        def colsum(z):
            return jnp.sum(z, axis=0, keepdims=True)

        def group(i, carry):
            t8 = T // G - 1 - i
            rows = pl.ds(pl.multiple_of(t8 * G, G), G)
            sls = [slice(p * LANES, (p + 1) * LANES) for p in range(pp)]
            ops = [[ref[0, rows, sl] for ref in (r_ref, w_ref, k_ref, v_ref, kk_ref, b_ref, dy_ref)] for sl in sls]
            for j in range(G):
                for p in range(pp):
                    _stage(pre_lhs, j * pp + p, sall_ref[0, p, t8 * G + j] * ops[p][4][j:j + 1])
                    _stage(pre_lhs, (G + j) * pp + p, diag * ops[p][3][j:j + 1])
                    _stage(pre_lhs, (2 * G + j) * pp + p, diag * ops[p][6][j:j + 1])
            pre = _seg_sums(pre_lhs, 3 * G * pp, seg)
            ds = list(carry)
            tiles = [[jnp.zeros((SUBLANES, LANES), F32) for _ in range(5)] for _ in range(pp)]
            for j in reversed(range(G)):
                d = []
                for p in range(pp):
                    rt, _, kt, _, _, bt, _ = ops[p]
                    d.append(ds[p] + pre[(2 * G + j) * pp + p] * rt[j:j + 1])
                    _stage(step_lhs, p, d[p] * bt[j:j + 1])
                    _stage(dv_lhs, j * pp + p, d[p] * kt[j:j + 1])
                dsas = _seg_sums(step_lhs, pp, seg)
                for p in range(pp):
                    rt, wt, kt, _, kkt, bt, _ = ops[p]
                    s = sall_ref[0, p, t8 * G + j]
                    sa, vcol, dycol = -pre[j * pp + p], pre[(G + j) * pp + p], pre[(2 * G + j) * pp + p]
                    s2 = s * wt[j:j + 1] + sa * bt[j:j + 1] + vcol * kt[j:j + 1]
                    vals = (colsum(s2 * dycol), colsum(d[p] * s), colsum(d[p] * vcol), -colsum(s * dsas[p]),
                            colsum(d[p] * sa))
                    tiles[p] = [jnp.where(rowid == j, val, tile) for val, tile in zip(vals, tiles[p])]
                    ds[p] = d[p] * wt[j:j + 1] - dsas[p] * kkt[j:j + 1]
            dvcols = _seg_sums(dv_lhs, G * pp, seg)
            for p in range(pp):
                dvt = jnp.zeros((SUBLANES, LANES), F32)
                for j in range(G):
                    dvt = jnp.where(rowid == j, colsum(diag * dvcols[j * pp + p]), dvt)
                dv_ref[0, rows, sls[p]] = dvt
                for ref, tile in zip((dr_ref, dw_ref, dk_ref, dkk_ref, db_ref), tiles[p]):
                    ref[0, rows, sls[p]] = tile
            return tuple(ds)

        final = lax.fori_loop(0, T // G, group, tuple(dstate[p] for p in range(pp)))
        for p in range(pp):
            dstate[p] = final[p]

    seq = pl.BlockSpec((1, T, pp * LANES), lambda bi, g, t: (bi, nt - 1 - t, g))
    sds = jax.ShapeDtypeStruct((B, S, C), F32)
    return pl.pallas_call(
        body, name='rwkv_scan_bwd', out_shape=(sds,) * 6,
        grid=(B, npairs // pp, nt),
        in_specs=[seq] * 6 + [pl.BlockSpec((1, pp, T, RN, LANES), lambda bi, g, t: (bi, g, nt - 1 - t, 0, 0)), seq],
        out_specs=(seq,) * 6,
        scratch_shapes=[pltpu.VMEM((pp, RN, LANES), F32), pltpu.VMEM((pp * LANES, LANES), BF16),
                        pltpu.VMEM((3 * G * pp * LANES, LANES), BF16), pltpu.VMEM((G * pp * LANES, LANES), BF16)],
        compiler_params=_params(('parallel', 'parallel', 'arbitrary')),
    )(r, w, k, v, kk, b, sall, dy)


@jax.custom_vjp
def rwkv_scan(r, w, k, v, kk, b):
    return _rwkv_fwd_call(r, w, k, v, kk, b)[0]


def _rwkv_scan_fwd(r, w, k, v, kk, b):
    y, sall = _rwkv_fwd_call(r, w, k, v, kk, b)
    return y, (r, w, k, v, kk, b, sall)


def _rwkv_scan_bwd(res, dy):
    return _rwkv_bwd_call(*res, dy)


rwkv_scan.defvjp(_rwkv_scan_fwd, _rwkv_scan_bwd)


def _loss_call(y, target):
    R, D = y.shape
    tr = _pick(R, max(8, (1 << 19) // D), 8)

    def body(y_ref, t_ref, dy_ref, part_ref):
        @pl.when(pl.program_id(0) == 0)
        def _():
            part_ref[...] = jnp.zeros_like(part_ref)

        diff = y_ref[...] - t_ref[...]
        dy_ref[...] = diff / D
        part_ref[...] += jnp.sum(jnp.mean(diff * diff, axis=-1, keepdims=True), axis=0, keepdims=True)

    row = pl.BlockSpec((tr, D), lambda i: (i, 0))
    dy, part = pl.pallas_call(
        body, name='loss_head',
        out_shape=(jax.ShapeDtypeStruct((R, D), F32), jax.ShapeDtypeStruct((1, 1), F32)),
        grid=(R // tr,), in_specs=[row, row], out_specs=(row, pl.BlockSpec((1, 1), lambda i: (0, 0))),
        compiler_params=_params(('arbitrary',)),
    )(y, target)
    return dy, part[0, 0]


def _sum_parts_call(parts):
    P, R, C = parts.shape
    tr = _pick(R, 512, 16)

    def body(p_ref, o_ref):
        acc = p_ref[0].astype(F32)
        for i in range(1, P):
            acc = acc + p_ref[i].astype(F32)
        o_ref[...] = acc

    return pl.pallas_call(
        body, name='sum_parts', out_shape=jax.ShapeDtypeStruct((R, C), F32), grid=(R // tr,),
        in_specs=[pl.BlockSpec((P, tr, C), lambda i: (0, i, 0))], out_specs=pl.BlockSpec((tr, C), lambda i: (i, 0)),
        compiler_params=_params(('parallel',)),
    )(parts)


def _adamw_call(w, g, m, v):
    R, C = w.shape
    tr = _pick(R, max(8, (1 << 18) // C), 8)

    def body(w_ref, g_ref, m_ref, v_ref, d_ref, nm_ref, nv_ref):
        g = g_ref[...]
        m = ADAM_B1 * m_ref[...] + (1.0 - ADAM_B1) * g
        v = ADAM_B2 * v_ref[...] + (1.0 - ADAM_B2) * (g * g)
        m_hat = m / (1.0 - ADAM_B1 ** ADAM_STEP)
        v_hat = v / (1.0 - ADAM_B2 ** ADAM_STEP)
        d_ref[...] = -ADAM_LR * (m_hat / (jnp.sqrt(v_hat) + ADAM_EPS) + ADAM_WD * w_ref[...])
        nm_ref[...] = m
        nv_ref[...] = v

    row = pl.BlockSpec((tr, C), lambda i: (i, 0))
    sds = jax.ShapeDtypeStruct((R, C), F32)
    return pl.pallas_call(
        body, name='adamw', out_shape=(sds, sds, sds), grid=(R // tr,),
        in_specs=[row] * 4, out_specs=(row,) * 3, compiler_params=_params(('parallel',)),
    )(w, g, m, v)


ANY = pl.BlockSpec(memory_space=pl.ANY)


def _place():
    return lax.axis_index('x'), lax.axis_index('y'), lax.axis_index('c')


COPY_SPLIT = 8
BF16_TILE_ROWS = 16
PACK_ROW_ALIGN = COPY_SPLIT * BF16_TILE_ROWS


def _row_split(rows):
    if rows % PACK_ROW_ALIGN == 0:
        return COPY_SPLIT, rows // COPY_SPLIT
    return 1, rows


def _all_gather_call(pack):
    _, R, C = pack.shape
    ns, rs = _row_split(R)

    def body(pk_ref, out_ref, send_sems, recv_sems):
        x, y, c = _place()
        chips = [(1 - x, y), (x, 1 - y), (1 - x, 1 - y)]
        me = 2 * x + y

        def copy(k, i, src, dst, to):
            rows = pl.ds(i * rs, rs)
            return pltpu.make_async_remote_copy(src_ref=src.at[rows], dst_ref=dst.at[rows], send_sem=send_sems.at[k * ns + i],
                                                recv_sem=recv_sems.at[k * ns + i], device_id=to, device_id_type=MESH)

        first = [copy(j, i, pk_ref.at[c], out_ref.at[me, c], (px, py, c))
                 for j, (px, py) in enumerate(chips) for i in range(ns)]
        for cp in first:
            cp.start()
        passed = []
        for i in range(ns):
            for j, (px, py) in enumerate(chips):
                landed = out_ref.at[2 * px + py, c]
                copy(j, i, landed, landed, (px, py, c)).wait_recv()
                fwd = copy(3 + j, i, landed, landed, (x, y, 1 - c))
                fwd.start()
                passed.append(fwd)
        for i in range(ns):
            for j, (px, py) in enumerate(chips):
                other = out_ref.at[2 * px + py, 1 - c]
                copy(3 + j, i, other, other, (x, y, 1 - c)).wait_recv()
        for cp in first + passed:
            cp.wait_send()

    others = pl.pallas_call(
        body, name='all_gather', out_shape=jax.ShapeDtypeStruct((4, 2, R, C), pack.dtype),
        in_specs=[ANY], out_specs=ANY,
        scratch_shapes=[pltpu.SemaphoreType.DMA((6 * ns,)), pltpu.SemaphoreType.DMA((6 * ns,))],
    )(pack)
    x, y, _ = _place()
    return lax.dynamic_update_slice(others, pack[None], (2 * x + y, 0, 0, 0))


def _scatter_call(src):
    _, _, R, C = src.shape
    ns, rs = _row_split(R)

    def body(src_ref, out_ref, send_sems, recv_sems):
        x, y, c = _place()
        me = 4 * x + 2 * y + c
        peers = []
        for rel in range(1, 8):
            px = 1 - x if rel & 4 else x
            py = 1 - y if rel & 2 else y
            pc = 1 - c if rel & 1 else c
            peers.append((rel - 1, px, py, pc))

        def copy(k, i, src, dst, to):
            rows = pl.ds(i * rs, rs)
            return pltpu.make_async_remote_copy(src_ref=src.at[rows], dst_ref=dst.at[rows], send_sem=send_sems.at[k * ns + i],
                                                recv_sem=recv_sems.at[k * ns + i], device_id=to, device_id_type=MESH)

        sends = [copy(k, i, src_ref.at[2 * px + py, pc], out_ref.at[me], (px, py, pc))
                 for i in range(ns) for k, px, py, pc in peers]
        for cp in sends:
            cp.start()
        for i in range(ns):
            for k, px, py, pc in peers:
                slot = out_ref.at[4 * px + 2 * py + pc]
                copy(k, i, slot, slot, (px, py, pc)).wait_recv()
        for cp in sends:
            cp.wait_send()

    others = pl.pallas_call(
        body, name='scatter_parts', out_shape=jax.ShapeDtypeStruct((8, R, C), src.dtype),
        in_specs=[ANY], out_specs=ANY,
        scratch_shapes=[pltpu.SemaphoreType.DMA((7 * ns,)), pltpu.SemaphoreType.DMA((7 * ns,))],
    )(src)
    x, y, c = _place()
    own = lax.dynamic_slice(src, (2 * x + y, c, 0, 0), (1, 1, R, C)).reshape(1, R, C)
    return lax.dynamic_update_slice(others, own, (4 * x + 2 * y + c, 0, 0))


def _sibling_exchange_call(half):
    R, C = half.shape
    ns, rs = _row_split(R)

    def body(h_ref, other_ref, send_sems, recv_sems):
        x, y, c = _place()

        def copy(i):
            rows = pl.ds(i * rs, rs)
            return pltpu.make_async_remote_copy(src_ref=h_ref.at[rows], dst_ref=other_ref.at[rows], send_sem=send_sems.at[i],
                                                recv_sem=recv_sems.at[i], device_id=(x, y, 1 - c), device_id_type=MESH)

        sends = [copy(i) for i in range(ns)]
        for cp in sends:
            cp.start()
        for cp in sends:
            cp.wait_recv()
        for cp in sends:
            cp.wait_send()

    other = pl.pallas_call(
        body, name='sibling_exchange', out_shape=jax.ShapeDtypeStruct((R, C), half.dtype),
        in_specs=[ANY], out_specs=ANY,
        scratch_shapes=[pltpu.SemaphoreType.DMA((ns,)), pltpu.SemaphoreType.DMA((ns,))],
    )(half)
    c = lax.axis_index('c')
    return jnp.where(c == 0, jnp.stack([half, other]), jnp.stack([other, half]))


def rope_tables(seq_len, dim):
    inv = ROPE_THETA ** (-jnp.arange(0, dim, 2, dtype=F32) / dim)
    ang = jnp.arange(seq_len, dtype=F32)[:, None] * inv[None, :]
    return jnp.cos(ang), jnp.sin(ang)


def apply_rope(x, cos, sin):
    x1, x2 = jnp.split(x, 2, axis=-1)
    return jnp.concatenate([x1 * cos - x2 * sin, x1 * sin + x2 * cos], axis=-1)


def _pad_to(n):
    return -(-n // LANES) * LANES


def _pad_cols(w, widths):
    parts, at = [], 0
    for n in widths:
        parts.append(jnp.pad(w[..., at:at + n], [(0, 0)] * (w.ndim - 1) + [(0, _pad_to(n) - n)]))
        at += n
    return jnp.concatenate(parts, axis=-1)


def _split_padded(t, widths):
    out, at = [], 0
    for n in widths:
        out.append(t[..., at:at + n])
        at += _pad_to(n)
    return out


def _pad_rows(w, rows):
    return jnp.pad(w, ((0, rows - w.shape[0]), (0, 0)))


def _heads_attention(q, k, v, bias, scale):
    s = bmm_nt(q, k)
    p, _ = softmax_lse(s[:, None], bias[None], scale)
    return bmm_nn(p[:, 0], v)


def gla(q, k, v, r, gate_lr, w_gate2, b_gate, norm_g, norm_b):
    B, S, _ = q.shape
    H, dk, dv, C = GLA_HEADS, GLA_DK, GLA_DV, GLA_CHUNK
    nc = S // C
    log_a = jax.nn.log_sigmoid(mm(gate_lr, w_gate2) + b_gate) / GLA_TAU

    def chunks(t, d):
        return t.reshape(B, nc, C, H, d).transpose(0, 3, 1, 2, 4)

    qc = chunks(q, dk) * (dk ** -0.5)
    kc = chunks(k, dk)
    vc = chunks(v, dv)
    b = jnp.cumsum(chunks(log_a, dk), axis=3)
    b_last = b[:, :, :, -1:, :]
    q_dec = qc * jnp.exp(b)
    k_inv = kc * jnp.exp(-b)
    k_end = kc * jnp.exp(b_last - b)
    causal = jnp.tril(jnp.ones((C, C), dtype=bool))
    G = B * H
    att = bmm_nt(q_dec.reshape(G * nc, C, dk), k_inv.reshape(G * nc, C, dk))
    att = jnp.where(causal, att, 0.0)
    o_intra = bmm_nn(att, vc.reshape(G * nc, C, dv)).reshape(B, H, nc, C, dv)
    dec = jnp.exp(b_last[:, :, :, 0, :]).reshape(G, nc, dk, 1)
    o_inter = gla_scan(q_dec.reshape(G, nc, C, dk), k_end.reshape(G, nc, C, dk), vc.reshape(G, nc, C, dv), dec)
    o = o_intra + o_inter.reshape(B, H, nc, C, dv)
    o = o.transpose(0, 2, 3, 1, 4).reshape(B, S, H, dv)
    o = layer_norm(o, norm_g, norm_b).reshape(B, S, H * dv)
    return o * jax.nn.silu(r)


def even_mixer(x, p):
    B, S, _ = x.shape
    H = MLA_HEADS
    cos, sin = rope_tables(S, MLA_ROPE)
    z = mm(x, _pad_cols(p['ev_w_in'][0], EVEN_IN_WIDTHS))
    c_q, c_kv, k_pe, q_g, k_g, v_g, r_g, _ = _split_padded(z, EVEN_IN_WIDTHS)
    lr_at = sum(_pad_to(n) for n in EVEN_IN_WIDTHS[:-1])
    lr_g = z[..., lr_at:]
    q = mm(rms_norm(c_q, p['ev_mla_q_norm'][0]), p['ev_mla_w_uq'][0])
    q = q.reshape(B, S, H, MLA_NOPE + MLA_ROPE).transpose(0, 2, 1, 3)
    kv = mm(rms_norm(c_kv, p['ev_mla_kv_norm'][0]), p['ev_mla_w_ukv'][0])
    kv = kv.reshape(B, S, H, MLA_NOPE + MLA_V).transpose(0, 2, 1, 3)
    q_pe = apply_rope(q[..., MLA_NOPE:], cos, sin)
    k_pe = jnp.broadcast_to(apply_rope(k_pe[:, None], cos, sin), (B, H, S, MLA_ROPE))
    qf = jnp.concatenate([q[..., :MLA_NOPE], q_pe], axis=-1)
    kf = jnp.concatenate([kv[..., :MLA_NOPE], k_pe], axis=-1)
    pos = jnp.arange(S)
    bias = jnp.where(pos[None, :] <= pos[:, None], 0.0, NEG_BIG).astype(F32)
    a_out = _heads_attention(qf.reshape(B * H, S, -1), kf.reshape(B * H, S, -1),
                             kv[..., MLA_NOPE:].reshape(B * H, S, MLA_V), bias, (MLA_NOPE + MLA_ROPE) ** -0.5)
    a_out = a_out.reshape(B, H, S, MLA_V).transpose(0, 2, 1, 3).reshape(B, S, H * MLA_V)
    w_gate2 = _pad_rows(p['ev_gla_w_gate2'][0], lr_g.shape[-1])
    b_out = gla(q_g, k_g, v_g, r_g, lr_g, w_gate2, p['ev_gla_b_gate'][0], p['ev_gla_norm_g'][0], p['ev_gla_norm_b'][0])
    return mm(jnp.concatenate([a_out, b_out], axis=-1), p['ev_w_out'][0])


def dilated_branch(q, k, v, window, dil):
    B, H, S, dh = q.shape
    span = window // dil
    L = S // dil
    nb = -(-L // span)
    Lp = nb * span

    def residues(t):
        t = t.reshape(B, H, L, dil, dh).transpose(0, 1, 3, 2, 4)
        t = jnp.pad(t, ((0, 0), (0, 0), (0, 0), (0, Lp - L), (0, 0)))
        return t.reshape(B, H, dil, nb, span, dh)

    def with_prev(t):
        prev = jnp.pad(t, ((0, 0), (0, 0), (0, 0), (1, 0), (0, 0), (0, 0)))[:, :, :, :-1]
        return jnp.concatenate([prev, t], axis=4)

    qb = residues(q)
    kw, vw = with_prev(residues(k)), with_prev(residues(v))
    G = B * H * dil * nb
    s = bmm_nt(qb.reshape(G, span, dh), kw.reshape(G, 2 * span, dh))
    qi = jnp.arange(span)[:, None] + span
    kj = jnp.arange(2 * span)[None, :]
    dist = qi - kj
    in_band = (dist >= 0) & (dist <= span)
    has_prev = (jnp.arange(nb) > 0)[:, None, None] | (kj >= span)[None]
    valid = in_band[None] & has_prev
    bias = jnp.where(valid, 0.0, NEG_BIG).astype(F32)
    p, lse = softmax_lse(s.reshape(B * H * dil, nb, span, 2 * span), bias, dh ** -0.5)
    o = bmm_nn(p.reshape(G, span, 2 * span), vw.reshape(G, 2 * span, dh)).reshape(B, H, dil, nb, span, dh)
    lse = lse.reshape(B, H, dil, nb, span)

    def back(t):
        t = t.reshape((B, H, dil, Lp) + t.shape[5:])[:, :, :, :L]
        return jnp.moveaxis(t, 2, 3).reshape((B, H, S) + t.shape[4:])

    return back(o), back(lse)


def dilated_mixture(q, k, v):
    outs, lses = [], []
    for window, dil in DIL_BRANCHES:
        o, lse = dilated_branch(q, k, v, window, dil)
        outs.append(o)
        lses.append(lse)
    wts = jax.nn.softmax(jnp.stack(lses, axis=0), axis=0)
    return jnp.sum(wts[..., None] * jnp.stack(outs, axis=0), axis=0)


def token_shift(t, mu):
    prev = jnp.pad(t, ((0, 0), (1, 0), (0, 0)))[:, :-1]
    return t + (prev - t) * mu


def rwkv7(r, k, v, w_lr, a_lr, g_lr, w0, w_decay2, a0, w_a2, w_gate2, k_k, k_a, r_k, gn_g, gn_b):
    B, S, _ = r.shape
    H, n = RWKV_HEADS, RWKV_HEAD_DIM
    w = -jax.nn.softplus(-(w0 + mm(jnp.tanh(w_lr), w_decay2))) - 0.5
    decay = jnp.exp(-jnp.exp(w))
    a = jax.nn.sigmoid(a0 + mm(a_lr, w_a2))
    g = mm(jax.nn.sigmoid(g_lr), w_gate2)
    kk = (k * k_k).reshape(B, S, H, n)
    kk = kk / jnp.maximum(jnp.sqrt(jnp.sum(kk * kk, axis=-1, keepdims=True)), 1e-12)
    kk = kk.reshape(B, S, H * n)
    kh = k * (1.0 + (a - 1.0) * k_a)
    y = rwkv_scan(r, decay, kh, v, kk, kk * a).reshape(B, S, H, n)
    y = layer_norm(y, jnp.ones((n,), F32), jnp.zeros((n,), F32), RWKV_GN_EPS).reshape(B, S, H * n) * gn_g + gn_b
    bonus = jnp.sum((r * kh).reshape(B, S, H, n) * r_k, axis=-1, keepdims=True) * v.reshape(B, S, H, n)
    y = y + bonus.reshape(B, S, H * n)
    return y * g


def odd_mixer(x, p):
    B, S, _ = x.shape
    cos, sin = rope_tables(S, DIL_HEAD_DIM)
    widths = (3 * DIL_WIDTH,) + RWKV_IN_WIDTHS
    h = mm(x, _pad_cols(p['od_w_in'][0], widths))
    c_in = h[..., :3 * DIL_WIDTH]
    d_in = h[..., 3 * DIL_WIDTH:]
    q, k, v = [t.reshape(B, S, DIL_HEADS, DIL_HEAD_DIM).transpose(0, 2, 1, 3) for t in jnp.split(c_in, 3, axis=-1)]
    q, k = apply_rope(q, cos, sin), apply_rope(k, cos, sin)
    c_out = dilated_mixture(q, k, v).transpose(0, 2, 1, 3).reshape(B, S, DIL_WIDTH)
    mu = _pad_cols(p['od_rwkv_mu'][0], RWKV_IN_WIDTHS)
    sh = token_shift(d_in, mu)
    at = [0]
    for n in RWKV_IN_WIDTHS:
        at.append(at[-1] + _pad_to(n))
    r, kd, vd = [sh[..., at[i]:at[i + 1]] for i in range(3)]
    w_lr, a_lr, g_lr = [sh[..., at[i]:at[i + 1]] for i in range(3, 6)]
    d_out = rwkv7(r, kd, vd, w_lr, a_lr, g_lr, p['od_rwkv_w0'][0], _pad_rows(p['od_rwkv_w_decay2'][0], w_lr.shape[-1]),
                  p['od_rwkv_a0'][0], _pad_rows(p['od_rwkv_w_a2'][0], a_lr.shape[-1]), p['od_rwkv_w_gate2'][0],
                  p['od_rwkv_k_k'][0], p['od_rwkv_k_a'][0], p['od_rwkv_r_k'][0], p['od_rwkv_gn_g'][0], p['od_rwkv_gn_b'][0])
    return mm(jnp.concatenate([c_out, d_out], axis=-1), p['od_w_out'][0])


def cross_attention(x, mem, w_q, w_k, w_v, w_o):
    B, S, D = x.shape
    M = mem.shape[1]
    hd = D // XA_HEADS

    def heads(t, n):
        return t.reshape(B, n, XA_HEADS, hd).transpose(0, 2, 1, 3).reshape(B * XA_HEADS, n, hd)

    q, k, v = heads(mm(x, w_q), S), heads(mm(mem, w_k), M), heads(mm(mem, w_v), M)
    o = _heads_attention(q, k, v, jnp.zeros((S, M), F32), hd ** -0.5)
    o = o.reshape(B, XA_HEADS, S, hd).transpose(0, 2, 1, 3).reshape(B, S, D)
    return mm(o, w_o)


def swiglu(x, w_gate, w_up, w_down):
    return mm(jax.nn.silu(mm(x, w_gate)) * mm(x, w_up), w_down)


def forward(p, x, mem):
    h = x
    for layer in range(DEPTH):
        mix = even_mixer(h, p) if layer % 2 == 0 else odd_mixer(h, p)
        h = layer_norm(DEEPNORM_ALPHA * h + mix, p['ln_mix_g'][layer], p['ln_mix_b'][layer])
        xa = cross_attention(h, mem, p['xa_w_q'][layer], p['xa_w_k'][layer], p['xa_w_v'][layer], p['xa_w_o'][layer])
        h = layer_norm(DEEPNORM_ALPHA * h + xa, p['ln_xa_g'][layer], p['ln_xa_b'][layer])
        ff = swiglu(h, p['ffn_w_gate'][layer], p['ffn_w_up'][layer], p['ffn_w_down'][layer])
        h = layer_norm(DEEPNORM_ALPHA * h + ff, p['ln_ffn_g'][layer], p['ln_ffn_b'][layer])
    return h


def _flat_pack(arrays, length, dtype):
    flat = jnp.concatenate([a.reshape(-1).astype(dtype) for a in arrays])
    return jnp.pad(flat, (0, length - flat.shape[0]))


def _unpack(flat, shapes):
    out, at = [], 0
    for shp in shapes:
        n = 1
        for d in shp:
            n *= d
        out.append(flat[at:at + n].reshape(shp))
        at += n
    return out


def _shard_of(full, axis, s):
    n = full.shape[axis] // 4
    return lax.slice_in_dim(full, s * n, (s + 1) * n, axis=axis)


def kernel(x, mem, ev_w_in, ev_mla_q_norm, ev_mla_w_uq, ev_mla_kv_norm, ev_mla_w_ukv, ev_gla_w_gate2, ev_gla_b_gate, ev_gla_norm_g, ev_gla_norm_b, ev_w_out, od_w_in, od_rwkv_mu, od_rwkv_w0, od_rwkv_w_decay2, od_rwkv_a0, od_rwkv_w_a2, od_rwkv_w_gate2, od_rwkv_k_k, od_rwkv_k_a, od_rwkv_r_k, od_rwkv_gn_g, od_rwkv_gn_b, od_w_out, ln_mix_g, ln_mix_b, xa_w_q, xa_w_k, xa_w_v, xa_w_o, ln_xa_g, ln_xa_b, ffn_w_gate, ffn_w_up, ffn_w_down, ln_ffn_g, ln_ffn_b, loss_target, m_ev_w_in, m_ev_mla_q_norm, m_ev_mla_w_uq, m_ev_mla_kv_norm, m_ev_mla_w_ukv, m_ev_gla_w_gate2, m_ev_gla_b_gate, m_ev_gla_norm_g, m_ev_gla_norm_b, m_ev_w_out, m_od_w_in, m_od_rwkv_mu, m_od_rwkv_w0, m_od_rwkv_w_decay2, m_od_rwkv_a0, m_od_rwkv_w_a2, m_od_rwkv_w_gate2, m_od_rwkv_k_k, m_od_rwkv_k_a, m_od_rwkv_r_k, m_od_rwkv_gn_g, m_od_rwkv_gn_b, m_od_w_out, m_ln_mix_g, m_ln_mix_b, m_xa_w_q, m_xa_w_k, m_xa_w_v, m_xa_w_o, m_ln_xa_g, m_ln_xa_b, m_ffn_w_gate, m_ffn_w_up, m_ffn_w_down, m_ln_ffn_g, m_ln_ffn_b, v_ev_w_in, v_ev_mla_q_norm, v_ev_mla_w_uq, v_ev_mla_kv_norm, v_ev_mla_w_ukv, v_ev_gla_w_gate2, v_ev_gla_b_gate, v_ev_gla_norm_g, v_ev_gla_norm_b, v_ev_w_out, v_od_w_in, v_od_rwkv_mu, v_od_rwkv_w0, v_od_rwkv_w_decay2, v_od_rwkv_a0, v_od_rwkv_w_a2, v_od_rwkv_w_gate2, v_od_rwkv_k_k, v_od_rwkv_k_a, v_od_rwkv_r_k, v_od_rwkv_gn_g, v_od_rwkv_gn_b, v_od_w_out, v_ln_mix_g, v_ln_mix_b, v_xa_w_q, v_xa_w_k, v_xa_w_v, v_xa_w_o, v_ln_xa_g, v_ln_xa_b, v_ffn_w_gate, v_ffn_w_up, v_ffn_w_down, v_ln_ffn_g, v_ln_ffn_b):
    given = dict(locals())
    W = {n: given[n] for n in WEIGHT_NAMES}
    Mo = {n: given['m_' + n] for n in WEIGHT_NAMES}
    Vo = {n: given['v_' + n] for n in WEIGHT_NAMES}
    mat_names = [n for n in WEIGHT_NAMES if n in MATRICES]
    vec_names = [n for n in WEIGHT_NAMES if n in SHARDED_VECTORS]
    rep_names = list(REPLICATED)
    mat_shapes = [W[n].shape for n in mat_names]
    vec_shapes = [W[n].shape for n in vec_names]
    rep_shapes = [W[n].shape for n in rep_names]

    def count(shapes):
        total = 0
        for shp in shapes:
            n = 1
            for d in shp:
                n *= d
            total += n
        return total

    rc = -(-count(mat_shapes) // (PACK_CHUNKS * 2 * PACK_COLS * PACK_ROW_ALIGN)) * PACK_ROW_ALIGN
    mat_len = PACK_CHUNKS * 2 * rc * PACK_COLS
    rv = -(-count(vec_shapes) // (2 * SMALL_COLS * 8)) * 8
    vec_len = 2 * rv * SMALL_COLS
    rr = -(-count(rep_shapes) // (SMALL_COLS * 8)) * 8
    rep_len = rr * SMALL_COLS

    wmat = _flat_pack([W[n] for n in mat_names], mat_len, BF16).reshape(PACK_CHUNKS, 2, rc, PACK_COLS)
    gathered = jnp.stack([_all_gather_call(wmat[i]) for i in range(PACK_CHUNKS)], axis=1)
    gvec = _all_gather_call(_flat_pack([W[n] for n in vec_names], vec_len, F32).reshape(2, rv, SMALL_COLS))
    full = {}
    mat_parts = [_unpack(gathered[s].reshape(-1), mat_shapes) for s in range(4)]
    for i, n in enumerate(mat_names):
        full[n] = jnp.concatenate([mat_parts[s][i] for s in range(4)], axis=MATRICES[n])
    vec_parts = [_unpack(gvec[s].reshape(-1), vec_shapes) for s in range(4)]
    for i, n in enumerate(vec_names):
        full[n] = jnp.concatenate([vec_parts[s][i] for s in range(4)], axis=SHARDED_VECTORS[n])
    for n in rep_names:
        full[n] = W[n]

    B, S, D = x.shape
    y, vjp = jax.vjp(lambda p, xx: forward(p, xx, mem), full, x)
    dy, part = _loss_call(y.reshape(B * S, D), loss_target.reshape(B * S, D))
    loss = lax.psum(0.5 * part, ('x', 'y', 'c'))
    gfull, grad_x = vjp(dy.reshape(B, S, D))

    gmat = jnp.stack([_flat_pack([_shard_of(gfull[n], MATRICES[n], s) for n in mat_names], mat_len, BF16)
                      for s in range(4)]).reshape(4, PACK_CHUNKS, 2, rc, PACK_COLS)
    halves = [_sum_parts_call(_scatter_call(gmat[:, i])) for i in range(PACK_CHUNKS)]
    gshard = jnp.stack([_sibling_exchange_call(h) for h in halves]).reshape(-1, PACK_COLS)
    grep = _flat_pack([gfull[n] for n in rep_names], rep_len, F32)
    gsmall = jnp.stack([jnp.concatenate([
        _flat_pack([_shard_of(gfull[n], SHARDED_VECTORS[n], s) for n in vec_names], vec_len, F32), grep])
        for s in range(4)]).reshape(4, 1, 2 * rv + rr, SMALL_COLS)
    gsmall = _sum_parts_call(_scatter_call(jnp.concatenate([gsmall, gsmall], axis=1)))

    def small_pack(src):
        return jnp.concatenate([_flat_pack([src[n] for n in vec_names], vec_len, F32),
                                _flat_pack([src[n] for n in rep_names], rep_len, F32)]).reshape(-1, SMALL_COLS)

    groups = [{}, {}, {}, {}]
    for n, g in zip(mat_names, _unpack(gshard.reshape(-1), mat_shapes)):
        rows = (-1, g.shape[-1])
        outs = _adamw_call(W[n].reshape(rows), g.reshape(rows), Mo[n].reshape(rows), Vo[n].reshape(rows))
        for grp, val in zip(groups, (g,) + outs):
            grp[n] = val.reshape(g.shape)
    small = (gsmall,) + _adamw_call(small_pack(W), gsmall, small_pack(Mo), small_pack(Vo))
    for grp, sm in zip(groups, small):
        sm = sm.reshape(-1)
        grp.update(zip(vec_names, _unpack(sm[:vec_len], vec_shapes)))
        grp.update(zip(rep_names, _unpack(sm[vec_len:], rep_shapes)))
    return (loss, grad_x, *[grp[n] for grp in groups for n in WEIGHT_NAMES])
```

```python
import functools

import jax
import jax.numpy as jnp
from jax import lax
from jax.experimental import pallas as pl
from jax.experimental.pallas import tpu as pltpu

F32 = jnp.float32
BF16 = jnp.bfloat16
MESH = pl.DeviceIdType.MESH

ROPE_THETA = 10000.0
LN_EPS = 1e-5
RMS_EPS = 1e-6
DEPTH = 2
DEEPNORM_ALPHA = (2.0 * DEPTH) ** 0.25
MLA_HEADS, MLA_NOPE, MLA_ROPE, MLA_V, MLA_Q_RANK, MLA_KV_RANK = 8, 128, 64, 128, 512, 256
GLA_HEADS, GLA_DK, GLA_DV, GLA_GATE_RANK, GLA_TAU, GLA_CHUNK = 4, 128, 256, 16, 16.0, 64
DIL_HEADS, DIL_HEAD_DIM = 8, 128
DIL_BRANCHES = ((128, 1), (512, 4), (2048, 16))
RWKV_HEADS, RWKV_HEAD_DIM = 16, 64
RWKV_DECAY_RANK, RWKV_A_RANK, RWKV_GATE_RANK = 96, 96, 256
RWKV_GN_EPS = 64e-5
XA_HEADS = 4
DIL_WIDTH = DIL_HEADS * DIL_HEAD_DIM
RWKV_WIDTH = RWKV_HEADS * RWKV_HEAD_DIM
EVEN_IN_WIDTHS = (MLA_Q_RANK, MLA_KV_RANK, MLA_ROPE, GLA_HEADS * GLA_DK, GLA_HEADS * GLA_DK,
                  GLA_HEADS * GLA_DV, GLA_HEADS * GLA_DV, GLA_GATE_RANK)
RWKV_IN_WIDTHS = (RWKV_WIDTH, RWKV_WIDTH, RWKV_WIDTH, RWKV_DECAY_RANK, RWKV_A_RANK, RWKV_GATE_RANK)
ADAM_LR, ADAM_B1, ADAM_B2, ADAM_EPS, ADAM_WD, ADAM_STEP = 0.001, 0.9, 0.999, 1e-08, 0.01, 10

LANES = 128
SUBLANES = 8
VMEM_LIMIT_BYTES = 48 * 1024 * 1024
NEG_BIG = -1e30

PACK_COLS = 1024
SMALL_COLS = 128

MATRICES = {
    'ev_w_in': 2, 'ev_mla_w_uq': 2, 'ev_mla_w_ukv': 2, 'ev_gla_w_gate2': 2, 'ev_w_out': 1, 'od_w_in': 2,
    'od_rwkv_w_decay2': 2, 'od_rwkv_w_a2': 2, 'od_rwkv_w_gate2': 2, 'od_w_out': 1,
    'xa_w_q': 1, 'xa_w_k': 1, 'xa_w_v': 1, 'xa_w_o': 1, 'ffn_w_gate': 2, 'ffn_w_up': 2, 'ffn_w_down': 1,
}
SHARDED_VECTORS = {
    'od_rwkv_mu': 1, 'od_rwkv_w0': 1, 'od_rwkv_a0': 1, 'od_rwkv_k_k': 1, 'od_rwkv_k_a': 1,
    'od_rwkv_gn_g': 1, 'od_rwkv_gn_b': 1,
}
REPLICATED = ('ev_mla_q_norm', 'ev_mla_kv_norm', 'ev_gla_b_gate', 'ev_gla_norm_g', 'ev_gla_norm_b', 'od_rwkv_r_k',
              'ln_mix_g', 'ln_mix_b', 'ln_xa_g', 'ln_xa_b', 'ln_ffn_g', 'ln_ffn_b')
WEIGHT_NAMES = ('ev_w_in', 'ev_mla_q_norm', 'ev_mla_w_uq', 'ev_mla_kv_norm', 'ev_mla_w_ukv', 'ev_gla_w_gate2',
                'ev_gla_b_gate', 'ev_gla_norm_g', 'ev_gla_norm_b', 'ev_w_out', 'od_w_in', 'od_rwkv_mu', 'od_rwkv_w0',
                'od_rwkv_w_decay2', 'od_rwkv_a0', 'od_rwkv_w_a2', 'od_rwkv_w_gate2', 'od_rwkv_k_k', 'od_rwkv_k_a',
                'od_rwkv_r_k', 'od_rwkv_gn_g', 'od_rwkv_gn_b', 'od_w_out', 'ln_mix_g', 'ln_mix_b', 'xa_w_q', 'xa_w_k',
                'xa_w_v', 'xa_w_o', 'ln_xa_g', 'ln_xa_b', 'ffn_w_gate', 'ffn_w_up', 'ffn_w_down', 'ln_ffn_g', 'ln_ffn_b')


def _pick(n, cap, mult):
    d = (min(cap, n) // mult) * mult
    while d >= mult:
        if n % d == 0:
            return d
        d -= mult
    return n


def _params(semantics):
    return pltpu.CompilerParams(dimension_semantics=semantics, vmem_limit_bytes=VMEM_LIMIT_BYTES)


_DIMS = {(False, False): (((1,), (0,)), ((), ())), (False, True): (((1,), (1,)), ((), ())),
         (True, False): (((0,), (0,)), ((), ()))}


def _bmm(a, b, ta, tb, out_dtype=F32):
    G = a.shape[0]
    K, M = (a.shape[1], a.shape[2]) if ta else (a.shape[2], a.shape[1])
    N = b.shape[1] if tb else b.shape[2]
    assert (b.shape[2] if tb else b.shape[1]) == K and b.shape[0] == G
    tm, tn = _pick(M, 1024, LANES), _pick(N, 512, LANES)
    tk = K if K <= 2048 else _pick(K, 2048, LANES)
    nk = K // tk
    gb = 1
    if tm == M and tn == N and nk == 1:
        per = 4 * (M * K + K * N + M * N)
        gb = _pick(G, max(1, min(8, (2 << 20) // per)), 1)
    dims = _DIMS[(ta, tb)]

    def body(a_ref, b_ref, o_ref, *scratch):
        def prod(i):
            return lax.dot_general(a_ref[i].astype(BF16), b_ref[i].astype(BF16), dims, preferred_element_type=F32)

        if nk == 1:
            for i in range(gb):
                o_ref[i] = prod(i).astype(o_ref.dtype)
        else:
            acc_ref, = scratch
            k = pl.program_id(3)

            @pl.when(k == 0)
            def _():
                acc_ref[...] = jnp.zeros_like(acc_ref)

            for i in range(gb):
                acc_ref[i] += prod(i)

            @pl.when(k == nk - 1)
            def _():
                o_ref[...] = acc_ref[...].astype(o_ref.dtype)

    a_spec = (pl.BlockSpec((gb, tk, tm), lambda g, i, j, k: (g, k, i)) if ta
              else pl.BlockSpec((gb, tm, tk), lambda g, i, j, k: (g, i, k)))
    b_spec = (pl.BlockSpec((gb, tn, tk), lambda g, i, j, k: (g, j, k)) if tb
              else pl.BlockSpec((gb, tk, tn), lambda g, i, j, k: (g, k, j)))
    return pl.pallas_call(
        body, name='bmm_' + ('t' if ta else 'n') + ('t' if tb else 'n'),
        out_shape=jax.ShapeDtypeStruct((G, M, N), out_dtype),
        grid=(G // gb, M // tm, N // tn, nk),
        in_specs=[a_spec, b_spec],
        out_specs=pl.BlockSpec((gb, tm, tn), lambda g, i, j, k: (g, i, j)),
        scratch_shapes=[] if nk == 1 else [pltpu.VMEM((gb, tm, tn), F32)],
        compiler_params=_params(('parallel', 'parallel', 'parallel', 'arbitrary')),
    )(a, b)


def _like(x):
    return jnp.zeros((), x.dtype)


@jax.custom_vjp
def bmm_nn(a, b):
    return _bmm(a, b, False, False)


def _bmm_nn_fwd(a, b):
    ab, bb = a.astype(BF16), b.astype(BF16)
    return _bmm(ab, bb, False, False), (ab, bb, _like(a), _like(b))


def _bmm_nn_bwd(res, g):
    a, b, la, lb = res
    g = g.astype(BF16)
    return _bmm(g, b, False, True, la.dtype), _bmm(a, g, True, False, lb.dtype)


bmm_nn.defvjp(_bmm_nn_fwd, _bmm_nn_bwd)


@jax.custom_vjp
def bmm_nt(a, b):
    return _bmm(a, b, False, True)


def _bmm_nt_fwd(a, b):
    ab, bb = a.astype(BF16), b.astype(BF16)
    return _bmm(ab, bb, False, True), (ab, bb, _like(a), _like(b))


def _bmm_nt_bwd(res, g):
    a, b, la, lb = res
    g = g.astype(BF16)
    return _bmm(g, b, False, False, la.dtype), _bmm(g, a, True, False, lb.dtype)


bmm_nt.defvjp(_bmm_nt_fwd, _bmm_nt_bwd)


def mm(x, w):
    lead = x.shape[:-1]
    out = bmm_nn(x.reshape(1, -1, x.shape[-1]), w[None])
    return out.reshape(lead + (w.shape[1],))


def _norm_stats(x, center, eps):
    if center:
        xc = x - jnp.mean(x, axis=-1, keepdims=True)
    else:
        xc = x
    rstd = lax.rsqrt(jnp.mean(xc * xc, axis=-1, keepdims=True) + eps)
    return xc * rstd, rstd


def _norm_fwd_call(x, g, b, center, eps):
    R, C = x.shape
    tr = _pick(R, max(8, (1 << 19) // C), 8)

    def body(x_ref, g_ref, b_ref, y_ref):
        xhat, _ = _norm_stats(x_ref[...], center, eps)
        y_ref[...] = xhat * g_ref[...] + b_ref[...]

    row = pl.BlockSpec((tr, C), lambda i: (i, 0))
    vec = pl.BlockSpec((1, C), lambda i: (0, 0))
    return pl.pallas_call(
        body, name='norm_fwd', out_shape=jax.ShapeDtypeStruct((R, C), F32), grid=(R // tr,),
        in_specs=[row, vec, vec], out_specs=row, compiler_params=_params(('parallel',)),
    )(x, g, b)


def _norm_bwd_call(x, g, dy, center, eps):
    R, C = x.shape
    tr = _pick(R, max(8, (1 << 19) // C), 8)

    def body(x_ref, g_ref, dy_ref, dx_ref, dg_ref, db_ref):
        @pl.when(pl.program_id(0) == 0)
        def _():
            dg_ref[...] = jnp.zeros_like(dg_ref)
            db_ref[...] = jnp.zeros_like(db_ref)

        xhat, rstd = _norm_stats(x_ref[...], center, eps)
        dy = dy_ref[...]
        dxh = dy * g_ref[...]
        proj = xhat * jnp.mean(dxh * xhat, axis=-1, keepdims=True)
        if center:
            dx_ref[...] = rstd * (dxh - jnp.mean(dxh, axis=-1, keepdims=True) - proj)
        else:
            dx_ref[...] = rstd * (dxh - proj)
        dg_ref[...] += jnp.sum(dy * xhat, axis=0, keepdims=True)
        db_ref[...] += jnp.sum(dy, axis=0, keepdims=True)

    row = pl.BlockSpec((tr, C), lambda i: (i, 0))
    vec = pl.BlockSpec((1, C), lambda i: (0, 0))
    return pl.pallas_call(
        body, name='norm_bwd',
        out_shape=(jax.ShapeDtypeStruct((R, C), F32), jax.ShapeDtypeStruct((1, C), F32), jax.ShapeDtypeStruct((1, C), F32)),
        grid=(R // tr,), in_specs=[row, vec, row], out_specs=(row, vec, vec), compiler_params=_params(('arbitrary',)),
    )(x, g, dy)


@functools.partial(jax.custom_vjp, nondiff_argnums=(3, 4))
def _norm2d(x, g, b, center, eps):
    return _norm_fwd_call(x, g, b, center, eps)


def _norm2d_fwd(x, g, b, center, eps):
    return _norm_fwd_call(x, g, b, center, eps), (x, g)


def _norm2d_bwd(center, eps, res, dy):
    x, g = res
    return _norm_bwd_call(x, g, dy, center, eps)


_norm2d.defvjp(_norm2d_fwd, _norm2d_bwd)


def layer_norm(x, g, b, eps=LN_EPS):
    C = x.shape[-1]
    return _norm2d(x.reshape(-1, C), g.reshape(1, C), b.reshape(1, C), True, eps).reshape(x.shape)


def rms_norm(x, g):
    C = x.shape[-1]
    return _norm2d(x.reshape(-1, C), g.reshape(1, C), jnp.zeros((1, C), F32), False, RMS_EPS).reshape(x.shape)


def _softmax_fwd_call(s, bias, scale):
    G1, G2, R, C = s.shape
    tr = _pick(R, max(8, (1 << 19) // C), 8)

    def body(s_ref, bias_ref, p_ref, lse_ref):
        z = s_ref[0, 0] * scale + bias_ref[0]
        m = jnp.max(z, axis=-1, keepdims=True)
        e = jnp.exp(z - m)
        den = jnp.sum(e, axis=-1, keepdims=True)
        p_ref[0, 0] = e / den
        lse_ref[0, 0] = m + jnp.log(den)

    blk = pl.BlockSpec((1, 1, tr, C), lambda a, b, r: (a, b, r, 0))
    col = pl.BlockSpec((1, 1, tr, 1), lambda a, b, r: (a, b, r, 0))
    return pl.pallas_call(
        body, name='softmax_fwd',
        out_shape=(jax.ShapeDtypeStruct(s.shape, F32), jax.ShapeDtypeStruct((G1, G2, R, 1), F32)),
        grid=(G1, G2, R // tr), in_specs=[blk, pl.BlockSpec((1, tr, C), lambda a, b, r: (b, r, 0))],
        out_specs=(blk, col), compiler_params=_params(('parallel', 'parallel', 'parallel')),
    )(s, bias)


def _softmax_bwd_call(p, dp, dlse, scale):
    G1, G2, R, C = p.shape
    tr = _pick(R, max(8, (1 << 19) // C), 8)

    def body(p_ref, dp_ref, dlse_ref, ds_ref):
        p = p_ref[0, 0]
        dp = dp_ref[0, 0]
        inner = jnp.sum(dp * p, axis=-1, keepdims=True)
        ds_ref[0, 0] = (p * (dp - inner + dlse_ref[0, 0])) * scale

    blk = pl.BlockSpec((1, 1, tr, C), lambda a, b, r: (a, b, r, 0))
    col = pl.BlockSpec((1, 1, tr, 1), lambda a, b, r: (a, b, r, 0))
    return pl.pallas_call(
        body, name='softmax_bwd', out_shape=jax.ShapeDtypeStruct(p.shape, F32),
        grid=(G1, G2, R // tr), in_specs=[blk, blk, col], out_specs=blk,
        compiler_params=_params(('parallel', 'parallel', 'parallel')),
    )(p, dp, dlse)


@functools.partial(jax.custom_vjp, nondiff_argnums=(2,))
def softmax_lse(s, bias, scale):
    return _softmax_fwd_call(s, bias, scale)


def _softmax_lse_fwd(s, bias, scale):
    p, lse = _softmax_fwd_call(s, bias, scale)
    return (p, lse), (p, bias)


def _softmax_lse_bwd(scale, res, cts):
    p, bias = res
    dp, dlse = cts
    return _softmax_bwd_call(p, dp, dlse, scale), jnp.zeros_like(bias)


softmax_lse.defvjp(_softmax_lse_fwd, _softmax_lse_bwd)


def _dot(a, b, dims):
    return lax.dot_general(a.astype(BF16), b.astype(BF16), dims, preferred_element_type=F32)


_NN, _NT, _TN = _DIMS[(False, False)], _DIMS[(False, True)], _DIMS[(True, False)]


def _gla_fwd_call(q, k, v, dec):
    G, nc, C, dk = q.shape
    dv = v.shape[-1]

    def body(q_ref, k_ref, v_ref, dec_ref, o_ref, st_ref, state):
        @pl.when(pl.program_id(1) == 0)
        def _():
            state[...] = jnp.zeros_like(state)

        s = state[...]
        st_ref[0, 0] = s
        o_ref[0, 0] = _dot(q_ref[0, 0], s, _NN)
        state[...] = s * dec_ref[0, 0] + _dot(k_ref[0, 0], v_ref[0, 0], _TN)

    def spec(r, c):
        return pl.BlockSpec((1, 1, r, c), lambda g, t: (g, t, 0, 0))

    return pl.pallas_call(
        body, name='gla_scan_fwd',
        out_shape=(jax.ShapeDtypeStruct((G, nc, C, dv), F32), jax.ShapeDtypeStruct((G, nc, dk, dv), F32)),
        grid=(G, nc), in_specs=[spec(C, dk), spec(C, dk), spec(C, dv), spec(dk, 1)],
        out_specs=(spec(C, dv), spec(dk, dv)), scratch_shapes=[pltpu.VMEM((dk, dv), F32)],
        compiler_params=_params(('parallel', 'arbitrary')),
    )(q, k, v, dec)


def _gla_bwd_call(q, k, v, dec, states, do):
    G, nc, C, dk = q.shape
    dv = v.shape[-1]

    def body(q_ref, k_ref, v_ref, dec_ref, st_ref, do_ref, dq_ref, dk_ref, dv_ref, ddec_ref, dstate):
        @pl.when(pl.program_id(1) == 0)
        def _():
            dstate[...] = jnp.zeros_like(dstate)

        s = st_ref[0, 0]
        d = dstate[...]
        do = do_ref[0, 0]
        dq_ref[0, 0] = _dot(do, s, _NT)
        dk_ref[0, 0] = _dot(v_ref[0, 0], d, _NT)
        dv_ref[0, 0] = _dot(k_ref[0, 0], d, _NN)
        ddec_ref[0, 0] = jnp.sum(s * d, axis=1, keepdims=True)
        dstate[...] = d * dec_ref[0, 0] + _dot(q_ref[0, 0], do, _TN)

    def spec(r, c):
        return pl.BlockSpec((1, 1, r, c), lambda g, t: (g, nc - 1 - t, 0, 0))

    return pl.pallas_call(
        body, name='gla_scan_bwd',
        out_shape=(jax.ShapeDtypeStruct(q.shape, F32), jax.ShapeDtypeStruct(k.shape, F32),
                   jax.ShapeDtypeStruct(v.shape, F32), jax.ShapeDtypeStruct(dec.shape, F32)),
        grid=(G, nc), in_specs=[spec(C, dk), spec(C, dk), spec(C, dv), spec(dk, 1), spec(dk, dv), spec(C, dv)],
        out_specs=(spec(C, dk), spec(C, dk), spec(C, dv), spec(dk, 1)), scratch_shapes=[pltpu.VMEM((dk, dv), F32)],
        compiler_params=_params(('parallel', 'arbitrary')),
    )(q, k, v, dec, states, do)


@jax.custom_vjp
def gla_scan(q, k, v, dec):
    return _gla_fwd_call(q, k, v, dec)[0]


def _gla_scan_fwd(q, k, v, dec):
    o, states = _gla_fwd_call(q, k, v, dec)
    return o, (q, k, v, dec, states)


def _gla_scan_bwd(res, do):
    return _gla_bwd_call(*res, do)


gla_scan.defvjp(_gla_scan_fwd, _gla_scan_bwd)


RWKV_PAIRS_PER_STEP = 4
RWKV_TIME_BLOCK = 64
RN = RWKV_HEAD_DIM


def _rwkv_consts():
    row = lax.broadcasted_iota(jnp.int32, (RN, LANES), 0)
    lane = lax.broadcasted_iota(jnp.int32, (RN, LANES), 1)
    diag = (lane % RN == row).astype(F32)
    r2 = lax.broadcasted_iota(jnp.int32, (LANES, LANES), 0)
    l2 = lax.broadcasted_iota(jnp.int32, (LANES, LANES), 1)
    seg = (r2 // RN == l2 // RN).astype(BF16)
    return diag, seg


def _stage(lhs_ref, slot, p):
    hi = p.astype(BF16)
    lhs_ref[pl.ds(slot * LANES, RN), :] = hi
    lhs_ref[pl.ds(slot * LANES + RN, RN), :] = (p - hi.astype(F32)).astype(BF16)


def _seg_sums(lhs_ref, nslots, seg):
    res = jnp.dot(lhs_ref[pl.ds(0, nslots * LANES), :], seg, preferred_element_type=F32)
    return [res[i * LANES:i * LANES + RN] + res[i * LANES + RN:(i + 1) * LANES] for i in range(nslots)]


def _rwkv_blocks(B, S, C):
    npairs = C // LANES
    pp = RWKV_PAIRS_PER_STEP if npairs % RWKV_PAIRS_PER_STEP == 0 else 1
    T = _pick(S, RWKV_TIME_BLOCK, 8)
    return npairs, pp, T


def _rwkv_fwd_call(r, w, k, v, kk, b):
    B, S, C = r.shape
    npairs, pp, T = _rwkv_blocks(B, S, C)
    G = SUBLANES

    def body(r_ref, w_ref, k_ref, v_ref, kk_ref, b_ref, y_ref, sall_ref, state, step_lhs, v_lhs, y_lhs):
        @pl.when(pl.program_id(2) == 0)
        def _():
            state[...] = jnp.zeros_like(state)

        diag, seg = _rwkv_consts()
        rowid = lax.broadcasted_iota(jnp.int32, (SUBLANES, LANES), 0)

        def group(t8, carry):
            rows = pl.ds(pl.multiple_of(t8 * G, G), G)
            sls = [slice(p * LANES, (p + 1) * LANES) for p in range(pp)]
            ops = [[ref[0, rows, sl] for ref in (r_ref, w_ref, k_ref, v_ref, kk_ref, b_ref)] for sl in sls]
            for j in range(G):
                for p in range(pp):
                    _stage(v_lhs, j * pp + p, diag * ops[p][3][j:j + 1])
            vcols = _seg_sums(v_lhs, G * pp, seg)
            s = list(carry)
            for j in range(G):
                for p in range(pp):
                    sall_ref[0, p, t8 * G + j] = s[p]
                    _stage(step_lhs, p, s[p] * ops[p][4][j:j + 1])
                sas = _seg_sums(step_lhs, pp, seg)
                for p in range(pp):
                    rt, wt, kt, _, _, bt = ops[p]
                    s[p] = s[p] * wt[j:j + 1] - sas[p] * bt[j:j + 1] + vcols[j * pp + p] * kt[j:j + 1]
                    _stage(y_lhs, j * pp + p, s[p] * rt[j:j + 1])
            ycols = _seg_sums(y_lhs, G * pp, seg)
            for p in range(pp):
                ytile = jnp.zeros((SUBLANES, LANES), F32)
                for j in range(G):
                    ytile = jnp.where(rowid == j, jnp.sum(diag * ycols[j * pp + p], axis=0, keepdims=True), ytile)
                y_ref[0, rows, sls[p]] = ytile
            return tuple(s)

        final = lax.fori_loop(0, T // G, group, tuple(state[p] for p in range(pp)))
        for p in range(pp):
            state[p] = final[p]

    seq = pl.BlockSpec((1, T, pp * LANES), lambda bi, g, t: (bi, t, g))
    return pl.pallas_call(
        body, name='rwkv_scan_fwd',
        out_shape=(jax.ShapeDtypeStruct((B, S, C), F32), jax.ShapeDtypeStruct((B, npairs, S, RN, LANES), F32)),
        grid=(B, npairs // pp, S // T), in_specs=[seq] * 6,
        out_specs=(seq, pl.BlockSpec((1, pp, T, RN, LANES), lambda bi, g, t: (bi, g, t, 0, 0))),
        scratch_shapes=[pltpu.VMEM((pp, RN, LANES), F32), pltpu.VMEM((pp * LANES, LANES), BF16),
                        pltpu.VMEM((G * pp * LANES, LANES), BF16), pltpu.VMEM((G * pp * LANES, LANES), BF16)],
        compiler_params=_params(('parallel', 'parallel', 'arbitrary')),
    )(r, w, k, v, kk, b)


def _rwkv_bwd_call(r, w, k, v, kk, b, sall, dy):
    B, S, C = r.shape
    npairs, pp, T = _rwkv_blocks(B, S, C)
    nt = S // T
    G = SUBLANES

    def body(r_ref, w_ref, k_ref, v_ref, kk_ref, b_ref, sall_ref, dy_ref,
             dr_ref, dw_ref, dk_ref, dv_ref, dkk_ref, db_ref, dstate, step_lhs, pre_lhs, dv_lhs):
        @pl.when(pl.program_id(2) == 0)
        def _():
            dstate[...] = jnp.zeros_like(dstate)

        diag, seg = _rwkv_consts()
        rowid = lax.broadcasted_iota(jnp.int32, (SUBLANES, LANES), 0)

        def colsum(z):
            return jnp.sum(z, axis=0, keepdims=True)

        def group(i, carry):
            t8 = T // G - 1 - i
            rows = pl.ds(pl.multiple_of(t8 * G, G), G)
            sls = [slice(p * LANES, (p + 1) * LANES) for p in range(pp)]
            ops = [[ref[0, rows, sl] for ref in (r_ref, w_ref, k_ref, v_ref, kk_ref, b_ref, dy_ref)] for sl in sls]
            for j in range(G):
                for p in range(pp):
                    _stage(pre_lhs, j * pp + p, sall_ref[0, p, t8 * G + j] * ops[p][4][j:j + 1])
                    _stage(pre_lhs, (G + j) * pp + p, diag * ops[p][3][j:j + 1])
                    _stage(pre_lhs, (2 * G + j) * pp + p, diag * ops[p][6][j:j + 1])
            pre = _seg_sums(pre_lhs, 3 * G * pp, seg)
            ds = list(carry)
            tiles = [[jnp.zeros((SUBLANES, LANES), F32) for _ in range(5)] for _ in range(pp)]
            for j in reversed(range(G)):
                d = []
                for p in range(pp):
                    rt, _, kt, _, _, bt, _ = ops[p]
                    d.append(ds[p] + pre[(2 * G + j) * pp + p] * rt[j:j + 1])
                    _stage(step_lhs, p, d[p] * bt[j:j + 1])
                    _stage(dv_lhs, j * pp + p, d[p] * kt[j:j + 1])
                dsas = _seg_sums(step_lhs, pp, seg)
                for p in range(pp):
                    rt, wt, kt, _, kkt, bt, _ = ops[p]
                    s = sall_ref[0, p, t8 * G + j]
                    sa, vcol, dycol = -pre[j * pp + p], pre[(G + j) * pp + p], pre[(2 * G + j) * pp + p]
                    s2 = s * wt[j:j + 1] + sa * bt[j:j + 1] + vcol * kt[j:j + 1]
                    vals = (colsum(s2 * dycol), colsum(d[p] * s), colsum(d[p] * vcol), -colsum(s * dsas[p]),
                            colsum(d[p] * sa))
                    tiles[p] = [jnp.where(rowid == j, val, tile) for val, tile in zip(vals, tiles[p])]
                    ds[p] = d[p] * wt[j:j + 1] - dsas[p] * kkt[j:j + 1]
            dvcols = _seg_sums(dv_lhs, G * pp, seg)
            for p in range(pp):
                dvt = jnp.zeros((SUBLANES, LANES), F32)
                for j in range(G):
                    dvt = jnp.where(rowid == j, colsum(diag * dvcols[j * pp + p]), dvt)
                dv_ref[0, rows, sls[p]] = dvt
                for ref, tile in zip((dr_ref, dw_ref, dk_ref, dkk_ref, db_ref), tiles[p]):
                    ref[0, rows, sls[p]] = tile
            return tuple(ds)

        final = lax.fori_loop(0, T // G, group, tuple(dstate[p] for p in range(pp)))
        for p in range(pp):
            dstate[p] = final[p]

    seq = pl.BlockSpec((1, T, pp * LANES), lambda bi, g, t: (bi, nt - 1 - t, g))
    sds = jax.ShapeDtypeStruct((B, S, C), F32)
    return pl.pallas_call(
        body, name='rwkv_scan_bwd', out_shape=(sds,) * 6,
        grid=(B, npairs // pp, nt),
        in_specs=[seq] * 6 + [pl.BlockSpec((1, pp, T, RN, LANES), lambda bi, g, t: (bi, g, nt - 1 - t, 0, 0)), seq],
        out_specs=(seq,) * 6,
        scratch_shapes=[pltpu.VMEM((pp, RN, LANES), F32), pltpu.VMEM((pp * LANES, LANES), BF16),
                        pltpu.VMEM((3 * G * pp * LANES, LANES), BF16), pltpu.VMEM((G * pp * LANES, LANES), BF16)],
        compiler_params=_params(('parallel', 'parallel', 'arbitrary')),
    )(r, w, k, v, kk, b, sall, dy)


@jax.custom_vjp
def rwkv_scan(r, w, k, v, kk, b):
    return _rwkv_fwd_call(r, w, k, v, kk, b)[0]


def _rwkv_scan_fwd(r, w, k, v, kk, b):
    y, sall = _rwkv_fwd_call(r, w, k, v, kk, b)
    return y, (r, w, k, v, kk, b, sall)


def _rwkv_scan_bwd(res, dy):
    return _rwkv_bwd_call(*res, dy)


rwkv_scan.defvjp(_rwkv_scan_fwd, _rwkv_scan_bwd)


def _loss_call(y, target):
    R, D = y.shape
    tr = _pick(R, max(8, (1 << 19) // D), 8)

    def body(y_ref, t_ref, dy_ref, part_ref):
        @pl.when(pl.program_id(0) == 0)
        def _():
            part_ref[...] = jnp.zeros_like(part_ref)

        diff = y_ref[...] - t_ref[...]
        dy_ref[...] = diff / D
        part_ref[...] += jnp.sum(jnp.mean(diff * diff, axis=-1, keepdims=True), axis=0, keepdims=True)

    row = pl.BlockSpec((tr, D), lambda i: (i, 0))
    dy, part = pl.pallas_call(
        body, name='loss_head',
        out_shape=(jax.ShapeDtypeStruct((R, D), F32), jax.ShapeDtypeStruct((1, 1), F32)),
        grid=(R // tr,), in_specs=[row, row], out_specs=(row, pl.BlockSpec((1, 1), lambda i: (0, 0))),
        compiler_params=_params(('arbitrary',)),
    )(y, target)
    return dy, part[0, 0]


def _sum_parts_call(parts):
    P, R, C = parts.shape
    tr = _pick(R, 512, 16)

    def body(p_ref, o_ref):
        acc = p_ref[0].astype(F32)
        for i in range(1, P):
            acc = acc + p_ref[i].astype(F32)
        o_ref[...] = acc

    return pl.pallas_call(
        body, name='sum_parts', out_shape=jax.ShapeDtypeStruct((R, C), F32), grid=(R // tr,),
        in_specs=[pl.BlockSpec((P, tr, C), lambda i: (0, i, 0))], out_specs=pl.BlockSpec((tr, C), lambda i: (i, 0)),
        compiler_params=_params(('parallel',)),
    )(parts)


def _adamw_call(w, g, m, v):
    R, C = w.shape
    tr = _pick(R, max(8, (1 << 18) // C), 8)

    def body(w_ref, g_ref, m_ref, v_ref, d_ref, nm_ref, nv_ref):
        g = g_ref[...]
        m = ADAM_B1 * m_ref[...] + (1.0 - ADAM_B1) * g
        v = ADAM_B2 * v_ref[...] + (1.0 - ADAM_B2) * (g * g)
        m_hat = m / (1.0 - ADAM_B1 ** ADAM_STEP)
        v_hat = v / (1.0 - ADAM_B2 ** ADAM_STEP)
        d_ref[...] = -ADAM_LR * (m_hat / (jnp.sqrt(v_hat) + ADAM_EPS) + ADAM_WD * w_ref[...])
        nm_ref[...] = m
        nv_ref[...] = v

    row = pl.BlockSpec((tr, C), lambda i: (i, 0))
    sds = jax.ShapeDtypeStruct((R, C), F32)
    return pl.pallas_call(
        body, name='adamw', out_shape=(sds, sds, sds), grid=(R // tr,),
        in_specs=[row] * 4, out_specs=(row,) * 3, compiler_params=_params(('parallel',)),
    )(w, g, m, v)


ANY = pl.BlockSpec(memory_space=pl.ANY)


def _place():
    return lax.axis_index('x'), lax.axis_index('y'), lax.axis_index('c')


COPY_SPLIT = 8
BF16_TILE_ROWS = 16
PACK_ROW_ALIGN = COPY_SPLIT * BF16_TILE_ROWS


def _row_split(rows):
    if rows % PACK_ROW_ALIGN == 0:
        return COPY_SPLIT, rows // COPY_SPLIT
    return 1, rows


def _all_gather_call(pack):
    _, R, C = pack.shape
    ns, rs = _row_split(R)

    def body(pk_ref, out_ref, send_sems, recv_sems):
        x, y, c = _place()
        chips = [(1 - x, y), (x, 1 - y), (1 - x, 1 - y)]
        me = 2 * x + y

        def copy(k, i, src, dst, to):
            rows = pl.ds(i * rs, rs)
            return pltpu.make_async_remote_copy(src_ref=src.at[rows], dst_ref=dst.at[rows], send_sem=send_sems.at[k * ns + i],
                                                recv_sem=recv_sems.at[k * ns + i], device_id=to, device_id_type=MESH)

        first = [copy(j, i, pk_ref.at[c], out_ref.at[me, c], (px, py, c))
                 for j, (px, py) in enumerate(chips) for i in range(ns)]
        for cp in first:
            cp.start()
        passed = []
        for i in range(ns):
            for j, (px, py) in enumerate(chips):
                landed = out_ref.at[2 * px + py, c]
                copy(j, i, landed, landed, (px, py, c)).wait_recv()
                fwd = copy(3 + j, i, landed, landed, (x, y, 1 - c))
                fwd.start()
                passed.append(fwd)
        for i in range(ns):
            for j, (px, py) in enumerate(chips):
                other = out_ref.at[2 * px + py, 1 - c]
                copy(3 + j, i, other, other, (x, y, 1 - c)).wait_recv()
        for cp in first + passed:
            cp.wait_send()

    others = pl.pallas_call(
        body, name='all_gather', out_shape=jax.ShapeDtypeStruct((4, 2, R, C), pack.dtype),
        in_specs=[ANY], out_specs=ANY,
        scratch_shapes=[pltpu.SemaphoreType.DMA((6 * ns,)), pltpu.SemaphoreType.DMA((6 * ns,))],
    )(pack)
    x, y, _ = _place()
    return lax.dynamic_update_slice(others, pack[None], (2 * x + y, 0, 0, 0))


def _scatter_call(src):
    _, _, R, C = src.shape
    ns, rs = _row_split(R)

    def body(src_ref, out_ref, send_sems, recv_sems):
        x, y, c = _place()
        me = 4 * x + 2 * y + c
        peers = []
        for rel in range(1, 8):
            px = 1 - x if rel & 4 else x
            py = 1 - y if rel & 2 else y
            pc = 1 - c if rel & 1 else c
            peers.append((rel - 1, px, py, pc))

        def copy(k, i, src, dst, to):
            rows = pl.ds(i * rs, rs)
            return pltpu.make_async_remote_copy(src_ref=src.at[rows], dst_ref=dst.at[rows], send_sem=send_sems.at[k * ns + i],
                                                recv_sem=recv_sems.at[k * ns + i], device_id=to, device_id_type=MESH)

        sends = [copy(k, i, src_ref.at[2 * px + py, pc], out_ref.at[me], (px, py, pc))
                 for i in range(ns) for k, px, py, pc in peers]
        for cp in sends:
            cp.start()
        for i in range(ns):
            for k, px, py, pc in peers:
                slot = out_ref.at[4 * px + 2 * py + pc]
                copy(k, i, slot, slot, (px, py, pc)).wait_recv()
        for cp in sends:
            cp.wait_send()

    others = pl.pallas_call(
        body, name='scatter_parts', out_shape=jax.ShapeDtypeStruct((8, R, C), src.dtype),
        in_specs=[ANY], out_specs=ANY,
        scratch_shapes=[pltpu.SemaphoreType.DMA((7 * ns,)), pltpu.SemaphoreType.DMA((7 * ns,))],
    )(src)
    x, y, c = _place()
    own = lax.dynamic_slice(src, (2 * x + y, c, 0, 0), (1, 1, R, C)).reshape(1, R, C)
    return lax.dynamic_update_slice(others, own, (4 * x + 2 * y + c, 0, 0))


def _sibling_exchange_call(half):
    R, C = half.shape
    ns, rs = _row_split(R)

    def body(h_ref, other_ref, send_sems, recv_sems):
        x, y, c = _place()

        def copy(i):
            rows = pl.ds(i * rs, rs)
            return pltpu.make_async_remote_copy(src_ref=h_ref.at[rows], dst_ref=other_ref.at[rows], send_sem=send_sems.at[i],
                                                recv_sem=recv_sems.at[i], device_id=(x, y, 1 - c), device_id_type=MESH)

        sends = [copy(i) for i in range(ns)]
        for cp in sends:
            cp.start()
        for cp in sends:
            cp.wait_recv()
        for cp in sends:
            cp.wait_send()

    return pl.pallas_call(
        body, name='sibling_exchange', out_shape=jax.ShapeDtypeStruct((R, C), half.dtype),
        in_specs=[ANY], out_specs=ANY,
        scratch_shapes=[pltpu.SemaphoreType.DMA((ns,)), pltpu.SemaphoreType.DMA((ns,))],
    )(half)


def rope_tables(seq_len, dim):
    inv = ROPE_THETA ** (-jnp.arange(0, dim, 2, dtype=F32) / dim)
    ang = jnp.arange(seq_len, dtype=F32)[:, None] * inv[None, :]
    return jnp.cos(ang), jnp.sin(ang)


def apply_rope(x, cos, sin):
    x1, x2 = jnp.split(x, 2, axis=-1)
    return jnp.concatenate([x1 * cos - x2 * sin, x1 * sin + x2 * cos], axis=-1)


def _pad_to(n):
    return -(-n // LANES) * LANES


def _pad_cols(w, widths):
    parts, at = [], 0
    for n in widths:
        parts.append(jnp.pad(w[..., at:at + n], [(0, 0)] * (w.ndim - 1) + [(0, _pad_to(n) - n)]))
        at += n
    return jnp.concatenate(parts, axis=-1)


def _split_padded(t, widths):
    out, at = [], 0
    for n in widths:
        out.append(t[..., at:at + n])
        at += _pad_to(n)
    return out


def _pad_rows(w, rows):
    return jnp.pad(w, ((0, rows - w.shape[0]), (0, 0)))


def _heads_attention(q, k, v, bias, scale):
    s = bmm_nt(q, k)
    p, _ = softmax_lse(s[:, None], bias[None], scale)
    return bmm_nn(p[:, 0], v)


def gla(q, k, v, r, gate_lr, w_gate2, b_gate, norm_g, norm_b):
    B, S, _ = q.shape
    H, dk, dv, C = GLA_HEADS, GLA_DK, GLA_DV, GLA_CHUNK
    nc = S // C
    log_a = jax.nn.log_sigmoid(mm(gate_lr, w_gate2) + b_gate) / GLA_TAU

    def chunks(t, d):
        return t.reshape(B, nc, C, H, d).transpose(0, 3, 1, 2, 4)

    qc = chunks(q, dk) * (dk ** -0.5)
    kc = chunks(k, dk)
    vc = chunks(v, dv)
    b = jnp.cumsum(chunks(log_a, dk), axis=3)
    b_last = b[:, :, :, -1:, :]
    q_dec = qc * jnp.exp(b)
    k_inv = kc * jnp.exp(-b)
    k_end = kc * jnp.exp(b_last - b)
    causal = jnp.tril(jnp.ones((C, C), dtype=bool))
    G = B * H
    att = bmm_nt(q_dec.reshape(G * nc, C, dk), k_inv.reshape(G * nc, C, dk))
    att = jnp.where(causal, att, 0.0)
    o_intra = bmm_nn(att, vc.reshape(G * nc, C, dv)).reshape(B, H, nc, C, dv)
    dec = jnp.exp(b_last[:, :, :, 0, :]).reshape(G, nc, dk, 1)
    o_inter = gla_scan(q_dec.reshape(G, nc, C, dk), k_end.reshape(G, nc, C, dk), vc.reshape(G, nc, C, dv), dec)
    o = o_intra + o_inter.reshape(B, H, nc, C, dv)
    o = o.transpose(0, 2, 3, 1, 4).reshape(B, S, H, dv)
    o = layer_norm(o, norm_g, norm_b).reshape(B, S, H * dv)
    return o * jax.nn.silu(r)


def even_mixer(x, p):
    B, S, _ = x.shape
    H = MLA_HEADS
    cos, sin = rope_tables(S, MLA_ROPE)
    z = mm(x, _pad_cols(p['ev_w_in'][0], EVEN_IN_WIDTHS))
    c_q, c_kv, k_pe, q_g, k_g, v_g, r_g, _ = _split_padded(z, EVEN_IN_WIDTHS)
    lr_at = sum(_pad_to(n) for n in EVEN_IN_WIDTHS[:-1])
    lr_g = z[..., lr_at:]
    q = mm(rms_norm(c_q, p['ev_mla_q_norm'][0]), p['ev_mla_w_uq'][0])
    q = q.reshape(B, S, H, MLA_NOPE + MLA_ROPE).transpose(0, 2, 1, 3)
    kv = mm(rms_norm(c_kv, p['ev_mla_kv_norm'][0]), p['ev_mla_w_ukv'][0])
    kv = kv.reshape(B, S, H, MLA_NOPE + MLA_V).transpose(0, 2, 1, 3)
    q_pe = apply_rope(q[..., MLA_NOPE:], cos, sin)
    k_pe = jnp.broadcast_to(apply_rope(k_pe[:, None], cos, sin), (B, H, S, MLA_ROPE))
    qf = jnp.concatenate([q[..., :MLA_NOPE], q_pe], axis=-1)
    kf = jnp.concatenate([kv[..., :MLA_NOPE], k_pe], axis=-1)
    pos = jnp.arange(S)
    bias = jnp.where(pos[None, :] <= pos[:, None], 0.0, NEG_BIG).astype(F32)
    a_out = _heads_attention(qf.reshape(B * H, S, -1), kf.reshape(B * H, S, -1),
                             kv[..., MLA_NOPE:].reshape(B * H, S, MLA_V), bias, (MLA_NOPE + MLA_ROPE) ** -0.5)
    a_out = a_out.reshape(B, H, S, MLA_V).transpose(0, 2, 1, 3).reshape(B, S, H * MLA_V)
    w_gate2 = _pad_rows(p['ev_gla_w_gate2'][0], lr_g.shape[-1])
    b_out = gla(q_g, k_g, v_g, r_g, lr_g, w_gate2, p['ev_gla_b_gate'][0], p['ev_gla_norm_g'][0], p['ev_gla_norm_b'][0])
    return mm(jnp.concatenate([a_out, b_out], axis=-1), p['ev_w_out'][0])


def dilated_branch(q, k, v, window, dil):
    B, H, S, dh = q.shape
    span = window // dil
    L = S // dil
    nb = -(-L // span)
    Lp = nb * span

    def residues(t):
        t = t.reshape(B, H, L, dil, dh).transpose(0, 1, 3, 2, 4)
        t = jnp.pad(t, ((0, 0), (0, 0), (0, 0), (0, Lp - L), (0, 0)))
        return t.reshape(B, H, dil, nb, span, dh)

    def with_prev(t):
        prev = jnp.pad(t, ((0, 0), (0, 0), (0, 0), (1, 0), (0, 0), (0, 0)))[:, :, :, :-1]
        return jnp.concatenate([prev, t], axis=4)

    qb = residues(q)
    kw, vw = with_prev(residues(k)), with_prev(residues(v))
    G = B * H * dil * nb
    s = bmm_nt(qb.reshape(G, span, dh), kw.reshape(G, 2 * span, dh))
    qi = jnp.arange(span)[:, None] + span
    kj = jnp.arange(2 * span)[None, :]
    dist = qi - kj
    in_band = (dist >= 0) & (dist <= span)
    has_prev = (jnp.arange(nb) > 0)[:, None, None] | (kj >= span)[None]
    valid = in_band[None] & has_prev
    bias = jnp.where(valid, 0.0, NEG_BIG).astype(F32)
    p, lse = softmax_lse(s.reshape(B * H * dil, nb, span, 2 * span), bias, dh ** -0.5)
    o = bmm_nn(p.reshape(G, span, 2 * span), vw.reshape(G, 2 * span, dh)).reshape(B, H, dil, nb, span, dh)
    lse = lse.reshape(B, H, dil, nb, span)

    def back(t):
        t = t.reshape((B, H, dil, Lp) + t.shape[5:])[:, :, :, :L]
        return jnp.moveaxis(t, 2, 3).reshape((B, H, S) + t.shape[4:])

    return back(o), back(lse)


def dilated_mixture(q, k, v):
    outs, lses = [], []
    for window, dil in DIL_BRANCHES:
        o, lse = dilated_branch(q, k, v, window, dil)
        outs.append(o)
        lses.append(lse)
    wts = jax.nn.softmax(jnp.stack(lses, axis=0), axis=0)
    return jnp.sum(wts[..., None] * jnp.stack(outs, axis=0), axis=0)


def token_shift(t, mu):
    prev = jnp.pad(t, ((0, 0), (1, 0), (0, 0)))[:, :-1]
    return t + (prev - t) * mu


def rwkv7(r, k, v, w_lr, a_lr, g_lr, w0, w_decay2, a0, w_a2, w_gate2, k_k, k_a, r_k, gn_g, gn_b):
    B, S, _ = r.shape
    H, n = RWKV_HEADS, RWKV_HEAD_DIM
    w = -jax.nn.softplus(-(w0 + mm(jnp.tanh(w_lr), w_decay2))) - 0.5
    decay = jnp.exp(-jnp.exp(w))
    a = jax.nn.sigmoid(a0 + mm(a_lr, w_a2))
    g = mm(jax.nn.sigmoid(g_lr), w_gate2)
    kk = (k * k_k).reshape(B, S, H, n)
    kk = kk / jnp.maximum(jnp.sqrt(jnp.sum(kk * kk, axis=-1, keepdims=True)), 1e-12)
    kk = kk.reshape(B, S, H * n)
    kh = k * (1.0 + (a - 1.0) * k_a)
    y = rwkv_scan(r, decay, kh, v, kk, kk * a).reshape(B, S, H, n)
    y = layer_norm(y, jnp.ones((n,), F32), jnp.zeros((n,), F32), RWKV_GN_EPS).reshape(B, S, H * n) * gn_g + gn_b
    bonus = jnp.sum((r * kh).reshape(B, S, H, n) * r_k, axis=-1, keepdims=True) * v.reshape(B, S, H, n)
    y = y + bonus.reshape(B, S, H * n)
    return y * g


def odd_mixer(x, p):
    B, S, _ = x.shape
    cos, sin = rope_tables(S, DIL_HEAD_DIM)
    widths = (3 * DIL_WIDTH,) + RWKV_IN_WIDTHS
    h = mm(x, _pad_cols(p['od_w_in'][0], widths))
    c_in = h[..., :3 * DIL_WIDTH]
    d_in = h[..., 3 * DIL_WIDTH:]
    q, k, v = [t.reshape(B, S, DIL_HEADS, DIL_HEAD_DIM).transpose(0, 2, 1, 3) for t in jnp.split(c_in, 3, axis=-1)]
    q, k = apply_rope(q, cos, sin), apply_rope(k, cos, sin)
    c_out = dilated_mixture(q, k, v).transpose(0, 2, 1, 3).reshape(B, S, DIL_WIDTH)
    mu = _pad_cols(p['od_rwkv_mu'][0], RWKV_IN_WIDTHS)
    sh = token_shift(d_in, mu)
    at = [0]
    for n in RWKV_IN_WIDTHS:
        at.append(at[-1] + _pad_to(n))
    r, kd, vd = [sh[..., at[i]:at[i + 1]] for i in range(3)]
    w_lr, a_lr, g_lr = [sh[..., at[i]:at[i + 1]] for i in range(3, 6)]
    d_out = rwkv7(r, kd, vd, w_lr, a_lr, g_lr, p['od_rwkv_w0'][0], _pad_rows(p['od_rwkv_w_decay2'][0], w_lr.shape[-1]),
                  p['od_rwkv_a0'][0], _pad_rows(p['od_rwkv_w_a2'][0], a_lr.shape[-1]), p['od_rwkv_w_gate2'][0],
                  p['od_rwkv_k_k'][0], p['od_rwkv_k_a'][0], p['od_rwkv_r_k'][0], p['od_rwkv_gn_g'][0], p['od_rwkv_gn_b'][0])
    return mm(jnp.concatenate([c_out, d_out], axis=-1), p['od_w_out'][0])


def cross_attention(x, mem, w_q, w_k, w_v, w_o):
    B, S, D = x.shape
    M = mem.shape[1]
    hd = D // XA_HEADS

    def heads(t, n):
        return t.reshape(B, n, XA_HEADS, hd).transpose(0, 2, 1, 3).reshape(B * XA_HEADS, n, hd)

    q, k, v = heads(mm(x, w_q), S), heads(mm(mem, w_k), M), heads(mm(mem, w_v), M)
    o = _heads_attention(q, k, v, jnp.zeros((S, M), F32), hd ** -0.5)
    o = o.reshape(B, XA_HEADS, S, hd).transpose(0, 2, 1, 3).reshape(B, S, D)
    return mm(o, w_o)


def swiglu(x, w_gate, w_up, w_down):
    return mm(jax.nn.silu(mm(x, w_gate)) * mm(x, w_up), w_down)


def forward(p, x, mem):
    h = x
    for layer in range(DEPTH):
        mix = even_mixer(h, p) if layer % 2 == 0 else odd_mixer(h, p)
        h = layer_norm(DEEPNORM_ALPHA * h + mix, p['ln_mix_g'][layer], p['ln_mix_b'][layer])
        xa = cross_attention(h, mem, p['xa_w_q'][layer], p['xa_w_k'][layer], p['xa_w_v'][layer], p['xa_w_o'][layer])
        h = layer_norm(DEEPNORM_ALPHA * h + xa, p['ln_xa_g'][layer], p['ln_xa_b'][layer])
        ff = swiglu(h, p['ffn_w_gate'][layer], p['ffn_w_up'][layer], p['ffn_w_down'][layer])
        h = layer_norm(DEEPNORM_ALPHA * h + ff, p['ln_ffn_g'][layer], p['ln_ffn_b'][layer])
    return h


def _flat_pack(arrays, length, dtype):
    flat = jnp.concatenate([a.reshape(-1).astype(dtype) for a in arrays])
    return jnp.pad(flat, (0, length - flat.shape[0]))


def _unpack(flat, shapes):
    out, at = [], 0
    for shp in shapes:
        n = 1
        for d in shp:
            n *= d
        out.append(flat[at:at + n].reshape(shp))
        at += n
    return out


def _unpack_halves(halves, shapes):
    first, second = halves
    cut = first.shape[0]
    out, at = [], 0
    for shp in shapes:
        n = 1
        for d in shp:
            n *= d
        if at + n <= cut:
            flat = first[at:at + n]
        elif at >= cut:
            flat = second[at - cut:at - cut + n]
        else:
            flat = jnp.concatenate([first[at:], second[:at + n - cut]])
        out.append(flat.reshape(shp))
        at += n
    return out


def _shard_of(full, axis, s):
    n = full.shape[axis] // 4
    return lax.slice_in_dim(full, s * n, (s + 1) * n, axis=axis)


def _to_shards(full, axis):
    shp = full.shape
    t = full.reshape(shp[:axis] + (4, shp[axis] // 4) + shp[axis + 1:])
    return jnp.moveaxis(t, axis, 0).reshape(4, -1)


def _from_shards(flat4, shard_shape, axis):
    t = jnp.moveaxis(flat4.reshape((4,) + tuple(shard_shape)), 0, axis)
    return t.reshape(tuple(shard_shape[:axis]) + (4 * shard_shape[axis],) + tuple(shard_shape[axis + 1:]))


def kernel(x, mem, ev_w_in, ev_mla_q_norm, ev_mla_w_uq, ev_mla_kv_norm, ev_mla_w_ukv, ev_gla_w_gate2, ev_gla_b_gate, ev_gla_norm_g, ev_gla_norm_b, ev_w_out, od_w_in, od_rwkv_mu, od_rwkv_w0, od_rwkv_w_decay2, od_rwkv_a0, od_rwkv_w_a2, od_rwkv_w_gate2, od_rwkv_k_k, od_rwkv_k_a, od_rwkv_r_k, od_rwkv_gn_g, od_rwkv_gn_b, od_w_out, ln_mix_g, ln_mix_b, xa_w_q, xa_w_k, xa_w_v, xa_w_o, ln_xa_g, ln_xa_b, ffn_w_gate, ffn_w_up, ffn_w_down, ln_ffn_g, ln_ffn_b, loss_target, m_ev_w_in, m_ev_mla_q_norm, m_ev_mla_w_uq, m_ev_mla_kv_norm, m_ev_mla_w_ukv, m_ev_gla_w_gate2, m_ev_gla_b_gate, m_ev_gla_norm_g, m_ev_gla_norm_b, m_ev_w_out, m_od_w_in, m_od_rwkv_mu, m_od_rwkv_w0, m_od_rwkv_w_decay2, m_od_rwkv_a0, m_od_rwkv_w_a2, m_od_rwkv_w_gate2, m_od_rwkv_k_k, m_od_rwkv_k_a, m_od_rwkv_r_k, m_od_rwkv_gn_g, m_od_rwkv_gn_b, m_od_w_out, m_ln_mix_g, m_ln_mix_b, m_xa_w_q, m_xa_w_k, m_xa_w_v, m_xa_w_o, m_ln_xa_g, m_ln_xa_b, m_ffn_w_gate, m_ffn_w_up, m_ffn_w_down, m_ln_ffn_g, m_ln_ffn_b, v_ev_w_in, v_ev_mla_q_norm, v_ev_mla_w_uq, v_ev_mla_kv_norm, v_ev_mla_w_ukv, v_ev_gla_w_gate2, v_ev_gla_b_gate, v_ev_gla_norm_g, v_ev_gla_norm_b, v_ev_w_out, v_od_w_in, v_od_rwkv_mu, v_od_rwkv_w0, v_od_rwkv_w_decay2, v_od_rwkv_a0, v_od_rwkv_w_a2, v_od_rwkv_w_gate2, v_od_rwkv_k_k, v_od_rwkv_k_a, v_od_rwkv_r_k, v_od_rwkv_gn_g, v_od_rwkv_gn_b, v_od_w_out, v_ln_mix_g, v_ln_mix_b, v_xa_w_q, v_xa_w_k, v_xa_w_v, v_xa_w_o, v_ln_xa_g, v_ln_xa_b, v_ffn_w_gate, v_ffn_w_up, v_ffn_w_down, v_ln_ffn_g, v_ln_ffn_b):
    given = dict(locals())
    W = {n: given[n] for n in WEIGHT_NAMES}
    Mo = {n: given['m_' + n] for n in WEIGHT_NAMES}
    Vo = {n: given['v_' + n] for n in WEIGHT_NAMES}
    mat_names = [n for n in WEIGHT_NAMES if n in MATRICES]
    vec_names = [n for n in WEIGHT_NAMES if n in SHARDED_VECTORS]
    rep_names = list(REPLICATED)
    mat_shapes = [W[n].shape for n in mat_names]
    vec_shapes = [W[n].shape for n in vec_names]
    rep_shapes = [W[n].shape for n in rep_names]

    def count(shapes):
        total = 0
        for shp in shapes:
            n = 1
            for d in shp:
                n *= d
            total += n
        return total

    rc = -(-count(mat_shapes) // (2 * PACK_COLS * PACK_ROW_ALIGN)) * PACK_ROW_ALIGN
    mat_len = 2 * rc * PACK_COLS
    rv = -(-count(vec_shapes) // (2 * SMALL_COLS * 8)) * 8
    vec_len = 2 * rv * SMALL_COLS
    rr = -(-count(rep_shapes) // (SMALL_COLS * 8)) * 8
    rep_len = rr * SMALL_COLS

    wmat = _flat_pack([W[n] for n in mat_names], mat_len, BF16).reshape(2, rc, PACK_COLS)
    gathered = _all_gather_call(wmat).reshape(4, mat_len)
    gvec = _all_gather_call(_flat_pack([W[n] for n in vec_names], vec_len, F32).reshape(2, rv, SMALL_COLS))
    full, at = {}, 0
    for n, shp in zip(mat_names, mat_shapes):
        cnt = count([shp])
        full[n] = _from_shards(gathered[:, at:at + cnt], shp, MATRICES[n])
        at += cnt
    vec_parts = [_unpack(gvec[s].reshape(-1), vec_shapes) for s in range(4)]
    for i, n in enumerate(vec_names):
        full[n] = jnp.concatenate([vec_parts[s][i] for s in range(4)], axis=SHARDED_VECTORS[n])
    for n in rep_names:
        full[n] = W[n]

    B, S, D = x.shape
    y, vjp = jax.vjp(lambda p, xx: forward(p, xx, mem), full, x)
    dy, part = _loss_call(y.reshape(B * S, D), loss_target.reshape(B * S, D))
    loss = lax.psum(0.5 * part, ('x', 'y', 'c'))
    gfull, grad_x = vjp(dy.reshape(B, S, D))

    gmat = jnp.concatenate([_to_shards(gfull[n], MATRICES[n]).astype(BF16) for n in mat_names], axis=1)
    gmat = jnp.pad(gmat, ((0, 0), (0, mat_len - gmat.shape[1]))).reshape(4, 2, rc, PACK_COLS)
    half = _sum_parts_call(_scatter_call(gmat))
    other = _sibling_exchange_call(half)
    south = lax.axis_index('c') == 0
    ghalves = (jnp.where(south, half, other).reshape(-1), jnp.where(south, other, half).reshape(-1))
    grep = _flat_pack([gfull[n] for n in rep_names], rep_len, F32)
    gsmall = jnp.stack([jnp.concatenate([
        _flat_pack([_shard_of(gfull[n], SHARDED_VECTORS[n], s) for n in vec_names], vec_len, F32), grep])
        for s in range(4)]).reshape(4, 1, 2 * rv + rr, SMALL_COLS)
    gsmall = _sum_parts_call(_scatter_call(jnp.concatenate([gsmall, gsmall], axis=1)))

    def small_pack(src):
        return jnp.concatenate([_flat_pack([src[n] for n in vec_names], vec_len, F32),
                                _flat_pack([src[n] for n in rep_names], rep_len, F32)]).reshape(-1, SMALL_COLS)

    groups = [{}, {}, {}, {}]
    for n, g in zip(mat_names, _unpack_halves(ghalves, mat_shapes)):
        rows = (-1, g.shape[-1])
        outs = _adamw_call(W[n].reshape(rows), g.reshape(rows), Mo[n].reshape(rows), Vo[n].reshape(rows))
        for grp, val in zip(groups, (g,) + outs):
            grp[n] = val.reshape(g.shape)
    small = (gsmall,) + _adamw_call(small_pack(W), gsmall, small_pack(Mo), small_pack(Vo))
    for grp, sm in zip(groups, small):
        sm = sm.reshape(-1)
        grp.update(zip(vec_names, _unpack(sm[:vec_len], vec_shapes)))
        grp.update(zip(rep_names, _unpack(sm[vec_len:], rep_shapes)))
    return (loss, grad_x, *[grp[n] for grp in groups for n in WEIGHT_NAMES])
```

```python
import functools

import jax
import jax.numpy as jnp
from jax import lax
from jax.experimental import pallas as pl
from jax.experimental.pallas import tpu as pltpu

F32 = jnp.float32
BF16 = jnp.bfloat16
MESH = pl.DeviceIdType.MESH

ROPE_THETA = 10000.0
LN_EPS = 1e-5
RMS_EPS = 1e-6
DEPTH = 2
DEEPNORM_ALPHA = (2.0 * DEPTH) ** 0.25
MLA_HEADS, MLA_NOPE, MLA_ROPE, MLA_V, MLA_Q_RANK, MLA_KV_RANK = 8, 128, 64, 128, 512, 256
GLA_HEADS, GLA_DK, GLA_DV, GLA_GATE_RANK, GLA_TAU, GLA_CHUNK = 4, 128, 256, 16, 16.0, 64
DIL_HEADS, DIL_HEAD_DIM = 8, 128
DIL_BRANCHES = ((128, 1), (512, 4), (2048, 16))
RWKV_HEADS, RWKV_HEAD_DIM = 16, 64
RWKV_DECAY_RANK, RWKV_A_RANK, RWKV_GATE_RANK = 96, 96, 256
RWKV_GN_EPS = 64e-5
XA_HEADS = 4
DIL_WIDTH = DIL_HEADS * DIL_HEAD_DIM
RWKV_WIDTH = RWKV_HEADS * RWKV_HEAD_DIM
EVEN_IN_WIDTHS = (MLA_Q_RANK, MLA_KV_RANK, MLA_ROPE, GLA_HEADS * GLA_DK, GLA_HEADS * GLA_DK,
                  GLA_HEADS * GLA_DV, GLA_HEADS * GLA_DV, GLA_GATE_RANK)
RWKV_IN_WIDTHS = (RWKV_WIDTH, RWKV_WIDTH, RWKV_WIDTH, RWKV_DECAY_RANK, RWKV_A_RANK, RWKV_GATE_RANK)
ADAM_LR, ADAM_B1, ADAM_B2, ADAM_EPS, ADAM_WD, ADAM_STEP = 0.001, 0.9, 0.999, 1e-08, 0.01, 10

LANES = 128
SUBLANES = 8
VMEM_LIMIT_BYTES = 48 * 1024 * 1024
NEG_BIG = -1e30

PACK_COLS = 1024
NATIVE_MIN_ELEMENTS = 1 << 20
SMALL_COLS = 128

MATRICES = {
    'ev_w_in': 2, 'ev_mla_w_uq': 2, 'ev_mla_w_ukv': 2, 'ev_gla_w_gate2': 2, 'ev_w_out': 1, 'od_w_in': 2,
    'od_rwkv_w_decay2': 2, 'od_rwkv_w_a2': 2, 'od_rwkv_w_gate2': 2, 'od_w_out': 1,
    'xa_w_q': 1, 'xa_w_k': 1, 'xa_w_v': 1, 'xa_w_o': 1, 'ffn_w_gate': 2, 'ffn_w_up': 2, 'ffn_w_down': 1,
}
SHARDED_VECTORS = {
    'od_rwkv_mu': 1, 'od_rwkv_w0': 1, 'od_rwkv_a0': 1, 'od_rwkv_k_k': 1, 'od_rwkv_k_a': 1,
    'od_rwkv_gn_g': 1, 'od_rwkv_gn_b': 1,
}
REPLICATED = ('ev_mla_q_norm', 'ev_mla_kv_norm', 'ev_gla_b_gate', 'ev_gla_norm_g', 'ev_gla_norm_b', 'od_rwkv_r_k',
              'ln_mix_g', 'ln_mix_b', 'ln_xa_g', 'ln_xa_b', 'ln_ffn_g', 'ln_ffn_b')
WEIGHT_NAMES = ('ev_w_in', 'ev_mla_q_norm', 'ev_mla_w_uq', 'ev_mla_kv_norm', 'ev_mla_w_ukv', 'ev_gla_w_gate2',
                'ev_gla_b_gate', 'ev_gla_norm_g', 'ev_gla_norm_b', 'ev_w_out', 'od_w_in', 'od_rwkv_mu', 'od_rwkv_w0',
                'od_rwkv_w_decay2', 'od_rwkv_a0', 'od_rwkv_w_a2', 'od_rwkv_w_gate2', 'od_rwkv_k_k', 'od_rwkv_k_a',
                'od_rwkv_r_k', 'od_rwkv_gn_g', 'od_rwkv_gn_b', 'od_w_out', 'ln_mix_g', 'ln_mix_b', 'xa_w_q', 'xa_w_k',
                'xa_w_v', 'xa_w_o', 'ln_xa_g', 'ln_xa_b', 'ffn_w_gate', 'ffn_w_up', 'ffn_w_down', 'ln_ffn_g', 'ln_ffn_b')


def _pick(n, cap, mult):
    d = (min(cap, n) // mult) * mult
    while d >= mult:
        if n % d == 0:
            return d
        d -= mult
    return n


def _params(semantics):
    return pltpu.CompilerParams(dimension_semantics=semantics, vmem_limit_bytes=VMEM_LIMIT_BYTES)


_DIMS = {(False, False): (((1,), (0,)), ((), ())), (False, True): (((1,), (1,)), ((), ())),
         (True, False): (((0,), (0,)), ((), ()))}


def _bmm(a, b, ta, tb, out_dtype=F32):
    G = a.shape[0]
    K, M = (a.shape[1], a.shape[2]) if ta else (a.shape[2], a.shape[1])
    N = b.shape[1] if tb else b.shape[2]
    assert (b.shape[2] if tb else b.shape[1]) == K and b.shape[0] == G
    tm, tn = _pick(M, 1024, LANES), _pick(N, 512, LANES)
    tk = K if K <= 2048 else _pick(K, 2048, LANES)
    nk = K // tk
    gb = 1
    if tm == M and tn == N and nk == 1:
        per = 4 * (M * K + K * N + M * N)
        gb = _pick(G, max(1, min(8, (2 << 20) // per)), 1)
    dims = _DIMS[(ta, tb)]

    def body(a_ref, b_ref, o_ref, *scratch):
        def prod(i):
            return lax.dot_general(a_ref[i].astype(BF16), b_ref[i].astype(BF16), dims, preferred_element_type=F32)

        if nk == 1:
            for i in range(gb):
                o_ref[i] = prod(i).astype(o_ref.dtype)
        else:
            acc_ref, = scratch
            k = pl.program_id(3)

            @pl.when(k == 0)
            def _():
                acc_ref[...] = jnp.zeros_like(acc_ref)

            for i in range(gb):
                acc_ref[i] += prod(i)

            @pl.when(k == nk - 1)
            def _():
                o_ref[...] = acc_ref[...].astype(o_ref.dtype)

    a_spec = (pl.BlockSpec((gb, tk, tm), lambda g, i, j, k: (g, k, i)) if ta
              else pl.BlockSpec((gb, tm, tk), lambda g, i, j, k: (g, i, k)))
    b_spec = (pl.BlockSpec((gb, tn, tk), lambda g, i, j, k: (g, j, k)) if tb
              else pl.BlockSpec((gb, tk, tn), lambda g, i, j, k: (g, k, j)))
    return pl.pallas_call(
        body, name='bmm_' + ('t' if ta else 'n') + ('t' if tb else 'n'),
        out_shape=jax.ShapeDtypeStruct((G, M, N), out_dtype),
        grid=(G // gb, M // tm, N // tn, nk),
        in_specs=[a_spec, b_spec],
        out_specs=pl.BlockSpec((gb, tm, tn), lambda g, i, j, k: (g, i, j)),
        scratch_shapes=[] if nk == 1 else [pltpu.VMEM((gb, tm, tn), F32)],
        compiler_params=_params(('parallel', 'parallel', 'parallel', 'arbitrary')),
    )(a, b)


def _like(x):
    return jnp.zeros((), x.dtype)


@jax.custom_vjp
def bmm_nn(a, b):
    return _bmm(a, b, False, False)


def _bmm_nn_fwd(a, b):
    ab, bb = a.astype(BF16), b.astype(BF16)
    return _bmm(ab, bb, False, False), (ab, bb, _like(a), _like(b))


def _bmm_nn_bwd(res, g):
    a, b, la, lb = res
    g = g.astype(BF16)
    return _bmm(g, b, False, True, la.dtype), _bmm(a, g, True, False, lb.dtype)


bmm_nn.defvjp(_bmm_nn_fwd, _bmm_nn_bwd)


@jax.custom_vjp
def bmm_nt(a, b):
    return _bmm(a, b, False, True)


def _bmm_nt_fwd(a, b):
    ab, bb = a.astype(BF16), b.astype(BF16)
    return _bmm(ab, bb, False, True), (ab, bb, _like(a), _like(b))


def _bmm_nt_bwd(res, g):
    a, b, la, lb = res
    g = g.astype(BF16)
    return _bmm(g, b, False, False, la.dtype), _bmm(g, a, True, False, lb.dtype)


bmm_nt.defvjp(_bmm_nt_fwd, _bmm_nt_bwd)


def mm(x, w):
    lead = x.shape[:-1]
    out = bmm_nn(x.reshape(1, -1, x.shape[-1]), w[None])
    return out.reshape(lead + (w.shape[1],))


def _norm_stats(x, center, eps):
    if center:
        xc = x - jnp.mean(x, axis=-1, keepdims=True)
    else:
        xc = x
    rstd = lax.rsqrt(jnp.mean(xc * xc, axis=-1, keepdims=True) + eps)
    return xc * rstd, rstd


def _norm_fwd_call(x, g, b, center, eps):
    R, C = x.shape
    tr = _pick(R, max(8, (1 << 19) // C), 8)

    def body(x_ref, g_ref, b_ref, y_ref):
        xhat, _ = _norm_stats(x_ref[...], center, eps)
        y_ref[...] = xhat * g_ref[...] + b_ref[...]

    row = pl.BlockSpec((tr, C), lambda i: (i, 0))
    vec = pl.BlockSpec((1, C), lambda i: (0, 0))
    return pl.pallas_call(
        body, name='norm_fwd', out_shape=jax.ShapeDtypeStruct((R, C), F32), grid=(R // tr,),
        in_specs=[row, vec, vec], out_specs=row, compiler_params=_params(('parallel',)),
    )(x, g, b)


def _norm_bwd_call(x, g, dy, center, eps):
    R, C = x.shape
    tr = _pick(R, max(8, (1 << 19) // C), 8)

    def body(x_ref, g_ref, dy_ref, dx_ref, dg_ref, db_ref):
        @pl.when(pl.program_id(0) == 0)
        def _():
            dg_ref[...] = jnp.zeros_like(dg_ref)
            db_ref[...] = jnp.zeros_like(db_ref)

        xhat, rstd = _norm_stats(x_ref[...], center, eps)
        dy = dy_ref[...]
        dxh = dy * g_ref[...]
        proj = xhat * jnp.mean(dxh * xhat, axis=-1, keepdims=True)
        if center:
            dx_ref[...] = rstd * (dxh - jnp.mean(dxh, axis=-1, keepdims=True) - proj)
        else:
            dx_ref[...] = rstd * (dxh - proj)
        dg_ref[...] += jnp.sum(dy * xhat, axis=0, keepdims=True)
        db_ref[...] += jnp.sum(dy, axis=0, keepdims=True)

    row = pl.BlockSpec((tr, C), lambda i: (i, 0))
    vec = pl.BlockSpec((1, C), lambda i: (0, 0))
    return pl.pallas_call(
        body, name='norm_bwd',
        out_shape=(jax.ShapeDtypeStruct((R, C), F32), jax.ShapeDtypeStruct((1, C), F32), jax.ShapeDtypeStruct((1, C), F32)),
        grid=(R // tr,), in_specs=[row, vec, row], out_specs=(row, vec, vec), compiler_params=_params(('arbitrary',)),
    )(x, g, dy)


@functools.partial(jax.custom_vjp, nondiff_argnums=(3, 4))
def _norm2d(x, g, b, center, eps):
    return _norm_fwd_call(x, g, b, center, eps)


def _norm2d_fwd(x, g, b, center, eps):
    return _norm_fwd_call(x, g, b, center, eps), (x, g)


def _norm2d_bwd(center, eps, res, dy):
    x, g = res
    return _norm_bwd_call(x, g, dy, center, eps)


_norm2d.defvjp(_norm2d_fwd, _norm2d_bwd)


def layer_norm(x, g, b, eps=LN_EPS):
    C = x.shape[-1]
    return _norm2d(x.reshape(-1, C), g.reshape(1, C), b.reshape(1, C), True, eps).reshape(x.shape)


def rms_norm(x, g):
    C = x.shape[-1]
    return _norm2d(x.reshape(-1, C), g.reshape(1, C), jnp.zeros((1, C), F32), False, RMS_EPS).reshape(x.shape)


def _softmax_fwd_call(s, bias, scale):
    G1, G2, R, C = s.shape
    tr = _pick(R, max(8, (1 << 19) // C), 8)

    def body(s_ref, bias_ref, p_ref, lse_ref):
        z = s_ref[0, 0] * scale + bias_ref[0]
        m = jnp.max(z, axis=-1, keepdims=True)
        e = jnp.exp(z - m)
        den = jnp.sum(e, axis=-1, keepdims=True)
        p_ref[0, 0] = e / den
        lse_ref[0, 0] = m + jnp.log(den)

    blk = pl.BlockSpec((1, 1, tr, C), lambda a, b, r: (a, b, r, 0))
    col = pl.BlockSpec((1, 1, tr, 1), lambda a, b, r: (a, b, r, 0))
    return pl.pallas_call(
        body, name='softmax_fwd',
        out_shape=(jax.ShapeDtypeStruct(s.shape, F32), jax.ShapeDtypeStruct((G1, G2, R, 1), F32)),
        grid=(G1, G2, R // tr), in_specs=[blk, pl.BlockSpec((1, tr, C), lambda a, b, r: (b, r, 0))],
        out_specs=(blk, col), compiler_params=_params(('parallel', 'parallel', 'parallel')),
    )(s, bias)


def _softmax_bwd_call(p, dp, dlse, scale):
    G1, G2, R, C = p.shape
    tr = _pick(R, max(8, (1 << 19) // C), 8)

    def body(p_ref, dp_ref, dlse_ref, ds_ref):
        p = p_ref[0, 0]
        dp = dp_ref[0, 0]
        inner = jnp.sum(dp * p, axis=-1, keepdims=True)
        ds_ref[0, 0] = (p * (dp - inner + dlse_ref[0, 0])) * scale

    blk = pl.BlockSpec((1, 1, tr, C), lambda a, b, r: (a, b, r, 0))
    col = pl.BlockSpec((1, 1, tr, 1), lambda a, b, r: (a, b, r, 0))
    return pl.pallas_call(
        body, name='softmax_bwd', out_shape=jax.ShapeDtypeStruct(p.shape, F32),
        grid=(G1, G2, R // tr), in_specs=[blk, blk, col], out_specs=blk,
        compiler_params=_params(('parallel', 'parallel', 'parallel')),
    )(p, dp, dlse)


@functools.partial(jax.custom_vjp, nondiff_argnums=(2,))
def softmax_lse(s, bias, scale):
    return _softmax_fwd_call(s, bias, scale)


def _softmax_lse_fwd(s, bias, scale):
    p, lse = _softmax_fwd_call(s, bias, scale)
    return (p, lse), (p, bias)


def _softmax_lse_bwd(scale, res, cts):
    p, bias = res
    dp, dlse = cts
    return _softmax_bwd_call(p, dp, dlse, scale), jnp.zeros_like(bias)


softmax_lse.defvjp(_softmax_lse_fwd, _softmax_lse_bwd)


def _dot(a, b, dims):
    return lax.dot_general(a.astype(BF16), b.astype(BF16), dims, preferred_element_type=F32)


_NN, _NT, _TN = _DIMS[(False, False)], _DIMS[(False, True)], _DIMS[(True, False)]


def _gla_fwd_call(q, k, v, dec):
    G, nc, C, dk = q.shape
    dv = v.shape[-1]

    def body(q_ref, k_ref, v_ref, dec_ref, o_ref, st_ref, state):
        @pl.when(pl.program_id(1) == 0)
        def _():
            state[...] = jnp.zeros_like(state)

        s = state[...]
        st_ref[0, 0] = s
        o_ref[0, 0] = _dot(q_ref[0, 0], s, _NN)
        state[...] = s * dec_ref[0, 0] + _dot(k_ref[0, 0], v_ref[0, 0], _TN)

    def spec(r, c):
        return pl.BlockSpec((1, 1, r, c), lambda g, t: (g, t, 0, 0))

    return pl.pallas_call(
        body, name='gla_scan_fwd',
        out_shape=(jax.ShapeDtypeStruct((G, nc, C, dv), F32), jax.ShapeDtypeStruct((G, nc, dk, dv), F32)),
        grid=(G, nc), in_specs=[spec(C, dk), spec(C, dk), spec(C, dv), spec(dk, 1)],
        out_specs=(spec(C, dv), spec(dk, dv)), scratch_shapes=[pltpu.VMEM((dk, dv), F32)],
        compiler_params=_params(('parallel', 'arbitrary')),
    )(q, k, v, dec)


def _gla_bwd_call(q, k, v, dec, states, do):
    G, nc, C, dk = q.shape
    dv = v.shape[-1]

    def body(q_ref, k_ref, v_ref, dec_ref, st_ref, do_ref, dq_ref, dk_ref, dv_ref, ddec_ref, dstate):
        @pl.when(pl.program_id(1) == 0)
        def _():
            dstate[...] = jnp.zeros_like(dstate)

        s = st_ref[0, 0]
        d = dstate[...]
        do = do_ref[0, 0]
        dq_ref[0, 0] = _dot(do, s, _NT)
        dk_ref[0, 0] = _dot(v_ref[0, 0], d, _NT)
        dv_ref[0, 0] = _dot(k_ref[0, 0], d, _NN)
        ddec_ref[0, 0] = jnp.sum(s * d, axis=1, keepdims=True)
        dstate[...] = d * dec_ref[0, 0] + _dot(q_ref[0, 0], do, _TN)

    def spec(r, c):
        return pl.BlockSpec((1, 1, r, c), lambda g, t: (g, nc - 1 - t, 0, 0))

    return pl.pallas_call(
        body, name='gla_scan_bwd',
        out_shape=(jax.ShapeDtypeStruct(q.shape, F32), jax.ShapeDtypeStruct(k.shape, F32),
                   jax.ShapeDtypeStruct(v.shape, F32), jax.ShapeDtypeStruct(dec.shape, F32)),
        grid=(G, nc), in_specs=[spec(C, dk), spec(C, dk), spec(C, dv), spec(dk, 1), spec(dk, dv), spec(C, dv)],
        out_specs=(spec(C, dk), spec(C, dk), spec(C, dv), spec(dk, 1)), scratch_shapes=[pltpu.VMEM((dk, dv), F32)],
        compiler_params=_params(('parallel', 'arbitrary')),
    )(q, k, v, dec, states, do)


@jax.custom_vjp
def gla_scan(q, k, v, dec):
    return _gla_fwd_call(q, k, v, dec)[0]


def _gla_scan_fwd(q, k, v, dec):
    o, states = _gla_fwd_call(q, k, v, dec)
    return o, (q, k, v, dec, states)


def _gla_scan_bwd(res, do):
    return _gla_bwd_call(*res, do)


gla_scan.defvjp(_gla_scan_fwd, _gla_scan_bwd)


RWKV_PAIRS_PER_STEP = 4
RWKV_TIME_BLOCK = 64
RN = RWKV_HEAD_DIM


def _rwkv_consts():
    row = lax.broadcasted_iota(jnp.int32, (RN, LANES), 0)
    lane = lax.broadcasted_iota(jnp.int32, (RN, LANES), 1)
    diag = (lane % RN == row).astype(F32)
    r2 = lax.broadcasted_iota(jnp.int32, (LANES, LANES), 0)
    l2 = lax.broadcasted_iota(jnp.int32, (LANES, LANES), 1)
    seg = (r2 // RN == l2 // RN).astype(BF16)
    return diag, seg


def _stage(lhs_ref, slot, p):
    hi = p.astype(BF16)
    lhs_ref[pl.ds(slot * LANES, RN), :] = hi
    lhs_ref[pl.ds(slot * LANES + RN, RN), :] = (p - hi.astype(F32)).astype(BF16)


def _seg_sums(lhs_ref, nslots, seg):
    res = jnp.dot(lhs_ref[pl.ds(0, nslots * LANES), :], seg, preferred_element_type=F32)
    return [res[i * LANES:i * LANES + RN] + res[i * LANES + RN:(i + 1) * LANES] for i in range(nslots)]


def _rwkv_blocks(B, S, C):
    npairs = C // LANES
    pp = RWKV_PAIRS_PER_STEP if npairs % RWKV_PAIRS_PER_STEP == 0 else 1
    T = _pick(S, RWKV_TIME_BLOCK, 8)
    return npairs, pp, T


def _rwkv_fwd_call(r, w, k, v, kk, b):
    B, S, C = r.shape
    npairs, pp, T = _rwkv_blocks(B, S, C)
    G = SUBLANES

    def body(r_ref, w_ref, k_ref, v_ref, kk_ref, b_ref, y_ref, sall_ref, state, step_lhs, v_lhs, y_lhs):
        @pl.when(pl.program_id(2) == 0)
        def _():
            state[...] = jnp.zeros_like(state)

        diag, seg = _rwkv_consts()
        rowid = lax.broadcasted_iota(jnp.int32, (SUBLANES, LANES), 0)

        def group(t8, carry):
            rows = pl.ds(pl.multiple_of(t8 * G, G), G)
            sls = [slice(p * LANES, (p + 1) * LANES) for p in range(pp)]
            ops = [[ref[0, rows, sl] for ref in (r_ref, w_ref, k_ref, v_ref, kk_ref, b_ref)] for sl in sls]
            for j in range(G):
                for p in range(pp):
                    _stage(v_lhs, j * pp + p, diag * ops[p][3][j:j + 1])
            vcols = _seg_sums(v_lhs, G * pp, seg)
            s = list(carry)
            for j in range(G):
                for p in range(pp):
                    sall_ref[0, p, t8 * G + j] = s[p]
                    _stage(step_lhs, p, s[p] * ops[p][4][j:j + 1])
                sas = _seg_sums(step_lhs, pp, seg)
                for p in range(pp):
                    rt, wt, kt, _, _, bt = ops[p]
                    s[p] = s[p] * wt[j:j + 1] - sas[p] * bt[j:j + 1] + vcols[j * pp + p] * kt[j:j + 1]
                    _stage(y_lhs, j * pp + p, s[p] * rt[j:j + 1])
            ycols = _seg_sums(y_lhs, G * pp, seg)
            for p in range(pp):
                ytile = jnp.zeros((SUBLANES, LANES), F32)
                for j in range(G):
                    ytile = jnp.where(rowid == j, jnp.sum(diag * ycols[j * pp + p], axis=0, keepdims=True), ytile)
                y_ref[0, rows, sls[p]] = ytile
            return tuple(s)

        final = lax.fori_loop(0, T // G, group, tuple(state[p] for p in range(pp)))
        for p in range(pp):
            state[p] = final[p]

    seq = pl.BlockSpec((1, T, pp * LANES), lambda bi, g, t: (bi, t, g))
    return pl.pallas_call(
        body, name='rwkv_scan_fwd',
        out_shape=(jax.ShapeDtypeStruct((B, S, C), F32), jax.ShapeDtypeStruct((B, npairs, S, RN, LANES), F32)),
        grid=(B, npairs // pp, S // T), in_specs=[seq] * 6,
        out_specs=(seq, pl.BlockSpec((1, pp, T, RN, LANES), lambda bi, g, t: (bi, g, t, 0, 0))),
        scratch_shapes=[pltpu.VMEM((pp, RN, LANES), F32), pltpu.VMEM((pp * LANES, LANES), BF16),
                        pltpu.VMEM((G * pp * LANES, LANES), BF16), pltpu.VMEM((G * pp * LANES, LANES), BF16)],
        compiler_params=_params(('parallel', 'parallel', 'arbitrary')),
    )(r, w, k, v, kk, b)


def _rwkv_bwd_call(r, w, k, v, kk, b, sall, dy):
    B, S, C = r.shape
    npairs, pp, T = _rwkv_blocks(B, S, C)
    nt = S // T
    G = SUBLANES

    def body(r_ref, w_ref, k_ref, v_ref, kk_ref, b_ref, sall_ref, dy_ref,
             dr_ref, dw_ref, dk_ref, dv_ref, dkk_ref, db_ref, dstate, step_lhs, pre_lhs, dv_lhs):
        @pl.when(pl.program_id(2) == 0)
        def _():
            dstate[...] = jnp.zeros_like(dstate)

        diag, seg = _rwkv_consts()
        rowid = lax.broadcasted_iota(jnp.int32, (SUBLANES, LANES), 0)

        def colsum(z):
            return jnp.sum(z, axis=0, keepdims=True)

        def group(i, carry):
            t8 = T // G - 1 - i
            rows = pl.ds(pl.multiple_of(t8 * G, G), G)
            sls = [slice(p * LANES, (p + 1) * LANES) for p in range(pp)]
            ops = [[ref[0, rows, sl] for ref in (r_ref, w_ref, k_ref, v_ref, kk_ref, b_ref, dy_ref)] for sl in sls]
            for j in range(G):
                for p in range(pp):
                    _stage(pre_lhs, j * pp + p, sall_ref[0, p, t8 * G + j] * ops[p][4][j:j + 1])
                    _stage(pre_lhs, (G + j) * pp + p, diag * ops[p][3][j:j + 1])
                    _stage(pre_lhs, (2 * G + j) * pp + p, diag * ops[p][6][j:j + 1])
            pre = _seg_sums(pre_lhs, 3 * G * pp, seg)
            ds = list(carry)
            tiles = [[jnp.zeros((SUBLANES, LANES), F32) for _ in range(5)] for _ in range(pp)]
            for j in reversed(range(G)):
                d = []
                for p in range(pp):
                    rt, _, kt, _, _, bt, _ = ops[p]
                    d.append(ds[p] + pre[(2 * G + j) * pp + p] * rt[j:j + 1])
                    _stage(step_lhs, p, d[p] * bt[j:j + 1])
                    _stage(dv_lhs, j * pp + p, d[p] * kt[j:j + 1])
                dsas = _seg_sums(step_lhs, pp, seg)
                for p in range(pp):
                    rt, wt, kt, _, kkt, bt, _ = ops[p]
                    s = sall_ref[0, p, t8 * G + j]
                    sa, vcol, dycol = -pre[j * pp + p], pre[(G + j) * pp + p], pre[(2 * G + j) * pp + p]
                    s2 = s * wt[j:j + 1] + sa * bt[j:j + 1] + vcol * kt[j:j + 1]
                    vals = (colsum(s2 * dycol), colsum(d[p] * s), colsum(d[p] * vcol), -colsum(s * dsas[p]),
                            colsum(d[p] * sa))
                    tiles[p] = [jnp.where(rowid == j, val, tile) for val, tile in zip(vals, tiles[p])]
                    ds[p] = d[p] * wt[j:j + 1] - dsas[p] * kkt[j:j + 1]
            dvcols = _seg_sums(dv_lhs, G * pp, seg)
            for p in range(pp):
                dvt = jnp.zeros((SUBLANES, LANES), F32)
                for j in range(G):
                    dvt = jnp.where(rowid == j, colsum(diag * dvcols[j * pp + p]), dvt)
                dv_ref[0, rows, sls[p]] = dvt
                for ref, tile in zip((dr_ref, dw_ref, dk_ref, dkk_ref, db_ref), tiles[p]):
                    ref[0, rows, sls[p]] = tile
            return tuple(ds)

        final = lax.fori_loop(0, T // G, group, tuple(dstate[p] for p in range(pp)))
        for p in range(pp):
            dstate[p] = final[p]

    seq = pl.BlockSpec((1, T, pp * LANES), lambda bi, g, t: (bi, nt - 1 - t, g))
    sds = jax.ShapeDtypeStruct((B, S, C), F32)
    return pl.pallas_call(
        body, name='rwkv_scan_bwd', out_shape=(sds,) * 6,
        grid=(B, npairs // pp, nt),
        in_specs=[seq] * 6 + [pl.BlockSpec((1, pp, T, RN, LANES), lambda bi, g, t: (bi, g, nt - 1 - t, 0, 0)), seq],
        out_specs=(seq,) * 6,
        scratch_shapes=[pltpu.VMEM((pp, RN, LANES), F32), pltpu.VMEM((pp * LANES, LANES), BF16),
                        pltpu.VMEM((3 * G * pp * LANES, LANES), BF16), pltpu.VMEM((G * pp * LANES, LANES), BF16)],
        compiler_params=_params(('parallel', 'parallel', 'arbitrary')),
    )(r, w, k, v, kk, b, sall, dy)


@jax.custom_vjp
def rwkv_scan(r, w, k, v, kk, b):
    return _rwkv_fwd_call(r, w, k, v, kk, b)[0]


def _rwkv_scan_fwd(r, w, k, v, kk, b):
    y, sall = _rwkv_fwd_call(r, w, k, v, kk, b)
    return y, (r, w, k, v, kk, b, sall)


def _rwkv_scan_bwd(res, dy):
    return _rwkv_bwd_call(*res, dy)


rwkv_scan.defvjp(_rwkv_scan_fwd, _rwkv_scan_bwd)


def _loss_call(y, target):
    R, D = y.shape
    tr = _pick(R, max(8, (1 << 19) // D), 8)

    def body(y_ref, t_ref, dy_ref, part_ref):
        @pl.when(pl.program_id(0) == 0)
        def _():
            part_ref[...] = jnp.zeros_like(part_ref)

        diff = y_ref[...] - t_ref[...]
        dy_ref[...] = diff / D
        part_ref[...] += jnp.sum(jnp.mean(diff * diff, axis=-1, keepdims=True), axis=0, keepdims=True)

    row = pl.BlockSpec((tr, D), lambda i: (i, 0))
    dy, part = pl.pallas_call(
        body, name='loss_head',
        out_shape=(jax.ShapeDtypeStruct((R, D), F32), jax.ShapeDtypeStruct((1, 1), F32)),
        grid=(R // tr,), in_specs=[row, row], out_specs=(row, pl.BlockSpec((1, 1), lambda i: (0, 0))),
        compiler_params=_params(('arbitrary',)),
    )(y, target)
    return dy, part[0, 0]


def _sum_parts_call(parts):
    P, R, C = parts.shape
    tr = _pick(R, max(BF16_TILE_ROWS, (1 << 18) // C), BF16_TILE_ROWS)

    def body(p_ref, o_ref):
        acc = p_ref[0].astype(F32)
        for i in range(1, P):
            acc = acc + p_ref[i].astype(F32)
        o_ref[...] = acc

    return pl.pallas_call(
        body, name='sum_parts', out_shape=jax.ShapeDtypeStruct((R, C), F32), grid=(R // tr,),
        in_specs=[pl.BlockSpec((P, tr, C), lambda i: (0, i, 0))], out_specs=pl.BlockSpec((tr, C), lambda i: (i, 0)),
        compiler_params=_params(('parallel',)),
    )(parts)


def _adamw_call(w, g, m, v):
    R, C = w.shape
    tr = _pick(R, max(8, (1 << 18) // C), 8)

    def body(w_ref, g_ref, m_ref, v_ref, d_ref, nm_ref, nv_ref):
        g = g_ref[...]
        m = ADAM_B1 * m_ref[...] + (1.0 - ADAM_B1) * g
        v = ADAM_B2 * v_ref[...] + (1.0 - ADAM_B2) * (g * g)
        m_hat = m / (1.0 - ADAM_B1 ** ADAM_STEP)
        v_hat = v / (1.0 - ADAM_B2 ** ADAM_STEP)
        d_ref[...] = -ADAM_LR * (m_hat / (jnp.sqrt(v_hat) + ADAM_EPS) + ADAM_WD * w_ref[...])
        nm_ref[...] = m
        nv_ref[...] = v

    row = pl.BlockSpec((tr, C), lambda i: (i, 0))
    sds = jax.ShapeDtypeStruct((R, C), F32)
    return pl.pallas_call(
        body, name='adamw', out_shape=(sds, sds, sds), grid=(R // tr,),
        in_specs=[row] * 4, out_specs=(row,) * 3, compiler_params=_params(('parallel',)),
    )(w, g, m, v)


ANY = pl.BlockSpec(memory_space=pl.ANY)


def _place():
    return lax.axis_index('x'), lax.axis_index('y'), lax.axis_index('c')


COPY_SPLIT = 8
BF16_TILE_ROWS = 16
PACK_ROW_ALIGN = COPY_SPLIT * BF16_TILE_ROWS


def _row_split(rows):
    if rows % PACK_ROW_ALIGN == 0:
        return COPY_SPLIT, rows // COPY_SPLIT
    return 1, rows


def _all_gather_call(pack):
    _, R, C = pack.shape
    ns, rs = _row_split(R)

    def body(pk_ref, out_ref, send_sems, recv_sems):
        x, y, c = _place()
        chips = [(1 - x, y), (x, 1 - y), (1 - x, 1 - y)]
        me = 2 * x + y

        def copy(k, i, src, dst, to):
            rows = pl.ds(i * rs, rs)
            return pltpu.make_async_remote_copy(src_ref=src.at[rows], dst_ref=dst.at[rows], send_sem=send_sems.at[k * ns + i],
                                                recv_sem=recv_sems.at[k * ns + i], device_id=to, device_id_type=MESH)

        first = [copy(j, i, pk_ref.at[c], out_ref.at[me, c], (px, py, c))
                 for j, (px, py) in enumerate(chips) for i in range(ns)]
        for cp in first:
            cp.start()
        passed = []
        for i in range(ns):
            for j, (px, py) in enumerate(chips):
                landed = out_ref.at[2 * px + py, c]
                copy(j, i, landed, landed, (px, py, c)).wait_recv()
                fwd = copy(3 + j, i, landed, landed, (x, y, 1 - c))
                fwd.start()
                passed.append(fwd)
        for i in range(ns):
            for j, (px, py) in enumerate(chips):
                other = out_ref.at[2 * px + py, 1 - c]
                copy(3 + j, i, other, other, (x, y, 1 - c)).wait_recv()
        for cp in first + passed:
            cp.wait_send()

    others = pl.pallas_call(
        body, name='all_gather', out_shape=jax.ShapeDtypeStruct((4, 2, R, C), pack.dtype),
        in_specs=[ANY], out_specs=ANY,
        scratch_shapes=[pltpu.SemaphoreType.DMA((6 * ns,)), pltpu.SemaphoreType.DMA((6 * ns,))],
    )(pack)
    x, y, _ = _place()
    return lax.dynamic_update_slice(others, pack[None], (2 * x + y, 0, 0, 0))


def _scatter_call(src):
    _, _, R, C = src.shape
    ns, rs = _row_split(R)

    def body(src_ref, out_ref, send_sems, recv_sems):
        x, y, c = _place()
        me = 4 * x + 2 * y + c
        peers = []
        for rel in range(1, 8):
            px = 1 - x if rel & 4 else x
            py = 1 - y if rel & 2 else y
            pc = 1 - c if rel & 1 else c
            peers.append((rel - 1, px, py, pc))

        def copy(k, i, src, dst, to):
            rows = pl.ds(i * rs, rs)
            return pltpu.make_async_remote_copy(src_ref=src.at[rows], dst_ref=dst.at[rows], send_sem=send_sems.at[k * ns + i],
                                                recv_sem=recv_sems.at[k * ns + i], device_id=to, device_id_type=MESH)

        sends = [copy(k, i, src_ref.at[2 * px + py, pc], out_ref.at[me], (px, py, pc))
                 for i in range(ns) for k, px, py, pc in peers]
        for cp in sends:
            cp.start()
        for i in range(ns):
            for k, px, py, pc in peers:
                slot = out_ref.at[4 * px + 2 * py + pc]
                copy(k, i, slot, slot, (px, py, pc)).wait_recv()
        for cp in sends:
            cp.wait_send()

    others = pl.pallas_call(
        body, name='scatter_parts', out_shape=jax.ShapeDtypeStruct((8, R, C), src.dtype),
        in_specs=[ANY], out_specs=ANY,
        scratch_shapes=[pltpu.SemaphoreType.DMA((7 * ns,)), pltpu.SemaphoreType.DMA((7 * ns,))],
    )(src)
    x, y, c = _place()
    own = lax.dynamic_slice(src, (2 * x + y, c, 0, 0), (1, 1, R, C)).reshape(1, R, C)
    return lax.dynamic_update_slice(others, own, (4 * x + 2 * y + c, 0, 0))


def _sibling_exchange_call(half):
    R, C = half.shape
    ns, rs = _row_split(R)

    def body(h_ref, other_ref, send_sems, recv_sems):
        x, y, c = _place()

        def copy(i):
            rows = pl.ds(i * rs, rs)
            return pltpu.make_async_remote_copy(src_ref=h_ref.at[rows], dst_ref=other_ref.at[rows], send_sem=send_sems.at[i],
                                                recv_sem=recv_sems.at[i], device_id=(x, y, 1 - c), device_id_type=MESH)

        sends = [copy(i) for i in range(ns)]
        for cp in sends:
            cp.start()
        for cp in sends:
            cp.wait_recv()
        for cp in sends:
            cp.wait_send()

    return pl.pallas_call(
        body, name='sibling_exchange', out_shape=jax.ShapeDtypeStruct((R, C), half.dtype),
        in_specs=[ANY], out_specs=ANY,
        scratch_shapes=[pltpu.SemaphoreType.DMA((ns,)), pltpu.SemaphoreType.DMA((ns,))],
    )(half)


def rope_tables(seq_len, dim):
    inv = ROPE_THETA ** (-jnp.arange(0, dim, 2, dtype=F32) / dim)
    ang = jnp.arange(seq_len, dtype=F32)[:, None] * inv[None, :]
    return jnp.cos(ang), jnp.sin(ang)


def apply_rope(x, cos, sin):
    x1, x2 = jnp.split(x, 2, axis=-1)
    return jnp.concatenate([x1 * cos - x2 * sin, x1 * sin + x2 * cos], axis=-1)


def _pad_to(n):
    return -(-n // LANES) * LANES


def _pad_cols(w, widths):
    parts, at = [], 0
    for n in widths:
        parts.append(jnp.pad(w[..., at:at + n], [(0, 0)] * (w.ndim - 1) + [(0, _pad_to(n) - n)]))
        at += n
    return jnp.concatenate(parts, axis=-1)


def _split_padded(t, widths):
    out, at = [], 0
    for n in widths:
        out.append(t[..., at:at + n])
        at += _pad_to(n)
    return out


def _pad_rows(w, rows):
    return jnp.pad(w, ((0, rows - w.shape[0]), (0, 0)))


def _heads_attention(q, k, v, bias, scale):
    s = bmm_nt(q, k)
    p, _ = softmax_lse(s[:, None], bias[None], scale)
    return bmm_nn(p[:, 0], v)


def gla(q, k, v, r, gate_lr, w_gate2, b_gate, norm_g, norm_b):
    B, S, _ = q.shape
    H, dk, dv, C = GLA_HEADS, GLA_DK, GLA_DV, GLA_CHUNK
    nc = S // C
    log_a = jax.nn.log_sigmoid(mm(gate_lr, w_gate2) + b_gate) / GLA_TAU

    def chunks(t, d):
        return t.reshape(B, nc, C, H, d).transpose(0, 3, 1, 2, 4)

    qc = chunks(q, dk) * (dk ** -0.5)
    kc = chunks(k, dk)
    vc = chunks(v, dv)
    b = jnp.cumsum(chunks(log_a, dk), axis=3)
    b_last = b[:, :, :, -1:, :]
    q_dec = qc * jnp.exp(b)
    k_inv = kc * jnp.exp(-b)
    k_end = kc * jnp.exp(b_last - b)
    causal = jnp.tril(jnp.ones((C, C), dtype=bool))
    G = B * H
    att = bmm_nt(q_dec.reshape(G * nc, C, dk), k_inv.reshape(G * nc, C, dk))
    att = jnp.where(causal, att, 0.0)
    o_intra = bmm_nn(att, vc.reshape(G * nc, C, dv)).reshape(B, H, nc, C, dv)
    dec = jnp.exp(b_last[:, :, :, 0, :]).reshape(G, nc, dk, 1)
    o_inter = gla_scan(q_dec.reshape(G, nc, C, dk), k_end.reshape(G, nc, C, dk), vc.reshape(G, nc, C, dv), dec)
    o = o_intra + o_inter.reshape(B, H, nc, C, dv)
    o = o.transpose(0, 2, 3, 1, 4).reshape(B, S, H, dv)
    o = layer_norm(o, norm_g, norm_b).reshape(B, S, H * dv)
    return o * jax.nn.silu(r)


def even_mixer(x, p):
    B, S, _ = x.shape
    H = MLA_HEADS
    cos, sin = rope_tables(S, MLA_ROPE)
    z = mm(x, _pad_cols(p['ev_w_in'][0], EVEN_IN_WIDTHS))
    c_q, c_kv, k_pe, q_g, k_g, v_g, r_g, _ = _split_padded(z, EVEN_IN_WIDTHS)
    lr_at = sum(_pad_to(n) for n in EVEN_IN_WIDTHS[:-1])
    lr_g = z[..., lr_at:]
    q = mm(rms_norm(c_q, p['ev_mla_q_norm'][0]), p['ev_mla_w_uq'][0])
    q = q.reshape(B, S, H, MLA_NOPE + MLA_ROPE).transpose(0, 2, 1, 3)
    kv = mm(rms_norm(c_kv, p['ev_mla_kv_norm'][0]), p['ev_mla_w_ukv'][0])
    kv = kv.reshape(B, S, H, MLA_NOPE + MLA_V).transpose(0, 2, 1, 3)
    q_pe = apply_rope(q[..., MLA_NOPE:], cos, sin)
    k_pe = jnp.broadcast_to(apply_rope(k_pe[:, None], cos, sin), (B, H, S, MLA_ROPE))
    qf = jnp.concatenate([q[..., :MLA_NOPE], q_pe], axis=-1)
    kf = jnp.concatenate([kv[..., :MLA_NOPE], k_pe], axis=-1)
    pos = jnp.arange(S)
    bias = jnp.where(pos[None, :] <= pos[:, None], 0.0, NEG_BIG).astype(F32)
    a_out = _heads_attention(qf.reshape(B * H, S, -1), kf.reshape(B * H, S, -1),
                             kv[..., MLA_NOPE:].reshape(B * H, S, MLA_V), bias, (MLA_NOPE + MLA_ROPE) ** -0.5)
    a_out = a_out.reshape(B, H, S, MLA_V).transpose(0, 2, 1, 3).reshape(B, S, H * MLA_V)
    w_gate2 = _pad_rows(p['ev_gla_w_gate2'][0], lr_g.shape[-1])
    b_out = gla(q_g, k_g, v_g, r_g, lr_g, w_gate2, p['ev_gla_b_gate'][0], p['ev_gla_norm_g'][0], p['ev_gla_norm_b'][0])
    return mm(jnp.concatenate([a_out, b_out], axis=-1), p['ev_w_out'][0])


def dilated_branch(q, k, v, window, dil):
    B, H, S, dh = q.shape
    span = window // dil
    L = S // dil
    nb = -(-L // span)
    Lp = nb * span

    def residues(t):
        t = t.reshape(B, H, L, dil, dh).transpose(0, 1, 3, 2, 4)
        t = jnp.pad(t, ((0, 0), (0, 0), (0, 0), (0, Lp - L), (0, 0)))
        return t.reshape(B, H, dil, nb, span, dh)

    def with_prev(t):
        prev = jnp.pad(t, ((0, 0), (0, 0), (0, 0), (1, 0), (0, 0), (0, 0)))[:, :, :, :-1]
        return jnp.concatenate([prev, t], axis=4)

    qb = residues(q)
    kw, vw = with_prev(residues(k)), with_prev(residues(v))
    G = B * H * dil * nb
    s = bmm_nt(qb.reshape(G, span, dh), kw.reshape(G, 2 * span, dh))
    qi = jnp.arange(span)[:, None] + span
    kj = jnp.arange(2 * span)[None, :]
    dist = qi - kj
    in_band = (dist >= 0) & (dist <= span)
    has_prev = (jnp.arange(nb) > 0)[:, None, None] | (kj >= span)[None]
    valid = in_band[None] & has_prev
    bias = jnp.where(valid, 0.0, NEG_BIG).astype(F32)
    p, lse = softmax_lse(s.reshape(B * H * dil, nb, span, 2 * span), bias, dh ** -0.5)
    o = bmm_nn(p.reshape(G, span, 2 * span), vw.reshape(G, 2 * span, dh)).reshape(B, H, dil, nb, span, dh)
    lse = lse.reshape(B, H, dil, nb, span)

    def back(t):
        t = t.reshape((B, H, dil, Lp) + t.shape[5:])[:, :, :, :L]
        return jnp.moveaxis(t, 2, 3).reshape((B, H, S) + t.shape[4:])

    return back(o), back(lse)


def dilated_mixture(q, k, v):
    outs, lses = [], []
    for window, dil in DIL_BRANCHES:
        o, lse = dilated_branch(q, k, v, window, dil)
        outs.append(o)
        lses.append(lse)
    wts = jax.nn.softmax(jnp.stack(lses, axis=0), axis=0)
    return jnp.sum(wts[..., None] * jnp.stack(outs, axis=0), axis=0)


def token_shift(t, mu):
    prev = jnp.pad(t, ((0, 0), (1, 0), (0, 0)))[:, :-1]
    return t + (prev - t) * mu


def rwkv7(r, k, v, w_lr, a_lr, g_lr, w0, w_decay2, a0, w_a2, w_gate2, k_k, k_a, r_k, gn_g, gn_b):
    B, S, _ = r.shape
    H, n = RWKV_HEADS, RWKV_HEAD_DIM
    w = -jax.nn.softplus(-(w0 + mm(jnp.tanh(w_lr), w_decay2))) - 0.5
    decay = jnp.exp(-jnp.exp(w))
    a = jax.nn.sigmoid(a0 + mm(a_lr, w_a2))
    g = mm(jax.nn.sigmoid(g_lr), w_gate2)
    kk = (k * k_k).reshape(B, S, H, n)
    kk = kk / jnp.maximum(jnp.sqrt(jnp.sum(kk * kk, axis=-1, keepdims=True)), 1e-12)
    kk = kk.reshape(B, S, H * n)
    kh = k * (1.0 + (a - 1.0) * k_a)
    y = rwkv_scan(r, decay, kh, v, kk, kk * a).reshape(B, S, H, n)
    y = layer_norm(y, jnp.ones((n,), F32), jnp.zeros((n,), F32), RWKV_GN_EPS).reshape(B, S, H * n) * gn_g + gn_b
    bonus = jnp.sum((r * kh).reshape(B, S, H, n) * r_k, axis=-1, keepdims=True) * v.reshape(B, S, H, n)
    y = y + bonus.reshape(B, S, H * n)
    return y * g


def odd_mixer(x, p):
    B, S, _ = x.shape
    cos, sin = rope_tables(S, DIL_HEAD_DIM)
    widths = (3 * DIL_WIDTH,) + RWKV_IN_WIDTHS
    h = mm(x, _pad_cols(p['od_w_in'][0], widths))
    c_in = h[..., :3 * DIL_WIDTH]
    d_in = h[..., 3 * DIL_WIDTH:]
    q, k, v = [t.reshape(B, S, DIL_HEADS, DIL_HEAD_DIM).transpose(0, 2, 1, 3) for t in jnp.split(c_in, 3, axis=-1)]
    q, k = apply_rope(q, cos, sin), apply_rope(k, cos, sin)
    c_out = dilated_mixture(q, k, v).transpose(0, 2, 1, 3).reshape(B, S, DIL_WIDTH)
    mu = _pad_cols(p['od_rwkv_mu'][0], RWKV_IN_WIDTHS)
    sh = token_shift(d_in, mu)
    at = [0]
    for n in RWKV_IN_WIDTHS:
        at.append(at[-1] + _pad_to(n))
    r, kd, vd = [sh[..., at[i]:at[i + 1]] for i in range(3)]
    w_lr, a_lr, g_lr = [sh[..., at[i]:at[i + 1]] for i in range(3, 6)]
    d_out = rwkv7(r, kd, vd, w_lr, a_lr, g_lr, p['od_rwkv_w0'][0], _pad_rows(p['od_rwkv_w_decay2'][0], w_lr.shape[-1]),
                  p['od_rwkv_a0'][0], _pad_rows(p['od_rwkv_w_a2'][0], a_lr.shape[-1]), p['od_rwkv_w_gate2'][0],
                  p['od_rwkv_k_k'][0], p['od_rwkv_k_a'][0], p['od_rwkv_r_k'][0], p['od_rwkv_gn_g'][0], p['od_rwkv_gn_b'][0])
    return mm(jnp.concatenate([c_out, d_out], axis=-1), p['od_w_out'][0])


def cross_attention(x, mem, w_q, w_k, w_v, w_o):
    B, S, D = x.shape
    M = mem.shape[1]
    hd = D // XA_HEADS

    def heads(t, n):
        return t.reshape(B, n, XA_HEADS, hd).transpose(0, 2, 1, 3).reshape(B * XA_HEADS, n, hd)

    q, k, v = heads(mm(x, w_q), S), heads(mm(mem, w_k), M), heads(mm(mem, w_v), M)
    o = _heads_attention(q, k, v, jnp.zeros((S, M), F32), hd ** -0.5)
    o = o.reshape(B, XA_HEADS, S, hd).transpose(0, 2, 1, 3).reshape(B, S, D)
    return mm(o, w_o)


def swiglu(x, w_gate, w_up, w_down):
    return mm(jax.nn.silu(mm(x, w_gate)) * mm(x, w_up), w_down)


def forward(p, x, mem):
    h = x
    for layer in range(DEPTH):
        mix = even_mixer(h, p) if layer % 2 == 0 else odd_mixer(h, p)
        h = layer_norm(DEEPNORM_ALPHA * h + mix, p['ln_mix_g'][layer], p['ln_mix_b'][layer])
        xa = cross_attention(h, mem, p['xa_w_q'][layer], p['xa_w_k'][layer], p['xa_w_v'][layer], p['xa_w_o'][layer])
        h = layer_norm(DEEPNORM_ALPHA * h + xa, p['ln_xa_g'][layer], p['ln_xa_b'][layer])
        ff = swiglu(h, p['ffn_w_gate'][layer], p['ffn_w_up'][layer], p['ffn_w_down'][layer])
        h = layer_norm(DEEPNORM_ALPHA * h + ff, p['ln_ffn_g'][layer], p['ln_ffn_b'][layer])
    return h


def _flat_pack(arrays, length, dtype):
    flat = jnp.concatenate([a.reshape(-1).astype(dtype) for a in arrays])
    return jnp.pad(flat, (0, length - flat.shape[0]))


def _unpack(flat, shapes):
    out, at = [], 0
    for shp in shapes:
        n = 1
        for d in shp:
            n *= d
        out.append(flat[at:at + n].reshape(shp))
        at += n
    return out


def _unpack_halves(halves, shapes):
    first, second = halves
    cut = first.shape[0]
    out, at = [], 0
    for shp in shapes:
        n = 1
        for d in shp:
            n *= d
        if at + n <= cut:
            flat = first[at:at + n]
        elif at >= cut:
            flat = second[at - cut:at - cut + n]
        else:
            flat = jnp.concatenate([first[at:], second[:at + n - cut]])
        out.append(flat.reshape(shp))
        at += n
    return out


def _shard_of(full, axis, s):
    n = full.shape[axis] // 4
    return lax.slice_in_dim(full, s * n, (s + 1) * n, axis=axis)


def kernel(x, mem, ev_w_in, ev_mla_q_norm, ev_mla_w_uq, ev_mla_kv_norm, ev_mla_w_ukv, ev_gla_w_gate2, ev_gla_b_gate, ev_gla_norm_g, ev_gla_norm_b, ev_w_out, od_w_in, od_rwkv_mu, od_rwkv_w0, od_rwkv_w_decay2, od_rwkv_a0, od_rwkv_w_a2, od_rwkv_w_gate2, od_rwkv_k_k, od_rwkv_k_a, od_rwkv_r_k, od_rwkv_gn_g, od_rwkv_gn_b, od_w_out, ln_mix_g, ln_mix_b, xa_w_q, xa_w_k, xa_w_v, xa_w_o, ln_xa_g, ln_xa_b, ffn_w_gate, ffn_w_up, ffn_w_down, ln_ffn_g, ln_ffn_b, loss_target, m_ev_w_in, m_ev_mla_q_norm, m_ev_mla_w_uq, m_ev_mla_kv_norm, m_ev_mla_w_ukv, m_ev_gla_w_gate2, m_ev_gla_b_gate, m_ev_gla_norm_g, m_ev_gla_norm_b, m_ev_w_out, m_od_w_in, m_od_rwkv_mu, m_od_rwkv_w0, m_od_rwkv_w_decay2, m_od_rwkv_a0, m_od_rwkv_w_a2, m_od_rwkv_w_gate2, m_od_rwkv_k_k, m_od_rwkv_k_a, m_od_rwkv_r_k, m_od_rwkv_gn_g, m_od_rwkv_gn_b, m_od_w_out, m_ln_mix_g, m_ln_mix_b, m_xa_w_q, m_xa_w_k, m_xa_w_v, m_xa_w_o, m_ln_xa_g, m_ln_xa_b, m_ffn_w_gate, m_ffn_w_up, m_ffn_w_down, m_ln_ffn_g, m_ln_ffn_b, v_ev_w_in, v_ev_mla_q_norm, v_ev_mla_w_uq, v_ev_mla_kv_norm, v_ev_mla_w_ukv, v_ev_gla_w_gate2, v_ev_gla_b_gate, v_ev_gla_norm_g, v_ev_gla_norm_b, v_ev_w_out, v_od_w_in, v_od_rwkv_mu, v_od_rwkv_w0, v_od_rwkv_w_decay2, v_od_rwkv_a0, v_od_rwkv_w_a2, v_od_rwkv_w_gate2, v_od_rwkv_k_k, v_od_rwkv_k_a, v_od_rwkv_r_k, v_od_rwkv_gn_g, v_od_rwkv_gn_b, v_od_w_out, v_ln_mix_g, v_ln_mix_b, v_xa_w_q, v_xa_w_k, v_xa_w_v, v_xa_w_o, v_ln_xa_g, v_ln_xa_b, v_ffn_w_gate, v_ffn_w_up, v_ffn_w_down, v_ln_ffn_g, v_ln_ffn_b):
    given = dict(locals())
    W = {n: given[n] for n in WEIGHT_NAMES}
    Mo = {n: given['m_' + n] for n in WEIGHT_NAMES}
    Vo = {n: given['v_' + n] for n in WEIGHT_NAMES}
    def count(shapes):
        total = 0
        for shp in shapes:
            n = 1
            for d in shp:
                n *= d
            total += n
        return total

    big_names = [n for n in WEIGHT_NAMES if n in MATRICES and count([given[n].shape]) >= NATIVE_MIN_ELEMENTS]
    mat_names = [n for n in WEIGHT_NAMES if n in MATRICES and n not in big_names]
    vec_names = [n for n in WEIGHT_NAMES if n in SHARDED_VECTORS]
    rep_names = list(REPLICATED)
    mat_shapes = [W[n].shape for n in mat_names]
    vec_shapes = [W[n].shape for n in vec_names]
    rep_shapes = [W[n].shape for n in rep_names]

    def halves_view(a):
        return a.reshape(2, count([a.shape[:-1]]) // 2, a.shape[-1])

    rc = -(-count(mat_shapes) // (2 * PACK_COLS * PACK_ROW_ALIGN)) * PACK_ROW_ALIGN
    mat_len = 2 * rc * PACK_COLS
    rv = -(-count(vec_shapes) // (2 * SMALL_COLS * 8)) * 8
    vec_len = 2 * rv * SMALL_COLS
    rr = -(-count(rep_shapes) // (SMALL_COLS * 8)) * 8
    rep_len = rr * SMALL_COLS

    wmat = _flat_pack([W[n] for n in mat_names], mat_len, BF16).reshape(2, rc, PACK_COLS)
    gathered = _all_gather_call(wmat)
    gvec = _all_gather_call(_flat_pack([W[n] for n in vec_names], vec_len, F32).reshape(2, rv, SMALL_COLS))
    full = {}
    mat_parts = [_unpack(gathered[s].reshape(-1), mat_shapes) for s in range(4)]
    for i, n in enumerate(mat_names):
        full[n] = jnp.concatenate([mat_parts[s][i] for s in range(4)], axis=MATRICES[n])
    for n in big_names:
        g4 = _all_gather_call(halves_view(W[n].astype(BF16))).reshape((4,) + W[n].shape)
        full[n] = jnp.concatenate([g4[s] for s in range(4)], axis=MATRICES[n])
    vec_parts =[_unpack(gvec[s].reshape(-1), vec_shapes) for s in range(4)]
    for i, n in enumerate(vec_names):
        full[n] = jnp.concatenate([vec_parts[s][i] for s in range(4)], axis=SHARDED_VECTORS[n])
    for n in rep_names:
        full[n] = W[n]

    B, S, D = x.shape
    y, vjp = jax.vjp(lambda p, xx: forward(p, xx, mem), full, x)
    dy, part = _loss_call(y.reshape(B * S, D), loss_target.reshape(B * S, D))
    loss = lax.psum(0.5 * part, ('x', 'y', 'c'))
    gfull, grad_x = vjp(dy.reshape(B, S, D))

    gmat = jnp.stack([_flat_pack([_shard_of(gfull[n], MATRICES[n], s) for n in mat_names], mat_len, BF16)
                      for s in range(4)]).reshape(4, 2, rc, PACK_COLS)
    half = _sum_parts_call(_scatter_call(gmat))
    other = _sibling_exchange_call(half)
    south = lax.axis_index('c') == 0
    ghalves = (jnp.where(south, half, other).reshape(-1), jnp.where(south, other, half).reshape(-1))
    grep = _flat_pack([gfull[n] for n in rep_names], rep_len, F32)
    gsmall = jnp.stack([jnp.concatenate([
        _flat_pack([_shard_of(gfull[n], SHARDED_VECTORS[n], s) for n in vec_names], vec_len, F32), grep])
        for s in range(4)]).reshape(4, 1, 2 * rv + rr, SMALL_COLS)
    gsmall = _sum_parts_call(_scatter_call(jnp.concatenate([gsmall, gsmall], axis=1)))

    def small_pack(src):
        return jnp.concatenate([_flat_pack([src[n] for n in vec_names], vec_len, F32),
                                _flat_pack([src[n] for n in rep_names], rep_len, F32)]).reshape(-1, SMALL_COLS)

    groups = [{}, {}, {}, {}]
    grads = dict(zip(mat_names, _unpack_halves(ghalves, mat_shapes)))
    for n in big_names:
        parts = jnp.stack([_shard_of(gfull[n], MATRICES[n], s) for s in range(4)])
        mine = _sum_parts_call(_scatter_call(parts.reshape((4,) + halves_view(W[n]).shape)))
        theirs = _sibling_exchange_call(mine)
        grads[n] = jnp.concatenate([jnp.where(south, mine, theirs), jnp.where(south, theirs, mine)]).reshape(W[n].shape)
    for n in big_names + mat_names:
        g = grads[n]
        rows = (-1, g.shape[-1])
        outs = _adamw_call(W[n].reshape(rows), g.reshape(rows), Mo[n].reshape(rows), Vo[n].reshape(rows))
        for grp, val in zip(groups, (g,) + outs):
            grp[n] = val.reshape(g.shape)
    small = (gsmall,) + _adamw_call(small_pack(W), gsmall, small_pack(Mo), small_pack(Vo))
    for grp, sm in zip(groups, small):
        sm = sm.reshape(-1)
        grp.update(zip(vec_names, _unpack(sm[:vec_len], vec_shapes)))
        grp.update(zip(rep_names, _unpack(sm[vec_len:], rep_shapes)))
    return (loss, grad_x, *[grp[n] for grp in groups for n in WEIGHT_NAMES])
```

```python
import functools

import jax
import jax.numpy as jnp
from jax import lax
from jax.experimental import pallas as pl
from jax.experimental.pallas import tpu as pltpu

F32 = jnp.float32
BF16 = jnp.bfloat16
MESH = pl.DeviceIdType.MESH

ROPE_THETA = 10000.0
LN_EPS = 1e-5
RMS_EPS = 1e-6
DEPTH = 2
DEEPNORM_ALPHA = (2.0 * DEPTH) ** 0.25
MLA_HEADS, MLA_NOPE, MLA_ROPE, MLA_V, MLA_Q_RANK, MLA_KV_RANK = 8, 128, 64, 128, 512, 256
GLA_HEADS, GLA_DK, GLA_DV, GLA_GATE_RANK, GLA_TAU, GLA_CHUNK = 4, 128, 256, 16, 16.0, 64
DIL_HEADS, DIL_HEAD_DIM = 8, 128
DIL_BRANCHES = ((128, 1), (512, 4), (2048, 16))
RWKV_HEADS, RWKV_HEAD_DIM = 16, 64
RWKV_DECAY_RANK, RWKV_A_RANK, RWKV_GATE_RANK = 96, 96, 256
RWKV_GN_EPS = 64e-5
XA_HEADS = 4
DIL_WIDTH = DIL_HEADS * DIL_HEAD_DIM
RWKV_WIDTH = RWKV_HEADS * RWKV_HEAD_DIM
EVEN_IN_WIDTHS = (MLA_Q_RANK, MLA_KV_RANK, MLA_ROPE, GLA_HEADS * GLA_DK, GLA_HEADS * GLA_DK,
                  GLA_HEADS * GLA_DV, GLA_HEADS * GLA_DV, GLA_GATE_RANK)
RWKV_IN_WIDTHS = (RWKV_WIDTH, RWKV_WIDTH, RWKV_WIDTH, RWKV_DECAY_RANK, RWKV_A_RANK, RWKV_GATE_RANK)
ADAM_LR, ADAM_B1, ADAM_B2, ADAM_EPS, ADAM_WD, ADAM_STEP = 0.001, 0.9, 0.999, 1e-08, 0.01, 10

LANES = 128
SUBLANES = 8
VMEM_LIMIT_BYTES = 48 * 1024 * 1024
NEG_BIG = -1e30

PACK_COLS = 1024
NATIVE_MIN_ELEMENTS = 1 << 20
SMALL_COLS = 128

MATRICES = {
    'ev_w_in': 2, 'ev_mla_w_uq': 2, 'ev_mla_w_ukv': 2, 'ev_gla_w_gate2': 2, 'ev_w_out': 1, 'od_w_in': 2,
    'od_rwkv_w_decay2': 2, 'od_rwkv_w_a2': 2, 'od_rwkv_w_gate2': 2, 'od_w_out': 1,
    'xa_w_q': 1, 'xa_w_k': 1, 'xa_w_v': 1, 'xa_w_o': 1, 'ffn_w_gate': 2, 'ffn_w_up': 2, 'ffn_w_down': 1,
}
SHARDED_VECTORS = {
    'od_rwkv_mu': 1, 'od_rwkv_w0': 1, 'od_rwkv_a0': 1, 'od_rwkv_k_k': 1, 'od_rwkv_k_a': 1,
    'od_rwkv_gn_g': 1, 'od_rwkv_gn_b': 1,
}
REPLICATED = ('ev_mla_q_norm', 'ev_mla_kv_norm', 'ev_gla_b_gate', 'ev_gla_norm_g', 'ev_gla_norm_b', 'od_rwkv_r_k',
              'ln_mix_g', 'ln_mix_b', 'ln_xa_g', 'ln_xa_b', 'ln_ffn_g', 'ln_ffn_b')
WEIGHT_NAMES = ('ev_w_in', 'ev_mla_q_norm', 'ev_mla_w_uq', 'ev_mla_kv_norm', 'ev_mla_w_ukv', 'ev_gla_w_gate2',
                'ev_gla_b_gate', 'ev_gla_norm_g', 'ev_gla_norm_b', 'ev_w_out', 'od_w_in', 'od_rwkv_mu', 'od_rwkv_w0',
                'od_rwkv_w_decay2', 'od_rwkv_a0', 'od_rwkv_w_a2', 'od_rwkv_w_gate2', 'od_rwkv_k_k', 'od_rwkv_k_a',
                'od_rwkv_r_k', 'od_rwkv_gn_g', 'od_rwkv_gn_b', 'od_w_out', 'ln_mix_g', 'ln_mix_b', 'xa_w_q', 'xa_w_k',
                'xa_w_v', 'xa_w_o', 'ln_xa_g', 'ln_xa_b', 'ffn_w_gate', 'ffn_w_up', 'ffn_w_down', 'ln_ffn_g', 'ln_ffn_b')


def _pick(n, cap, mult):
    d = (min(cap, n) // mult) * mult
    while d >= mult:
        if n % d == 0:
            return d
        d -= mult
    return n


def _params(semantics):
    return pltpu.CompilerParams(dimension_semantics=semantics, vmem_limit_bytes=VMEM_LIMIT_BYTES)


_DIMS = {(False, False): (((1,), (0,)), ((), ())), (False, True): (((1,), (1,)), ((), ())),
         (True, False): (((0,), (0,)), ((), ()))}


def _bmm(a, b, ta, tb, out_dtype=F32):
    G = a.shape[0]
    K, M = (a.shape[1], a.shape[2]) if ta else (a.shape[2], a.shape[1])
    N = b.shape[1] if tb else b.shape[2]
    assert (b.shape[2] if tb else b.shape[1]) == K and b.shape[0] == G
    tm, tn = _pick(M, 1024, LANES), _pick(N, 512, LANES)
    tk = K if K <= 2048 else _pick(K, 2048, LANES)
    nk = K // tk
    gb = 1
    if tm == M and tn == N and nk == 1:
        per = 4 * (M * K + K * N + M * N)
        gb = _pick(G, max(1, min(8, (2 << 20) // per)), 1)
    dims = _DIMS[(ta, tb)]

    def body(a_ref, b_ref, o_ref, *scratch):
        def prod(i):
            return lax.dot_general(a_ref[i].astype(BF16), b_ref[i].astype(BF16), dims, preferred_element_type=F32)

        if nk == 1:
            for i in range(gb):
                o_ref[i] = prod(i).astype(o_ref.dtype)
        else:
            acc_ref, = scratch
            k = pl.program_id(3)

            @pl.when(k == 0)
            def _():
                acc_ref[...] = jnp.zeros_like(acc_ref)

            for i in range(gb):
                acc_ref[i] += prod(i)

            @pl.when(k == nk - 1)
            def _():
                o_ref[...] = acc_ref[...].astype(o_ref.dtype)

    a_spec = (pl.BlockSpec((gb, tk, tm), lambda g, i, j, k: (g, k, i)) if ta
              else pl.BlockSpec((gb, tm, tk), lambda g, i, j, k: (g, i, k)))
    b_spec = (pl.BlockSpec((gb, tn, tk), lambda g, i, j, k: (g, j, k)) if tb
              else pl.BlockSpec((gb, tk, tn), lambda g, i, j, k: (g, k, j)))
    return pl.pallas_call(
        body, name='bmm_' + ('t' if ta else 'n') + ('t' if tb else 'n'),
        out_shape=jax.ShapeDtypeStruct((G, M, N), out_dtype),
        grid=(G // gb, M // tm, N // tn, nk),
        in_specs=[a_spec, b_spec],
        out_specs=pl.BlockSpec((gb, tm, tn), lambda g, i, j, k: (g, i, j)),
        scratch_shapes=[] if nk == 1 else [pltpu.VMEM((gb, tm, tn), F32)],
        compiler_params=_params(('parallel', 'parallel', 'parallel', 'arbitrary')),
    )(a, b)


TRANSPOSE_OUTSIDE_MIN_ROWS = 1024


def _like(x):
    return jnp.zeros((), x.dtype)


@jax.custom_vjp
def bmm_nn(a, b):
    return _bmm(a, b, False, False)


def _bmm_nn_fwd(a, b):
    ab, bb = a.astype(BF16), b.astype(BF16)
    return _bmm(ab, bb, False, False), (ab, bb, _like(a), _like(b))


def _bmm_nn_bwd(res, g):
    a, b, la, lb = res
    g = g.astype(BF16)
    if a.shape[0] == 1 and a.shape[1] >= TRANSPOSE_OUTSIDE_MIN_ROWS:
        db = _bmm(jnp.swapaxes(a, 1, 2), g, False, False, lb.dtype)
    else:
        db = _bmm(a, g, True, False, lb.dtype)
    return _bmm(g, b, False, True, la.dtype), db


bmm_nn.defvjp(_bmm_nn_fwd, _bmm_nn_bwd)


@jax.custom_vjp
def bmm_nt(a, b):
    return _bmm(a, b, False, True)


def _bmm_nt_fwd(a, b):
    ab, bb = a.astype(BF16), b.astype(BF16)
    return _bmm(ab, bb, False, True), (ab, bb, _like(a), _like(b))


def _bmm_nt_bwd(res, g):
    a, b, la, lb = res
    g = g.astype(BF16)
    return _bmm(g, b, False, False, la.dtype), _bmm(g, a, True, False, lb.dtype)


bmm_nt.defvjp(_bmm_nt_fwd, _bmm_nt_bwd)


def mm(x, w):
    lead = x.shape[:-1]
    out = bmm_nn(x.reshape(1, -1, x.shape[-1]), w[None])
    return out.reshape(lead + (w.shape[1],))


def _norm_stats(x, center, eps):
    if center:
        xc = x - jnp.mean(x, axis=-1, keepdims=True)
    else:
        xc = x
    rstd = lax.rsqrt(jnp.mean(xc * xc, axis=-1, keepdims=True) + eps)
    return xc * rstd, rstd


def _norm_fwd_call(x, g, b, center, eps):
    R, C = x.shape
    tr = _pick(R, max(8, (1 << 19) // C), 8)

    def body(x_ref, g_ref, b_ref, y_ref):
        xhat, _ = _norm_stats(x_ref[...], center, eps)
        y_ref[...] = xhat * g_ref[...] + b_ref[...]

    row = pl.BlockSpec((tr, C), lambda i: (i, 0))
    vec = pl.BlockSpec((1, C), lambda i: (0, 0))
    return pl.pallas_call(
        body, name='norm_fwd', out_shape=jax.ShapeDtypeStruct((R, C), F32), grid=(R // tr,),
        in_specs=[row, vec, vec], out_specs=row, compiler_params=_params(('parallel',)),
    )(x, g, b)


def _norm_bwd_call(x, g, dy, center, eps):
    R, C = x.shape
    tr = _pick(R, max(8, (1 << 19) // C), 8)

    def body(x_ref, g_ref, dy_ref, dx_ref, dg_ref, db_ref):
        @pl.when(pl.program_id(0) == 0)
        def _():
            dg_ref[...] = jnp.zeros_like(dg_ref)
            db_ref[...] = jnp.zeros_like(db_ref)

        xhat, rstd = _norm_stats(x_ref[...], center, eps)
        dy = dy_ref[...]
        dxh = dy * g_ref[...]
        proj = xhat * jnp.mean(dxh * xhat, axis=-1, keepdims=True)
        if center:
            dx_ref[...] = rstd * (dxh - jnp.mean(dxh, axis=-1, keepdims=True) - proj)
        else:
            dx_ref[...] = rstd * (dxh - proj)
        dg_ref[...] += jnp.sum(dy * xhat, axis=0, keepdims=True)
        db_ref[...] += jnp.sum(dy, axis=0, keepdims=True)

    row = pl.BlockSpec((tr, C), lambda i: (i, 0))
    vec = pl.BlockSpec((1, C), lambda i: (0, 0))
    return pl.pallas_call(
        body, name='norm_bwd',
        out_shape=(jax.ShapeDtypeStruct((R, C), F32), jax.ShapeDtypeStruct((1, C), F32), jax.ShapeDtypeStruct((1, C), F32)),
        grid=(R // tr,), in_specs=[row, vec, row], out_specs=(row, vec, vec), compiler_params=_params(('arbitrary',)),
    )(x, g, dy)


@functools.partial(jax.custom_vjp, nondiff_argnums=(3, 4))
def _norm2d(x, g, b, center, eps):
    return _norm_fwd_call(x, g, b, center, eps)


def _norm2d_fwd(x, g, b, center, eps):
    return _norm_fwd_call(x, g, b, center, eps), (x, g)


def _norm2d_bwd(center, eps, res, dy):
    x, g = res
    return _norm_bwd_call(x, g, dy, center, eps)


_norm2d.defvjp(_norm2d_fwd, _norm2d_bwd)


def layer_norm(x, g, b, eps=LN_EPS):
    C = x.shape[-1]
    return _norm2d(x.reshape(-1, C), g.reshape(1, C), b.reshape(1, C), True, eps).reshape(x.shape)


def rms_norm(x, g):
    C = x.shape[-1]
    return _norm2d(x.reshape(-1, C), g.reshape(1, C), jnp.zeros((1, C), F32), False, RMS_EPS).reshape(x.shape)


def _softmax_fwd_call(s, bias, scale):
    G1, G2, R, C = s.shape
    tr = _pick(R, max(8, (1 << 19) // C), 8)

    def body(s_ref, bias_ref, p_ref, lse_ref):
        z = s_ref[0, 0] * scale + bias_ref[0]
        m = jnp.max(z, axis=-1, keepdims=True)
        e = jnp.exp(z - m)
        den = jnp.sum(e, axis=-1, keepdims=True)
        p_ref[0, 0] = e / den
        lse_ref[0, 0] = m + jnp.log(den)

    blk = pl.BlockSpec((1, 1, tr, C), lambda a, b, r: (a, b, r, 0))
    col = pl.BlockSpec((1, 1, tr, 1), lambda a, b, r: (a, b, r, 0))
    return pl.pallas_call(
        body, name='softmax_fwd',
        out_shape=(jax.ShapeDtypeStruct(s.shape, F32), jax.ShapeDtypeStruct((G1, G2, R, 1), F32)),
        grid=(G1, G2, R // tr), in_specs=[blk, pl.BlockSpec((1, tr, C), lambda a, b, r: (b, r, 0))],
        out_specs=(blk, col), compiler_params=_params(('parallel', 'parallel', 'parallel')),
    )(s, bias)


def _softmax_bwd_call(p, dp, dlse, scale):
    G1, G2, R, C = p.shape
    tr = _pick(R, max(8, (1 << 19) // C), 8)

    def body(p_ref, dp_ref, dlse_ref, ds_ref):
        p = p_ref[0, 0]
        dp = dp_ref[0, 0]
        inner = jnp.sum(dp * p, axis=-1, keepdims=True)
        ds_ref[0, 0] = (p * (dp - inner + dlse_ref[0, 0])) * scale

    blk = pl.BlockSpec((1, 1, tr, C), lambda a, b, r: (a, b, r, 0))
    col = pl.BlockSpec((1, 1, tr, 1), lambda a, b, r: (a, b, r, 0))
    return pl.pallas_call(
        body, name='softmax_bwd', out_shape=jax.ShapeDtypeStruct(p.shape, F32),
        grid=(G1, G2, R // tr), in_specs=[blk, blk, col], out_specs=blk,
        compiler_params=_params(('parallel', 'parallel', 'parallel')),
    )(p, dp, dlse)


@functools.partial(jax.custom_vjp, nondiff_argnums=(2,))
def softmax_lse(s, bias, scale):
    return _softmax_fwd_call(s, bias, scale)


def _softmax_lse_fwd(s, bias, scale):
    p, lse = _softmax_fwd_call(s, bias, scale)
    return (p, lse), (p, bias)


def _softmax_lse_bwd(scale, res, cts):
    p, bias = res
    dp, dlse = cts
    return _softmax_bwd_call(p, dp, dlse, scale), jnp.zeros_like(bias)


softmax_lse.defvjp(_softmax_lse_fwd, _softmax_lse_bwd)


def _dot(a, b, dims):
    return lax.dot_general(a.astype(BF16), b.astype(BF16), dims, preferred_element_type=F32)


_NN, _NT, _TN = _DIMS[(False, False)], _DIMS[(False, True)], _DIMS[(True, False)]


def _gla_fwd_call(q, k, v, dec):
    G, nc, C, dk = q.shape
    dv = v.shape[-1]

    def body(q_ref, k_ref, v_ref, dec_ref, o_ref, st_ref, state):
        @pl.when(pl.program_id(1) == 0)
        def _():
            state[...] = jnp.zeros_like(state)

        s = state[...]
        st_ref[0, 0] = s
        o_ref[0, 0] = _dot(q_ref[0, 0], s, _NN)
        state[...] = s * dec_ref[0, 0] + _dot(k_ref[0, 0], v_ref[0, 0], _TN)

    def spec(r, c):
        return pl.BlockSpec((1, 1, r, c), lambda g, t: (g, t, 0, 0))

    return pl.pallas_call(
        body, name='gla_scan_fwd',
        out_shape=(jax.ShapeDtypeStruct((G, nc, C, dv), F32), jax.ShapeDtypeStruct((G, nc, dk, dv), F32)),
        grid=(G, nc), in_specs=[spec(C, dk), spec(C, dk), spec(C, dv), spec(dk, 1)],
        out_specs=(spec(C, dv), spec(dk, dv)), scratch_shapes=[pltpu.VMEM((dk, dv), F32)],
        compiler_params=_params(('parallel', 'arbitrary')),
    )(q, k, v, dec)


def _gla_bwd_call(q, k, v, dec, states, do):
    G, nc, C, dk = q.shape
    dv = v.shape[-1]

    def body(q_ref, k_ref, v_ref, dec_ref, st_ref, do_ref, dq_ref, dk_ref, dv_ref, ddec_ref, dstate):
        @pl.when(pl.program_id(1) == 0)
        def _():
            dstate[...] = jnp.zeros_like(dstate)

        s = st_ref[0, 0]
        d = dstate[...]
        do = do_ref[0, 0]
        dq_ref[0, 0] = _dot(do, s, _NT)
        dk_ref[0, 0] = _dot(v_ref[0, 0], d, _NT)
        dv_ref[0, 0] = _dot(k_ref[0, 0], d, _NN)
        ddec_ref[0, 0] = jnp.sum(s * d, axis=1, keepdims=True)
        dstate[...] = d * dec_ref[0, 0] + _dot(q_ref[0, 0], do, _TN)

    def spec(r, c):
        return pl.BlockSpec((1, 1, r, c), lambda g, t: (g, nc - 1 - t, 0, 0))

    return pl.pallas_call(
        body, name='gla_scan_bwd',
        out_shape=(jax.ShapeDtypeStruct(q.shape, F32), jax.ShapeDtypeStruct(k.shape, F32),
                   jax.ShapeDtypeStruct(v.shape, F32), jax.ShapeDtypeStruct(dec.shape, F32)),
        grid=(G, nc), in_specs=[spec(C, dk), spec(C, dk), spec(C, dv), spec(dk, 1), spec(dk, dv), spec(C, dv)],
        out_specs=(spec(C, dk), spec(C, dk), spec(C, dv), spec(dk, 1)), scratch_shapes=[pltpu.VMEM((dk, dv), F32)],
        compiler_params=_params(('parallel', 'arbitrary')),
    )(q, k, v, dec, states, do)


@jax.custom_vjp
def gla_scan(q, k, v, dec):
    return _gla_fwd_call(q, k, v, dec)[0]


def _gla_scan_fwd(q, k, v, dec):
    o, states = _gla_fwd_call(q, k, v, dec)
    return o, (q, k, v, dec, states)


def _gla_scan_bwd(res, do):
    return _gla_bwd_call(*res, do)


gla_scan.defvjp(_gla_scan_fwd, _gla_scan_bwd)


RWKV_PAIRS_PER_STEP = 8
RWKV_TIME_BLOCK = 32
RN = RWKV_HEAD_DIM


def _rwkv_consts():
    row = lax.broadcasted_iota(jnp.int32, (RN, LANES), 0)
    lane = lax.broadcasted_iota(jnp.int32, (RN, LANES), 1)
    diag = (lane % RN == row).astype(F32)
    r2 = lax.broadcasted_iota(jnp.int32, (LANES, LANES), 0)
    l2 = lax.broadcasted_iota(jnp.int32, (LANES, LANES), 1)
    seg = (r2 // RN == l2 // RN).astype(BF16)
    return diag, seg


def _stage(lhs_ref, slot, p):
    hi = p.astype(BF16)
    lhs_ref[pl.ds(slot * LANES, RN), :] = hi
    lhs_ref[pl.ds(slot * LANES + RN, RN), :] = (p - hi.astype(F32)).astype(BF16)


def _seg_sums(lhs_ref, nslots, seg):
    res = jnp.dot(lhs_ref[pl.ds(0, nslots * LANES), :], seg, preferred_element_type=F32)
    return [res[i * LANES:i * LANES + RN] + res[i * LANES + RN:(i + 1) * LANES] for i in range(nslots)]


def _rwkv_blocks(B, S, C):
    npairs = C // LANES
    pp = RWKV_PAIRS_PER_STEP if npairs % RWKV_PAIRS_PER_STEP == 0 else 1
    T = _pick(S, RWKV_TIME_BLOCK, 8)
    return npairs, pp, T


def _rwkv_fwd_call(r, w, k, v, kk, b):
    B, S, C = r.shape
    npairs, pp, T = _rwkv_blocks(B, S, C)
    G = SUBLANES

    def body(r_ref, w_ref, k_ref, v_ref, kk_ref, b_ref, y_ref, sall_ref, state, step_lhs, v_lhs, y_lhs):
        @pl.when(pl.program_id(2) == 0)
        def _():
            state[...] = jnp.zeros_like(state)

        diag, seg = _rwkv_consts()
        rowid = lax.broadcasted_iota(jnp.int32, (SUBLANES, LANES), 0)

        def group(t8, carry):
            rows = pl.ds(pl.multiple_of(t8 * G, G), G)
            sls = [slice(p * LANES, (p + 1) * LANES) for p in range(pp)]
            ops = [[ref[0, rows, sl] for ref in (r_ref, w_ref, k_ref, v_ref, kk_ref, b_ref)] for sl in sls]
            for j in range(G):
                for p in range(pp):
                    _stage(v_lhs, j * pp + p, diag * ops[p][3][j:j + 1])
            vcols = _seg_sums(v_lhs, G * pp, seg)
            s = list(carry)
            for j in range(G):
                for p in range(pp):
                    sall_ref[0, p, t8 * G + j] = s[p]
                    _stage(step_lhs, p, s[p] * ops[p][4][j:j + 1])
                sas = _seg_sums(step_lhs, pp, seg)
                for p in range(pp):
                    rt, wt, kt, _, _, bt = ops[p]
                    s[p] = s[p] * wt[j:j + 1] - sas[p] * bt[j:j + 1] + vcols[j * pp + p] * kt[j:j + 1]
                    _stage(y_lhs, j * pp + p, s[p] * rt[j:j + 1])
            ycols = _seg_sums(y_lhs, G * pp, seg)
            for p in range(pp):
                ytile = jnp.zeros((SUBLANES, LANES), F32)
                for j in range(G):
                    ytile = jnp.where(rowid == j, jnp.sum(diag * ycols[j * pp + p], axis=0, keepdims=True), ytile)
                y_ref[0, rows, sls[p]] = ytile
            return tuple(s)

        final = lax.fori_loop(0, T // G, group, tuple(state[p] for p in range(pp)))
        for p in range(pp):
            state[p] = final[p]

    seq = pl.BlockSpec((1, T, pp * LANES), lambda bi, g, t: (bi, t, g))
    return pl.pallas_call(
        body, name='rwkv_scan_fwd',
        out_shape=(jax.ShapeDtypeStruct((B, S, C), F32), jax.ShapeDtypeStruct((B, npairs, S, RN, LANES), F32)),
        grid=(B, npairs // pp, S // T), in_specs=[seq] * 6,
        out_specs=(seq, pl.BlockSpec((1, pp, T, RN, LANES), lambda bi, g, t: (bi, g, t, 0, 0))),
        scratch_shapes=[pltpu.VMEM((pp, RN, LANES), F32), pltpu.VMEM((pp * LANES, LANES), BF16),
                        pltpu.VMEM((G * pp * LANES, LANES), BF16), pltpu.VMEM((G * pp * LANES, LANES), BF16)],
        compiler_params=_params(('parallel', 'parallel', 'arbitrary')),
    )(r, w, k, v, kk, b)


def _rwkv_bwd_call(r, w, k, v, kk, b, sall, dy):
    B, S, C = r.shape
    npairs, pp, T = _rwkv_blocks(B, S, C)
    nt = S // T
    G = SUBLANES

    def body(r_ref, w_ref, k_ref, v_ref, kk_ref, b_ref, sall_ref, dy_ref,
             dr_ref, dw_ref, dk_ref, dv_ref, dkk_ref, db_ref, dstate, step_lhs, pre_lhs, dv_lhs):
        @pl.when(pl.program_id(2) == 0)
        def _():
            dstate[...] = jnp.zeros_like(dstate)

        diag, seg = _rwkv_consts()
        rowid = lax.broadcasted_iota(jnp.int32, (SUBLANES, LANES), 0)

        def colsum(z):
            return jnp.sum(z, axis=0, keepdims=True)

        def group(i, carry):
            t8 = T // G - 1 - i
            rows = pl.ds(pl.multiple_of(t8 * G, G), G)
            sls = [slice(p * LANES, (p + 1) * LANES) for p in range(pp)]
            ops = [[ref[0, rows, sl] for ref in (r_ref, w_ref, k_ref, v_ref, kk_ref, b_ref, dy_ref)] for sl in sls]
            for j in range(G):
                for p in range(pp):
                    _stage(pre_lhs, j * pp + p, sall_ref[0, p, t8 * G + j] * ops[p][4][j:j + 1])
                    _stage(pre_lhs, (G + j) * pp + p, diag * ops[p][3][j:j + 1])
                    _stage(pre_lhs, (2 * G + j) * pp + p, diag * ops[p][6][j:j + 1])
            pre = _seg_sums(pre_lhs, 3 * G * pp, seg)
            ds = list(carry)
            tiles = [[jnp.zeros((SUBLANES, LANES), F32) for _ in range(5)] for _ in range(pp)]
            for j in reversed(range(G)):
                d = []
                for p in range(pp):
                    rt, _, kt, _, _, bt, _ = ops[p]
                    d.append(ds[p] + pre[(2 * G + j) * pp + p] * rt[j:j + 1])
                    _stage(step_lhs, p, d[p] * bt[j:j + 1])
                    _stage(dv_lhs, j * pp + p, d[p] * kt[j:j + 1])
                dsas = _seg_sums(step_lhs, pp, seg)
                for p in range(pp):
                    rt, wt, kt, _, kkt, bt, _ = ops[p]
                    s = sall_ref[0, p, t8 * G + j]
                    sa, vcol, dycol = -pre[j * pp + p], pre[(G + j) * pp + p], pre[(2 * G + j) * pp + p]
                    s2 = s * wt[j:j + 1] + sa * bt[j:j + 1] + vcol * kt[j:j + 1]
                    vals = (colsum(s2 * dycol), colsum(d[p] * s), colsum(d[p] * vcol), -colsum(s * dsas[p]),
                            colsum(d[p] * sa))
                    tiles[p] = [jnp.where(rowid == j, val, tile) for val, tile in zip(vals, tiles[p])]
                    ds[p] = d[p] * wt[j:j + 1] - dsas[p] * kkt[j:j + 1]
            dvcols = _seg_sums(dv_lhs, G * pp, seg)
            for p in range(pp):
                dvt = jnp.zeros((SUBLANES, LANES), F32)
                for j in range(G):
                    dvt = jnp.where(rowid == j, colsum(diag * dvcols[j * pp + p]), dvt)
                dv_ref[0, rows, sls[p]] = dvt
                for ref, tile in zip((dr_ref, dw_ref, dk_ref, dkk_ref, db_ref), tiles[p]):
                    ref[0, rows, sls[p]] = tile
            return tuple(ds)

        final = lax.fori_loop(0, T // G, group, tuple(dstate[p] for p in range(pp)))
        for p in range(pp):
            dstate[p] = final[p]

    seq = pl.BlockSpec((1, T, pp * LANES), lambda bi, g, t: (bi, nt - 1 - t, g))
    sds = jax.ShapeDtypeStruct((B, S, C), F32)
    return pl.pallas_call(
        body, name='rwkv_scan_bwd', out_shape=(sds,) * 6,
        grid=(B, npairs // pp, nt),
        in_specs=[seq] * 6 + [pl.BlockSpec((1, pp, T, RN, LANES), lambda bi, g, t: (bi, g, nt - 1 - t, 0, 0)), seq],
        out_specs=(seq,) * 6,
        scratch_shapes=[pltpu.VMEM((pp, RN, LANES), F32), pltpu.VMEM((pp * LANES, LANES), BF16),
                        pltpu.VMEM((3 * G * pp * LANES, LANES), BF16), pltpu.VMEM((G * pp * LANES, LANES), BF16)],
        compiler_params=_params(('parallel', 'parallel', 'arbitrary')),
    )(r, w, k, v, kk, b, sall, dy)


@jax.custom_vjp
def rwkv_scan(r, w, k, v, kk, b):
    return _rwkv_fwd_call(r, w, k, v, kk, b)[0]


def _rwkv_scan_fwd(r, w, k, v, kk, b):
    y, sall = _rwkv_fwd_call(r, w, k, v, kk, b)
    return y, (r, w, k, v, kk, b, sall)


def _rwkv_scan_bwd(res, dy):
    return _rwkv_bwd_call(*res, dy)


rwkv_scan.defvjp(_rwkv_scan_fwd, _rwkv_scan_bwd)


def _loss_call(y, target):
    R, D = y.shape
    tr = _pick(R, max(8, (1 << 19) // D), 8)

    def body(y_ref, t_ref, dy_ref, part_ref):
        @pl.when(pl.program_id(0) == 0)
        def _():
            part_ref[...] = jnp.zeros_like(part_ref)

        diff = y_ref[...] - t_ref[...]
        dy_ref[...] = diff / D
        part_ref[...] += jnp.sum(jnp.mean(diff * diff, axis=-1, keepdims=True), axis=0, keepdims=True)

    row = pl.BlockSpec((tr, D), lambda i: (i, 0))
    dy, part = pl.pallas_call(
        body, name='loss_head',
        out_shape=(jax.ShapeDtypeStruct((R, D), F32), jax.ShapeDtypeStruct((1, 1), F32)),
        grid=(R // tr,), in_specs=[row, row], out_specs=(row, pl.BlockSpec((1, 1), lambda i: (0, 0))),
        compiler_params=_params(('arbitrary',)),
    )(y, target)
    return dy, part[0, 0]


def _sum_parts_call(parts):
    P, R, C = parts.shape
    tr = _pick(R, max(BF16_TILE_ROWS, (1 << 18) // C), BF16_TILE_ROWS)

    def body(p_ref, o_ref):
        acc = p_ref[0].astype(F32)
        for i in range(1, P):
            acc = acc + p_ref[i].astype(F32)
        o_ref[...] = acc

    return pl.pallas_call(
        body, name='sum_parts', out_shape=jax.ShapeDtypeStruct((R, C), F32), grid=(R // tr,),
        in_specs=[pl.BlockSpec((P, tr, C), lambda i: (0, i, 0))], out_specs=pl.BlockSpec((tr, C), lambda i: (i, 0)),
        compiler_params=_params(('parallel',)),
    )(parts)


def _adamw_call(w, g, m, v):
    R, C = w.shape
    tr = _pick(R, max(8, (1 << 18) // C), 8)

    def body(w_ref, g_ref, m_ref, v_ref, d_ref, nm_ref, nv_ref):
        g = g_ref[...]
        m = ADAM_B1 * m_ref[...] + (1.0 - ADAM_B1) * g
        v = ADAM_B2 * v_ref[...] + (1.0 - ADAM_B2) * (g * g)
        m_hat = m / (1.0 - ADAM_B1 ** ADAM_STEP)
        v_hat = v / (1.0 - ADAM_B2 ** ADAM_STEP)
        d_ref[...] = -ADAM_LR * (m_hat / (jnp.sqrt(v_hat) + ADAM_EPS) + ADAM_WD * w_ref[...])
        nm_ref[...] = m
        nv_ref[...] = v

    row = pl.BlockSpec((tr, C), lambda i: (i, 0))
    sds = jax.ShapeDtypeStruct((R, C), F32)
    return pl.pallas_call(
        body, name='adamw', out_shape=(sds, sds, sds), grid=(R // tr,),
        in_specs=[row] * 4, out_specs=(row,) * 3, compiler_params=_params(('parallel',)),
    )(w, g, m, v)


ANY = pl.BlockSpec(memory_space=pl.ANY)


def _place():
    return lax.axis_index('x'), lax.axis_index('y'), lax.axis_index('c')


COPY_SPLIT = 8
BF16_TILE_ROWS = 16
PACK_ROW_ALIGN = COPY_SPLIT * BF16_TILE_ROWS


def _row_split(rows):
    if rows % PACK_ROW_ALIGN == 0:
        return COPY_SPLIT, rows // COPY_SPLIT
    return 1, rows


def _all_gather_call(pack):
    _, R, C = pack.shape
    ns, rs = _row_split(R)

    def body(pk_ref, out_ref, send_sems, recv_sems):
        x, y, c = _place()
        chips = [(1 - x, y), (x, 1 - y), (1 - x, 1 - y)]
        me = 2 * x + y

        def copy(k, i, src, dst, to):
            rows = pl.ds(i * rs, rs)
            return pltpu.make_async_remote_copy(src_ref=src.at[rows], dst_ref=dst.at[rows], send_sem=send_sems.at[k * ns + i],
                                                recv_sem=recv_sems.at[k * ns + i], device_id=to, device_id_type=MESH)

        first = [copy(j, i, pk_ref.at[c], out_ref.at[me, c], (px, py, c))
                 for j, (px, py) in enumerate(chips) for i in range(ns)]
        for cp in first:
            cp.start()
        passed = []
        for i in range(ns):
            for j, (px, py) in enumerate(chips):
                landed = out_ref.at[2 * px + py, c]
                copy(j, i, landed, landed, (px, py, c)).wait_recv()
                fwd = copy(3 + j, i, landed, landed, (x, y, 1 - c))
                fwd.start()
                passed.append(fwd)
        for i in range(ns):
            for j, (px, py) in enumerate(chips):
                other = out_ref.at[2 * px + py, 1 - c]
                copy(3 + j, i, other, other, (x, y, 1 - c)).wait_recv()
        for cp in first + passed:
            cp.wait_send()

    others = pl.pallas_call(
        body, name='all_gather', out_shape=jax.ShapeDtypeStruct((4, 2, R, C), pack.dtype),
        in_specs=[ANY], out_specs=ANY,
        scratch_shapes=[pltpu.SemaphoreType.DMA((6 * ns,)), pltpu.SemaphoreType.DMA((6 * ns,))],
    )(pack)
    x, y, _ = _place()
    return lax.dynamic_update_slice(others, pack[None], (2 * x + y, 0, 0, 0))


def _scatter_call(src):
    _, _, R, C = src.shape
    ns, rs = _row_split(R)

    def body(src_ref, out_ref, send_sems, recv_sems):
        x, y, c = _place()
        me = 4 * x + 2 * y + c
        peers = []
        for rel in range(1, 8):
            px = 1 - x if rel & 4 else x
            py = 1 - y if rel & 2 else y
            pc = 1 - c if rel & 1 else c
            peers.append((rel - 1, px, py, pc))

        def copy(k, i, src, dst, to):
            rows = pl.ds(i * rs, rs)
            return pltpu.make_async_remote_copy(src_ref=src.at[rows], dst_ref=dst.at[rows], send_sem=send_sems.at[k * ns + i],
                                                recv_sem=recv_sems.at[k * ns + i], device_id=to, device_id_type=MESH)

        sends = [copy(k, i, src_ref.at[2 * px + py, pc], out_ref.at[me], (px, py, pc))
                 for i in range(ns) for k, px, py, pc in peers]
        for cp in sends:
            cp.start()
        for i in range(ns):
            for k, px, py, pc in peers:
                slot = out_ref.at[4 * px + 2 * py + pc]
                copy(k, i, slot, slot, (px, py, pc)).wait_recv()
        for cp in sends:
            cp.wait_send()

    others = pl.pallas_call(
        body, name='scatter_parts', out_shape=jax.ShapeDtypeStruct((8, R, C), src.dtype),
        in_specs=[ANY], out_specs=ANY,
        scratch_shapes=[pltpu.SemaphoreType.DMA((7 * ns,)), pltpu.SemaphoreType.DMA((7 * ns,))],
    )(src)
    x, y, c = _place()
    own = lax.dynamic_slice(src, (2 * x + y, c, 0, 0), (1, 1, R, C)).reshape(1, R, C)
    return lax.dynamic_update_slice(others, own, (4 * x + 2 * y + c, 0, 0))


def _sibling_exchange_call(half):
    R, C = half.shape
    ns, rs = _row_split(R)

    def body(h_ref, other_ref, send_sems, recv_sems):
        x, y, c = _place()

        def copy(i):
            rows = pl.ds(i * rs, rs)
            return pltpu.make_async_remote_copy(src_ref=h_ref.at[rows], dst_ref=other_ref.at[rows], send_sem=send_sems.at[i],
                                                recv_sem=recv_sems.at[i], device_id=(x, y, 1 - c), device_id_type=MESH)

        sends = [copy(i) for i in range(ns)]
        for cp in sends:
            cp.start()
        for cp in sends:
            cp.wait_recv()
        for cp in sends:
            cp.wait_send()

    return pl.pallas_call(
        body, name='sibling_exchange', out_shape=jax.ShapeDtypeStruct((R, C), half.dtype),
        in_specs=[ANY], out_specs=ANY,
        scratch_shapes=[pltpu.SemaphoreType.DMA((ns,)), pltpu.SemaphoreType.DMA((ns,))],
    )(half)


def rope_tables(seq_len, dim):
    inv = ROPE_THETA ** (-jnp.arange(0, dim, 2, dtype=F32) / dim)
    ang = jnp.arange(seq_len, dtype=F32)[:, None] * inv[None, :]
    return jnp.cos(ang), jnp.sin(ang)


def apply_rope(x, cos, sin):
    x1, x2 = jnp.split(x, 2, axis=-1)
    return jnp.concatenate([x1 * cos - x2 * sin, x1 * sin + x2 * cos], axis=-1)


def _pad_to(n):
    return -(-n // LANES) * LANES


def _pad_cols(w, widths):
    parts, at = [], 0
    for n in widths:
        parts.append(jnp.pad(w[..., at:at + n], [(0, 0)] * (w.ndim - 1) + [(0, _pad_to(n) - n)]))
        at += n
    return jnp.concatenate(parts, axis=-1)


def _split_padded(t, widths):
    out, at = [], 0
    for n in widths:
        out.append(t[..., at:at + n])
        at += _pad_to(n)
    return out


def _pad_rows(w, rows):
    return jnp.pad(w, ((0, rows - w.shape[0]), (0, 0)))


def _heads_attention(q, k, v, bias, scale):
    s = bmm_nt(q, k)
    p, _ = softmax_lse(s[:, None], bias[None], scale)
    return bmm_nn(p[:, 0], v)


def gla(q, k, v, r, gate_lr, w_gate2, b_gate, norm_g, norm_b):
    B, S, _ = q.shape
    H, dk, dv, C = GLA_HEADS, GLA_DK, GLA_DV, GLA_CHUNK
    nc = S // C
    log_a = jax.nn.log_sigmoid(mm(gate_lr, w_gate2) + b_gate) / GLA_TAU

    def chunks(t, d):
        return t.reshape(B, nc, C, H, d).transpose(0, 3, 1, 2, 4)

    qc = chunks(q, dk) * (dk ** -0.5)
    kc = chunks(k, dk)
    vc = chunks(v, dv)
    b = jnp.cumsum(chunks(log_a, dk), axis=3)
    b_last = b[:, :, :, -1:, :]
    q_dec = qc * jnp.exp(b)
    k_inv = kc * jnp.exp(-b)
    k_end = kc * jnp.exp(b_last - b)
    causal = jnp.tril(jnp.ones((C, C), dtype=bool))
    G = B * H
    att = bmm_nt(q_dec.reshape(G * nc, C, dk), k_inv.reshape(G * nc, C, dk))
    att = jnp.where(causal, att, 0.0)
    o_intra = bmm_nn(att, vc.reshape(G * nc, C, dv)).reshape(B, H, nc, C, dv)
    dec = jnp.exp(b_last[:, :, :, 0, :]).reshape(G, nc, dk, 1)
    o_inter = gla_scan(q_dec.reshape(G, nc, C, dk), k_end.reshape(G, nc, C, dk), vc.reshape(G, nc, C, dv), dec)
    o = o_intra + o_inter.reshape(B, H, nc, C, dv)
    o = o.transpose(0, 2, 3, 1, 4).reshape(B, S, H, dv)
    o = layer_norm(o, norm_g, norm_b).reshape(B, S, H * dv)
    return o * jax.nn.silu(r)


def even_mixer(x, p):
    B, S, _ = x.shape
    H = MLA_HEADS
    cos, sin = rope_tables(S, MLA_ROPE)
    z = mm(x, _pad_cols(p['ev_w_in'][0], EVEN_IN_WIDTHS))
    c_q, c_kv, k_pe, q_g, k_g, v_g, r_g, _ = _split_padded(z, EVEN_IN_WIDTHS)
    lr_at = sum(_pad_to(n) for n in EVEN_IN_WIDTHS[:-1])
    lr_g = z[..., lr_at:]
    q = mm(rms_norm(c_q, p['ev_mla_q_norm'][0]), p['ev_mla_w_uq'][0])
    q = q.reshape(B, S, H, MLA_NOPE + MLA_ROPE).transpose(0, 2, 1, 3)
    kv = mm(rms_norm(c_kv, p['ev_mla_kv_norm'][0]), p['ev_mla_w_ukv'][0])
    kv = kv.reshape(B, S, H, MLA_NOPE + MLA_V).transpose(0, 2, 1, 3)
    q_pe = apply_rope(q[..., MLA_NOPE:], cos, sin)
    k_pe = jnp.broadcast_to(apply_rope(k_pe[:, None], cos, sin), (B, H, S, MLA_ROPE))
    qf = jnp.concatenate([q[..., :MLA_NOPE], q_pe], axis=-1)
    kf = jnp.concatenate([kv[..., :MLA_NOPE], k_pe], axis=-1)
    pos = jnp.arange(S)
    bias = jnp.where(pos[None, :] <= pos[:, None], 0.0, NEG_BIG).astype(F32)
    a_out = _heads_attention(qf.reshape(B * H, S, -1), kf.reshape(B * H, S, -1),
                             kv[..., MLA_NOPE:].reshape(B * H, S, MLA_V), bias, (MLA_NOPE + MLA_ROPE) ** -0.5)
    a_out = a_out.reshape(B, H, S, MLA_V).transpose(0, 2, 1, 3).reshape(B, S, H * MLA_V)
    w_gate2 = _pad_rows(p['ev_gla_w_gate2'][0], lr_g.shape[-1])
    b_out = gla(q_g, k_g, v_g, r_g, lr_g, w_gate2, p['ev_gla_b_gate'][0], p['ev_gla_norm_g'][0], p['ev_gla_norm_b'][0])
    return mm(jnp.concatenate([a_out, b_out], axis=-1), p['ev_w_out'][0])


def dilated_branch(q, k, v, window, dil):
    B, H, S, dh = q.shape
    span = window // dil
    L = S // dil
    nb = -(-L // span)
    Lp = nb * span

    def residues(t):
        t = t.reshape(B, H, L, dil, dh).transpose(0, 1, 3, 2, 4)
        t = jnp.pad(t, ((0, 0), (0, 0), (0, 0), (0, Lp - L), (0, 0)))
        return t.reshape(B, H, dil, nb, span, dh)

    def with_prev(t):
        prev = jnp.pad(t, ((0, 0), (0, 0), (0, 0), (1, 0), (0, 0), (0, 0)))[:, :, :, :-1]
        return jnp.concatenate([prev, t], axis=4)

    qb = residues(q)
    kw, vw = with_prev(residues(k)), with_prev(residues(v))
    G = B * H * dil * nb
    s = bmm_nt(qb.reshape(G, span, dh), kw.reshape(G, 2 * span, dh))
    qi = jnp.arange(span)[:, None] + span
    kj = jnp.arange(2 * span)[None, :]
    dist = qi - kj
    in_band = (dist >= 0) & (dist <= span)
    has_prev = (jnp.arange(nb) > 0)[:, None, None] | (kj >= span)[None]
    valid = in_band[None] & has_prev
    bias = jnp.where(valid, 0.0, NEG_BIG).astype(F32)
    p, lse = softmax_lse(s.reshape(B * H * dil, nb, span, 2 * span), bias, dh ** -0.5)
    o = bmm_nn(p.reshape(G, span, 2 * span), vw.reshape(G, 2 * span, dh)).reshape(B, H, dil, nb, span, dh)
    lse = lse.reshape(B, H, dil, nb, span)

    def back(t):
        t = t.reshape((B, H, dil, Lp) + t.shape[5:])[:, :, :, :L]
        return jnp.moveaxis(t, 2, 3).reshape((B, H, S) + t.shape[4:])

    return back(o), back(lse)


def dilated_mixture(q, k, v):
    outs, lses = [], []
    for window, dil in DIL_BRANCHES:
        o, lse = dilated_branch(q, k, v, window, dil)
        outs.append(o)
        lses.append(lse)
    wts = jax.nn.softmax(jnp.stack(lses, axis=0), axis=0)
    return jnp.sum(wts[..., None] * jnp.stack(outs, axis=0), axis=0)


def token_shift(t, mu):
    prev = jnp.pad(t, ((0, 0), (1, 0), (0, 0)))[:, :-1]
    return t + (prev - t) * mu


def rwkv7(r, k, v, w_lr, a_lr, g_lr, w0, w_decay2, a0, w_a2, w_gate2, k_k, k_a, r_k, gn_g, gn_b):
    B, S, _ = r.shape
    H, n = RWKV_HEADS, RWKV_HEAD_DIM
    w = -jax.nn.softplus(-(w0 + mm(jnp.tanh(w_lr), w_decay2))) - 0.5
    decay = jnp.exp(-jnp.exp(w))
    a = jax.nn.sigmoid(a0 + mm(a_lr, w_a2))
    g = mm(jax.nn.sigmoid(g_lr), w_gate2)
    kk = (k * k_k).reshape(B, S, H, n)
    kk = kk / jnp.maximum(jnp.sqrt(jnp.sum(kk * kk, axis=-1, keepdims=True)), 1e-12)
    kk = kk.reshape(B, S, H * n)
    kh = k * (1.0 + (a - 1.0) * k_a)
    y = rwkv_scan(r, decay, kh, v, kk, kk * a).reshape(B, S, H, n)
    y = layer_norm(y, jnp.ones((n,), F32), jnp.zeros((n,), F32), RWKV_GN_EPS).reshape(B, S, H * n) * gn_g + gn_b
    bonus = jnp.sum((r * kh).reshape(B, S, H, n) * r_k, axis=-1, keepdims=True) * v.reshape(B, S, H, n)
    y = y + bonus.reshape(B, S, H * n)
    return y * g


def odd_mixer(x, p):
    B, S, _ = x.shape
    cos, sin = rope_tables(S, DIL_HEAD_DIM)
    widths = (3 * DIL_WIDTH,) + RWKV_IN_WIDTHS
    h = mm(x, _pad_cols(p['od_w_in'][0], widths))
    c_in = h[..., :3 * DIL_WIDTH]
    d_in = h[..., 3 * DIL_WIDTH:]
    q, k, v = [t.reshape(B, S, DIL_HEADS, DIL_HEAD_DIM).transpose(0, 2, 1, 3) for t in jnp.split(c_in, 3, axis=-1)]
    q, k = apply_rope(q, cos, sin), apply_rope(k, cos, sin)
    c_out = dilated_mixture(q, k, v).transpose(0, 2, 1, 3).reshape(B, S, DIL_WIDTH)
    mu = _pad_cols(p['od_rwkv_mu'][0], RWKV_IN_WIDTHS)
    sh = token_shift(d_in, mu)
    at = [0]
    for n in RWKV_IN_WIDTHS:
        at.append(at[-1] + _pad_to(n))
    r, kd, vd = [sh[..., at[i]:at[i + 1]] for i in range(3)]
    w_lr, a_lr, g_lr = [sh[..., at[i]:at[i + 1]] for i in range(3, 6)]
    d_out = rwkv7(r, kd, vd, w_lr, a_lr, g_lr, p['od_rwkv_w0'][0], _pad_rows(p['od_rwkv_w_decay2'][0], w_lr.shape[-1]),
                  p['od_rwkv_a0'][0], _pad_rows(p['od_rwkv_w_a2'][0], a_lr.shape[-1]), p['od_rwkv_w_gate2'][0],
                  p['od_rwkv_k_k'][0], p['od_rwkv_k_a'][0], p['od_rwkv_r_k'][0], p['od_rwkv_gn_g'][0], p['od_rwkv_gn_b'][0])
    return mm(jnp.concatenate([c_out, d_out], axis=-1), p['od_w_out'][0])


def cross_attention(x, mem, w_q, w_k, w_v, w_o):
    B, S, D = x.shape
    M = mem.shape[1]
    hd = D // XA_HEADS

    def heads(t, n):
        return t.reshape(B, n, XA_HEADS, hd).transpose(0, 2, 1, 3).reshape(B * XA_HEADS, n, hd)

    q, k, v = heads(mm(x, w_q), S), heads(mm(mem, w_k), M), heads(mm(mem, w_v), M)
    o = _heads_attention(q, k, v, jnp.zeros((S, M), F32), hd ** -0.5)
    o = o.reshape(B, XA_HEADS, S, hd).transpose(0, 2, 1, 3).reshape(B, S, D)
    return mm(o, w_o)


def swiglu(x, w_gate, w_up, w_down):
    return mm(jax.nn.silu(mm(x, w_gate)) * mm(x, w_up), w_down)


def forward(p, x, mem):
    h = x
    for layer in range(DEPTH):
        mix = even_mixer(h, p) if layer % 2 == 0 else odd_mixer(h, p)
        h = layer_norm(DEEPNORM_ALPHA * h + mix, p['ln_mix_g'][layer], p['ln_mix_b'][layer])
        xa = cross_attention(h, mem, p['xa_w_q'][layer], p['xa_w_k'][layer], p['xa_w_v'][layer], p['xa_w_o'][layer])
        h = layer_norm(DEEPNORM_ALPHA * h + xa, p['ln_xa_g'][layer], p['ln_xa_b'][layer])
        ff = swiglu(h, p['ffn_w_gate'][layer], p['ffn_w_up'][layer], p['ffn_w_down'][layer])
        h = layer_norm(DEEPNORM_ALPHA * h + ff, p['ln_ffn_g'][layer], p['ln_ffn_b'][layer])
    return h


def _flat_pack(arrays, length, dtype):
    flat = jnp.concatenate([a.reshape(-1).astype(dtype) for a in arrays])
    return jnp.pad(flat, (0, length - flat.shape[0]))


def _unpack(flat, shapes):
    out, at = [], 0
    for shp in shapes:
        n = 1
        for d in shp:
            n *= d
        out.append(flat[at:at + n].reshape(shp))
        at += n
    return out


def _unpack_halves(halves, shapes):
    first, second = halves
    cut = first.shape[0]
    out, at = [], 0
    for shp in shapes:
        n = 1
        for d in shp:
            n *= d
        if at + n <= cut:
            flat = first[at:at + n]
        elif at >= cut:
            flat = second[at - cut:at - cut + n]
        else:
            flat = jnp.concatenate([first[at:], second[:at + n - cut]])
        out.append(flat.reshape(shp))
        at += n
    return out


def _shard_of(full, axis, s):
    n = full.shape[axis] // 4
    return lax.slice_in_dim(full, s * n, (s + 1) * n, axis=axis)


def kernel(x, mem, ev_w_in, ev_mla_q_norm, ev_mla_w_uq, ev_mla_kv_norm, ev_mla_w_ukv, ev_gla_w_gate2, ev_gla_b_gate, ev_gla_norm_g, ev_gla_norm_b, ev_w_out, od_w_in, od_rwkv_mu, od_rwkv_w0, od_rwkv_w_decay2, od_rwkv_a0, od_rwkv_w_a2, od_rwkv_w_gate2, od_rwkv_k_k, od_rwkv_k_a, od_rwkv_r_k, od_rwkv_gn_g, od_rwkv_gn_b, od_w_out, ln_mix_g, ln_mix_b, xa_w_q, xa_w_k, xa_w_v, xa_w_o, ln_xa_g, ln_xa_b, ffn_w_gate, ffn_w_up, ffn_w_down, ln_ffn_g, ln_ffn_b, loss_target, m_ev_w_in, m_ev_mla_q_norm, m_ev_mla_w_uq, m_ev_mla_kv_norm, m_ev_mla_w_ukv, m_ev_gla_w_gate2, m_ev_gla_b_gate, m_ev_gla_norm_g, m_ev_gla_norm_b, m_ev_w_out, m_od_w_in, m_od_rwkv_mu, m_od_rwkv_w0, m_od_rwkv_w_decay2, m_od_rwkv_a0, m_od_rwkv_w_a2, m_od_rwkv_w_gate2, m_od_rwkv_k_k, m_od_rwkv_k_a, m_od_rwkv_r_k, m_od_rwkv_gn_g, m_od_rwkv_gn_b, m_od_w_out, m_ln_mix_g, m_ln_mix_b, m_xa_w_q, m_xa_w_k, m_xa_w_v, m_xa_w_o, m_ln_xa_g, m_ln_xa_b, m_ffn_w_gate, m_ffn_w_up, m_ffn_w_down, m_ln_ffn_g, m_ln_ffn_b, v_ev_w_in, v_ev_mla_q_norm, v_ev_mla_w_uq, v_ev_mla_kv_norm, v_ev_mla_w_ukv, v_ev_gla_w_gate2, v_ev_gla_b_gate, v_ev_gla_norm_g, v_ev_gla_norm_b, v_ev_w_out, v_od_w_in, v_od_rwkv_mu, v_od_rwkv_w0, v_od_rwkv_w_decay2, v_od_rwkv_a0, v_od_rwkv_w_a2, v_od_rwkv_w_gate2, v_od_rwkv_k_k, v_od_rwkv_k_a, v_od_rwkv_r_k, v_od_rwkv_gn_g, v_od_rwkv_gn_b, v_od_w_out, v_ln_mix_g, v_ln_mix_b, v_xa_w_q, v_xa_w_k, v_xa_w_v, v_xa_w_o, v_ln_xa_g, v_ln_xa_b, v_ffn_w_gate, v_ffn_w_up, v_ffn_w_down, v_ln_ffn_g, v_ln_ffn_b):
    given = dict(locals())
    W = {n: given[n] for n in WEIGHT_NAMES}
    Mo = {n: given['m_' + n] for n in WEIGHT_NAMES}
    Vo = {n: given['v_' + n] for n in WEIGHT_NAMES}
    def count(shapes):
        total = 0
        for shp in shapes:
            n = 1
            for d in shp:
                n *= d
            total += n
        return total

    big_names = [n for n in WEIGHT_NAMES if n in MATRICES and count([given[n].shape]) >= NATIVE_MIN_ELEMENTS]
    mat_names = [n for n in WEIGHT_NAMES if n in MATRICES and n not in big_names]
    vec_names = [n for n in WEIGHT_NAMES if n in SHARDED_VECTORS]
    rep_names = list(REPLICATED)
    mat_shapes = [W[n].shape for n in mat_names]
    vec_shapes = [W[n].shape for n in vec_names]
    rep_shapes = [W[n].shape for n in rep_names]

    def halves_view(a):
        return a.reshape(2, count([a.shape[:-1]]) // 2, a.shape[-1])

    rc = -(-count(mat_shapes) // (2 * PACK_COLS * PACK_ROW_ALIGN)) * PACK_ROW_ALIGN
    mat_len = 2 * rc * PACK_COLS
    rv = -(-count(vec_shapes) // (2 * SMALL_COLS * 8)) * 8
    vec_len = 2 * rv * SMALL_COLS
    rr = -(-count(rep_shapes) // (SMALL_COLS * 8)) * 8
    rep_len = rr * SMALL_COLS

    wmat = _flat_pack([W[n] for n in mat_names], mat_len, BF16).reshape(2, rc, PACK_COLS)
    gathered = _all_gather_call(wmat)
    gvec = _all_gather_call(_flat_pack([W[n] for n in vec_names], vec_len, F32).reshape(2, rv, SMALL_COLS))
    full = {}
    mat_parts = [_unpack(gathered[s].reshape(-1), mat_shapes) for s in range(4)]
    for i, n in enumerate(mat_names):
        full[n] = jnp.concatenate([mat_parts[s][i] for s in range(4)], axis=MATRICES[n])
    for n in big_names:
        g4 = _all_gather_call(halves_view(W[n].astype(BF16))).reshape((4,) + W[n].shape)
        full[n] = jnp.concatenate([g4[s] for s in range(4)], axis=MATRICES[n])
    vec_parts =[_unpack(gvec[s].reshape(-1), vec_shapes) for s in range(4)]
    for i, n in enumerate(vec_names):
        full[n] = jnp.concatenate([vec_parts[s][i] for s in range(4)], axis=SHARDED_VECTORS[n])
    for n in rep_names:
        full[n] = W[n]

    B, S, D = x.shape
    y, vjp = jax.vjp(lambda p, xx: forward(p, xx, mem), full, x)
    dy, part = _loss_call(y.reshape(B * S, D), loss_target.reshape(B * S, D))
    loss = lax.psum(0.5 * part, ('x', 'y', 'c'))
    gfull, grad_x = vjp(dy.reshape(B, S, D))

    gmat = jnp.stack([_flat_pack([_shard_of(gfull[n], MATRICES[n], s) for n in mat_names], mat_len, BF16)
                      for s in range(4)]).reshape(4, 2, rc, PACK_COLS)
    half = _sum_parts_call(_scatter_call(gmat))
    other = _sibling_exchange_call(half)
    south = lax.axis_index('c') == 0
    ghalves = (jnp.where(south, half, other).reshape(-1), jnp.where(south, other, half).reshape(-1))
    grep = _flat_pack([gfull[n] for n in rep_names], rep_len, F32)
    gsmall = jnp.stack([jnp.concatenate([
        _flat_pack([_shard_of(gfull[n], SHARDED_VECTORS[n], s) for n in vec_names], vec_len, F32), grep])
        for s in range(4)]).reshape(4, 1, 2 * rv + rr, SMALL_COLS)
    gsmall = _sum_parts_call(_scatter_call(jnp.concatenate([gsmall, gsmall], axis=1)))

    def small_pack(src):
        return jnp.concatenate([_flat_pack([src[n] for n in vec_names], vec_len, F32),
                                _flat_pack([src[n] for n in rep_names], rep_len, F32)]).reshape(-1, SMALL_COLS)

    groups = [{}, {}, {}, {}]
    grads = dict(zip(mat_names, _unpack_halves(ghalves, mat_shapes)))
    for n in big_names:
        parts = jnp.stack([_shard_of(gfull[n], MATRICES[n], s) for s in range(4)])
        mine = _sum_parts_call(_scatter_call(parts.reshape((4,) + halves_view(W[n]).shape)))
        theirs = _sibling_exchange_call(mine)
        grads[n] = jnp.concatenate([jnp.where(south, mine, theirs), jnp.where(south, theirs, mine)]).reshape(W[n].shape)
    for n in big_names + mat_names:
        g = grads[n]
        rows = (-1, g.shape[-1])
        outs = _adamw_call(W[n].reshape(rows), g.reshape(rows), Mo[n].reshape(rows), Vo[n].reshape(rows))
        for grp, val in zip(groups, (g,) + outs):
            grp[n] = val.reshape(g.shape)
    small = (gsmall,) + _adamw_call(small_pack(W), gsmall, small_pack(Mo), small_pack(Vo))
    for grp, sm in zip(groups, small):
        sm = sm.reshape(-1)
        grp.update(zip(vec_names, _unpack(sm[:vec_len], vec_shapes)))
        grp.update(zip(rep_names, _unpack(sm[vec_len:], rep_shapes)))
    return (loss, grad_x, *[grp[n] for grp in groups for n in WEIGHT_NAMES])
```

```python
import functools

import jax
import jax.numpy as jnp
from jax import lax
from jax.experimental import pallas as pl
from jax.experimental.pallas import tpu as pltpu

F32 = jnp.float32
BF16 = jnp.bfloat16
MESH = pl.DeviceIdType.MESH

ROPE_THETA = 10000.0
LN_EPS = 1e-5
RMS_EPS = 1e-6
DEPTH = 2
DEEPNORM_ALPHA = (2.0 * DEPTH) ** 0.25
MLA_HEADS, MLA_NOPE, MLA_ROPE, MLA_V, MLA_Q_RANK, MLA_KV_RANK = 8, 128, 64, 128, 512, 256
GLA_HEADS, GLA_DK, GLA_DV, GLA_GATE_RANK, GLA_TAU, GLA_CHUNK = 4, 128, 256, 16, 16.0, 64
DIL_HEADS, DIL_HEAD_DIM = 8, 128
DIL_BRANCHES = ((128, 1), (512, 4), (2048, 16))
RWKV_HEADS, RWKV_HEAD_DIM = 16, 64
RWKV_DECAY_RANK, RWKV_A_RANK, RWKV_GATE_RANK = 96, 96, 256
RWKV_GN_EPS = 64e-5
XA_HEADS = 4
DIL_WIDTH = DIL_HEADS * DIL_HEAD_DIM
RWKV_WIDTH = RWKV_HEADS * RWKV_HEAD_DIM
EVEN_IN_WIDTHS = (MLA_Q_RANK, MLA_KV_RANK, MLA_ROPE, GLA_HEADS * GLA_DK, GLA_HEADS * GLA_DK,
                  GLA_HEADS * GLA_DV, GLA_HEADS * GLA_DV, GLA_GATE_RANK)
RWKV_IN_WIDTHS = (RWKV_WIDTH, RWKV_WIDTH, RWKV_WIDTH, RWKV_DECAY_RANK, RWKV_A_RANK, RWKV_GATE_RANK)
ADAM_LR, ADAM_B1, ADAM_B2, ADAM_EPS, ADAM_WD, ADAM_STEP = 0.001, 0.9, 0.999, 1e-08, 0.01, 10

LANES = 128
SUBLANES = 8
VMEM_LIMIT_BYTES = 48 * 1024 * 1024
NEG_BIG = -1e30

PACK_COLS = 1024
NATIVE_MIN_ELEMENTS = 1 << 20
SMALL_COLS = 128

MATRICES = {
    'ev_w_in': 2, 'ev_mla_w_uq': 2, 'ev_mla_w_ukv': 2, 'ev_gla_w_gate2': 2, 'ev_w_out': 1, 'od_w_in': 2,
    'od_rwkv_w_decay2': 2, 'od_rwkv_w_a2': 2, 'od_rwkv_w_gate2': 2, 'od_w_out': 1,
    'xa_w_q': 1, 'xa_w_k': 1, 'xa_w_v': 1, 'xa_w_o': 1, 'ffn_w_gate': 2, 'ffn_w_up': 2, 'ffn_w_down': 1,
}
SHARDED_VECTORS = {
    'od_rwkv_mu': 1, 'od_rwkv_w0': 1, 'od_rwkv_a0': 1, 'od_rwkv_k_k': 1, 'od_rwkv_k_a': 1,
    'od_rwkv_gn_g': 1, 'od_rwkv_gn_b': 1,
}
REPLICATED = ('ev_mla_q_norm', 'ev_mla_kv_norm', 'ev_gla_b_gate', 'ev_gla_norm_g', 'ev_gla_norm_b', 'od_rwkv_r_k',
              'ln_mix_g', 'ln_mix_b', 'ln_xa_g', 'ln_xa_b', 'ln_ffn_g', 'ln_ffn_b')
WEIGHT_NAMES = ('ev_w_in', 'ev_mla_q_norm', 'ev_mla_w_uq', 'ev_mla_kv_norm', 'ev_mla_w_ukv', 'ev_gla_w_gate2',
                'ev_gla_b_gate', 'ev_gla_norm_g', 'ev_gla_norm_b', 'ev_w_out', 'od_w_in', 'od_rwkv_mu', 'od_rwkv_w0',
                'od_rwkv_w_decay2', 'od_rwkv_a0', 'od_rwkv_w_a2', 'od_rwkv_w_gate2', 'od_rwkv_k_k', 'od_rwkv_k_a',
                'od_rwkv_r_k', 'od_rwkv_gn_g', 'od_rwkv_gn_b', 'od_w_out', 'ln_mix_g', 'ln_mix_b', 'xa_w_q', 'xa_w_k',
                'xa_w_v', 'xa_w_o', 'ln_xa_g', 'ln_xa_b', 'ffn_w_gate', 'ffn_w_up', 'ffn_w_down', 'ln_ffn_g', 'ln_ffn_b')


def _pick(n, cap, mult):
    d = (min(cap, n) // mult) * mult
    while d >= mult:
        if n % d == 0:
            return d
        d -= mult
    return n


def _params(semantics):
    return pltpu.CompilerParams(dimension_semantics=semantics, vmem_limit_bytes=VMEM_LIMIT_BYTES)


_DIMS = {(False, False): (((1,), (0,)), ((), ())), (False, True): (((1,), (1,)), ((), ())),
         (True, False): (((0,), (0,)), ((), ()))}


def _bmm(a, b, ta, tb, out_dtype=F32):
    G = a.shape[0]
    K, M = (a.shape[1], a.shape[2]) if ta else (a.shape[2], a.shape[1])
    N = b.shape[1] if tb else b.shape[2]
    assert (b.shape[2] if tb else b.shape[1]) == K and b.shape[0] == G
    tm, tn = _pick(M, 1024, LANES), _pick(N, 512, LANES)
    tk = K if K <= 2048 else _pick(K, 2048, LANES)
    nk = K // tk
    gb = 1
    if tm == M and tn == N and nk == 1:
        per = 4 * (M * K + K * N + M * N)
        gb = _pick(G, max(1, min(8, (2 << 20) // per)), 1)
    dims = _DIMS[(ta, tb)]

    def body(a_ref, b_ref, o_ref, *scratch):
        def prod(i):
            return lax.dot_general(a_ref[i].astype(BF16), b_ref[i].astype(BF16), dims, preferred_element_type=F32)

        if nk == 1:
            for i in range(gb):
                o_ref[i] = prod(i).astype(o_ref.dtype)
        else:
            acc_ref, = scratch
            k = pl.program_id(3)

            @pl.when(k == 0)
            def _():
                acc_ref[...] = jnp.zeros_like(acc_ref)

            for i in range(gb):
                acc_ref[i] += prod(i)

            @pl.when(k == nk - 1)
            def _():
                o_ref[...] = acc_ref[...].astype(o_ref.dtype)

    a_spec = (pl.BlockSpec((gb, tk, tm), lambda g, i, j, k: (g, k, i)) if ta
              else pl.BlockSpec((gb, tm, tk), lambda g, i, j, k: (g, i, k)))
    b_spec = (pl.BlockSpec((gb, tn, tk), lambda g, i, j, k: (g, j, k)) if tb
              else pl.BlockSpec((gb, tk, tn), lambda g, i, j, k: (g, k, j)))
    return pl.pallas_call(
        body, name='bmm_' + ('t' if ta else 'n') + ('t' if tb else 'n'),
        out_shape=jax.ShapeDtypeStruct((G, M, N), out_dtype),
        grid=(G // gb, M // tm, N // tn, nk),
        in_specs=[a_spec, b_spec],
        out_specs=pl.BlockSpec((gb, tm, tn), lambda g, i, j, k: (g, i, j)),
        scratch_shapes=[] if nk == 1 else [pltpu.VMEM((gb, tm, tn), F32)],
        compiler_params=_params(('parallel', 'parallel', 'parallel', 'arbitrary')),
    )(a, b)


def _like(x):
    return jnp.zeros((), x.dtype)


@jax.custom_vjp
def bmm_nn(a, b):
    return _bmm(a, b, False, False)


def _bmm_nn_fwd(a, b):
    ab, bb = a.astype(BF16), b.astype(BF16)
    return _bmm(ab, bb, False, False), (ab, bb, _like(a), _like(b))


def _bmm_nn_bwd(res, g):
    a, b, la, lb = res
    g = g.astype(BF16)
    return _bmm(g, b, False, True, la.dtype), _bmm(a, g, True, False, lb.dtype)


bmm_nn.defvjp(_bmm_nn_fwd, _bmm_nn_bwd)


@jax.custom_vjp
def bmm_nt(a, b):
    return _bmm(a, b, False, True)


def _bmm_nt_fwd(a, b):
    ab, bb = a.astype(BF16), b.astype(BF16)
    return _bmm(ab, bb, False, True), (ab, bb, _like(a), _like(b))


def _bmm_nt_bwd(res, g):
    a, b, la, lb = res
    g = g.astype(BF16)
    return _bmm(g, b, False, False, la.dtype), _bmm(g, a, True, False, lb.dtype)


bmm_nt.defvjp(_bmm_nt_fwd, _bmm_nt_bwd)


def mm(x, w):
    lead = x.shape[:-1]
    out = bmm_nn(x.reshape(1, -1, x.shape[-1]), w[None])
    return out.reshape(lead + (w.shape[1],))


def _norm_stats(x, center, eps):
    if center:
        xc = x - jnp.mean(x, axis=-1, keepdims=True)
    else:
        xc = x
    rstd = lax.rsqrt(jnp.mean(xc * xc, axis=-1, keepdims=True) + eps)
    return xc * rstd, rstd


def _norm_fwd_call(x, g, b, center, eps):
    R, C = x.shape
    tr = _pick(R, max(8, (1 << 19) // C), 8)

    def body(x_ref, g_ref, b_ref, y_ref):
        xhat, _ = _norm_stats(x_ref[...], center, eps)
        y_ref[...] = xhat * g_ref[...] + b_ref[...]

    row = pl.BlockSpec((tr, C), lambda i: (i, 0))
    vec = pl.BlockSpec((1, C), lambda i: (0, 0))
    return pl.pallas_call(
        body, name='norm_fwd', out_shape=jax.ShapeDtypeStruct((R, C), F32), grid=(R // tr,),
        in_specs=[row, vec, vec], out_specs=row, compiler_params=_params(('parallel',)),
    )(x, g, b)


def _norm_bwd_call(x, g, dy, center, eps):
    R, C = x.shape
    tr = _pick(R, max(8, (1 << 19) // C), 8)

    def body(x_ref, g_ref, dy_ref, dx_ref, dg_ref, db_ref):
        @pl.when(pl.program_id(0) == 0)
        def _():
            dg_ref[...] = jnp.zeros_like(dg_ref)
            db_ref[...] = jnp.zeros_like(db_ref)

        xhat, rstd = _norm_stats(x_ref[...], center, eps)
        dy = dy_ref[...]
        dxh = dy * g_ref[...]
        proj = xhat * jnp.mean(dxh * xhat, axis=-1, keepdims=True)
        if center:
            dx_ref[...] = rstd * (dxh - jnp.mean(dxh, axis=-1, keepdims=True) - proj)
        else:
            dx_ref[...] = rstd * (dxh - proj)
        dg_ref[...] += jnp.sum(dy * xhat, axis=0, keepdims=True)
        db_ref[...] += jnp.sum(dy, axis=0, keepdims=True)

    row = pl.BlockSpec((tr, C), lambda i: (i, 0))
    vec = pl.BlockSpec((1, C), lambda i: (0, 0))
    return pl.pallas_call(
        body, name='norm_bwd',
        out_shape=(jax.ShapeDtypeStruct((R, C), F32), jax.ShapeDtypeStruct((1, C), F32), jax.ShapeDtypeStruct((1, C), F32)),
        grid=(R // tr,), in_specs=[row, vec, row], out_specs=(row, vec, vec), compiler_params=_params(('arbitrary',)),
    )(x, g, dy)


@functools.partial(jax.custom_vjp, nondiff_argnums=(3, 4))
def _norm2d(x, g, b, center, eps):
    return _norm_fwd_call(x, g, b, center, eps)


def _norm2d_fwd(x, g, b, center, eps):
    return _norm_fwd_call(x, g, b, center, eps), (x, g)


def _norm2d_bwd(center, eps, res, dy):
    x, g = res
    return _norm_bwd_call(x, g, dy, center, eps)


_norm2d.defvjp(_norm2d_fwd, _norm2d_bwd)


def _resnorm_fwd_call(h, s, g, b, alpha, eps):
    R, C = h.shape
    tr = _pick(R, max(8, (1 << 19) // C), 8)

    def body(h_ref, s_ref, g_ref, b_ref, y_ref):
        xhat, _ = _norm_stats(alpha * h_ref[...] + s_ref[...], True, eps)
        y_ref[...] = xhat * g_ref[...] + b_ref[...]

    row = pl.BlockSpec((tr, C), lambda i: (i, 0))
    vec = pl.BlockSpec((1, C), lambda i: (0, 0))
    return pl.pallas_call(
        body, name='resnorm_fwd', out_shape=jax.ShapeDtypeStruct((R, C), F32), grid=(R // tr,),
        in_specs=[row, row, vec, vec], out_specs=row, compiler_params=_params(('parallel',)),
    )(h, s, g, b)


def _resnorm_bwd_call(h, s, g, dy, alpha, eps):
    R, C = h.shape
    tr = _pick(R, max(8, (1 << 19) // C), 8)

    def body(h_ref, s_ref, g_ref, dy_ref, dx_ref, dg_ref, db_ref):
        @pl.when(pl.program_id(0) == 0)
        def _():
            dg_ref[...] = jnp.zeros_like(dg_ref)
            db_ref[...] = jnp.zeros_like(db_ref)

        xhat, rstd = _norm_stats(alpha * h_ref[...] + s_ref[...], True, eps)
        dy = dy_ref[...]
        dxh = dy * g_ref[...]
        proj = xhat * jnp.mean(dxh * xhat, axis=-1, keepdims=True)
        dx_ref[...] = rstd * (dxh - jnp.mean(dxh, axis=-1, keepdims=True) - proj)
        dg_ref[...] += jnp.sum(dy * xhat, axis=0, keepdims=True)
        db_ref[...] += jnp.sum(dy, axis=0, keepdims=True)

    row = pl.BlockSpec((tr, C), lambda i: (i, 0))
    vec = pl.BlockSpec((1, C), lambda i: (0, 0))
    return pl.pallas_call(
        body, name='resnorm_bwd',
        out_shape=(jax.ShapeDtypeStruct((R, C), F32), jax.ShapeDtypeStruct((1, C), F32), jax.ShapeDtypeStruct((1, C), F32)),
        grid=(R // tr,), in_specs=[row, row, vec, row], out_specs=(row, vec, vec), compiler_params=_params(('arbitrary',)),
    )(h, s, g, dy)


@functools.partial(jax.custom_vjp, nondiff_argnums=(4, 5))
def _resnorm2d(h, s, g, b, alpha, eps):
    return _resnorm_fwd_call(h, s, g, b, alpha, eps)


def _resnorm2d_fwd(h, s, g, b, alpha, eps):
    return _resnorm_fwd_call(h, s, g, b, alpha, eps), (h, s, g)


def _resnorm2d_bwd(alpha, eps, res, dy):
    h, s, g = res
    dx, dg, db = _resnorm_bwd_call(h, s, g, dy, alpha, eps)
    return alpha * dx, dx, dg, db


_resnorm2d.defvjp(_resnorm2d_fwd, _resnorm2d_bwd)


def residual_layer_norm(h, s, g, b, alpha):
    C = h.shape[-1]
    return _resnorm2d(h.reshape(-1, C), s.reshape(-1, C), g.reshape(1, C), b.reshape(1, C), alpha, LN_EPS).reshape(h.shape)


def layer_norm(x, g, b, eps=LN_EPS):
    C = x.shape[-1]
    return _norm2d(x.reshape(-1, C), g.reshape(1, C), b.reshape(1, C), True, eps).reshape(x.shape)


def rms_norm(x, g):
    C = x.shape[-1]
    return _norm2d(x.reshape(-1, C), g.reshape(1, C), jnp.zeros((1, C), F32), False, RMS_EPS).reshape(x.shape)


def _softmax_fwd_call(s, bias, scale):
    G1, G2, R, C = s.shape
    tr = _pick(R, max(8, (1 << 19) // C), 8)

    def body(s_ref, bias_ref, p_ref, lse_ref):
        z = s_ref[0, 0] * scale + bias_ref[0]
        m = jnp.max(z, axis=-1, keepdims=True)
        e = jnp.exp(z - m)
        den = jnp.sum(e, axis=-1, keepdims=True)
        p_ref[0, 0] = e / den
        lse_ref[0, 0] = m + jnp.log(den)

    blk = pl.BlockSpec((1, 1, tr, C), lambda a, b, r: (a, b, r, 0))
    col = pl.BlockSpec((1, 1, tr, 1), lambda a, b, r: (a, b, r, 0))
    return pl.pallas_call(
        body, name='softmax_fwd',
        out_shape=(jax.ShapeDtypeStruct(s.shape, F32), jax.ShapeDtypeStruct((G1, G2, R, 1), F32)),
        grid=(G1, G2, R // tr), in_specs=[blk, pl.BlockSpec((1, tr, C), lambda a, b, r: (b, r, 0))],
        out_specs=(blk, col), compiler_params=_params(('parallel', 'parallel', 'parallel')),
    )(s, bias)


def _softmax_bwd_call(p, dp, dlse, scale):
    G1, G2, R, C = p.shape
    tr = _pick(R, max(8, (1 << 19) // C), 8)

    def body(p_ref, dp_ref, dlse_ref, ds_ref):
        p = p_ref[0, 0]
        dp = dp_ref[0, 0]
        inner = jnp.sum(dp * p, axis=-1, keepdims=True)
        ds_ref[0, 0] = (p * (dp - inner + dlse_ref[0, 0])) * scale

    blk = pl.BlockSpec((1, 1, tr, C), lambda a, b, r: (a, b, r, 0))
    col = pl.BlockSpec((1, 1, tr, 1), lambda a, b, r: (a, b, r, 0))
    return pl.pallas_call(
        body, name='softmax_bwd', out_shape=jax.ShapeDtypeStruct(p.shape, F32),
        grid=(G1, G2, R // tr), in_specs=[blk, blk, col], out_specs=blk,
        compiler_params=_params(('parallel', 'parallel', 'parallel')),
    )(p, dp, dlse)


@functools.partial(jax.custom_vjp, nondiff_argnums=(2,))
def softmax_lse(s, bias, scale):
    return _softmax_fwd_call(s, bias, scale)


def _softmax_lse_fwd(s, bias, scale):
    p, lse = _softmax_fwd_call(s, bias, scale)
    return (p, lse), (p, bias)


def _softmax_lse_bwd(scale, res, cts):
    p, bias = res
    dp, dlse = cts
    return _softmax_bwd_call(p, dp, dlse, scale), jnp.zeros_like(bias)


softmax_lse.defvjp(_softmax_lse_fwd, _softmax_lse_bwd)


def _dot(a, b, dims):
    return lax.dot_general(a.astype(BF16), b.astype(BF16), dims, preferred_element_type=F32)


_NN, _NT, _TN = _DIMS[(False, False)], _DIMS[(False, True)], _DIMS[(True, False)]


def _gla_fwd_call(q, k, v, dec):
    G, nc, C, dk = q.shape
    dv = v.shape[-1]

    def body(q_ref, k_ref, v_ref, dec_ref, o_ref, st_ref, state):
        @pl.when(pl.program_id(1) == 0)
        def _():
            state[...] = jnp.zeros_like(state)

        s = state[...]
        st_ref[0, 0] = s
        o_ref[0, 0] = _dot(q_ref[0, 0], s, _NN)
        state[...] = s * dec_ref[0, 0] + _dot(k_ref[0, 0], v_ref[0, 0], _TN)

    def spec(r, c):
        return pl.BlockSpec((1, 1, r, c), lambda g, t: (g, t, 0, 0))

    return pl.pallas_call(
        body, name='gla_scan_fwd',
        out_shape=(jax.ShapeDtypeStruct((G, nc, C, dv), F32), jax.ShapeDtypeStruct((G, nc, dk, dv), F32)),
        grid=(G, nc), in_specs=[spec(C, dk), spec(C, dk), spec(C, dv), spec(dk, 1)],
        out_specs=(spec(C, dv), spec(dk, dv)), scratch_shapes=[pltpu.VMEM((dk, dv), F32)],
        compiler_params=_params(('parallel', 'arbitrary')),
    )(q, k, v, dec)


def _gla_bwd_call(q, k, v, dec, states, do):
    G, nc, C, dk = q.shape
    dv = v.shape[-1]

    def body(q_ref, k_ref, v_ref, dec_ref, st_ref, do_ref, dq_ref, dk_ref, dv_ref, ddec_ref, dstate):
        @pl.when(pl.program_id(1) == 0)
        def _():
            dstate[...] = jnp.zeros_like(dstate)

        s = st_ref[0, 0]
        d = dstate[...]
        do = do_ref[0, 0]
        dq_ref[0, 0] = _dot(do, s, _NT)
        dk_ref[0, 0] = _dot(v_ref[0, 0], d, _NT)
        dv_ref[0, 0] = _dot(k_ref[0, 0], d, _NN)
        ddec_ref[0, 0] = jnp.sum(s * d, axis=1, keepdims=True)
        dstate[...] = d * dec_ref[0, 0] + _dot(q_ref[0, 0], do, _TN)

    def spec(r, c):
        return pl.BlockSpec((1, 1, r, c), lambda g, t: (g, nc - 1 - t, 0, 0))

    return pl.pallas_call(
        body, name='gla_scan_bwd',
        out_shape=(jax.ShapeDtypeStruct(q.shape, F32), jax.ShapeDtypeStruct(k.shape, F32),
                   jax.ShapeDtypeStruct(v.shape, F32), jax.ShapeDtypeStruct(dec.shape, F32)),
        grid=(G, nc), in_specs=[spec(C, dk), spec(C, dk), spec(C, dv), spec(dk, 1), spec(dk, dv), spec(C, dv)],
        out_specs=(spec(C, dk), spec(C, dk), spec(C, dv), spec(dk, 1)), scratch_shapes=[pltpu.VMEM((dk, dv), F32)],
        compiler_params=_params(('parallel', 'arbitrary')),
    )(q, k, v, dec, states, do)


@jax.custom_vjp
def gla_scan(q, k, v, dec):
    return _gla_fwd_call(q, k, v, dec)[0]


def _gla_scan_fwd(q, k, v, dec):
    o, states = _gla_fwd_call(q, k, v, dec)
    return o, (q, k, v, dec, states)


def _gla_scan_bwd(res, do):
    return _gla_bwd_call(*res, do)


gla_scan.defvjp(_gla_scan_fwd, _gla_scan_bwd)


RWKV_PAIRS_PER_STEP = 8
RWKV_TIME_BLOCK = 32
RN = RWKV_HEAD_DIM


def _rwkv_consts():
    row = lax.broadcasted_iota(jnp.int32, (RN, LANES), 0)
    lane = lax.broadcasted_iota(jnp.int32, (RN, LANES), 1)
    diag = (lane % RN == row).astype(F32)
    r2 = lax.broadcasted_iota(jnp.int32, (LANES, LANES), 0)
    l2 = lax.broadcasted_iota(jnp.int32, (LANES, LANES), 1)
    seg = (r2 // RN == l2 // RN).astype(BF16)
    return diag, seg


def _stage(lhs_ref, slot, p):
    hi = p.astype(BF16)
    lhs_ref[pl.ds(slot * LANES, RN), :] = hi
    lhs_ref[pl.ds(slot * LANES + RN, RN), :] = (p - hi.astype(F32)).astype(BF16)


def _seg_sums(lhs_ref, nslots, seg):
    res = jnp.dot(lhs_ref[pl.ds(0, nslots * LANES), :], seg, preferred_element_type=F32)
    return [res[i * LANES:i * LANES + RN] + res[i * LANES + RN:(i + 1) * LANES] for i in range(nslots)]


def _rwkv_blocks(B, S, C):
    npairs = C // LANES
    pp = RWKV_PAIRS_PER_STEP if npairs % RWKV_PAIRS_PER_STEP == 0 else 1
    T = _pick(S, RWKV_TIME_BLOCK, 8)
    return npairs, pp, T


def _rwkv_fwd_call(r, w, k, v, kk, b):
    B, S, C = r.shape
    npairs, pp, T = _rwkv_blocks(B, S, C)
    G = SUBLANES

    def body(r_ref, w_ref, k_ref, v_ref, kk_ref, b_ref, y_ref, sall_ref, state, step_lhs, v_lhs, y_lhs):
        @pl.when(pl.program_id(2) == 0)
        def _():
            state[...] = jnp.zeros_like(state)

        diag, seg = _rwkv_consts()
        rowid = lax.broadcasted_iota(jnp.int32, (SUBLANES, LANES), 0)

        def group(t8, carry):
            rows = pl.ds(pl.multiple_of(t8 * G, G), G)
            sls = [slice(p * LANES, (p + 1) * LANES) for p in range(pp)]
            ops = [[ref[0, rows, sl] for ref in (r_ref, w_ref, k_ref, v_ref, kk_ref, b_ref)] for sl in sls]
            for j in range(G):
                for p in range(pp):
                    _stage(v_lhs, j * pp + p, diag * ops[p][3][j:j + 1])
            vcols = _seg_sums(v_lhs, G * pp, seg)
            s = list(carry)
            for j in range(G):
                for p in range(pp):
                    sall_ref[0, p, t8 * G + j] = s[p]
                    _stage(step_lhs, p, s[p] * ops[p][4][j:j + 1])
                sas = _seg_sums(step_lhs, pp, seg)
                for p in range(pp):
                    rt, wt, kt, _, _, bt = ops[p]
                    s[p] = s[p] * wt[j:j + 1] - sas[p] * bt[j:j + 1] + vcols[j * pp + p] * kt[j:j + 1]
                    _stage(y_lhs, j * pp + p, s[p] * rt[j:j + 1])
            ycols = _seg_sums(y_lhs, G * pp, seg)
            for p in range(pp):
                ytile = jnp.zeros((SUBLANES, LANES), F32)
                for j in range(G):
                    ytile = jnp.where(rowid == j, jnp.sum(diag * ycols[j * pp + p], axis=0, keepdims=True), ytile)
                y_ref[0, rows, sls[p]] = ytile
            return tuple(s)

        final = lax.fori_loop(0, T // G, group, tuple(state[p] for p in range(pp)))
        for p in range(pp):
            state[p] = final[p]

    seq = pl.BlockSpec((1, T, pp * LANES), lambda bi, g, t: (bi, t, g))
    return pl.pallas_call(
        body, name='rwkv_scan_fwd',
        out_shape=(jax.ShapeDtypeStruct((B, S, C), F32), jax.ShapeDtypeStruct((B, npairs, S, RN, LANES), F32)),
        grid=(B, npairs // pp, S // T), in_specs=[seq] * 6,
        out_specs=(seq, pl.BlockSpec((1, pp, T, RN, LANES), lambda bi, g, t: (bi, g, t, 0, 0))),
        scratch_shapes=[pltpu.VMEM((pp, RN, LANES), F32), pltpu.VMEM((pp * LANES, LANES), BF16),
                        pltpu.VMEM((G * pp * LANES, LANES), BF16), pltpu.VMEM((G * pp * LANES, LANES), BF16)],
        compiler_params=_params(('parallel', 'parallel', 'arbitrary')),
    )(r, w, k, v, kk, b)


def _rwkv_bwd_call(r, w, k, v, kk, b, sall, dy):
    B, S, C = r.shape
    npairs, pp, T = _rwkv_blocks(B, S, C)
    nt = S // T
    G = SUBLANES

    def body(r_ref, w_ref, k_ref, v_ref, kk_ref, b_ref, sall_ref, dy_ref,
             dr_ref, dw_ref, dk_ref, dv_ref, dkk_ref, db_ref, dstate, step_lhs, pre_lhs, dv_lhs):
        @pl.when(pl.program_id(2) == 0)
        def _():
            dstate[...] = jnp.zeros_like(dstate)

        diag, seg = _rwkv_consts()
        rowid = lax.broadcasted_iota(jnp.int32, (SUBLANES, LANES), 0)

        def colsum(z):
            return jnp.sum(z, axis=0, keepdims=True)

        def group(i, carry):
            t8 = T // G - 1 - i
            rows = pl.ds(pl.multiple_of(t8 * G, G), G)
            sls = [slice(p * LANES, (p + 1) * LANES) for p in range(pp)]
            ops = [[ref[0, rows, sl] for ref in (r_ref, w_ref, k_ref, v_ref, kk_ref, b_ref, dy_ref)] for sl in sls]
            for j in range(G):
                for p in range(pp):
                    _stage(pre_lhs, j * pp + p, sall_ref[0, p, t8 * G + j] * ops[p][4][j:j + 1])
                    _stage(pre_lhs, (G + j) * pp + p, diag * ops[p][3][j:j + 1])
                    _stage(pre_lhs, (2 * G + j) * pp + p, diag * ops[p][6][j:j + 1])
            pre = _seg_sums(pre_lhs, 3 * G * pp, seg)
            ds = list(carry)
            tiles = [[jnp.zeros((SUBLANES, LANES), F32) for _ in range(5)] for _ in range(pp)]
            for j in reversed(range(G)):
                d = []
                for p in range(pp):
                    rt, _, kt, _, _, bt, _ = ops[p]
                    d.append(ds[p] + pre[(2 * G + j) * pp + p] * rt[j:j + 1])
                    _stage(step_lhs, p, d[p] * bt[j:j + 1])
                    _stage(dv_lhs, j * pp + p, d[p] * kt[j:j + 1])
                dsas = _seg_sums(step_lhs, pp, seg)
                for p in range(pp):
                    rt, wt, kt, _, kkt, bt, _ = ops[p]
                    s = sall_ref[0, p, t8 * G + j]
                    sa, vcol, dycol = -pre[j * pp + p], pre[(G + j) * pp + p], pre[(2 * G + j) * pp + p]
                    s2 = s * wt[j:j + 1] + sa * bt[j:j + 1] + vcol * kt[j:j + 1]
                    vals = (colsum(s2 * dycol), colsum(d[p] * s), colsum(d[p] * vcol), -colsum(s * dsas[p]),
                            colsum(d[p] * sa))
                    tiles[p] = [jnp.where(rowid == j, val, tile) for val, tile in zip(vals, tiles[p])]
                    ds[p] = d[p] * wt[j:j + 1] - dsas[p] * kkt[j:j + 1]
            dvcols = _seg_sums(dv_lhs, G * pp, seg)
            for p in range(pp):
                dvt = jnp.zeros((SUBLANES, LANES), F32)
                for j in range(G):
                    dvt = jnp.where(rowid == j, colsum(diag * dvcols[j * pp + p]), dvt)
                dv_ref[0, rows, sls[p]] = dvt
                for ref, tile in zip((dr_ref, dw_ref, dk_ref, dkk_ref, db_ref), tiles[p]):
                    ref[0, rows, sls[p]] = tile
            return tuple(ds)

        final = lax.fori_loop(0, T // G, group, tuple(dstate[p] for p in range(pp)))
        for p in range(pp):
            dstate[p] = final[p]

    seq = pl.BlockSpec((1, T, pp * LANES), lambda bi, g, t: (bi, nt - 1 - t, g))
    sds = jax.ShapeDtypeStruct((B, S, C), F32)
    return pl.pallas_call(
        body, name='rwkv_scan_bwd', out_shape=(sds,) * 6,
        grid=(B, npairs // pp, nt),
        in_specs=[seq] * 6 + [pl.BlockSpec((1, pp, T, RN, LANES), lambda bi, g, t: (bi, g, nt - 1 - t, 0, 0)), seq],
        out_specs=(seq,) * 6,
        scratch_shapes=[pltpu.VMEM((pp, RN, LANES), F32), pltpu.VMEM((pp * LANES, LANES), BF16),
                        pltpu.VMEM((3 * G * pp * LANES, LANES), BF16), pltpu.VMEM((G * pp * LANES, LANES), BF16)],
        compiler_params=_params(('parallel', 'parallel', 'arbitrary')),
    )(r, w, k, v, kk, b, sall, dy)


@jax.custom_vjp
def rwkv_scan(r, w, k, v, kk, b):
    return _rwkv_fwd_call(r, w, k, v, kk, b)[0]


def _rwkv_scan_fwd(r, w, k, v, kk, b):
    y, sall = _rwkv_fwd_call(r, w, k, v, kk, b)
    return y, (r, w, k, v, kk, b, sall)


def _rwkv_scan_bwd(res, dy):
    return _rwkv_bwd_call(*res, dy)


rwkv_scan.defvjp(_rwkv_scan_fwd, _rwkv_scan_bwd)


def _loss_call(y, target):
    R, D = y.shape
    tr = _pick(R, max(8, (1 << 19) // D), 8)

    def body(y_ref, t_ref, dy_ref, part_ref):
        @pl.when(pl.program_id(0) == 0)
        def _():
            part_ref[...] = jnp.zeros_like(part_ref)

        diff = y_ref[...] - t_ref[...]
        dy_ref[...] = diff / D
        part_ref[...] += jnp.sum(jnp.mean(diff * diff, axis=-1, keepdims=True), axis=0, keepdims=True)

    row = pl.BlockSpec((tr, D), lambda i: (i, 0))
    dy, part = pl.pallas_call(
        body, name='loss_head',
        out_shape=(jax.ShapeDtypeStruct((R, D), F32), jax.ShapeDtypeStruct((1, 1), F32)),
        grid=(R // tr,), in_specs=[row, row], out_specs=(row, pl.BlockSpec((1, 1), lambda i: (0, 0))),
        compiler_params=_params(('arbitrary',)),
    )(y, target)
    return dy, part[0, 0]


def _sum_parts_call(parts):
    P, R, C = parts.shape
    tr = _pick(R, max(BF16_TILE_ROWS, (1 << 18) // C), BF16_TILE_ROWS)

    def body(p_ref, o_ref):
        acc = p_ref[0].astype(F32)
        for i in range(1, P):
            acc = acc + p_ref[i].astype(F32)
        o_ref[...] = acc

    return pl.pallas_call(
        body, name='sum_parts', out_shape=jax.ShapeDtypeStruct((R, C), F32), grid=(R // tr,),
        in_specs=[pl.BlockSpec((P, tr, C), lambda i: (0, i, 0))], out_specs=pl.BlockSpec((tr, C), lambda i: (i, 0)),
        compiler_params=_params(('parallel',)),
    )(parts)


def _adamw_call(w, g, m, v):
    R, C = w.shape
    tr = _pick(R, max(8, (1 << 18) // C), 8)

    def body(w_ref, g_ref, m_ref, v_ref, d_ref, nm_ref, nv_ref):
        g = g_ref[...]
        m = ADAM_B1 * m_ref[...] + (1.0 - ADAM_B1) * g
        v = ADAM_B2 * v_ref[...] + (1.0 - ADAM_B2) * (g * g)
        m_hat = m / (1.0 - ADAM_B1 ** ADAM_STEP)
        v_hat = v / (1.0 - ADAM_B2 ** ADAM_STEP)
        d_ref[...] = -ADAM_LR * (m_hat / (jnp.sqrt(v_hat) + ADAM_EPS) + ADAM_WD * w_ref[...])
        nm_ref[...] = m
        nv_ref[...] = v

    row = pl.BlockSpec((tr, C), lambda i: (i, 0))
    sds = jax.ShapeDtypeStruct((R, C), F32)
    return pl.pallas_call(
        body, name='adamw', out_shape=(sds, sds, sds), grid=(R // tr,),
        in_specs=[row] * 4, out_specs=(row,) * 3, compiler_params=_params(('parallel',)),
    )(w, g, m, v)


ANY = pl.BlockSpec(memory_space=pl.ANY)


def _place():
    return lax.axis_index('x'), lax.axis_index('y'), lax.axis_index('c')


COPY_SPLIT = 8
BF16_TILE_ROWS = 16
PACK_ROW_ALIGN = COPY_SPLIT * BF16_TILE_ROWS


def _row_split(rows):
    if rows % PACK_ROW_ALIGN == 0:
        return COPY_SPLIT, rows // COPY_SPLIT
    return 1, rows


def _all_gather_call(pack):
    _, R, C = pack.shape
    ns, rs = _row_split(R)

    def body(pk_ref, out_ref, send_sems, recv_sems):
        x, y, c = _place()
        chips = [(1 - x, y), (x, 1 - y), (1 - x, 1 - y)]
        me = 2 * x + y

        def copy(k, i, src, dst, to):
            rows = pl.ds(i * rs, rs)
            return pltpu.make_async_remote_copy(src_ref=src.at[rows], dst_ref=dst.at[rows], send_sem=send_sems.at[k * ns + i],
                                                recv_sem=recv_sems.at[k * ns + i], device_id=to, device_id_type=MESH)

        first = [copy(j, i, pk_ref.at[c], out_ref.at[me, c], (px, py, c))
                 for j, (px, py) in enumerate(chips) for i in range(ns)]
        for cp in first:
            cp.start()
        passed = []
        for i in range(ns):
            for j, (px, py) in enumerate(chips):
                landed = out_ref.at[2 * px + py, c]
                copy(j, i, landed, landed, (px, py, c)).wait_recv()
                fwd = copy(3 + j, i, landed, landed, (x, y, 1 - c))
                fwd.start()
                passed.append(fwd)
        for i in range(ns):
            for j, (px, py) in enumerate(chips):
                other = out_ref.at[2 * px + py, 1 - c]
                copy(3 + j, i, other, other, (x, y, 1 - c)).wait_recv()
        for cp in first + passed:
            cp.wait_send()

    others = pl.pallas_call(
        body, name='all_gather', out_shape=jax.ShapeDtypeStruct((4, 2, R, C), pack.dtype),
        in_specs=[ANY], out_specs=ANY,
        scratch_shapes=[pltpu.SemaphoreType.DMA((6 * ns,)), pltpu.SemaphoreType.DMA((6 * ns,))],
    )(pack)
    x, y, _ = _place()
    return lax.dynamic_update_slice(others, pack[None], (2 * x + y, 0, 0, 0))


def _scatter_call(src):
    _, _, R, C = src.shape
    ns, rs = _row_split(R)

    def body(src_ref, out_ref, send_sems, recv_sems):
        x, y, c = _place()
        me = 4 * x + 2 * y + c
        peers = []
        for rel in range(1, 8):
            px = 1 - x if rel & 4 else x
            py = 1 - y if rel & 2 else y
            pc = 1 - c if rel & 1 else c
            peers.append((rel - 1, px, py, pc))

        def copy(k, i, src, dst, to):
            rows = pl.ds(i * rs, rs)
            return pltpu.make_async_remote_copy(src_ref=src.at[rows], dst_ref=dst.at[rows], send_sem=send_sems.at[k * ns + i],
                                                recv_sem=recv_sems.at[k * ns + i], device_id=to, device_id_type=MESH)

        sends = [copy(k, i, src_ref.at[2 * px + py, pc], out_ref.at[me], (px, py, pc))
                 for i in range(ns) for k, px, py, pc in peers]
        for cp in sends:
            cp.start()
        for i in range(ns):
            for k, px, py, pc in peers:
                slot = out_ref.at[4 * px + 2 * py + pc]
                copy(k, i, slot, slot, (px, py, pc)).wait_recv()
        for cp in sends:
            cp.wait_send()

    others = pl.pallas_call(
        body, name='scatter_parts', out_shape=jax.ShapeDtypeStruct((8, R, C), src.dtype),
        in_specs=[ANY], out_specs=ANY,
        scratch_shapes=[pltpu.SemaphoreType.DMA((7 * ns,)), pltpu.SemaphoreType.DMA((7 * ns,))],
    )(src)
    x, y, c = _place()
    own = lax.dynamic_slice(src, (2 * x + y, c, 0, 0), (1, 1, R, C)).reshape(1, R, C)
    return lax.dynamic_update_slice(others, own, (4 * x + 2 * y + c, 0, 0))


def _sibling_exchange_call(half):
    R, C = half.shape
    ns, rs = _row_split(R)

    def body(h_ref, other_ref, send_sems, recv_sems):
        x, y, c = _place()

        def copy(i):
            rows = pl.ds(i * rs, rs)
            return pltpu.make_async_remote_copy(src_ref=h_ref.at[rows], dst_ref=other_ref.at[rows], send_sem=send_sems.at[i],
                                                recv_sem=recv_sems.at[i], device_id=(x, y, 1 - c), device_id_type=MESH)

        sends = [copy(i) for i in range(ns)]
        for cp in sends:
            cp.start()
        for cp in sends:
            cp.wait_recv()
        for cp in sends:
            cp.wait_send()

    return pl.pallas_call(
        body, name='sibling_exchange', out_shape=jax.ShapeDtypeStruct((R, C), half.dtype),
        in_specs=[ANY], out_specs=ANY,
        scratch_shapes=[pltpu.SemaphoreType.DMA((ns,)), pltpu.SemaphoreType.DMA((ns,))],
    )(half)


def rope_tables(seq_len, dim):
    inv = ROPE_THETA ** (-jnp.arange(0, dim, 2, dtype=F32) / dim)
    ang = jnp.arange(seq_len, dtype=F32)[:, None] * inv[None, :]
    return jnp.cos(ang), jnp.sin(ang)


def apply_rope(x, cos, sin):
    x1, x2 = jnp.split(x, 2, axis=-1)
    return jnp.concatenate([x1 * cos - x2 * sin, x1 * sin + x2 * cos], axis=-1)


def _pad_to(n):
    return -(-n // LANES) * LANES


def _pad_cols(w, widths):
    parts, at = [], 0
    for n in widths:
        parts.append(jnp.pad(w[..., at:at + n], [(0, 0)] * (w.ndim - 1) + [(0, _pad_to(n) - n)]))
        at += n
    return jnp.concatenate(parts, axis=-1)


def _split_padded(t, widths):
    out, at = [], 0
    for n in widths:
        out.append(t[..., at:at + n])
        at += _pad_to(n)
    return out


def _pad_rows(w, rows):
    return jnp.pad(w, ((0, rows - w.shape[0]), (0, 0)))


def _heads_attention(q, k, v, bias, scale):
    s = bmm_nt(q, k)
    p, _ = softmax_lse(s[:, None], bias[None], scale)
    return bmm_nn(p[:, 0], v)


def gla(q, k, v, r, gate_lr, w_gate2, b_gate, norm_g, norm_b):
    B, S, _ = q.shape
    H, dk, dv, C = GLA_HEADS, GLA_DK, GLA_DV, GLA_CHUNK
    nc = S // C
    log_a = jax.nn.log_sigmoid(mm(gate_lr, w_gate2) + b_gate) / GLA_TAU

    def chunks(t, d):
        return t.reshape(B, nc, C, H, d).transpose(0, 3, 1, 2, 4)

    qc = chunks(q, dk) * (dk ** -0.5)
    kc = chunks(k, dk)
    vc = chunks(v, dv)
    b = jnp.cumsum(chunks(log_a, dk), axis=3)
    b_last = b[:, :, :, -1:, :]
    q_dec = qc * jnp.exp(b)
    k_inv = kc * jnp.exp(-b)
    k_end = kc * jnp.exp(b_last - b)
    causal = jnp.tril(jnp.ones((C, C), dtype=bool))
    G = B * H
    att = bmm_nt(q_dec.reshape(G * nc, C, dk), k_inv.reshape(G * nc, C, dk))
    att = jnp.where(causal, att, 0.0)
    o_intra = bmm_nn(att, vc.reshape(G * nc, C, dv)).reshape(B, H, nc, C, dv)
    dec = jnp.exp(b_last[:, :, :, 0, :]).reshape(G, nc, dk, 1)
    o_inter = gla_scan(q_dec.reshape(G, nc, C, dk), k_end.reshape(G, nc, C, dk), vc.reshape(G, nc, C, dv), dec)
    o = o_intra + o_inter.reshape(B, H, nc, C, dv)
    o = o.transpose(0, 2, 3, 1, 4).reshape(B, S, H, dv)
    o = layer_norm(o, norm_g, norm_b).reshape(B, S, H * dv)
    return o * jax.nn.silu(r)


def even_mixer(x, p):
    B, S, _ = x.shape
    H = MLA_HEADS
    cos, sin = rope_tables(S, MLA_ROPE)
    z = mm(x, _pad_cols(p['ev_w_in'][0], EVEN_IN_WIDTHS))
    c_q, c_kv, k_pe, q_g, k_g, v_g, r_g, _ = _split_padded(z, EVEN_IN_WIDTHS)
    lr_at = sum(_pad_to(n) for n in EVEN_IN_WIDTHS[:-1])
    lr_g = z[..., lr_at:]
    q = mm(rms_norm(c_q, p['ev_mla_q_norm'][0]), p['ev_mla_w_uq'][0])
    q = q.reshape(B, S, H, MLA_NOPE + MLA_ROPE).transpose(0, 2, 1, 3)
    kv = mm(rms_norm(c_kv, p['ev_mla_kv_norm'][0]), p['ev_mla_w_ukv'][0])
    kv = kv.reshape(B, S, H, MLA_NOPE + MLA_V).transpose(0, 2, 1, 3)
    q_pe = apply_rope(q[..., MLA_NOPE:], cos, sin)
    k_pe = jnp.broadcast_to(apply_rope(k_pe[:, None], cos, sin), (B, H, S, MLA_ROPE))
    qf = jnp.concatenate([q[..., :MLA_NOPE], q_pe], axis=-1)
    kf = jnp.concatenate([kv[..., :MLA_NOPE], k_pe], axis=-1)
    pos = jnp.arange(S)
    bias = jnp.where(pos[None, :] <= pos[:, None], 0.0, NEG_BIG).astype(F32)
    a_out = _heads_attention(qf.reshape(B * H, S, -1), kf.reshape(B * H, S, -1),
                             kv[..., MLA_NOPE:].reshape(B * H, S, MLA_V), bias, (MLA_NOPE + MLA_ROPE) ** -0.5)
    a_out = a_out.reshape(B, H, S, MLA_V).transpose(0, 2, 1, 3).reshape(B, S, H * MLA_V)
    w_gate2 = _pad_rows(p['ev_gla_w_gate2'][0], lr_g.shape[-1])
    b_out = gla(q_g, k_g, v_g, r_g, lr_g, w_gate2, p['ev_gla_b_gate'][0], p['ev_gla_norm_g'][0], p['ev_gla_norm_b'][0])
    return mm(jnp.concatenate([a_out, b_out], axis=-1), p['ev_w_out'][0])


def dilated_branch(q, k, v, window, dil):
    B, H, S, dh = q.shape
    span = window // dil
    L = S // dil
    nb = -(-L // span)
    Lp = nb * span

    def residues(t):
        t = t.reshape(B, H, L, dil, dh).transpose(0, 1, 3, 2, 4)
        t = jnp.pad(t, ((0, 0), (0, 0), (0, 0), (0, Lp - L), (0, 0)))
        return t.reshape(B, H, dil, nb, span, dh)

    def with_prev(t):
        prev = jnp.pad(t, ((0, 0), (0, 0), (0, 0), (1, 0), (0, 0), (0, 0)))[:, :, :, :-1]
        return jnp.concatenate([prev, t], axis=4)

    qb = residues(q)
    kw, vw = with_prev(residues(k)), with_prev(residues(v))
    G = B * H * dil * nb
    s = bmm_nt(qb.reshape(G, span, dh), kw.reshape(G, 2 * span, dh))
    qi = jnp.arange(span)[:, None] + span
    kj = jnp.arange(2 * span)[None, :]
    dist = qi - kj
    in_band = (dist >= 0) & (dist <= span)
    has_prev = (jnp.arange(nb) > 0)[:, None, None] | (kj >= span)[None]
    valid = in_band[None] & has_prev
    bias = jnp.where(valid, 0.0, NEG_BIG).astype(F32)
    p, lse = softmax_lse(s.reshape(B * H * dil, nb, span, 2 * span), bias, dh ** -0.5)
    o = bmm_nn(p.reshape(G, span, 2 * span), vw.reshape(G, 2 * span, dh)).reshape(B, H, dil, nb, span, dh)
    lse = lse.reshape(B, H, dil, nb, span)

    def back(t):
        t = t.reshape((B, H, dil, Lp) + t.shape[5:])[:, :, :, :L]
        return jnp.moveaxis(t, 2, 3).reshape((B, H, S) + t.shape[4:])

    return back(o), back(lse)


def dilated_mixture(q, k, v):
    outs, lses = [], []
    for window, dil in DIL_BRANCHES:
        o, lse = dilated_branch(q, k, v, window, dil)
        outs.append(o)
        lses.append(lse)
    wts = jax.nn.softmax(jnp.stack(lses, axis=0), axis=0)
    return jnp.sum(wts[..., None] * jnp.stack(outs, axis=0), axis=0)


def token_shift(t, mu):
    prev = jnp.pad(t, ((0, 0), (1, 0), (0, 0)))[:, :-1]
    return t + (prev - t) * mu


def rwkv7(r, k, v, w_lr, a_lr, g_lr, w0, w_decay2, a0, w_a2, w_gate2, k_k, k_a, r_k, gn_g, gn_b):
    B, S, _ = r.shape
    H, n = RWKV_HEADS, RWKV_HEAD_DIM
    w = -jax.nn.softplus(-(w0 + mm(jnp.tanh(w_lr), w_decay2))) - 0.5
    decay = jnp.exp(-jnp.exp(w))
    a = jax.nn.sigmoid(a0 + mm(a_lr, w_a2))
    g = mm(jax.nn.sigmoid(g_lr), w_gate2)
    kk = (k * k_k).reshape(B, S, H, n)
    kk = kk / jnp.maximum(jnp.sqrt(jnp.sum(kk * kk, axis=-1, keepdims=True)), 1e-12)
    kk = kk.reshape(B, S, H * n)
    kh = k * (1.0 + (a - 1.0) * k_a)
    y = rwkv_scan(r, decay, kh, v, kk, kk * a).reshape(B, S, H, n)
    y = layer_norm(y, jnp.ones((n,), F32), jnp.zeros((n,), F32), RWKV_GN_EPS).reshape(B, S, H * n) * gn_g + gn_b
    bonus = jnp.sum((r * kh).reshape(B, S, H, n) * r_k, axis=-1, keepdims=True) * v.reshape(B, S, H, n)
    y = y + bonus.reshape(B, S, H * n)
    return y * g


def odd_mixer(x, p):
    B, S, _ = x.shape
    cos, sin = rope_tables(S, DIL_HEAD_DIM)
    widths = (3 * DIL_WIDTH,) + RWKV_IN_WIDTHS
    h = mm(x, _pad_cols(p['od_w_in'][0], widths))
    c_in = h[..., :3 * DIL_WIDTH]
    d_in = h[..., 3 * DIL_WIDTH:]
    q, k, v = [t.reshape(B, S, DIL_HEADS, DIL_HEAD_DIM).transpose(0, 2, 1, 3) for t in jnp.split(c_in, 3, axis=-1)]
    q, k = apply_rope(q, cos, sin), apply_rope(k, cos, sin)
    c_out = dilated_mixture(q, k, v).transpose(0, 2, 1, 3).reshape(B, S, DIL_WIDTH)
    mu = _pad_cols(p['od_rwkv_mu'][0], RWKV_IN_WIDTHS)
    sh = token_shift(d_in, mu)
    at = [0]
    for n in RWKV_IN_WIDTHS:
        at.append(at[-1] + _pad_to(n))
    r, kd, vd = [sh[..., at[i]:at[i + 1]] for i in range(3)]
    w_lr, a_lr, g_lr = [sh[..., at[i]:at[i + 1]] for i in range(3, 6)]
    d_out = rwkv7(r, kd, vd, w_lr, a_lr, g_lr, p['od_rwkv_w0'][0], _pad_rows(p['od_rwkv_w_decay2'][0], w_lr.shape[-1]),
                  p['od_rwkv_a0'][0], _pad_rows(p['od_rwkv_w_a2'][0], a_lr.shape[-1]), p['od_rwkv_w_gate2'][0],
                  p['od_rwkv_k_k'][0], p['od_rwkv_k_a'][0], p['od_rwkv_r_k'][0], p['od_rwkv_gn_g'][0], p['od_rwkv_gn_b'][0])
    return mm(jnp.concatenate([c_out, d_out], axis=-1), p['od_w_out'][0])


def cross_attention(x, mem, w_q, w_k, w_v, w_o):
    B, S, D = x.shape
    M = mem.shape[1]
    hd = D // XA_HEADS

    def heads(t, n):
        return t.reshape(B, n, XA_HEADS, hd).transpose(0, 2, 1, 3).reshape(B * XA_HEADS, n, hd)

    q, k, v = heads(mm(x, w_q), S), heads(mm(mem, w_k), M), heads(mm(mem, w_v), M)
    o = _heads_attention(q, k, v, jnp.zeros((S, M), F32), hd ** -0.5)
    o = o.reshape(B, XA_HEADS, S, hd).transpose(0, 2, 1, 3).reshape(B, S, D)
    return mm(o, w_o)


def swiglu(x, w_gate, w_up, w_down):
    return mm(jax.nn.silu(mm(x, w_gate)) * mm(x, w_up), w_down)


def forward(p, x, mem):
    h = x
    for layer in range(DEPTH):
        mix = even_mixer(h, p) if layer % 2 == 0 else odd_mixer(h, p)
        h = residual_layer_norm(h, mix, p['ln_mix_g'][layer], p['ln_mix_b'][layer], DEEPNORM_ALPHA)
        xa = cross_attention(h, mem, p['xa_w_q'][layer], p['xa_w_k'][layer], p['xa_w_v'][layer], p['xa_w_o'][layer])
        h = residual_layer_norm(h, xa, p['ln_xa_g'][layer], p['ln_xa_b'][layer], DEEPNORM_ALPHA)
        ff = swiglu(h, p['ffn_w_gate'][layer], p['ffn_w_up'][layer], p['ffn_w_down'][layer])
        h = residual_layer_norm(h, ff, p['ln_ffn_g'][layer], p['ln_ffn_b'][layer], DEEPNORM_ALPHA)
    return h


def _flat_pack(arrays, length, dtype):
    flat = jnp.concatenate([a.reshape(-1).astype(dtype) for a in arrays])
    return jnp.pad(flat, (0, length - flat.shape[0]))


def _unpack(flat, shapes):
    out, at = [], 0
    for shp in shapes:
        n = 1
        for d in shp:
            n *= d
        out.append(flat[at:at + n].reshape(shp))
        at += n
    return out


def _unpack_halves(halves, shapes):
    first, second = halves
    cut = first.shape[0]
    out, at = [], 0
    for shp in shapes:
        n = 1
        for d in shp:
            n *= d
        if at + n <= cut:
            flat = first[at:at + n]
        elif at >= cut:
            flat = second[at - cut:at - cut + n]
        else:
            flat = jnp.concatenate([first[at:], second[:at + n - cut]])
        out.append(flat.reshape(shp))
        at += n
    return out


def _shard_of(full, axis, s):
    n = full.shape[axis] // 4
    return lax.slice_in_dim(full, s * n, (s + 1) * n, axis=axis)


def kernel(x, mem, ev_w_in, ev_mla_q_norm, ev_mla_w_uq, ev_mla_kv_norm, ev_mla_w_ukv, ev_gla_w_gate2, ev_gla_b_gate, ev_gla_norm_g, ev_gla_norm_b, ev_w_out, od_w_in, od_rwkv_mu, od_rwkv_w0, od_rwkv_w_decay2, od_rwkv_a0, od_rwkv_w_a2, od_rwkv_w_gate2, od_rwkv_k_k, od_rwkv_k_a, od_rwkv_r_k, od_rwkv_gn_g, od_rwkv_gn_b, od_w_out, ln_mix_g, ln_mix_b, xa_w_q, xa_w_k, xa_w_v, xa_w_o, ln_xa_g, ln_xa_b, ffn_w_gate, ffn_w_up, ffn_w_down, ln_ffn_g, ln_ffn_b, loss_target, m_ev_w_in, m_ev_mla_q_norm, m_ev_mla_w_uq, m_ev_mla_kv_norm, m_ev_mla_w_ukv, m_ev_gla_w_gate2, m_ev_gla_b_gate, m_ev_gla_norm_g, m_ev_gla_norm_b, m_ev_w_out, m_od_w_in, m_od_rwkv_mu, m_od_rwkv_w0, m_od_rwkv_w_decay2, m_od_rwkv_a0, m_od_rwkv_w_a2, m_od_rwkv_w_gate2, m_od_rwkv_k_k, m_od_rwkv_k_a, m_od_rwkv_r_k, m_od_rwkv_gn_g, m_od_rwkv_gn_b, m_od_w_out, m_ln_mix_g, m_ln_mix_b, m_xa_w_q, m_xa_w_k, m_xa_w_v, m_xa_w_o, m_ln_xa_g, m_ln_xa_b, m_ffn_w_gate, m_ffn_w_up, m_ffn_w_down, m_ln_ffn_g, m_ln_ffn_b, v_ev_w_in, v_ev_mla_q_norm, v_ev_mla_w_uq, v_ev_mla_kv_norm, v_ev_mla_w_ukv, v_ev_gla_w_gate2, v_ev_gla_b_gate, v_ev_gla_norm_g, v_ev_gla_norm_b, v_ev_w_out, v_od_w_in, v_od_rwkv_mu, v_od_rwkv_w0, v_od_rwkv_w_decay2, v_od_rwkv_a0, v_od_rwkv_w_a2, v_od_rwkv_w_gate2, v_od_rwkv_k_k, v_od_rwkv_k_a, v_od_rwkv_r_k, v_od_rwkv_gn_g, v_od_rwkv_gn_b, v_od_w_out, v_ln_mix_g, v_ln_mix_b, v_xa_w_q, v_xa_w_k, v_xa_w_v, v_xa_w_o, v_ln_xa_g, v_ln_xa_b, v_ffn_w_gate, v_ffn_w_up, v_ffn_w_down, v_ln_ffn_g, v_ln_ffn_b):
    given = dict(locals())
    W = {n: given[n] for n in WEIGHT_NAMES}
    Mo = {n: given['m_' + n] for n in WEIGHT_NAMES}
    Vo = {n: given['v_' + n] for n in WEIGHT_NAMES}
    def count(shapes):
        total = 0
        for shp in shapes:
            n = 1
            for d in shp:
                n *= d
            total += n
        return total

    big_names = [n for n in WEIGHT_NAMES if n in MATRICES and count([given[n].shape]) >= NATIVE_MIN_ELEMENTS]
    mat_names = [n for n in WEIGHT_NAMES if n in MATRICES and n not in big_names]
    vec_names = [n for n in WEIGHT_NAMES if n in SHARDED_VECTORS]
    rep_names = list(REPLICATED)
    mat_shapes = [W[n].shape for n in mat_names]
    vec_shapes = [W[n].shape for n in vec_names]
    rep_shapes = [W[n].shape for n in rep_names]

    def halves_view(a):
        return a.reshape(2, count([a.shape[:-1]]) // 2, a.shape[-1])

    rc = -(-count(mat_shapes) // (2 * PACK_COLS * PACK_ROW_ALIGN)) * PACK_ROW_ALIGN
    mat_len = 2 * rc * PACK_COLS
    rv = -(-count(vec_shapes) // (2 * SMALL_COLS * 8)) * 8
    vec_len = 2 * rv * SMALL_COLS
    rr = -(-count(rep_shapes) // (SMALL_COLS * 8)) * 8
    rep_len = rr * SMALL_COLS

    wmat = _flat_pack([W[n] for n in mat_names], mat_len, BF16).reshape(2, rc, PACK_COLS)
    gathered = _all_gather_call(wmat)
    gvec = _all_gather_call(_flat_pack([W[n] for n in vec_names], vec_len, F32).reshape(2, rv, SMALL_COLS))
    full = {}
    mat_parts = [_unpack(gathered[s].reshape(-1), mat_shapes) for s in range(4)]
    for i, n in enumerate(mat_names):
        full[n] = jnp.concatenate([mat_parts[s][i] for s in range(4)], axis=MATRICES[n])
    for n in big_names:
        g4 = _all_gather_call(halves_view(W[n].astype(BF16))).reshape((4,) + W[n].shape)
        full[n] = jnp.concatenate([g4[s] for s in range(4)], axis=MATRICES[n])
    vec_parts =[_unpack(gvec[s].reshape(-1), vec_shapes) for s in range(4)]
    for i, n in enumerate(vec_names):
        full[n] = jnp.concatenate([vec_parts[s][i] for s in range(4)], axis=SHARDED_VECTORS[n])
    for n in rep_names:
        full[n] = W[n]

    B, S, D = x.shape
    y, vjp = jax.vjp(lambda p, xx: forward(p, xx, mem), full, x)
    dy, part = _loss_call(y.reshape(B * S, D), loss_target.reshape(B * S, D))
    loss = lax.psum(0.5 * part, ('x', 'y', 'c'))
    gfull, grad_x = vjp(dy.reshape(B, S, D))

    gmat = jnp.stack([_flat_pack([_shard_of(gfull[n], MATRICES[n], s) for n in mat_names], mat_len, BF16)
                      for s in range(4)]).reshape(4, 2, rc, PACK_COLS)
    half = _sum_parts_call(_scatter_call(gmat))
    other = _sibling_exchange_call(half)
    south = lax.axis_index('c') == 0
    ghalves = (jnp.where(south, half, other).reshape(-1), jnp.where(south, other, half).reshape(-1))
    grep = _flat_pack([gfull[n] for n in rep_names], rep_len, F32)
    gsmall = jnp.stack([jnp.concatenate([
        _flat_pack([_shard_of(gfull[n], SHARDED_VECTORS[n], s) for n in vec_names], vec_len, F32), grep])
        for s in range(4)]).reshape(4, 1, 2 * rv + rr, SMALL_COLS)
    gsmall = _sum_parts_call(_scatter_call(jnp.concatenate([gsmall, gsmall], axis=1)))

    def small_pack(src):
        return jnp.concatenate([_flat_pack([src[n] for n in vec_names], vec_len, F32),
                                _flat_pack([src[n] for n in rep_names], rep_len, F32)]).reshape(-1, SMALL_COLS)

    groups = [{}, {}, {}, {}]
    grads = dict(zip(mat_names, _unpack_halves(ghalves, mat_shapes)))
    for n in big_names:
        parts = jnp.stack([_shard_of(gfull[n], MATRICES[n], s) for s in range(4)])
        mine = _sum_parts_call(_scatter_call(parts.reshape((4,) + halves_view(W[n]).shape)))
        theirs = _sibling_exchange_call(mine)
        grads[n] = jnp.concatenate([jnp.where(south, mine, theirs), jnp.where(south, theirs, mine)]).reshape(W[n].shape)
    for n in big_names + mat_names:
        g = grads[n]
        rows = (-1, g.shape[-1])
        outs = _adamw_call(W[n].reshape(rows), g.reshape(rows), Mo[n].reshape(rows), Vo[n].reshape(rows))
        for grp, val in zip(groups, (g,) + outs):
            grp[n] = val.reshape(g.shape)
    small = (gsmall,) + _adamw_call(small_pack(W), gsmall, small_pack(Mo), small_pack(Vo))
    for grp, sm in zip(groups, small):
        sm = sm.reshape(-1)
        grp.update(zip(vec_names, _unpack(sm[:vec_len], vec_shapes)))
        grp.update(zip(rep_names, _unpack(sm[vec_len:], rep_shapes)))
    return (loss, grad_x, *[grp[n] for grp in groups for n in WEIGHT_NAMES])
```

```python
import functools

import jax
import jax.numpy as jnp
from jax import lax
from jax.experimental import pallas as pl
from jax.experimental.pallas import tpu as pltpu

F32 = jnp.float32
BF16 = jnp.bfloat16
MESH = pl.DeviceIdType.MESH

ROPE_THETA = 10000.0
LN_EPS = 1e-5
RMS_EPS = 1e-6
DEPTH = 2
DEEPNORM_ALPHA = (2.0 * DEPTH) ** 0.25
MLA_HEADS, MLA_NOPE, MLA_ROPE, MLA_V, MLA_Q_RANK, MLA_KV_RANK = 8, 128, 64, 128, 512, 256
GLA_HEADS, GLA_DK, GLA_DV, GLA_GATE_RANK, GLA_TAU, GLA_CHUNK = 4, 128, 256, 16, 16.0, 64
DIL_HEADS, DIL_HEAD_DIM = 8, 128
DIL_BRANCHES = ((128, 1), (512, 4), (2048, 16))
RWKV_HEADS, RWKV_HEAD_DIM = 16, 64
RWKV_DECAY_RANK, RWKV_A_RANK, RWKV_GATE_RANK = 96, 96, 256
RWKV_GN_EPS = 64e-5
XA_HEADS = 4
DIL_WIDTH = DIL_HEADS * DIL_HEAD_DIM
RWKV_WIDTH = RWKV_HEADS * RWKV_HEAD_DIM
EVEN_IN_WIDTHS = (MLA_Q_RANK, MLA_KV_RANK, MLA_ROPE, GLA_HEADS * GLA_DK, GLA_HEADS * GLA_DK,
                  GLA_HEADS * GLA_DV, GLA_HEADS * GLA_DV, GLA_GATE_RANK)
RWKV_IN_WIDTHS = (RWKV_WIDTH, RWKV_WIDTH, RWKV_WIDTH, RWKV_DECAY_RANK, RWKV_A_RANK, RWKV_GATE_RANK)
ADAM_LR, ADAM_B1, ADAM_B2, ADAM_EPS, ADAM_WD, ADAM_STEP = 0.001, 0.9, 0.999, 1e-08, 0.01, 10

LANES = 128
SUBLANES = 8
VMEM_LIMIT_BYTES = 48 * 1024 * 1024
NEG_BIG = -1e30

PACK_COLS = 1024
NATIVE_MIN_ELEMENTS = 1 << 20
SMALL_COLS = 128

MATRICES = {
    'ev_w_in': 2, 'ev_mla_w_uq': 2, 'ev_mla_w_ukv': 2, 'ev_gla_w_gate2': 2, 'ev_w_out': 1, 'od_w_in': 2,
    'od_rwkv_w_decay2': 2, 'od_rwkv_w_a2': 2, 'od_rwkv_w_gate2': 2, 'od_w_out': 1,
    'xa_w_q': 1, 'xa_w_k': 1, 'xa_w_v': 1, 'xa_w_o': 1, 'ffn_w_gate': 2, 'ffn_w_up': 2, 'ffn_w_down': 1,
}
SHARDED_VECTORS = {
    'od_rwkv_mu': 1, 'od_rwkv_w0': 1, 'od_rwkv_a0': 1, 'od_rwkv_k_k': 1, 'od_rwkv_k_a': 1,
    'od_rwkv_gn_g': 1, 'od_rwkv_gn_b': 1,
}
REPLICATED = ('ev_mla_q_norm', 'ev_mla_kv_norm', 'ev_gla_b_gate', 'ev_gla_norm_g', 'ev_gla_norm_b', 'od_rwkv_r_k',
              'ln_mix_g', 'ln_mix_b', 'ln_xa_g', 'ln_xa_b', 'ln_ffn_g', 'ln_ffn_b')
WEIGHT_NAMES = ('ev_w_in', 'ev_mla_q_norm', 'ev_mla_w_uq', 'ev_mla_kv_norm', 'ev_mla_w_ukv', 'ev_gla_w_gate2',
                'ev_gla_b_gate', 'ev_gla_norm_g', 'ev_gla_norm_b', 'ev_w_out', 'od_w_in', 'od_rwkv_mu', 'od_rwkv_w0',
                'od_rwkv_w_decay2', 'od_rwkv_a0', 'od_rwkv_w_a2', 'od_rwkv_w_gate2', 'od_rwkv_k_k', 'od_rwkv_k_a',
                'od_rwkv_r_k', 'od_rwkv_gn_g', 'od_rwkv_gn_b', 'od_w_out', 'ln_mix_g', 'ln_mix_b', 'xa_w_q', 'xa_w_k',
                'xa_w_v', 'xa_w_o', 'ln_xa_g', 'ln_xa_b', 'ffn_w_gate', 'ffn_w_up', 'ffn_w_down', 'ln_ffn_g', 'ln_ffn_b')


def _pick(n, cap, mult):
    d = (min(cap, n) // mult) * mult
    while d >= mult:
        if n % d == 0:
            return d
        d -= mult
    return n


def _params(semantics):
    return pltpu.CompilerParams(dimension_semantics=semantics, vmem_limit_bytes=VMEM_LIMIT_BYTES)


_DIMS = {(False, False): (((1,), (0,)), ((), ())), (False, True): (((1,), (1,)), ((), ())),
         (True, False): (((0,), (0,)), ((), ()))}


def _bmm(a, b, ta, tb, out_dtype=F32):
    G = a.shape[0]
    K, M = (a.shape[1], a.shape[2]) if ta else (a.shape[2], a.shape[1])
    N = b.shape[1] if tb else b.shape[2]
    assert (b.shape[2] if tb else b.shape[1]) == K and b.shape[0] == G
    tm, tn = _pick(M, 1024, LANES), _pick(N, 1024, LANES)
    tk = K if K <= 2048 else _pick(K, 2048, LANES)
    nk = K // tk
    gb = 1
    if tm == M and tn == N and nk == 1:
        per = 4 * (M * K + K * N + M * N)
        gb = _pick(G, max(1, min(8, (2 << 20) // per)), 1)
    dims = _DIMS[(ta, tb)]

    def body(a_ref, b_ref, o_ref, *scratch):
        def prod(i):
            return lax.dot_general(a_ref[i].astype(BF16), b_ref[i].astype(BF16), dims, preferred_element_type=F32)

        if nk == 1:
            for i in range(gb):
                o_ref[i] = prod(i).astype(o_ref.dtype)
        else:
            acc_ref, = scratch
            k = pl.program_id(3)

            @pl.when(k == 0)
            def _():
                acc_ref[...] = jnp.zeros_like(acc_ref)

            for i in range(gb):
                acc_ref[i] += prod(i)

            @pl.when(k == nk - 1)
            def _():
                o_ref[...] = acc_ref[...].astype(o_ref.dtype)

    a_spec = (pl.BlockSpec((gb, tk, tm), lambda g, i, j, k: (g, k, i)) if ta
              else pl.BlockSpec((gb, tm, tk), lambda g, i, j, k: (g, i, k)))
    b_spec = (pl.BlockSpec((gb, tn, tk), lambda g, i, j, k: (g, j, k)) if tb
              else pl.BlockSpec((gb, tk, tn), lambda g, i, j, k: (g, k, j)))
    return pl.pallas_call(
        body, name='bmm_' + ('t' if ta else 'n') + ('t' if tb else 'n'),
        out_shape=jax.ShapeDtypeStruct((G, M, N), out_dtype),
        grid=(G // gb, M // tm, N // tn, nk),
        in_specs=[a_spec, b_spec],
        out_specs=pl.BlockSpec((gb, tm, tn), lambda g, i, j, k: (g, i, j)),
        scratch_shapes=[] if nk == 1 else [pltpu.VMEM((gb, tm, tn), F32)],
        compiler_params=_params(('parallel', 'parallel', 'parallel', 'arbitrary')),
    )(a, b)


def _like(x):
    return jnp.zeros((), x.dtype)


@jax.custom_vjp
def bmm_nn(a, b):
    return _bmm(a, b, False, False)


def _bmm_nn_fwd(a, b):
    ab, bb = a.astype(BF16), b.astype(BF16)
    return _bmm(ab, bb, False, False), (ab, bb, _like(a), _like(b))


def _bmm_nn_bwd(res, g):
    a, b, la, lb = res
    g = g.astype(BF16)
    return _bmm(g, b, False, True, la.dtype), _bmm(a, g, True, False, lb.dtype)


bmm_nn.defvjp(_bmm_nn_fwd, _bmm_nn_bwd)


@jax.custom_vjp
def bmm_nt(a, b):
    return _bmm(a, b, False, True)


def _bmm_nt_fwd(a, b):
    ab, bb = a.astype(BF16), b.astype(BF16)
    return _bmm(ab, bb, False, True), (ab, bb, _like(a), _like(b))


def _bmm_nt_bwd(res, g):
    a, b, la, lb = res
    g = g.astype(BF16)
    return _bmm(g, b, False, False, la.dtype), _bmm(g, a, True, False, lb.dtype)


bmm_nt.defvjp(_bmm_nt_fwd, _bmm_nt_bwd)


def mm(x, w):
    lead = x.shape[:-1]
    out = bmm_nn(x.reshape(1, -1, x.shape[-1]), w[None])
    return out.reshape(lead + (w.shape[1],))


def _norm_stats(x, center, eps):
    if center:
        xc = x - jnp.mean(x, axis=-1, keepdims=True)
    else:
        xc = x
    rstd = lax.rsqrt(jnp.mean(xc * xc, axis=-1, keepdims=True) + eps)
    return xc * rstd, rstd


def _norm_fwd_call(x, g, b, center, eps):
    R, C = x.shape
    tr = _pick(R, max(8, (1 << 19) // C), 8)

    def body(x_ref, g_ref, b_ref, y_ref):
        xhat, _ = _norm_stats(x_ref[...], center, eps)
        y_ref[...] = xhat * g_ref[...] + b_ref[...]

    row = pl.BlockSpec((tr, C), lambda i: (i, 0))
    vec = pl.BlockSpec((1, C), lambda i: (0, 0))
    return pl.pallas_call(
        body, name='norm_fwd', out_shape=jax.ShapeDtypeStruct((R, C), F32), grid=(R // tr,),
        in_specs=[row, vec, vec], out_specs=row, compiler_params=_params(('parallel',)),
    )(x, g, b)


def _norm_bwd_call(x, g, dy, center, eps):
    R, C = x.shape
    tr = _pick(R, max(8, (1 << 19) // C), 8)

    def body(x_ref, g_ref, dy_ref, dx_ref, dg_ref, db_ref):
        @pl.when(pl.program_id(0) == 0)
        def _():
            dg_ref[...] = jnp.zeros_like(dg_ref)
            db_ref[...] = jnp.zeros_like(db_ref)

        xhat, rstd = _norm_stats(x_ref[...], center, eps)
        dy = dy_ref[...]
        dxh = dy * g_ref[...]
        proj = xhat * jnp.mean(dxh * xhat, axis=-1, keepdims=True)
        if center:
            dx_ref[...] = rstd * (dxh - jnp.mean(dxh, axis=-1, keepdims=True) - proj)
        else:
            dx_ref[...] = rstd * (dxh - proj)
        dg_ref[...] += jnp.sum(dy * xhat, axis=0, keepdims=True)
        db_ref[...] += jnp.sum(dy, axis=0, keepdims=True)

    row = pl.BlockSpec((tr, C), lambda i: (i, 0))
    vec = pl.BlockSpec((1, C), lambda i: (0, 0))
    return pl.pallas_call(
        body, name='norm_bwd',
        out_shape=(jax.ShapeDtypeStruct((R, C), F32), jax.ShapeDtypeStruct((1, C), F32), jax.ShapeDtypeStruct((1, C), F32)),
        grid=(R // tr,), in_specs=[row, vec, row], out_specs=(row, vec, vec), compiler_params=_params(('arbitrary',)),
    )(x, g, dy)


@functools.partial(jax.custom_vjp, nondiff_argnums=(3, 4))
def _norm2d(x, g, b, center, eps):
    return _norm_fwd_call(x, g, b, center, eps)


def _norm2d_fwd(x, g, b, center, eps):
    return _norm_fwd_call(x, g, b, center, eps), (x, g)


def _norm2d_bwd(center, eps, res, dy):
    x, g = res
    return _norm_bwd_call(x, g, dy, center, eps)


_norm2d.defvjp(_norm2d_fwd, _norm2d_bwd)


def _resnorm_fwd_call(h, s, g, b, alpha, eps):
    R, C = h.shape
    tr = _pick(R, max(8, (1 << 19) // C), 8)

    def body(h_ref, s_ref, g_ref, b_ref, y_ref):
        xhat, _ = _norm_stats(alpha * h_ref[...] + s_ref[...], True, eps)
        y_ref[...] = xhat * g_ref[...] + b_ref[...]

    row = pl.BlockSpec((tr, C), lambda i: (i, 0))
    vec = pl.BlockSpec((1, C), lambda i: (0, 0))
    return pl.pallas_call(
        body, name='resnorm_fwd', out_shape=jax.ShapeDtypeStruct((R, C), F32), grid=(R // tr,),
        in_specs=[row, row, vec, vec], out_specs=row, compiler_params=_params(('parallel',)),
    )(h, s, g, b)


def _resnorm_bwd_call(h, s, g, dy, alpha, eps):
    R, C = h.shape
    tr = _pick(R, max(8, (1 << 19) // C), 8)

    def body(h_ref, s_ref, g_ref, dy_ref, dx_ref, dg_ref, db_ref):
        @pl.when(pl.program_id(0) == 0)
        def _():
            dg_ref[...] = jnp.zeros_like(dg_ref)
            db_ref[...] = jnp.zeros_like(db_ref)

        xhat, rstd = _norm_stats(alpha * h_ref[...] + s_ref[...], True, eps)
        dy = dy_ref[...]
        dxh = dy * g_ref[...]
        proj = xhat * jnp.mean(dxh * xhat, axis=-1, keepdims=True)
        dx_ref[...] = rstd * (dxh - jnp.mean(dxh, axis=-1, keepdims=True) - proj)
        dg_ref[...] += jnp.sum(dy * xhat, axis=0, keepdims=True)
        db_ref[...] += jnp.sum(dy, axis=0, keepdims=True)

    row = pl.BlockSpec((tr, C), lambda i: (i, 0))
    vec = pl.BlockSpec((1, C), lambda i: (0, 0))
    return pl.pallas_call(
        body, name='resnorm_bwd',
        out_shape=(jax.ShapeDtypeStruct((R, C), F32), jax.ShapeDtypeStruct((1, C), F32), jax.ShapeDtypeStruct((1, C), F32)),
        grid=(R // tr,), in_specs=[row, row, vec, row], out_specs=(row, vec, vec), compiler_params=_params(('arbitrary',)),
    )(h, s, g, dy)


@functools.partial(jax.custom_vjp, nondiff_argnums=(4, 5))
def _resnorm2d(h, s, g, b, alpha, eps):
    return _resnorm_fwd_call(h, s, g, b, alpha, eps)


def _resnorm2d_fwd(h, s, g, b, alpha, eps):
    return _resnorm_fwd_call(h, s, g, b, alpha, eps), (h, s, g)


def _resnorm2d_bwd(alpha, eps, res, dy):
    h, s, g = res
    dx, dg, db = _resnorm_bwd_call(h, s, g, dy, alpha, eps)
    return alpha * dx, dx, dg, db


_resnorm2d.defvjp(_resnorm2d_fwd, _resnorm2d_bwd)


def residual_layer_norm(h, s, g, b, alpha):
    C = h.shape[-1]
    return _resnorm2d(h.reshape(-1, C), s.reshape(-1, C), g.reshape(1, C), b.reshape(1, C), alpha, LN_EPS).reshape(h.shape)


def layer_norm(x, g, b, eps=LN_EPS):
    C = x.shape[-1]
    return _norm2d(x.reshape(-1, C), g.reshape(1, C), b.reshape(1, C), True, eps).reshape(x.shape)


def rms_norm(x, g):
    C = x.shape[-1]
    return _norm2d(x.reshape(-1, C), g.reshape(1, C), jnp.zeros((1, C), F32), False, RMS_EPS).reshape(x.shape)


def _softmax_fwd_call(s, bias, scale):
    G1, G2, R, C = s.shape
    tr = _pick(R, max(8, (1 << 19) // C), 8)

    def body(s_ref, bias_ref, p_ref, lse_ref):
        z = s_ref[0, 0] * scale + bias_ref[0]
        m = jnp.max(z, axis=-1, keepdims=True)
        e = jnp.exp(z - m)
        den = jnp.sum(e, axis=-1, keepdims=True)
        p_ref[0, 0] = e / den
        lse_ref[0, 0] = m + jnp.log(den)

    blk = pl.BlockSpec((1, 1, tr, C), lambda a, b, r: (a, b, r, 0))
    col = pl.BlockSpec((1, 1, tr, 1), lambda a, b, r: (a, b, r, 0))
    return pl.pallas_call(
        body, name='softmax_fwd',
        out_shape=(jax.ShapeDtypeStruct(s.shape, F32), jax.ShapeDtypeStruct((G1, G2, R, 1), F32)),
        grid=(G1, G2, R // tr), in_specs=[blk, pl.BlockSpec((1, tr, C), lambda a, b, r: (b, r, 0))],
        out_specs=(blk, col), compiler_params=_params(('parallel', 'parallel', 'parallel')),
    )(s, bias)


def _softmax_bwd_call(p, dp, dlse, scale):
    G1, G2, R, C = p.shape
    tr = _pick(R, max(8, (1 << 19) // C), 8)

    def body(p_ref, dp_ref, dlse_ref, ds_ref):
        p = p_ref[0, 0]
        dp = dp_ref[0, 0]
        inner = jnp.sum(dp * p, axis=-1, keepdims=True)
        ds_ref[0, 0] = (p * (dp - inner + dlse_ref[0, 0])) * scale

    blk = pl.BlockSpec((1, 1, tr, C), lambda a, b, r: (a, b, r, 0))
    col = pl.BlockSpec((1, 1, tr, 1), lambda a, b, r: (a, b, r, 0))
    return pl.pallas_call(
        body, name='softmax_bwd', out_shape=jax.ShapeDtypeStruct(p.shape, F32),
        grid=(G1, G2, R // tr), in_specs=[blk, blk, col], out_specs=blk,
        compiler_params=_params(('parallel', 'parallel', 'parallel')),
    )(p, dp, dlse)


@functools.partial(jax.custom_vjp, nondiff_argnums=(2,))
def softmax_lse(s, bias, scale):
    return _softmax_fwd_call(s, bias, scale)


def _softmax_lse_fwd(s, bias, scale):
    p, lse = _softmax_fwd_call(s, bias, scale)
    return (p, lse), (p, bias)


def _softmax_lse_bwd(scale, res, cts):
    p, bias = res
    dp, dlse = cts
    return _softmax_bwd_call(p, dp, dlse, scale), jnp.zeros_like(bias)


softmax_lse.defvjp(_softmax_lse_fwd, _softmax_lse_bwd)


def _dot(a, b, dims):
    return lax.dot_general(a.astype(BF16), b.astype(BF16), dims, preferred_element_type=F32)


_NN, _NT, _TN = _DIMS[(False, False)], _DIMS[(False, True)], _DIMS[(True, False)]


def _gla_fwd_call(q, k, v, dec):
    G, nc, C, dk = q.shape
    dv = v.shape[-1]

    def body(q_ref, k_ref, v_ref, dec_ref, o_ref, st_ref, state):
        @pl.when(pl.program_id(1) == 0)
        def _():
            state[...] = jnp.zeros_like(state)

        s = state[...]
        st_ref[0, 0] = s
        o_ref[0, 0] = _dot(q_ref[0, 0], s, _NN)
        state[...] = s * dec_ref[0, 0] + _dot(k_ref[0, 0], v_ref[0, 0], _TN)

    def spec(r, c):
        return pl.BlockSpec((1, 1, r, c), lambda g, t: (g, t, 0, 0))

    return pl.pallas_call(
        body, name='gla_scan_fwd',
        out_shape=(jax.ShapeDtypeStruct((G, nc, C, dv), F32), jax.ShapeDtypeStruct((G, nc, dk, dv), F32)),
        grid=(G, nc), in_specs=[spec(C, dk), spec(C, dk), spec(C, dv), spec(dk, 1)],
        out_specs=(spec(C, dv), spec(dk, dv)), scratch_shapes=[pltpu.VMEM((dk, dv), F32)],
        compiler_params=_params(('parallel', 'arbitrary')),
    )(q, k, v, dec)


def _gla_bwd_call(q, k, v, dec, states, do):
    G, nc, C, dk = q.shape
    dv = v.shape[-1]

    def body(q_ref, k_ref, v_ref, dec_ref, st_ref, do_ref, dq_ref, dk_ref, dv_ref, ddec_ref, dstate):
        @pl.when(pl.program_id(1) == 0)
        def _():
            dstate[...] = jnp.zeros_like(dstate)

        s = st_ref[0, 0]
        d = dstate[...]
        do = do_ref[0, 0]
        dq_ref[0, 0] = _dot(do, s, _NT)
        dk_ref[0, 0] = _dot(v_ref[0, 0], d, _NT)
        dv_ref[0, 0] = _dot(k_ref[0, 0], d, _NN)
        ddec_ref[0, 0] = jnp.sum(s * d, axis=1, keepdims=True)
        dstate[...] = d * dec_ref[0, 0] + _dot(q_ref[0, 0], do, _TN)

    def spec(r, c):
        return pl.BlockSpec((1, 1, r, c), lambda g, t: (g, nc - 1 - t, 0, 0))

    return pl.pallas_call(
        body, name='gla_scan_bwd',
        out_shape=(jax.ShapeDtypeStruct(q.shape, F32), jax.ShapeDtypeStruct(k.shape, F32),
                   jax.ShapeDtypeStruct(v.shape, F32), jax.ShapeDtypeStruct(dec.shape, F32)),
        grid=(G, nc), in_specs=[spec(C, dk), spec(C, dk), spec(C, dv), spec(dk, 1), spec(dk, dv), spec(C, dv)],
        out_specs=(spec(C, dk), spec(C, dk), spec(C, dv), spec(dk, 1)), scratch_shapes=[pltpu.VMEM((dk, dv), F32)],
        compiler_params=_params(('parallel', 'arbitrary')),
    )(q, k, v, dec, states, do)


@jax.custom_vjp
def gla_scan(q, k, v, dec):
    return _gla_fwd_call(q, k, v, dec)[0]


def _gla_scan_fwd(q, k, v, dec):
    o, states = _gla_fwd_call(q, k, v, dec)
    return o, (q, k, v, dec, states)


def _gla_scan_bwd(res, do):
    return _gla_bwd_call(*res, do)


gla_scan.defvjp(_gla_scan_fwd, _gla_scan_bwd)


RWKV_PAIRS_PER_STEP = 8
RWKV_TIME_BLOCK = 32
RN = RWKV_HEAD_DIM


def _rwkv_consts():
    row = lax.broadcasted_iota(jnp.int32, (RN, LANES), 0)
    lane = lax.broadcasted_iota(jnp.int32, (RN, LANES), 1)
    diag = (lane % RN == row).astype(F32)
    r2 = lax.broadcasted_iota(jnp.int32, (LANES, LANES), 0)
    l2 = lax.broadcasted_iota(jnp.int32, (LANES, LANES), 1)
    seg = (r2 // RN == l2 // RN).astype(BF16)
    return diag, seg


def _stage(lhs_ref, slot, p):
    hi = p.astype(BF16)
    lhs_ref[pl.ds(slot * LANES, RN), :] = hi
    lhs_ref[pl.ds(slot * LANES + RN, RN), :] = (p - hi.astype(F32)).astype(BF16)


def _seg_sums(lhs_ref, nslots, seg):
    res = jnp.dot(lhs_ref[pl.ds(0, nslots * LANES), :], seg, preferred_element_type=F32)
    return [res[i * LANES:i * LANES + RN] + res[i * LANES + RN:(i + 1) * LANES] for i in range(nslots)]


def _rwkv_blocks(B, S, C):
    npairs = C // LANES
    pp = RWKV_PAIRS_PER_STEP if npairs % RWKV_PAIRS_PER_STEP == 0 else 1
    T = _pick(S, RWKV_TIME_BLOCK, 8)
    return npairs, pp, T


def _rwkv_fwd_call(r, w, k, v, kk, b):
    B, S, C = r.shape
    npairs, pp, T = _rwkv_blocks(B, S, C)
    G = SUBLANES

    def body(r_ref, w_ref, k_ref, v_ref, kk_ref, b_ref, y_ref, sall_ref, state, step_lhs, v_lhs, y_lhs):
        @pl.when(pl.program_id(2) == 0)
        def _():
            state[...] = jnp.zeros_like(state)

        diag, seg = _rwkv_consts()
        rowid = lax.broadcasted_iota(jnp.int32, (SUBLANES, LANES), 0)

        def group(t8, carry):
            rows = pl.ds(pl.multiple_of(t8 * G, G), G)
            sls = [slice(p * LANES, (p + 1) * LANES) for p in range(pp)]
            ops = [[ref[0, rows, sl] for ref in (r_ref, w_ref, k_ref, v_ref, kk_ref, b_ref)] for sl in sls]
            for j in range(G):
                for p in range(pp):
                    _stage(v_lhs, j * pp + p, diag * ops[p][3][j:j + 1])
            vcols = _seg_sums(v_lhs, G * pp, seg)
            s = list(carry)
            for j in range(G):
                for p in range(pp):
                    sall_ref[0, p, t8 * G + j] = s[p]
                    _stage(step_lhs, p, s[p] * ops[p][4][j:j + 1])
                sas = _seg_sums(step_lhs, pp, seg)
                for p in range(pp):
                    rt, wt, kt, _, _, bt = ops[p]
                    s[p] = s[p] * wt[j:j + 1] - sas[p] * bt[j:j + 1] + vcols[j * pp + p] * kt[j:j + 1]
                    _stage(y_lhs, j * pp + p, s[p] * rt[j:j + 1])
            ycols = _seg_sums(y_lhs, G * pp, seg)
            for p in range(pp):
                ytile = jnp.zeros((SUBLANES, LANES), F32)
                for j in range(G):
                    ytile = jnp.where(rowid == j, jnp.sum(diag * ycols[j * pp + p], axis=0, keepdims=True), ytile)
                y_ref[0, rows, sls[p]] = ytile
            return tuple(s)

        final = lax.fori_loop(0, T // G, group, tuple(state[p] for p in range(pp)))
        for p in range(pp):
            state[p] = final[p]

    seq = pl.BlockSpec((1, T, pp * LANES), lambda bi, g, t: (bi, t, g))
    return pl.pallas_call(
        body, name='rwkv_scan_fwd',
        out_shape=(jax.ShapeDtypeStruct((B, S, C), F32), jax.ShapeDtypeStruct((B, npairs, S, RN, LANES), F32)),
        grid=(B, npairs // pp, S // T), in_specs=[seq] * 6,
        out_specs=(seq, pl.BlockSpec((1, pp, T, RN, LANES), lambda bi, g, t: (bi, g, t, 0, 0))),
        scratch_shapes=[pltpu.VMEM((pp, RN, LANES), F32), pltpu.VMEM((pp * LANES, LANES), BF16),
                        pltpu.VMEM((G * pp * LANES, LANES), BF16), pltpu.VMEM((G * pp * LANES, LANES), BF16)],
        compiler_params=_params(('parallel', 'parallel', 'arbitrary')),
    )(r, w, k, v, kk, b)


def _rwkv_bwd_call(r, w, k, v, kk, b, sall, dy):
    B, S, C = r.shape
    npairs, pp, T = _rwkv_blocks(B, S, C)
    nt = S // T
    G = SUBLANES

    def body(r_ref, w_ref, k_ref, v_ref, kk_ref, b_ref, sall_ref, dy_ref,
             dr_ref, dw_ref, dk_ref, dv_ref, dkk_ref, db_ref, dstate, step_lhs, pre_lhs, dv_lhs):
        @pl.when(pl.program_id(2) == 0)
        def _():
            dstate[...] = jnp.zeros_like(dstate)

        diag, seg = _rwkv_consts()
        rowid = lax.broadcasted_iota(jnp.int32, (SUBLANES, LANES), 0)

        def colsum(z):
            return jnp.sum(z, axis=0, keepdims=True)

        def group(i, carry):
            t8 = T // G - 1 - i
            rows = pl.ds(pl.multiple_of(t8 * G, G), G)
            sls = [slice(p * LANES, (p + 1) * LANES) for p in range(pp)]
            ops = [[ref[0, rows, sl] for ref in (r_ref, w_ref, k_ref, v_ref, kk_ref, b_ref, dy_ref)] for sl in sls]
            for j in range(G):
                for p in range(pp):
                    _stage(pre_lhs, j * pp + p, sall_ref[0, p, t8 * G + j] * ops[p][4][j:j + 1])
                    _stage(pre_lhs, (G + j) * pp + p, diag * ops[p][3][j:j + 1])
                    _stage(pre_lhs, (2 * G + j) * pp + p, diag * ops[p][6][j:j + 1])
            pre = _seg_sums(pre_lhs, 3 * G * pp, seg)
            ds = list(carry)
            tiles = [[jnp.zeros((SUBLANES, LANES), F32) for _ in range(5)] for _ in range(pp)]
            for j in reversed(range(G)):
                d = []
                for p in range(pp):
                    rt, _, kt, _, _, bt, _ = ops[p]
                    d.append(ds[p] + pre[(2 * G + j) * pp + p] * rt[j:j + 1])
                    _stage(step_lhs, p, d[p] * bt[j:j + 1])
                    _stage(dv_lhs, j * pp + p, d[p] * kt[j:j + 1])
                dsas = _seg_sums(step_lhs, pp, seg)
                for p in range(pp):
                    rt, wt, kt, _, kkt, bt, _ = ops[p]
                    s = sall_ref[0, p, t8 * G + j]
                    sa, vcol, dycol = -pre[j * pp + p], pre[(G + j) * pp + p], pre[(2 * G + j) * pp + p]
                    if j < G - 1:
                        s2 = sall_ref[0, p, t8 * G + j + 1]
                    else:
                        s2 = s * wt[j:j + 1] + sa * bt[j:j + 1] + vcol * kt[j:j + 1]
                    vals = (colsum(s2 * dycol), colsum(d[p] * s), colsum(d[p] * vcol), -colsum(s * dsas[p]),
                            colsum(d[p] * sa))
                    tiles[p] = [jnp.where(rowid == j, val, tile) for val, tile in zip(vals, tiles[p])]
                    ds[p] = d[p] * wt[j:j + 1] - dsas[p] * kkt[j:j + 1]
            dvcols = _seg_sums(dv_lhs, G * pp, seg)
            for p in range(pp):
                dvt = jnp.zeros((SUBLANES, LANES), F32)
                for j in range(G):
                    dvt = jnp.where(rowid == j, colsum(diag * dvcols[j * pp + p]), dvt)
                dv_ref[0, rows, sls[p]] = dvt
                for ref, tile in zip((dr_ref, dw_ref, dk_ref, dkk_ref, db_ref), tiles[p]):
                    ref[0, rows, sls[p]] = tile
            return tuple(ds)

        final = lax.fori_loop(0, T // G, group, tuple(dstate[p] for p in range(pp)))
        for p in range(pp):
            dstate[p] = final[p]

    seq = pl.BlockSpec((1, T, pp * LANES), lambda bi, g, t: (bi, nt - 1 - t, g))
    sds = jax.ShapeDtypeStruct((B, S, C), F32)
    return pl.pallas_call(
        body, name='rwkv_scan_bwd', out_shape=(sds,) * 6,
        grid=(B, npairs // pp, nt),
        in_specs=[seq] * 6 + [pl.BlockSpec((1, pp, T, RN, LANES), lambda bi, g, t: (bi, g, nt - 1 - t, 0, 0)), seq],
        out_specs=(seq,) * 6,
        scratch_shapes=[pltpu.VMEM((pp, RN, LANES), F32), pltpu.VMEM((pp * LANES, LANES), BF16),
                        pltpu.VMEM((3 * G * pp * LANES, LANES), BF16), pltpu.VMEM((G * pp * LANES, LANES), BF16)],
        compiler_params=_params(('parallel', 'parallel', 'arbitrary')),
    )(r, w, k, v, kk, b, sall, dy)


@jax.custom_vjp
def rwkv_scan(r, w, k, v, kk, b):
    return _rwkv_fwd_call(r, w, k, v, kk, b)[0]


def _rwkv_scan_fwd(r, w, k, v, kk, b):
    y, sall = _rwkv_fwd_call(r, w, k, v, kk, b)
    return y, (r, w, k, v, kk, b, sall)


def _rwkv_scan_bwd(res, dy):
    return _rwkv_bwd_call(*res, dy)


rwkv_scan.defvjp(_rwkv_scan_fwd, _rwkv_scan_bwd)


def _loss_call(y, target):
    R, D = y.shape
    tr = _pick(R, max(8, (1 << 19) // D), 8)

    def body(y_ref, t_ref, dy_ref, part_ref):
        @pl.when(pl.program_id(0) == 0)
        def _():
            part_ref[...] = jnp.zeros_like(part_ref)

        diff = y_ref[...] - t_ref[...]
        dy_ref[...] = diff / D
        part_ref[...] += jnp.sum(jnp.mean(diff * diff, axis=-1, keepdims=True), axis=0, keepdims=True)

    row = pl.BlockSpec((tr, D), lambda i: (i, 0))
    dy, part = pl.pallas_call(
        body, name='loss_head',
        out_shape=(jax.ShapeDtypeStruct((R, D), F32), jax.ShapeDtypeStruct((1, 1), F32)),
        grid=(R // tr,), in_specs=[row, row], out_specs=(row, pl.BlockSpec((1, 1), lambda i: (0, 0))),
        compiler_params=_params(('arbitrary',)),
    )(y, target)
    return dy, part[0, 0]


def _sum_parts_call(parts):
    P, R, C = parts.shape
    tr = _pick(R, max(BF16_TILE_ROWS, (1 << 18) // C), BF16_TILE_ROWS)

    def body(p_ref, o_ref):
        acc = p_ref[0].astype(F32)
        for i in range(1, P):
            acc = acc + p_ref[i].astype(F32)
        o_ref[...] = acc

    return pl.pallas_call(
        body, name='sum_parts', out_shape=jax.ShapeDtypeStruct((R, C), F32), grid=(R // tr,),
        in_specs=[pl.BlockSpec((P, tr, C), lambda i: (0, i, 0))], out_specs=pl.BlockSpec((tr, C), lambda i: (i, 0)),
        compiler_params=_params(('parallel',)),
    )(parts)


def _adamw_call(w, g, m, v):
    R, C = w.shape
    tr = _pick(R, max(8, (1 << 18) // C), 8)

    def body(w_ref, g_ref, m_ref, v_ref, d_ref, nm_ref, nv_ref):
        g = g_ref[...]
        m = ADAM_B1 * m_ref[...] + (1.0 - ADAM_B1) * g
        v = ADAM_B2 * v_ref[...] + (1.0 - ADAM_B2) * (g * g)
        m_hat = m / (1.0 - ADAM_B1 ** ADAM_STEP)
        v_hat = v / (1.0 - ADAM_B2 ** ADAM_STEP)
        d_ref[...] = -ADAM_LR * (m_hat / (jnp.sqrt(v_hat) + ADAM_EPS) + ADAM_WD * w_ref[...])
        nm_ref[...] = m
        nv_ref[...] = v

    row = pl.BlockSpec((tr, C), lambda i: (i, 0))
    sds = jax.ShapeDtypeStruct((R, C), F32)
    return pl.pallas_call(
        body, name='adamw', out_shape=(sds, sds, sds), grid=(R // tr,),
        in_specs=[row] * 4, out_specs=(row,) * 3, compiler_params=_params(('parallel',)),
    )(w, g, m, v)


ANY = pl.BlockSpec(memory_space=pl.ANY)


def _place():
    return lax.axis_index('x'), lax.axis_index('y'), lax.axis_index('c')


COPY_SPLIT = 8
BF16_TILE_ROWS = 16
PACK_ROW_ALIGN = COPY_SPLIT * BF16_TILE_ROWS


def _row_split(rows):
    if rows % PACK_ROW_ALIGN == 0:
        return COPY_SPLIT, rows // COPY_SPLIT
    return 1, rows


def _all_gather_call(pack):
    _, R, C = pack.shape
    ns, rs = _row_split(R)

    def body(pk_ref, out_ref, send_sems, recv_sems):
        x, y, c = _place()
        chips = [(1 - x, y), (x, 1 - y), (1 - x, 1 - y)]
        me = 2 * x + y

        def copy(k, i, src, dst, to):
            rows = pl.ds(i * rs, rs)
            return pltpu.make_async_remote_copy(src_ref=src.at[rows], dst_ref=dst.at[rows], send_sem=send_sems.at[k * ns + i],
                                                recv_sem=recv_sems.at[k * ns + i], device_id=to, device_id_type=MESH)

        first = [copy(j, i, pk_ref.at[c], out_ref.at[me, c], (px, py, c))
                 for j, (px, py) in enumerate(chips) for i in range(ns)]
        for cp in first:
            cp.start()
        passed = []
        for i in range(ns):
            for j, (px, py) in enumerate(chips):
                landed = out_ref.at[2 * px + py, c]
                copy(j, i, landed, landed, (px, py, c)).wait_recv()
                fwd = copy(3 + j, i, landed, landed, (x, y, 1 - c))
                fwd.start()
                passed.append(fwd)
        for i in range(ns):
            for j, (px, py) in enumerate(chips):
                other = out_ref.at[2 * px + py, 1 - c]
                copy(3 + j, i, other, other, (x, y, 1 - c)).wait_recv()
        for cp in first + passed:
            cp.wait_send()

    others = pl.pallas_call(
        body, name='all_gather', out_shape=jax.ShapeDtypeStruct((4, 2, R, C), pack.dtype),
        in_specs=[ANY], out_specs=ANY,
        scratch_shapes=[pltpu.SemaphoreType.DMA((6 * ns,)), pltpu.SemaphoreType.DMA((6 * ns,))],
    )(pack)
    x, y, _ = _place()
    return lax.dynamic_update_slice(others, pack[None], (2 * x + y, 0, 0, 0))


def _scatter_call(src):
    _, _, R, C = src.shape
    ns, rs = _row_split(R)

    def body(src_ref, out_ref, send_sems, recv_sems):
        x, y, c = _place()
        me = 4 * x + 2 * y + c
        peers = []
        for rel in range(1, 8):
            px = 1 - x if rel & 4 else x
            py = 1 - y if rel & 2 else y
            pc = 1 - c if rel & 1 else c
            peers.append((rel - 1, px, py, pc))

        def copy(k, i, src, dst, to):
            rows = pl.ds(i * rs, rs)
            return pltpu.make_async_remote_copy(src_ref=src.at[rows], dst_ref=dst.at[rows], send_sem=send_sems.at[k * ns + i],
                                                recv_sem=recv_sems.at[k * ns + i], device_id=to, device_id_type=MESH)

        sends = [copy(k, i, src_ref.at[2 * px + py, pc], out_ref.at[me], (px, py, pc))
                 for i in range(ns) for k, px, py, pc in peers]
        for cp in sends:
            cp.start()
        for i in range(ns):
            for k, px, py, pc in peers:
                slot = out_ref.at[4 * px + 2 * py + pc]
                copy(k, i, slot, slot, (px, py, pc)).wait_recv()
        for cp in sends:
            cp.wait_send()

    others = pl.pallas_call(
        body, name='scatter_parts', out_shape=jax.ShapeDtypeStruct((8, R, C), src.dtype),
        in_specs=[ANY], out_specs=ANY,
        scratch_shapes=[pltpu.SemaphoreType.DMA((7 * ns,)), pltpu.SemaphoreType.DMA((7 * ns,))],
    )(src)
    x, y, c = _place()
    own = lax.dynamic_slice(src, (2 * x + y, c, 0, 0), (1, 1, R, C)).reshape(1, R, C)
    return lax.dynamic_update_slice(others, own, (4 * x + 2 * y + c, 0, 0))


def _sibling_exchange_call(half):
    R, C = half.shape
    ns, rs = _row_split(R)

    def body(h_ref, other_ref, send_sems, recv_sems):
        x, y, c = _place()

        def copy(i):
            rows = pl.ds(i * rs, rs)
            return pltpu.make_async_remote_copy(src_ref=h_ref.at[rows], dst_ref=other_ref.at[rows], send_sem=send_sems.at[i],
                                                recv_sem=recv_sems.at[i], device_id=(x, y, 1 - c), device_id_type=MESH)

        sends = [copy(i) for i in range(ns)]
        for cp in sends:
            cp.start()
        for cp in sends:
            cp.wait_recv()
        for cp in sends:
            cp.wait_send()

    return pl.pallas_call(
        body, name='sibling_exchange', out_shape=jax.ShapeDtypeStruct((R, C), half.dtype),
        in_specs=[ANY], out_specs=ANY,
        scratch_shapes=[pltpu.SemaphoreType.DMA((ns,)), pltpu.SemaphoreType.DMA((ns,))],
    )(half)


def rope_tables(seq_len, dim):
    inv = ROPE_THETA ** (-jnp.arange(0, dim, 2, dtype=F32) / dim)
    ang = jnp.arange(seq_len, dtype=F32)[:, None] * inv[None, :]
    return jnp.cos(ang), jnp.sin(ang)


def apply_rope(x, cos, sin):
    x1, x2 = jnp.split(x, 2, axis=-1)
    return jnp.concatenate([x1 * cos - x2 * sin, x1 * sin + x2 * cos], axis=-1)


def _pad_to(n):
    return -(-n // LANES) * LANES


def _pad_cols(w, widths):
    parts, at = [], 0
    for n in widths:
        parts.append(jnp.pad(w[..., at:at + n], [(0, 0)] * (w.ndim - 1) + [(0, _pad_to(n) - n)]))
        at += n
    return jnp.concatenate(parts, axis=-1)


def _split_padded(t, widths):
    out, at = [], 0
    for n in widths:
        out.append(t[..., at:at + n])
        at += _pad_to(n)
    return out


def _pad_rows(w, rows):
    return jnp.pad(w, ((0, rows - w.shape[0]), (0, 0)))


def _heads_attention(q, k, v, bias, scale):
    s = bmm_nt(q, k)
    p, _ = softmax_lse(s[:, None], bias[None], scale)
    return bmm_nn(p[:, 0], v)


def gla(q, k, v, r, gate_lr, w_gate2, b_gate, norm_g, norm_b):
    B, S, _ = q.shape
    H, dk, dv, C = GLA_HEADS, GLA_DK, GLA_DV, GLA_CHUNK
    nc = S // C
    log_a = jax.nn.log_sigmoid(mm(gate_lr, w_gate2) + b_gate) / GLA_TAU

    def chunks(t, d):
        return t.reshape(B, nc, C, H, d).transpose(0, 3, 1, 2, 4)

    qc = chunks(q, dk) * (dk ** -0.5)
    kc = chunks(k, dk)
    vc = chunks(v, dv)
    b = jnp.cumsum(chunks(log_a, dk), axis=3)
    b_last = b[:, :, :, -1:, :]
    q_dec = qc * jnp.exp(b)
    k_inv = kc * jnp.exp(-b)
    k_end = kc * jnp.exp(b_last - b)
    causal = jnp.tril(jnp.ones((C, C), dtype=bool))
    G = B * H
    att = bmm_nt(q_dec.reshape(G * nc, C, dk), k_inv.reshape(G * nc, C, dk))
    att = jnp.where(causal, att, 0.0)
    o_intra = bmm_nn(att, vc.reshape(G * nc, C, dv)).reshape(B, H, nc, C, dv)
    dec = jnp.exp(b_last[:, :, :, 0, :]).reshape(G, nc, dk, 1)
    o_inter = gla_scan(q_dec.reshape(G, nc, C, dk), k_end.reshape(G, nc, C, dk), vc.reshape(G, nc, C, dv), dec)
    o = o_intra + o_inter.reshape(B, H, nc, C, dv)
    o = o.transpose(0, 2, 3, 1, 4).reshape(B, S, H, dv)
    o = layer_norm(o, norm_g, norm_b).reshape(B, S, H * dv)
    return o * jax.nn.silu(r)


def even_mixer(x, p):
    B, S, _ = x.shape
    H = MLA_HEADS
    cos, sin = rope_tables(S, MLA_ROPE)
    z = mm(x, _pad_cols(p['ev_w_in'][0], EVEN_IN_WIDTHS))
    c_q, c_kv, k_pe, q_g, k_g, v_g, r_g, _ = _split_padded(z, EVEN_IN_WIDTHS)
    lr_at = sum(_pad_to(n) for n in EVEN_IN_WIDTHS[:-1])
    lr_g = z[..., lr_at:]
    q = mm(rms_norm(c_q, p['ev_mla_q_norm'][0]), p['ev_mla_w_uq'][0])
    q = q.reshape(B, S, H, MLA_NOPE + MLA_ROPE).transpose(0, 2, 1, 3)
    kv = mm(rms_norm(c_kv, p['ev_mla_kv_norm'][0]), p['ev_mla_w_ukv'][0])
    kv = kv.reshape(B, S, H, MLA_NOPE + MLA_V).transpose(0, 2, 1, 3)
    q_pe = apply_rope(q[..., MLA_NOPE:], cos, sin)
    k_pe = jnp.broadcast_to(apply_rope(k_pe[:, None], cos, sin), (B, H, S, MLA_ROPE))
    qf = jnp.concatenate([q[..., :MLA_NOPE], q_pe], axis=-1)
    kf = jnp.concatenate([kv[..., :MLA_NOPE], k_pe], axis=-1)
    pos = jnp.arange(S)
    bias = jnp.where(pos[None, :] <= pos[:, None], 0.0, NEG_BIG).astype(F32)
    a_out = _heads_attention(qf.reshape(B * H, S, -1), kf.reshape(B * H, S, -1),
                             kv[..., MLA_NOPE:].reshape(B * H, S, MLA_V), bias, (MLA_NOPE + MLA_ROPE) ** -0.5)
    a_out = a_out.reshape(B, H, S, MLA_V).transpose(0, 2, 1, 3).reshape(B, S, H * MLA_V)
    w_gate2 = _pad_rows(p['ev_gla_w_gate2'][0], lr_g.shape[-1])
    b_out = gla(q_g, k_g, v_g, r_g, lr_g, w_gate2, p['ev_gla_b_gate'][0], p['ev_gla_norm_g'][0], p['ev_gla_norm_b'][0])
    return mm(jnp.concatenate([a_out, b_out], axis=-1), p['ev_w_out'][0])


def dilated_branch(q, k, v, window, dil):
    B, H, S, dh = q.shape
    span = window // dil
    L = S // dil
    nb = -(-L // span)
    Lp = nb * span

    def residues(t):
        t = t.reshape(B, H, L, dil, dh).transpose(0, 1, 3, 2, 4)
        t = jnp.pad(t, ((0, 0), (0, 0), (0, 0), (0, Lp - L), (0, 0)))
        return t.reshape(B, H, dil, nb, span, dh)

    def with_prev(t):
        prev = jnp.pad(t, ((0, 0), (0, 0), (0, 0), (1, 0), (0, 0), (0, 0)))[:, :, :, :-1]
        return jnp.concatenate([prev, t], axis=4)

    qb = residues(q)
    kw, vw = with_prev(residues(k)), with_prev(residues(v))
    G = B * H * dil * nb
    s = bmm_nt(qb.reshape(G, span, dh), kw.reshape(G, 2 * span, dh))
    qi = jnp.arange(span)[:, None] + span
    kj = jnp.arange(2 * span)[None, :]
    dist = qi - kj
    in_band = (dist >= 0) & (dist <= span)
    has_prev = (jnp.arange(nb) > 0)[:, None, None] | (kj >= span)[None]
    valid = in_band[None] & has_prev
    bias = jnp.where(valid, 0.0, NEG_BIG).astype(F32)
    p, lse = softmax_lse(s.reshape(B * H * dil, nb, span, 2 * span), bias, dh ** -0.5)
    o = bmm_nn(p.reshape(G, span, 2 * span), vw.reshape(G, 2 * span, dh)).reshape(B, H, dil, nb, span, dh)
    lse = lse.reshape(B, H, dil, nb, span)

    def back(t):
        t = t.reshape((B, H, dil, Lp) + t.shape[5:])[:, :, :, :L]
        return jnp.moveaxis(t, 2, 3).reshape((B, H, S) + t.shape[4:])

    return back(o), back(lse)


def dilated_mixture(q, k, v):
    outs, lses = [], []
    for window, dil in DIL_BRANCHES:
        o, lse = dilated_branch(q, k, v, window, dil)
        outs.append(o)
        lses.append(lse)
    wts = jax.nn.softmax(jnp.stack(lses, axis=0), axis=0)
    return jnp.sum(wts[..., None] * jnp.stack(outs, axis=0), axis=0)


def token_shift(t, mu):
    prev = jnp.pad(t, ((0, 0), (1, 0), (0, 0)))[:, :-1]
    return t + (prev - t) * mu


def rwkv7(r, k, v, w_lr, a_lr, g_lr, w0, w_decay2, a0, w_a2, w_gate2, k_k, k_a, r_k, gn_g, gn_b):
    B, S, _ = r.shape
    H, n = RWKV_HEADS, RWKV_HEAD_DIM
    w = -jax.nn.softplus(-(w0 + mm(jnp.tanh(w_lr), w_decay2))) - 0.5
    decay = jnp.exp(-jnp.exp(w))
    a = jax.nn.sigmoid(a0 + mm(a_lr, w_a2))
    g = mm(jax.nn.sigmoid(g_lr), w_gate2)
    kk = (k * k_k).reshape(B, S, H, n)
    kk = kk / jnp.maximum(jnp.sqrt(jnp.sum(kk * kk, axis=-1, keepdims=True)), 1e-12)
    kk = kk.reshape(B, S, H * n)
    kh = k * (1.0 + (a - 1.0) * k_a)
    y = rwkv_scan(r, decay, kh, v, kk, kk * a).reshape(B, S, H, n)
    y = layer_norm(y, jnp.ones((n,), F32), jnp.zeros((n,), F32), RWKV_GN_EPS).reshape(B, S, H * n) * gn_g + gn_b
    bonus = jnp.sum((r * kh).reshape(B, S, H, n) * r_k, axis=-1, keepdims=True) * v.reshape(B, S, H, n)
    y = y + bonus.reshape(B, S, H * n)
    return y * g


def odd_mixer(x, p):
    B, S, _ = x.shape
    cos, sin = rope_tables(S, DIL_HEAD_DIM)
    widths = (3 * DIL_WIDTH,) + RWKV_IN_WIDTHS
    h = mm(x, _pad_cols(p['od_w_in'][0], widths))
    c_in = h[..., :3 * DIL_WIDTH]
    d_in = h[..., 3 * DIL_WIDTH:]
    q, k, v = [t.reshape(B, S, DIL_HEADS, DIL_HEAD_DIM).transpose(0, 2, 1, 3) for t in jnp.split(c_in, 3, axis=-1)]
    q, k = apply_rope(q, cos, sin), apply_rope(k, cos, sin)
    c_out = dilated_mixture(q, k, v).transpose(0, 2, 1, 3).reshape(B, S, DIL_WIDTH)
    mu = _pad_cols(p['od_rwkv_mu'][0], RWKV_IN_WIDTHS)
    sh = token_shift(d_in, mu)
    at = [0]
    for n in RWKV_IN_WIDTHS:
        at.append(at[-1] + _pad_to(n))
    r, kd, vd = [sh[..., at[i]:at[i + 1]] for i in range(3)]
    w_lr, a_lr, g_lr = [sh[..., at[i]:at[i + 1]] for i in range(3, 6)]
    d_out = rwkv7(r, kd, vd, w_lr, a_lr, g_lr, p['od_rwkv_w0'][0], _pad_rows(p['od_rwkv_w_decay2'][0], w_lr.shape[-1]),
                  p['od_rwkv_a0'][0], _pad_rows(p['od_rwkv_w_a2'][0], a_lr.shape[-1]), p['od_rwkv_w_gate2'][0],
                  p['od_rwkv_k_k'][0], p['od_rwkv_k_a'][0], p['od_rwkv_r_k'][0], p['od_rwkv_gn_g'][0], p['od_rwkv_gn_b'][0])
    return mm(jnp.concatenate([c_out, d_out], axis=-1), p['od_w_out'][0])


def cross_attention(x, mem, w_q, w_k, w_v, w_o):
    B, S, D = x.shape
    M = mem.shape[1]
    hd = D // XA_HEADS

    def heads(t, n):
        return t.reshape(B, n, XA_HEADS, hd).transpose(0, 2, 1, 3).reshape(B * XA_HEADS, n, hd)

    q, k, v = heads(mm(x, w_q), S), heads(mm(mem, w_k), M), heads(mm(mem, w_v), M)
    o = _heads_attention(q, k, v, jnp.zeros((S, M), F32), hd ** -0.5)
    o = o.reshape(B, XA_HEADS, S, hd).transpose(0, 2, 1, 3).reshape(B, S, D)
    return mm(o, w_o)


def swiglu(x, w_gate, w_up, w_down):
    return mm(jax.nn.silu(mm(x, w_gate)) * mm(x, w_up), w_down)


def forward(p, x, mem):
    h = x
    for layer in range(DEPTH):
        mix = even_mixer(h, p) if layer % 2 == 0 else odd_mixer(h, p)
        h = residual_layer_norm(h, mix, p['ln_mix_g'][layer], p['ln_mix_b'][layer], DEEPNORM_ALPHA)
        xa = cross_attention(h, mem, p['xa_w_q'][layer], p['xa_w_k'][layer], p['xa_w_v'][layer], p['xa_w_o'][layer])
        h = residual_layer_norm(h, xa, p['ln_xa_g'][layer], p['ln_xa_b'][layer], DEEPNORM_ALPHA)
        ff = swiglu(h, p['ffn_w_gate'][layer], p['ffn_w_up'][layer], p['ffn_w_down'][layer])
        h = residual_layer_norm(h, ff, p['ln_ffn_g'][layer], p['ln_ffn_b'][layer], DEEPNORM_ALPHA)
    return h


def _flat_pack(arrays, length, dtype):
    flat = jnp.concatenate([a.reshape(-1).astype(dtype) for a in arrays])
    return jnp.pad(flat, (0, length - flat.shape[0]))


def _unpack(flat, shapes):
    out, at = [], 0
    for shp in shapes:
        n = 1
        for d in shp:
            n *= d
        out.append(flat[at:at + n].reshape(shp))
        at += n
    return out


def _unpack_halves(halves, shapes):
    first, second = halves
    cut = first.shape[0]
    out, at = [], 0
    for shp in shapes:
        n = 1
        for d in shp:
            n *= d
        if at + n <= cut:
            flat = first[at:at + n]
        elif at >= cut:
            flat = second[at - cut:at - cut + n]
        else:
            flat = jnp.concatenate([first[at:], second[:at + n - cut]])
        out.append(flat.reshape(shp))
        at += n
    return out


def _shard_of(full, axis, s):
    n = full.shape[axis] // 4
    return lax.slice_in_dim(full, s * n, (s + 1) * n, axis=axis)


def kernel(x, mem, ev_w_in, ev_mla_q_norm, ev_mla_w_uq, ev_mla_kv_norm, ev_mla_w_ukv, ev_gla_w_gate2, ev_gla_b_gate, ev_gla_norm_g, ev_gla_norm_b, ev_w_out, od_w_in, od_rwkv_mu, od_rwkv_w0, od_rwkv_w_decay2, od_rwkv_a0, od_rwkv_w_a2, od_rwkv_w_gate2, od_rwkv_k_k, od_rwkv_k_a, od_rwkv_r_k, od_rwkv_gn_g, od_rwkv_gn_b, od_w_out, ln_mix_g, ln_mix_b, xa_w_q, xa_w_k, xa_w_v, xa_w_o, ln_xa_g, ln_xa_b, ffn_w_gate, ffn_w_up, ffn_w_down, ln_ffn_g, ln_ffn_b, loss_target, m_ev_w_in, m_ev_mla_q_norm, m_ev_mla_w_uq, m_ev_mla_kv_norm, m_ev_mla_w_ukv, m_ev_gla_w_gate2, m_ev_gla_b_gate, m_ev_gla_norm_g, m_ev_gla_norm_b, m_ev_w_out, m_od_w_in, m_od_rwkv_mu, m_od_rwkv_w0, m_od_rwkv_w_decay2, m_od_rwkv_a0, m_od_rwkv_w_a2, m_od_rwkv_w_gate2, m_od_rwkv_k_k, m_od_rwkv_k_a, m_od_rwkv_r_k, m_od_rwkv_gn_g, m_od_rwkv_gn_b, m_od_w_out, m_ln_mix_g, m_ln_mix_b, m_xa_w_q, m_xa_w_k, m_xa_w_v, m_xa_w_o, m_ln_xa_g, m_ln_xa_b, m_ffn_w_gate, m_ffn_w_up, m_ffn_w_down, m_ln_ffn_g, m_ln_ffn_b, v_ev_w_in, v_ev_mla_q_norm, v_ev_mla_w_uq, v_ev_mla_kv_norm, v_ev_mla_w_ukv, v_ev_gla_w_gate2, v_ev_gla_b_gate, v_ev_gla_norm_g, v_ev_gla_norm_b, v_ev_w_out, v_od_w_in, v_od_rwkv_mu, v_od_rwkv_w0, v_od_rwkv_w_decay2, v_od_rwkv_a0, v_od_rwkv_w_a2, v_od_rwkv_w_gate2, v_od_rwkv_k_k, v_od_rwkv_k_a, v_od_rwkv_r_k, v_od_rwkv_gn_g, v_od_rwkv_gn_b, v_od_w_out, v_ln_mix_g, v_ln_mix_b, v_xa_w_q, v_xa_w_k, v_xa_w_v, v_xa_w_o, v_ln_xa_g, v_ln_xa_b, v_ffn_w_gate, v_ffn_w_up, v_ffn_w_down, v_ln_ffn_g, v_ln_ffn_b):
    given = dict(locals())
    W = {n: given[n] for n in WEIGHT_NAMES}
    Mo = {n: given['m_' + n] for n in WEIGHT_NAMES}
    Vo = {n: given['v_' + n] for n in WEIGHT_NAMES}
    def count(shapes):
        total = 0
        for shp in shapes:
            n = 1
            for d in shp:
                n *= d
            total += n
        return total

    big_names = [n for n in WEIGHT_NAMES if n in MATRICES and count([given[n].shape]) >= NATIVE_MIN_ELEMENTS]
    mat_names = [n for n in WEIGHT_NAMES if n in MATRICES and n not in big_names]
    vec_names = [n for n in WEIGHT_NAMES if n in SHARDED_VECTORS]
    rep_names = list(REPLICATED)
    mat_shapes = [W[n].shape for n in mat_names]
    vec_shapes = [W[n].shape for n in vec_names]
    rep_shapes = [W[n].shape for n in rep_names]

    def halves_view(a):
        return a.reshape(2, count([a.shape[:-1]]) // 2, a.shape[-1])

    rc = -(-count(mat_shapes) // (2 * PACK_COLS * PACK_ROW_ALIGN)) * PACK_ROW_ALIGN
    mat_len = 2 * rc * PACK_COLS
    rv = -(-count(vec_shapes) // (2 * SMALL_COLS * 8)) * 8
    vec_len = 2 * rv * SMALL_COLS
    rr = -(-count(rep_shapes) // (SMALL_COLS * 8)) * 8
    rep_len = rr * SMALL_COLS

    wmat = _flat_pack([W[n] for n in mat_names], mat_len, BF16).reshape(2, rc, PACK_COLS)
    gathered = _all_gather_call(wmat)
    gvec = _all_gather_call(_flat_pack([W[n] for n in vec_names], vec_len, F32).reshape(2, rv, SMALL_COLS))
    full = {}
    mat_parts = [_unpack(gathered[s].reshape(-1), mat_shapes) for s in range(4)]
    for i, n in enumerate(mat_names):
        full[n] = jnp.concatenate([mat_parts[s][i] for s in range(4)], axis=MATRICES[n])
    for n in big_names:
        g4 = _all_gather_call(halves_view(W[n].astype(BF16))).reshape((4,) + W[n].shape)
        full[n] = jnp.concatenate([g4[s] for s in range(4)], axis=MATRICES[n])
    vec_parts =[_unpack(gvec[s].reshape(-1), vec_shapes) for s in range(4)]
    for i, n in enumerate(vec_names):
        full[n] = jnp.concatenate([vec_parts[s][i] for s in range(4)], axis=SHARDED_VECTORS[n])
    for n in rep_names:
        full[n] = W[n]

    B, S, D = x.shape
    y, vjp = jax.vjp(lambda p, xx: forward(p, xx, mem), full, x)
    dy, part = _loss_call(y.reshape(B * S, D), loss_target.reshape(B * S, D))
    loss = lax.psum(0.5 * part, ('x', 'y', 'c'))
    gfull, grad_x = vjp(dy.reshape(B, S, D))

    gmat = jnp.stack([_flat_pack([_shard_of(gfull[n], MATRICES[n], s) for n in mat_names], mat_len, BF16)
                      for s in range(4)]).reshape(4, 2, rc, PACK_COLS)
    half = _sum_parts_call(_scatter_call(gmat))
    other = _sibling_exchange_call(half)
    south = lax.axis_index('c') == 0
    ghalves = (jnp.where(south, half, other).reshape(-1), jnp.where(south, other, half).reshape(-1))
    grep = _flat_pack([gfull[n] for n in rep_names], rep_len, F32)
    gsmall = jnp.stack([jnp.concatenate([
        _flat_pack([_shard_of(gfull[n], SHARDED_VECTORS[n], s) for n in vec_names], vec_len, F32), grep])
        for s in range(4)]).reshape(4, 1, 2 * rv + rr, SMALL_COLS)
    gsmall = _sum_parts_call(_scatter_call(jnp.concatenate([gsmall, gsmall], axis=1)))

    def small_pack(src):
        return jnp.concatenate([_flat_pack([src[n] for n in vec_names], vec_len, F32),
                                _flat_pack([src[n] for n in rep_names], rep_len, F32)]).reshape(-1, SMALL_COLS)

    groups = [{}, {}, {}, {}]
    grads = dict(zip(mat_names, _unpack_halves(ghalves, mat_shapes)))
    for n in big_names:
        parts = jnp.stack([_shard_of(gfull[n], MATRICES[n], s) for s in range(4)])
        mine = _sum_parts_call(_scatter_call(parts.reshape((4,) + halves_view(W[n]).shape)))
        theirs = _sibling_exchange_call(mine)
        grads[n] = jnp.concatenate([jnp.where(south, mine, theirs), jnp.where(south, theirs, mine)]).reshape(W[n].shape)
    for n in big_names + mat_names:
        g = grads[n]
        rows = (-1, g.shape[-1])
        outs = _adamw_call(W[n].reshape(rows), g.reshape(rows), Mo[n].reshape(rows), Vo[n].reshape(rows))
        for grp, val in zip(groups, (g,) + outs):
            grp[n] = val.reshape(g.shape)
    small = (gsmall,) + _adamw_call(small_pack(W), gsmall, small_pack(Mo), small_pack(Vo))
    for grp, sm in zip(groups, small):
        sm = sm.reshape(-1)
        grp.update(zip(vec_names, _unpack(sm[:vec_len], vec_shapes)))
        grp.update(zip(rep_names, _unpack(sm[vec_len:], rep_shapes)))
    return (loss, grad_x, *[grp[n] for grp in groups for n in WEIGHT_NAMES])
```

```python
import functools

import jax
import jax.numpy as jnp
from jax import lax
from jax.experimental import pallas as pl
from jax.experimental.pallas import tpu as pltpu

F32 = jnp.float32
BF16 = jnp.bfloat16
MESH = pl.DeviceIdType.MESH

ROPE_THETA = 10000.0
LN_EPS = 1e-5
RMS_EPS = 1e-6
DEPTH = 2
DEEPNORM_ALPHA = (2.0 * DEPTH) ** 0.25
MLA_HEADS, MLA_NOPE, MLA_ROPE, MLA_V, MLA_Q_RANK, MLA_KV_RANK = 8, 128, 64, 128, 512, 256
GLA_HEADS, GLA_DK, GLA_DV, GLA_GATE_RANK, GLA_TAU, GLA_CHUNK = 4, 128, 256, 16, 16.0, 64
DIL_HEADS, DIL_HEAD_DIM = 8, 128
DIL_BRANCHES = ((128, 1), (512, 4), (2048, 16))
RWKV_HEADS, RWKV_HEAD_DIM = 16, 64
RWKV_DECAY_RANK, RWKV_A_RANK, RWKV_GATE_RANK = 96, 96, 256
RWKV_GN_EPS = 64e-5
XA_HEADS = 4
DIL_WIDTH = DIL_HEADS * DIL_HEAD_DIM
RWKV_WIDTH = RWKV_HEADS * RWKV_HEAD_DIM
EVEN_IN_WIDTHS = (MLA_Q_RANK, MLA_KV_RANK, MLA_ROPE, GLA_HEADS * GLA_DK, GLA_HEADS * GLA_DK,
                  GLA_HEADS * GLA_DV, GLA_HEADS * GLA_DV, GLA_GATE_RANK)
RWKV_IN_WIDTHS = (RWKV_WIDTH, RWKV_WIDTH, RWKV_WIDTH, RWKV_DECAY_RANK, RWKV_A_RANK, RWKV_GATE_RANK)
ADAM_LR, ADAM_B1, ADAM_B2, ADAM_EPS, ADAM_WD, ADAM_STEP = 0.001, 0.9, 0.999, 1e-08, 0.01, 10

LANES = 128
SUBLANES = 8
VMEM_LIMIT_BYTES = 48 * 1024 * 1024
NEG_BIG = -1e30

PACK_COLS = 1024
NATIVE_MIN_ELEMENTS = 1 << 20
SMALL_COLS = 128

MATRICES = {
    'ev_w_in': 2, 'ev_mla_w_uq': 2, 'ev_mla_w_ukv': 2, 'ev_gla_w_gate2': 2, 'ev_w_out': 1, 'od_w_in': 2,
    'od_rwkv_w_decay2': 2, 'od_rwkv_w_a2': 2, 'od_rwkv_w_gate2': 2, 'od_w_out': 1,
    'xa_w_q': 1, 'xa_w_k': 1, 'xa_w_v': 1, 'xa_w_o': 1, 'ffn_w_gate': 2, 'ffn_w_up': 2, 'ffn_w_down': 1,
}
SHARDED_VECTORS = {
    'od_rwkv_mu': 1, 'od_rwkv_w0': 1, 'od_rwkv_a0': 1, 'od_rwkv_k_k': 1, 'od_rwkv_k_a': 1,
    'od_rwkv_gn_g': 1, 'od_rwkv_gn_b': 1,
}
REPLICATED = ('ev_mla_q_norm', 'ev_mla_kv_norm', 'ev_gla_b_gate', 'ev_gla_norm_g', 'ev_gla_norm_b', 'od_rwkv_r_k',
              'ln_mix_g', 'ln_mix_b', 'ln_xa_g', 'ln_xa_b', 'ln_ffn_g', 'ln_ffn_b')
WEIGHT_NAMES = ('ev_w_in', 'ev_mla_q_norm', 'ev_mla_w_uq', 'ev_mla_kv_norm', 'ev_mla_w_ukv', 'ev_gla_w_gate2',
                'ev_gla_b_gate', 'ev_gla_norm_g', 'ev_gla_norm_b', 'ev_w_out', 'od_w_in', 'od_rwkv_mu', 'od_rwkv_w0',
                'od_rwkv_w_decay2', 'od_rwkv_a0', 'od_rwkv_w_a2', 'od_rwkv_w_gate2', 'od_rwkv_k_k', 'od_rwkv_k_a',
                'od_rwkv_r_k', 'od_rwkv_gn_g', 'od_rwkv_gn_b', 'od_w_out', 'ln_mix_g', 'ln_mix_b', 'xa_w_q', 'xa_w_k',
                'xa_w_v', 'xa_w_o', 'ln_xa_g', 'ln_xa_b', 'ffn_w_gate', 'ffn_w_up', 'ffn_w_down', 'ln_ffn_g', 'ln_ffn_b')


def _pick(n, cap, mult):
    d = (min(cap, n) // mult) * mult
    while d >= mult:
        if n % d == 0:
            return d
        d -= mult
    return n


def _params(semantics):
    return pltpu.CompilerParams(dimension_semantics=semantics, vmem_limit_bytes=VMEM_LIMIT_BYTES)


_DIMS = {(False, False): (((1,), (0,)), ((), ())), (False, True): (((1,), (1,)), ((), ())),
         (True, False): (((0,), (0,)), ((), ()))}


def _bmm(a, b, ta, tb, out_dtype=F32):
    G = a.shape[0]
    K, M = (a.shape[1], a.shape[2]) if ta else (a.shape[2], a.shape[1])
    N = b.shape[1] if tb else b.shape[2]
    assert (b.shape[2] if tb else b.shape[1]) == K and b.shape[0] == G
    tm, tn = _pick(M, 1024, LANES), _pick(N, 1024, LANES)
    tk = K if K <= 2048 else _pick(K, 2048, LANES)
    nk = K // tk
    gb = 1
    if tm == M and tn == N and nk == 1:
        per = 4 * (M * K + K * N + M * N)
        gb = _pick(G, max(1, min(8, (2 << 20) // per)), 1)
    dims = _DIMS[(ta, tb)]

    def body(a_ref, b_ref, o_ref, *scratch):
        def prod(i):
            return lax.dot_general(a_ref[i].astype(BF16), b_ref[i].astype(BF16), dims, preferred_element_type=F32)

        if nk == 1:
            for i in range(gb):
                o_ref[i] = prod(i).astype(o_ref.dtype)
        else:
            acc_ref, = scratch
            k = pl.program_id(3)

            @pl.when(k == 0)
            def _():
                acc_ref[...] = jnp.zeros_like(acc_ref)

            for i in range(gb):
                acc_ref[i] += prod(i)

            @pl.when(k == nk - 1)
            def _():
                o_ref[...] = acc_ref[...].astype(o_ref.dtype)

    a_spec = (pl.BlockSpec((gb, tk, tm), lambda g, i, j, k: (g, k, i)) if ta
              else pl.BlockSpec((gb, tm, tk), lambda g, i, j, k: (g, i, k)))
    b_spec = (pl.BlockSpec((gb, tn, tk), lambda g, i, j, k: (g, j, k)) if tb
              else pl.BlockSpec((gb, tk, tn), lambda g, i, j, k: (g, k, j)))
    return pl.pallas_call(
        body, name='bmm_' + ('t' if ta else 'n') + ('t' if tb else 'n'),
        out_shape=jax.ShapeDtypeStruct((G, M, N), out_dtype),
        grid=(G // gb, M // tm, N // tn, nk),
        in_specs=[a_spec, b_spec],
        out_specs=pl.BlockSpec((gb, tm, tn), lambda g, i, j, k: (g, i, j)),
        scratch_shapes=[] if nk == 1 else [pltpu.VMEM((gb, tm, tn), F32)],
        compiler_params=_params(('parallel', 'parallel', 'parallel', 'arbitrary')),
    )(a, b)


TN_CACHED_MAX_K = 4096


def _mm_tn_cached(a, b, out_dtype):
    K, M = a.shape
    N = b.shape[1]
    tm, tn = _pick(M, 512, LANES), _pick(N, 512, LANES)

    def body(a_ref, b_ref, o_ref, at_ref):
        @pl.when(pl.program_id(1) == 0)
        def _():
            at_ref[...] = a_ref[...].astype(BF16).T

        o_ref[...] = jnp.dot(at_ref[...], b_ref[...].astype(BF16), preferred_element_type=F32).astype(o_ref.dtype)

    return pl.pallas_call(
        body, name='mm_tn_cached', out_shape=jax.ShapeDtypeStruct((M, N), out_dtype),
        grid=(M // tm, N // tn),
        in_specs=[pl.BlockSpec((K, tm), lambda i, j: (0, i)), pl.BlockSpec((K, tn), lambda i, j: (0, j))],
        out_specs=pl.BlockSpec((tm, tn), lambda i, j: (i, j)),
        scratch_shapes=[pltpu.VMEM((tm, K), BF16)],
        compiler_params=_params(('parallel', 'arbitrary')),
    )(a, b)


def _weight_grad(a, g, out_dtype):
    if a.shape[0] == 1 and a.shape[1] <= TN_CACHED_MAX_K:
        return _mm_tn_cached(a[0], g[0], out_dtype)[None]
    return _bmm(a, g, True, False, out_dtype)


def _like(x):
    return jnp.zeros((), x.dtype)


@jax.custom_vjp
def bmm_nn(a, b):
    return _bmm(a, b, False, False)


def _bmm_nn_fwd(a, b):
    ab, bb = a.astype(BF16), b.astype(BF16)
    return _bmm(ab, bb, False, False), (ab, bb, _like(a), _like(b))


def _bmm_nn_bwd(res, g):
    a, b, la, lb = res
    g = g.astype(BF16)
    return _bmm(g, b, False, True, la.dtype), _weight_grad(a, g, lb.dtype)


bmm_nn.defvjp(_bmm_nn_fwd, _bmm_nn_bwd)


@jax.custom_vjp
def bmm_nt(a, b):
    return _bmm(a, b, False, True)


def _bmm_nt_fwd(a, b):
    ab, bb = a.astype(BF16), b.astype(BF16)
    return _bmm(ab, bb, False, True), (ab, bb, _like(a), _like(b))


def _bmm_nt_bwd(res, g):
    a, b, la, lb = res
    g = g.astype(BF16)
    return _bmm(g, b, False, False, la.dtype), _bmm(g, a, True, False, lb.dtype)


bmm_nt.defvjp(_bmm_nt_fwd, _bmm_nt_bwd)


def mm(x, w):
    lead = x.shape[:-1]
    out = bmm_nn(x.reshape(1, -1, x.shape[-1]), w[None])
    return out.reshape(lead + (w.shape[1],))


def _norm_stats(x, center, eps):
    if center:
        xc = x - jnp.mean(x, axis=-1, keepdims=True)
    else:
        xc = x
    rstd = lax.rsqrt(jnp.mean(xc * xc, axis=-1, keepdims=True) + eps)
    return xc * rstd, rstd


def _norm_fwd_call(x, g, b, center, eps):
    R, C = x.shape
    tr = _pick(R, max(8, (1 << 19) // C), 8)

    def body(x_ref, g_ref, b_ref, y_ref):
        xhat, _ = _norm_stats(x_ref[...], center, eps)
        y_ref[...] = xhat * g_ref[...] + b_ref[...]

    row = pl.BlockSpec((tr, C), lambda i: (i, 0))
    vec = pl.BlockSpec((1, C), lambda i: (0, 0))
    return pl.pallas_call(
        body, name='norm_fwd', out_shape=jax.ShapeDtypeStruct((R, C), F32), grid=(R // tr,),
        in_specs=[row, vec, vec], out_specs=row, compiler_params=_params(('parallel',)),
    )(x, g, b)


def _norm_bwd_call(x, g, dy, center, eps):
    R, C = x.shape
    tr = _pick(R, max(8, (1 << 19) // C), 8)

    def body(x_ref, g_ref, dy_ref, dx_ref, dg_ref, db_ref):
        @pl.when(pl.program_id(0) == 0)
        def _():
            dg_ref[...] = jnp.zeros_like(dg_ref)
            db_ref[...] = jnp.zeros_like(db_ref)

        xhat, rstd = _norm_stats(x_ref[...], center, eps)
        dy = dy_ref[...]
        dxh = dy * g_ref[...]
        proj = xhat * jnp.mean(dxh * xhat, axis=-1, keepdims=True)
        if center:
            dx_ref[...] = rstd * (dxh - jnp.mean(dxh, axis=-1, keepdims=True) - proj)
        else:
            dx_ref[...] = rstd * (dxh - proj)
        dg_ref[...] += jnp.sum(dy * xhat, axis=0, keepdims=True)
        db_ref[...] += jnp.sum(dy, axis=0, keepdims=True)

    row = pl.BlockSpec((tr, C), lambda i: (i, 0))
    vec = pl.BlockSpec((1, C), lambda i: (0, 0))
    return pl.pallas_call(
        body, name='norm_bwd',
        out_shape=(jax.ShapeDtypeStruct((R, C), F32), jax.ShapeDtypeStruct((1, C), F32), jax.ShapeDtypeStruct((1, C), F32)),
        grid=(R // tr,), in_specs=[row, vec, row], out_specs=(row, vec, vec), compiler_params=_params(('arbitrary',)),
    )(x, g, dy)


@functools.partial(jax.custom_vjp, nondiff_argnums=(3, 4))
def _norm2d(x, g, b, center, eps):
    return _norm_fwd_call(x, g, b, center, eps)


def _norm2d_fwd(x, g, b, center, eps):
    return _norm_fwd_call(x, g, b, center, eps), (x, g)


def _norm2d_bwd(center, eps, res, dy):
    x, g = res
    return _norm_bwd_call(x, g, dy, center, eps)


_norm2d.defvjp(_norm2d_fwd, _norm2d_bwd)


def _resnorm_fwd_call(h, s, g, b, alpha, eps):
    R, C = h.shape
    tr = _pick(R, max(8, (1 << 19) // C), 8)

    def body(h_ref, s_ref, g_ref, b_ref, y_ref):
        xhat, _ = _norm_stats(alpha * h_ref[...] + s_ref[...], True, eps)
        y_ref[...] = xhat * g_ref[...] + b_ref[...]

    row = pl.BlockSpec((tr, C), lambda i: (i, 0))
    vec = pl.BlockSpec((1, C), lambda i: (0, 0))
    return pl.pallas_call(
        body, name='resnorm_fwd', out_shape=jax.ShapeDtypeStruct((R, C), F32), grid=(R // tr,),
        in_specs=[row, row, vec, vec], out_specs=row, compiler_params=_params(('parallel',)),
    )(h, s, g, b)


def _resnorm_bwd_call(h, s, g, dy, alpha, eps):
    R, C = h.shape
    tr = _pick(R, max(8, (1 << 19) // C), 8)

    def body(h_ref, s_ref, g_ref, dy_ref, dx_ref, dg_ref, db_ref):
        @pl.when(pl.program_id(0) == 0)
        def _():
            dg_ref[...] = jnp.zeros_like(dg_ref)
            db_ref[...] = jnp.zeros_like(db_ref)

        xhat, rstd = _norm_stats(alpha * h_ref[...] + s_ref[...], True, eps)
        dy = dy_ref[...]
        dxh = dy * g_ref[...]
        proj = xhat * jnp.mean(dxh * xhat, axis=-1, keepdims=True)
        dx_ref[...] = rstd * (dxh - jnp.mean(dxh, axis=-1, keepdims=True) - proj)
        dg_ref[...] += jnp.sum(dy * xhat, axis=0, keepdims=True)
        db_ref[...] += jnp.sum(dy, axis=0, keepdims=True)

    row = pl.BlockSpec((tr, C), lambda i: (i, 0))
    vec = pl.BlockSpec((1, C), lambda i: (0, 0))
    return pl.pallas_call(
        body, name='resnorm_bwd',
        out_shape=(jax.ShapeDtypeStruct((R, C), F32), jax.ShapeDtypeStruct((1, C), F32), jax.ShapeDtypeStruct((1, C), F32)),
        grid=(R // tr,), in_specs=[row, row, vec, row], out_specs=(row, vec, vec), compiler_params=_params(('arbitrary',)),
    )(h, s, g, dy)


@functools.partial(jax.custom_vjp, nondiff_argnums=(4, 5))
def _resnorm2d(h, s, g, b, alpha, eps):
    return _resnorm_fwd_call(h, s, g, b, alpha, eps)


def _resnorm2d_fwd(h, s, g, b, alpha, eps):
    return _resnorm_fwd_call(h, s, g, b, alpha, eps), (h, s, g)


def _resnorm2d_bwd(alpha, eps, res, dy):
    h, s, g = res
    dx, dg, db = _resnorm_bwd_call(h, s, g, dy, alpha, eps)
    return alpha * dx, dx, dg, db


_resnorm2d.defvjp(_resnorm2d_fwd, _resnorm2d_bwd)


def residual_layer_norm(h, s, g, b, alpha):
    C = h.shape[-1]
    return _resnorm2d(h.reshape(-1, C), s.reshape(-1, C), g.reshape(1, C), b.reshape(1, C), alpha, LN_EPS).reshape(h.shape)


def layer_norm(x, g, b, eps=LN_EPS):
    C = x.shape[-1]
    return _norm2d(x.reshape(-1, C), g.reshape(1, C), b.reshape(1, C), True, eps).reshape(x.shape)


def rms_norm(x, g):
    C = x.shape[-1]
    return _norm2d(x.reshape(-1, C), g.reshape(1, C), jnp.zeros((1, C), F32), False, RMS_EPS).reshape(x.shape)


def _softmax_fwd_call(s, bias, scale):
    G1, G2, R, C = s.shape
    tr = _pick(R, max(8, (1 << 19) // C), 8)

    def body(s_ref, bias_ref, p_ref, lse_ref):
        z = s_ref[0, 0] * scale + bias_ref[0]
        m = jnp.max(z, axis=-1, keepdims=True)
        e = jnp.exp(z - m)
        den = jnp.sum(e, axis=-1, keepdims=True)
        p_ref[0, 0] = e / den
        lse_ref[0, 0] = m + jnp.log(den)

    blk = pl.BlockSpec((1, 1, tr, C), lambda a, b, r: (a, b, r, 0))
    col = pl.BlockSpec((1, 1, tr, 1), lambda a, b, r: (a, b, r, 0))
    return pl.pallas_call(
        body, name='softmax_fwd',
        out_shape=(jax.ShapeDtypeStruct(s.shape, F32), jax.ShapeDtypeStruct((G1, G2, R, 1), F32)),
        grid=(G1, G2, R // tr), in_specs=[blk, pl.BlockSpec((1, tr, C), lambda a, b, r: (b, r, 0))],
        out_specs=(blk, col), compiler_params=_params(('parallel', 'parallel', 'parallel')),
    )(s, bias)


def _softmax_bwd_call(p, dp, dlse, scale):
    G1, G2, R, C = p.shape
    tr = _pick(R, max(8, (1 << 19) // C), 8)

    def body(p_ref, dp_ref, dlse_ref, ds_ref):
        p = p_ref[0, 0]
        dp = dp_ref[0, 0]
        inner = jnp.sum(dp * p, axis=-1, keepdims=True)
        ds_ref[0, 0] = (p * (dp - inner + dlse_ref[0, 0])) * scale

    blk = pl.BlockSpec((1, 1, tr, C), lambda a, b, r: (a, b, r, 0))
    col = pl.BlockSpec((1, 1, tr, 1), lambda a, b, r: (a, b, r, 0))
    return pl.pallas_call(
        body, name='softmax_bwd', out_shape=jax.ShapeDtypeStruct(p.shape, F32),
        grid=(G1, G2, R // tr), in_specs=[blk, blk, col], out_specs=blk,
        compiler_params=_params(('parallel', 'parallel', 'parallel')),
    )(p, dp, dlse)


@functools.partial(jax.custom_vjp, nondiff_argnums=(2,))
def softmax_lse(s, bias, scale):
    return _softmax_fwd_call(s, bias, scale)


def _softmax_lse_fwd(s, bias, scale):
    p, lse = _softmax_fwd_call(s, bias, scale)
    return (p, lse), (p, bias)


def _softmax_lse_bwd(scale, res, cts):
    p, bias = res
    dp, dlse = cts
    return _softmax_bwd_call(p, dp, dlse, scale), jnp.zeros_like(bias)


softmax_lse.defvjp(_softmax_lse_fwd, _softmax_lse_bwd)


def _dot(a, b, dims):
    return lax.dot_general(a.astype(BF16), b.astype(BF16), dims, preferred_element_type=F32)


_NN, _NT, _TN = _DIMS[(False, False)], _DIMS[(False, True)], _DIMS[(True, False)]


def _gla_fwd_call(q, k, v, dec):
    G, nc, C, dk = q.shape
    dv = v.shape[-1]

    def body(q_ref, k_ref, v_ref, dec_ref, o_ref, st_ref, state):
        @pl.when(pl.program_id(1) == 0)
        def _():
            state[...] = jnp.zeros_like(state)

        s = state[...]
        st_ref[0, 0] = s
        o_ref[0, 0] = _dot(q_ref[0, 0], s, _NN)
        state[...] = s * dec_ref[0, 0] + _dot(k_ref[0, 0], v_ref[0, 0], _TN)

    def spec(r, c):
        return pl.BlockSpec((1, 1, r, c), lambda g, t: (g, t, 0, 0))

    return pl.pallas_call(
        body, name='gla_scan_fwd',
        out_shape=(jax.ShapeDtypeStruct((G, nc, C, dv), F32), jax.ShapeDtypeStruct((G, nc, dk, dv), F32)),
        grid=(G, nc), in_specs=[spec(C, dk), spec(C, dk), spec(C, dv), spec(dk, 1)],
        out_specs=(spec(C, dv), spec(dk, dv)), scratch_shapes=[pltpu.VMEM((dk, dv), F32)],
        compiler_params=_params(('parallel', 'arbitrary')),
    )(q, k, v, dec)


def _gla_bwd_call(q, k, v, dec, states, do):
    G, nc, C, dk = q.shape
    dv = v.shape[-1]

    def body(q_ref, k_ref, v_ref, dec_ref, st_ref, do_ref, dq_ref, dk_ref, dv_ref, ddec_ref, dstate):
        @pl.when(pl.program_id(1) == 0)
        def _():
            dstate[...] = jnp.zeros_like(dstate)

        s = st_ref[0, 0]
        d = dstate[...]
        do = do_ref[0, 0]
        dq_ref[0, 0] = _dot(do, s, _NT)
        dk_ref[0, 0] = _dot(v_ref[0, 0], d, _NT)
        dv_ref[0, 0] = _dot(k_ref[0, 0], d, _NN)
        ddec_ref[0, 0] = jnp.sum(s * d, axis=1, keepdims=True)
        dstate[...] = d * dec_ref[0, 0] + _dot(q_ref[0, 0], do, _TN)

    def spec(r, c):
        return pl.BlockSpec((1, 1, r, c), lambda g, t: (g, nc - 1 - t, 0, 0))

    return pl.pallas_call(
        body, name='gla_scan_bwd',
        out_shape=(jax.ShapeDtypeStruct(q.shape, F32), jax.ShapeDtypeStruct(k.shape, F32),
                   jax.ShapeDtypeStruct(v.shape, F32), jax.ShapeDtypeStruct(dec.shape, F32)),
        grid=(G, nc), in_specs=[spec(C, dk), spec(C, dk), spec(C, dv), spec(dk, 1), spec(dk, dv), spec(C, dv)],
        out_specs=(spec(C, dk), spec(C, dk), spec(C, dv), spec(dk, 1)), scratch_shapes=[pltpu.VMEM((dk, dv), F32)],
        compiler_params=_params(('parallel', 'arbitrary')),
    )(q, k, v, dec, states, do)


@jax.custom_vjp
def gla_scan(q, k, v, dec):
    return _gla_fwd_call(q, k, v, dec)[0]


def _gla_scan_fwd(q, k, v, dec):
    o, states = _gla_fwd_call(q, k, v, dec)
    return o, (q, k, v, dec, states)


def _gla_scan_bwd(res, do):
    return _gla_bwd_call(*res, do)


gla_scan.defvjp(_gla_scan_fwd, _gla_scan_bwd)


RWKV_PAIRS_PER_STEP = 8
RWKV_TIME_BLOCK = 32
RN = RWKV_HEAD_DIM


def _rwkv_consts():
    row = lax.broadcasted_iota(jnp.int32, (RN, LANES), 0)
    lane = lax.broadcasted_iota(jnp.int32, (RN, LANES), 1)
    diag = (lane % RN == row).astype(F32)
    r2 = lax.broadcasted_iota(jnp.int32, (LANES, LANES), 0)
    l2 = lax.broadcasted_iota(jnp.int32, (LANES, LANES), 1)
    seg = (r2 // RN == l2 // RN).astype(BF16)
    return diag, seg


def _stage(lhs_ref, slot, p):
    hi = p.astype(BF16)
    lhs_ref[pl.ds(slot * LANES, RN), :] = hi
    lhs_ref[pl.ds(slot * LANES + RN, RN), :] = (p - hi.astype(F32)).astype(BF16)


def _seg_sums(lhs_ref, nslots, seg):
    res = jnp.dot(lhs_ref[pl.ds(0, nslots * LANES), :], seg, preferred_element_type=F32)
    return [res[i * LANES:i * LANES + RN] + res[i * LANES + RN:(i + 1) * LANES] for i in range(nslots)]


def _rwkv_blocks(B, S, C):
    npairs = C // LANES
    pp = RWKV_PAIRS_PER_STEP if npairs % RWKV_PAIRS_PER_STEP == 0 else 1
    T = _pick(S, RWKV_TIME_BLOCK, 8)
    return npairs, pp, T


def _rwkv_fwd_call(r, w, k, v, kk, b):
    B, S, C = r.shape
    npairs, pp, T = _rwkv_blocks(B, S, C)
    G = SUBLANES

    def body(r_ref, w_ref, k_ref, v_ref, kk_ref, b_ref, y_ref, sall_ref, state, step_lhs, v_lhs, y_lhs):
        @pl.when(pl.program_id(2) == 0)
        def _():
            state[...] = jnp.zeros_like(state)

        diag, seg = _rwkv_consts()
        rowid = lax.broadcasted_iota(jnp.int32, (SUBLANES, LANES), 0)

        def group(t8, carry):
            rows = pl.ds(pl.multiple_of(t8 * G, G), G)
            sls = [slice(p * LANES, (p + 1) * LANES) for p in range(pp)]
            ops = [[ref[0, rows, sl] for ref in (r_ref, w_ref, k_ref, v_ref, kk_ref, b_ref)] for sl in sls]
            for j in range(G):
                for p in range(pp):
                    _stage(v_lhs, j * pp + p, diag * ops[p][3][j:j + 1])
            vcols = _seg_sums(v_lhs, G * pp, seg)
            s = list(carry)
            for j in range(G):
                for p in range(pp):
                    sall_ref[0, p, t8 * G + j] = s[p]
                    _stage(step_lhs, p, s[p] * ops[p][4][j:j + 1])
                sas = _seg_sums(step_lhs, pp, seg)
                for p in range(pp):
                    rt, wt, kt, _, _, bt = ops[p]
                    s[p] = s[p] * wt[j:j + 1] - sas[p] * bt[j:j + 1] + vcols[j * pp + p] * kt[j:j + 1]
                    _stage(y_lhs, j * pp + p, s[p] * rt[j:j + 1])
            ycols = _seg_sums(y_lhs, G * pp, seg)
            for p in range(pp):
                ytile = jnp.zeros((SUBLANES, LANES), F32)
                for j in range(G):
                    ytile = jnp.where(rowid == j, jnp.sum(diag * ycols[j * pp + p], axis=0, keepdims=True), ytile)
                y_ref[0, rows, sls[p]] = ytile
            return tuple(s)

        final = lax.fori_loop(0, T // G, group, tuple(state[p] for p in range(pp)))
        for p in range(pp):
            state[p] = final[p]

    seq = pl.BlockSpec((1, T, pp * LANES), lambda bi, g, t: (bi, t, g))
    return pl.pallas_call(
        body, name='rwkv_scan_fwd',
        out_shape=(jax.ShapeDtypeStruct((B, S, C), F32), jax.ShapeDtypeStruct((B, npairs, S, RN, LANES), F32)),
        grid=(B, npairs // pp, S // T), in_specs=[seq] * 6,
        out_specs=(seq, pl.BlockSpec((1, pp, T, RN, LANES), lambda bi, g, t: (bi, g, t, 0, 0))),
        scratch_shapes=[pltpu.VMEM((pp, RN, LANES), F32), pltpu.VMEM((pp * LANES, LANES), BF16),
                        pltpu.VMEM((G * pp * LANES, LANES), BF16), pltpu.VMEM((G * pp * LANES, LANES), BF16)],
        compiler_params=_params(('parallel', 'parallel', 'arbitrary')),
    )(r, w, k, v, kk, b)


def _rwkv_bwd_call(r, w, k, v, kk, b, sall, dy):
    B, S, C = r.shape
    npairs, pp, T = _rwkv_blocks(B, S, C)
    nt = S // T
    G = SUBLANES

    def body(r_ref, w_ref, k_ref, v_ref, kk_ref, b_ref, sall_ref, dy_ref,
             dr_ref, dw_ref, dk_ref, dv_ref, dkk_ref, db_ref, dstate, step_lhs, pre_lhs, dv_lhs):
        @pl.when(pl.program_id(2) == 0)
        def _():
            dstate[...] = jnp.zeros_like(dstate)

        diag, seg = _rwkv_consts()
        rowid = lax.broadcasted_iota(jnp.int32, (SUBLANES, LANES), 0)

        def colsum(z):
            return jnp.sum(z, axis=0, keepdims=True)

        def group(i, carry):
            t8 = T // G - 1 - i
            rows = pl.ds(pl.multiple_of(t8 * G, G), G)
            sls = [slice(p * LANES, (p + 1) * LANES) for p in range(pp)]
            ops = [[ref[0, rows, sl] for ref in (r_ref, w_ref, k_ref, v_ref, kk_ref, b_ref, dy_ref)] for sl in sls]
            for j in range(G):
                for p in range(pp):
                    _stage(pre_lhs, j * pp + p, sall_ref[0, p, t8 * G + j] * ops[p][4][j:j + 1])
                    _stage(pre_lhs, (G + j) * pp + p, diag * ops[p][3][j:j + 1])
                    _stage(pre_lhs, (2 * G + j) * pp + p, diag * ops[p][6][j:j + 1])
            pre = _seg_sums(pre_lhs, 3 * G * pp, seg)
            ds = list(carry)
            tiles = [[jnp.zeros((SUBLANES, LANES), F32) for _ in range(5)] for _ in range(pp)]
            for j in reversed(range(G)):
                d = []
                for p in range(pp):
                    rt, _, kt, _, _, bt, _ = ops[p]
                    d.append(ds[p] + pre[(2 * G + j) * pp + p] * rt[j:j + 1])
                    _stage(step_lhs, p, d[p] * bt[j:j + 1])
                    _stage(dv_lhs, j * pp + p, d[p] * kt[j:j + 1])
                dsas = _seg_sums(step_lhs, pp, seg)
                for p in range(pp):
                    rt, wt, kt, _, kkt, bt, _ = ops[p]
                    s = sall_ref[0, p, t8 * G + j]
                    sa, vcol, dycol = -pre[j * pp + p], pre[(G + j) * pp + p], pre[(2 * G + j) * pp + p]
                    if j < G - 1:
                        s2 = sall_ref[0, p, t8 * G + j + 1]
                    else:
                        s2 = s * wt[j:j + 1] + sa * bt[j:j + 1] + vcol * kt[j:j + 1]
                    vals = (colsum(s2 * dycol), colsum(d[p] * s), colsum(d[p] * vcol), -colsum(s * dsas[p]),
                            colsum(d[p] * sa))
                    tiles[p] = [jnp.where(rowid == j, val, tile) for val, tile in zip(vals, tiles[p])]
                    ds[p] = d[p] * wt[j:j + 1] - dsas[p] * kkt[j:j + 1]
            dvcols = _seg_sums(dv_lhs, G * pp, seg)
            for p in range(pp):
                dvt = jnp.zeros((SUBLANES, LANES), F32)
                for j in range(G):
                    dvt = jnp.where(rowid == j, colsum(diag * dvcols[j * pp + p]), dvt)
                dv_ref[0, rows, sls[p]] = dvt
                for ref, tile in zip((dr_ref, dw_ref, dk_ref, dkk_ref, db_ref), tiles[p]):
                    ref[0, rows, sls[p]] = tile
            return tuple(ds)

        final = lax.fori_loop(0, T // G, group, tuple(dstate[p] for p in range(pp)))
        for p in range(pp):
            dstate[p] = final[p]

    seq = pl.BlockSpec((1, T, pp * LANES), lambda bi, g, t: (bi, nt - 1 - t, g))
    sds = jax.ShapeDtypeStruct((B, S, C), F32)
    return pl.pallas_call(
        body, name='rwkv_scan_bwd', out_shape=(sds,) * 6,
        grid=(B, npairs // pp, nt),
        in_specs=[seq] * 6 + [pl.BlockSpec((1, pp, T, RN, LANES), lambda bi, g, t: (bi, g, nt - 1 - t, 0, 0)), seq],
        out_specs=(seq,) * 6,
        scratch_shapes=[pltpu.VMEM((pp, RN, LANES), F32), pltpu.VMEM((pp * LANES, LANES), BF16),
                        pltpu.VMEM((3 * G * pp * LANES, LANES), BF16), pltpu.VMEM((G * pp * LANES, LANES), BF16)],
        compiler_params=_params(('parallel', 'parallel', 'arbitrary')),
    )(r, w, k, v, kk, b, sall, dy)


@jax.custom_vjp
def rwkv_scan(r, w, k, v, kk, b):
    return _rwkv_fwd_call(r, w, k, v, kk, b)[0]


def _rwkv_scan_fwd(r, w, k, v, kk, b):
    y, sall = _rwkv_fwd_call(r, w, k, v, kk, b)
    return y, (r, w, k, v, kk, b, sall)


def _rwkv_scan_bwd(res, dy):
    return _rwkv_bwd_call(*res, dy)


rwkv_scan.defvjp(_rwkv_scan_fwd, _rwkv_scan_bwd)


def _loss_call(y, target):
    R, D = y.shape
    tr = _pick(R, max(8, (1 << 19) // D), 8)

    def body(y_ref, t_ref, dy_ref, part_ref):
        @pl.when(pl.program_id(0) == 0)
        def _():
            part_ref[...] = jnp.zeros_like(part_ref)

        diff = y_ref[...] - t_ref[...]
        dy_ref[...] = diff / D
        part_ref[...] += jnp.sum(jnp.mean(diff * diff, axis=-1, keepdims=True), axis=0, keepdims=True)

    row = pl.BlockSpec((tr, D), lambda i: (i, 0))
    dy, part = pl.pallas_call(
        body, name='loss_head',
        out_shape=(jax.ShapeDtypeStruct((R, D), F32), jax.ShapeDtypeStruct((1, 1), F32)),
        grid=(R // tr,), in_specs=[row, row], out_specs=(row, pl.BlockSpec((1, 1), lambda i: (0, 0))),
        compiler_params=_params(('arbitrary',)),
    )(y, target)
    return dy, part[0, 0]


def _sum_parts_call(parts):
    P, R, C = parts.shape
    tr = _pick(R, max(BF16_TILE_ROWS, (1 << 18) // C), BF16_TILE_ROWS)

    def body(p_ref, o_ref):
        acc = p_ref[0].astype(F32)
        for i in range(1, P):
            acc = acc + p_ref[i].astype(F32)
        o_ref[...] = acc

    return pl.pallas_call(
        body, name='sum_parts', out_shape=jax.ShapeDtypeStruct((R, C), F32), grid=(R // tr,),
        in_specs=[pl.BlockSpec((P, tr, C), lambda i: (0, i, 0))], out_specs=pl.BlockSpec((tr, C), lambda i: (i, 0)),
        compiler_params=_params(('parallel',)),
    )(parts)


def _adamw_call(w, g, m, v):
    R, C = w.shape
    tr = _pick(R, max(8, (1 << 18) // C), 8)

    def body(w_ref, g_ref, m_ref, v_ref, d_ref, nm_ref, nv_ref):
        g = g_ref[...]
        m = ADAM_B1 * m_ref[...] + (1.0 - ADAM_B1) * g
        v = ADAM_B2 * v_ref[...] + (1.0 - ADAM_B2) * (g * g)
        m_hat = m / (1.0 - ADAM_B1 ** ADAM_STEP)
        v_hat = v / (1.0 - ADAM_B2 ** ADAM_STEP)
        d_ref[...] = -ADAM_LR * (m_hat / (jnp.sqrt(v_hat) + ADAM_EPS) + ADAM_WD * w_ref[...])
        nm_ref[...] = m
        nv_ref[...] = v

    row = pl.BlockSpec((tr, C), lambda i: (i, 0))
    sds = jax.ShapeDtypeStruct((R, C), F32)
    return pl.pallas_call(
        body, name='adamw', out_shape=(sds, sds, sds), grid=(R // tr,),
        in_specs=[row] * 4, out_specs=(row,) * 3, compiler_params=_params(('parallel',)),
    )(w, g, m, v)


ANY = pl.BlockSpec(memory_space=pl.ANY)


def _place():
    return lax.axis_index('x'), lax.axis_index('y'), lax.axis_index('c')


COPY_SPLIT = 8
BF16_TILE_ROWS = 16
PACK_ROW_ALIGN = COPY_SPLIT * BF16_TILE_ROWS


def _row_split(rows):
    if rows % PACK_ROW_ALIGN == 0:
        return COPY_SPLIT, rows // COPY_SPLIT
    return 1, rows


def _all_gather_call(pack):
    _, R, C = pack.shape
    ns, rs = _row_split(R)

    def body(pk_ref, out_ref, send_sems, recv_sems):
        x, y, c = _place()
        chips = [(1 - x, y), (x, 1 - y), (1 - x, 1 - y)]
        me = 2 * x + y

        def copy(k, i, src, dst, to):
            rows = pl.ds(i * rs, rs)
            return pltpu.make_async_remote_copy(src_ref=src.at[rows], dst_ref=dst.at[rows], send_sem=send_sems.at[k * ns + i],
                                                recv_sem=recv_sems.at[k * ns + i], device_id=to, device_id_type=MESH)

        first = [copy(j, i, pk_ref.at[c], out_ref.at[me, c], (px, py, c))
                 for j, (px, py) in enumerate(chips) for i in range(ns)]
        for cp in first:
            cp.start()
        passed = []
        for i in range(ns):
            for j, (px, py) in enumerate(chips):
                landed = out_ref.at[2 * px + py, c]
                copy(j, i, landed, landed, (px, py, c)).wait_recv()
                fwd = copy(3 + j, i, landed, landed, (x, y, 1 - c))
                fwd.start()
                passed.append(fwd)
        for i in range(ns):
            for j, (px, py) in enumerate(chips):
                other = out_ref.at[2 * px + py, 1 - c]
                copy(3 + j, i, other, other, (x, y, 1 - c)).wait_recv()
        for cp in first + passed:
            cp.wait_send()

    others = pl.pallas_call(
        body, name='all_gather', out_shape=jax.ShapeDtypeStruct((4, 2, R, C), pack.dtype),
        in_specs=[ANY], out_specs=ANY,
        scratch_shapes=[pltpu.SemaphoreType.DMA((6 * ns,)), pltpu.SemaphoreType.DMA((6 * ns,))],
    )(pack)
    x, y, _ = _place()
    return lax.dynamic_update_slice(others, pack[None], (2 * x + y, 0, 0, 0))


def _scatter_call(src):
    _, _, R, C = src.shape
    ns, rs = _row_split(R)

    def body(src_ref, out_ref, send_sems, recv_sems):
        x, y, c = _place()
        me = 4 * x + 2 * y + c
        peers = []
        for rel in range(1, 8):
            px = 1 - x if rel & 4 else x
            py = 1 - y if rel & 2 else y
            pc = 1 - c if rel & 1 else c
            peers.append((rel - 1, px, py, pc))

        def copy(k, i, src, dst, to):
            rows = pl.ds(i * rs, rs)
            return pltpu.make_async_remote_copy(src_ref=src.at[rows], dst_ref=dst.at[rows], send_sem=send_sems.at[k * ns + i],
                                                recv_sem=recv_sems.at[k * ns + i], device_id=to, device_id_type=MESH)

        sends = [copy(k, i, src_ref.at[2 * px + py, pc], out_ref.at[me], (px, py, pc))
                 for i in range(ns) for k, px, py, pc in peers]
        for cp in sends:
            cp.start()
        for i in range(ns):
            for k, px, py, pc in peers:
                slot = out_ref.at[4 * px + 2 * py + pc]
                copy(k, i, slot, slot, (px, py, pc)).wait_recv()
        for cp in sends:
            cp.wait_send()

    others = pl.pallas_call(
        body, name='scatter_parts', out_shape=jax.ShapeDtypeStruct((8, R, C), src.dtype),
        in_specs=[ANY], out_specs=ANY,
        scratch_shapes=[pltpu.SemaphoreType.DMA((7 * ns,)), pltpu.SemaphoreType.DMA((7 * ns,))],
    )(src)
    x, y, c = _place()
    own = lax.dynamic_slice(src, (2 * x + y, c, 0, 0), (1, 1, R, C)).reshape(1, R, C)
    return lax.dynamic_update_slice(others, own, (4 * x + 2 * y + c, 0, 0))


def _sibling_exchange_call(half):
    R, C = half.shape
    ns, rs = _row_split(R)

    def body(h_ref, other_ref, send_sems, recv_sems):
        x, y, c = _place()

        def copy(i):
            rows = pl.ds(i * rs, rs)
            return pltpu.make_async_remote_copy(src_ref=h_ref.at[rows], dst_ref=other_ref.at[rows], send_sem=send_sems.at[i],
                                                recv_sem=recv_sems.at[i], device_id=(x, y, 1 - c), device_id_type=MESH)

        sends = [copy(i) for i in range(ns)]
        for cp in sends:
            cp.start()
        for cp in sends:
            cp.wait_recv()
        for cp in sends:
            cp.wait_send()

    return pl.pallas_call(
        body, name='sibling_exchange', out_shape=jax.ShapeDtypeStruct((R, C), half.dtype),
        in_specs=[ANY], out_specs=ANY,
        scratch_shapes=[pltpu.SemaphoreType.DMA((ns,)), pltpu.SemaphoreType.DMA((ns,))],
    )(half)


def rope_tables(seq_len, dim):
    inv = ROPE_THETA ** (-jnp.arange(0, dim, 2, dtype=F32) / dim)
    ang = jnp.arange(seq_len, dtype=F32)[:, None] * inv[None, :]
    return jnp.cos(ang), jnp.sin(ang)


def apply_rope(x, cos, sin):
    x1, x2 = jnp.split(x, 2, axis=-1)
    return jnp.concatenate([x1 * cos - x2 * sin, x1 * sin + x2 * cos], axis=-1)


def _pad_to(n):
    return -(-n // LANES) * LANES


def _pad_cols(w, widths):
    parts, at = [], 0
    for n in widths:
        parts.append(jnp.pad(w[..., at:at + n], [(0, 0)] * (w.ndim - 1) + [(0, _pad_to(n) - n)]))
        at += n
    return jnp.concatenate(parts, axis=-1)


def _split_padded(t, widths):
    out, at = [], 0
    for n in widths:
        out.append(t[..., at:at + n])
        at += _pad_to(n)
    return out


def _pad_rows(w, rows):
    return jnp.pad(w, ((0, rows - w.shape[0]), (0, 0)))


def _heads_attention(q, k, v, bias, scale):
    s = bmm_nt(q, k)
    p, _ = softmax_lse(s[:, None], bias[None], scale)
    return bmm_nn(p[:, 0], v)


def gla(q, k, v, r, gate_lr, w_gate2, b_gate, norm_g, norm_b):
    B, S, _ = q.shape
    H, dk, dv, C = GLA_HEADS, GLA_DK, GLA_DV, GLA_CHUNK
    nc = S // C
    log_a = jax.nn.log_sigmoid(mm(gate_lr, w_gate2) + b_gate) / GLA_TAU

    def chunks(t, d):
        return t.reshape(B, nc, C, H, d).transpose(0, 3, 1, 2, 4)

    qc = chunks(q, dk) * (dk ** -0.5)
    kc = chunks(k, dk)
    vc = chunks(v, dv)
    b = jnp.cumsum(chunks(log_a, dk), axis=3)
    b_last = b[:, :, :, -1:, :]
    q_dec = qc * jnp.exp(b)
    k_inv = kc * jnp.exp(-b)
    k_end = kc * jnp.exp(b_last - b)
    causal = jnp.tril(jnp.ones((C, C), dtype=bool))
    G = B * H
    att = bmm_nt(q_dec.reshape(G * nc, C, dk), k_inv.reshape(G * nc, C, dk))
    att = jnp.where(causal, att, 0.0)
    o_intra = bmm_nn(att, vc.reshape(G * nc, C, dv)).reshape(B, H, nc, C, dv)
    dec = jnp.exp(b_last[:, :, :, 0, :]).reshape(G, nc, dk, 1)
    o_inter = gla_scan(q_dec.reshape(G, nc, C, dk), k_end.reshape(G, nc, C, dk), vc.reshape(G, nc, C, dv), dec)
    o = o_intra + o_inter.reshape(B, H, nc, C, dv)
    o = o.transpose(0, 2, 3, 1, 4).reshape(B, S, H, dv)
    o = layer_norm(o, norm_g, norm_b).reshape(B, S, H * dv)
    return o * jax.nn.silu(r)


def even_mixer(x, p):
    B, S, _ = x.shape
    H = MLA_HEADS
    cos, sin = rope_tables(S, MLA_ROPE)
    z = mm(x, _pad_cols(p['ev_w_in'][0], EVEN_IN_WIDTHS))
    c_q, c_kv, k_pe, q_g, k_g, v_g, r_g, _ = _split_padded(z, EVEN_IN_WIDTHS)
    lr_at = sum(_pad_to(n) for n in EVEN_IN_WIDTHS[:-1])
    lr_g = z[..., lr_at:]
    q = mm(rms_norm(c_q, p['ev_mla_q_norm'][0]), p['ev_mla_w_uq'][0])
    q = q.reshape(B, S, H, MLA_NOPE + MLA_ROPE).transpose(0, 2, 1, 3)
    kv = mm(rms_norm(c_kv, p['ev_mla_kv_norm'][0]), p['ev_mla_w_ukv'][0])
    kv = kv.reshape(B, S, H, MLA_NOPE + MLA_V).transpose(0, 2, 1, 3)
    q_pe = apply_rope(q[..., MLA_NOPE:], cos, sin)
    k_pe = jnp.broadcast_to(apply_rope(k_pe[:, None], cos, sin), (B, H, S, MLA_ROPE))
    qf = jnp.concatenate([q[..., :MLA_NOPE], q_pe], axis=-1)
    kf = jnp.concatenate([kv[..., :MLA_NOPE], k_pe], axis=-1)
    pos = jnp.arange(S)
    bias = jnp.where(pos[None, :] <= pos[:, None], 0.0, NEG_BIG).astype(F32)
    a_out = _heads_attention(qf.reshape(B * H, S, -1), kf.reshape(B * H, S, -1),
                             kv[..., MLA_NOPE:].reshape(B * H, S, MLA_V), bias, (MLA_NOPE + MLA_ROPE) ** -0.5)
    a_out = a_out.reshape(B, H, S, MLA_V).transpose(0, 2, 1, 3).reshape(B, S, H * MLA_V)
    w_gate2 = _pad_rows(p['ev_gla_w_gate2'][0], lr_g.shape[-1])
    b_out = gla(q_g, k_g, v_g, r_g, lr_g, w_gate2, p['ev_gla_b_gate'][0], p['ev_gla_norm_g'][0], p['ev_gla_norm_b'][0])
    return mm(jnp.concatenate([a_out, b_out], axis=-1), p['ev_w_out'][0])


def dilated_branch(q, k, v, window, dil):
    B, H, S, dh = q.shape
    span = window // dil
    L = S // dil
    nb = -(-L // span)
    Lp = nb * span

    def residues(t):
        t = t.reshape(B, H, L, dil, dh).transpose(0, 1, 3, 2, 4)
        t = jnp.pad(t, ((0, 0), (0, 0), (0, 0), (0, Lp - L), (0, 0)))
        return t.reshape(B, H, dil, nb, span, dh)

    def with_prev(t):
        prev = jnp.pad(t, ((0, 0), (0, 0), (0, 0), (1, 0), (0, 0), (0, 0)))[:, :, :, :-1]
        return jnp.concatenate([prev, t], axis=4)

    qb = residues(q)
    kw, vw = with_prev(residues(k)), with_prev(residues(v))
    G = B * H * dil * nb
    s = bmm_nt(qb.reshape(G, span, dh), kw.reshape(G, 2 * span, dh))
    qi = jnp.arange(span)[:, None] + span
    kj = jnp.arange(2 * span)[None, :]
    dist = qi - kj
    in_band = (dist >= 0) & (dist <= span)
    has_prev = (jnp.arange(nb) > 0)[:, None, None] | (kj >= span)[None]
    valid = in_band[None] & has_prev
    bias = jnp.where(valid, 0.0, NEG_BIG).astype(F32)
    p, lse = softmax_lse(s.reshape(B * H * dil, nb, span, 2 * span), bias, dh ** -0.5)
    o = bmm_nn(p.reshape(G, span, 2 * span), vw.reshape(G, 2 * span, dh)).reshape(B, H, dil, nb, span, dh)
    lse = lse.reshape(B, H, dil, nb, span)

    def back(t):
        t = t.reshape((B, H, dil, Lp) + t.shape[5:])[:, :, :, :L]
        return jnp.moveaxis(t, 2, 3).reshape((B, H, S) + t.shape[4:])

    return back(o), back(lse)


def dilated_mixture(q, k, v):
    outs, lses = [], []
    for window, dil in DIL_BRANCHES:
        o, lse = dilated_branch(q, k, v, window, dil)
        outs.append(o)
        lses.append(lse)
    wts = jax.nn.softmax(jnp.stack(lses, axis=0), axis=0)
    return jnp.sum(wts[..., None] * jnp.stack(outs, axis=0), axis=0)


def token_shift(t, mu):
    prev = jnp.pad(t, ((0, 0), (1, 0), (0, 0)))[:, :-1]
    return t + (prev - t) * mu


def rwkv7(r, k, v, w_lr, a_lr, g_lr, w0, w_decay2, a0, w_a2, w_gate2, k_k, k_a, r_k, gn_g, gn_b):
    B, S, _ = r.shape
    H, n = RWKV_HEADS, RWKV_HEAD_DIM
    w = -jax.nn.softplus(-(w0 + mm(jnp.tanh(w_lr), w_decay2))) - 0.5
    decay = jnp.exp(-jnp.exp(w))
    a = jax.nn.sigmoid(a0 + mm(a_lr, w_a2))
    g = mm(jax.nn.sigmoid(g_lr), w_gate2)
    kk = (k * k_k).reshape(B, S, H, n)
    kk = kk / jnp.maximum(jnp.sqrt(jnp.sum(kk * kk, axis=-1, keepdims=True)), 1e-12)
    kk = kk.reshape(B, S, H * n)
    kh = k * (1.0 + (a - 1.0) * k_a)
    y = rwkv_scan(r, decay, kh, v, kk, kk * a).reshape(B, S, H, n)
    y = layer_norm(y, jnp.ones((n,), F32), jnp.zeros((n,), F32), RWKV_GN_EPS).reshape(B, S, H * n) * gn_g + gn_b
    bonus = jnp.sum((r * kh).reshape(B, S, H, n) * r_k, axis=-1, keepdims=True) * v.reshape(B, S, H, n)
    y = y + bonus.reshape(B, S, H * n)
    return y * g


def odd_mixer(x, p):
    B, S, _ = x.shape
    cos, sin = rope_tables(S, DIL_HEAD_DIM)
    widths = (3 * DIL_WIDTH,) + RWKV_IN_WIDTHS
    h = mm(x, _pad_cols(p['od_w_in'][0], widths))
    c_in = h[..., :3 * DIL_WIDTH]
    d_in = h[..., 3 * DIL_WIDTH:]
    q, k, v = [t.reshape(B, S, DIL_HEADS, DIL_HEAD_DIM).transpose(0, 2, 1, 3) for t in jnp.split(c_in, 3, axis=-1)]
    q, k = apply_rope(q, cos, sin), apply_rope(k, cos, sin)
    c_out = dilated_mixture(q, k, v).transpose(0, 2, 1, 3).reshape(B, S, DIL_WIDTH)
    mu = _pad_cols(p['od_rwkv_mu'][0], RWKV_IN_WIDTHS)
    sh = token_shift(d_in, mu)
    at = [0]
    for n in RWKV_IN_WIDTHS:
        at.append(at[-1] + _pad_to(n))
    r, kd, vd = [sh[..., at[i]:at[i + 1]] for i in range(3)]
    w_lr, a_lr, g_lr = [sh[..., at[i]:at[i + 1]] for i in range(3, 6)]
    d_out = rwkv7(r, kd, vd, w_lr, a_lr, g_lr, p['od_rwkv_w0'][0], _pad_rows(p['od_rwkv_w_decay2'][0], w_lr.shape[-1]),
                  p['od_rwkv_a0'][0], _pad_rows(p['od_rwkv_w_a2'][0], a_lr.shape[-1]), p['od_rwkv_w_gate2'][0],
                  p['od_rwkv_k_k'][0], p['od_rwkv_k_a'][0], p['od_rwkv_r_k'][0], p['od_rwkv_gn_g'][0], p['od_rwkv_gn_b'][0])
    return mm(jnp.concatenate([c_out, d_out], axis=-1), p['od_w_out'][0])


def cross_attention(x, mem, w_q, w_k, w_v, w_o):
    B, S, D = x.shape
    M = mem.shape[1]
    hd = D // XA_HEADS

    def heads(t, n):
        return t.reshape(B, n, XA_HEADS, hd).transpose(0, 2, 1, 3).reshape(B * XA_HEADS, n, hd)

    q, k, v = heads(mm(x, w_q), S), heads(mm(mem, w_k), M), heads(mm(mem, w_v), M)
    o = _heads_attention(q, k, v, jnp.zeros((S, M), F32), hd ** -0.5)
    o = o.reshape(B, XA_HEADS, S, hd).transpose(0, 2, 1, 3).reshape(B, S, D)
    return mm(o, w_o)


def swiglu(x, w_gate, w_up, w_down):
    return mm(jax.nn.silu(mm(x, w_gate)) * mm(x, w_up), w_down)


def forward(p, x, mem):
    h = x
    for layer in range(DEPTH):
        mix = even_mixer(h, p) if layer % 2 == 0 else odd_mixer(h, p)
        h = residual_layer_norm(h, mix, p['ln_mix_g'][layer], p['ln_mix_b'][layer], DEEPNORM_ALPHA)
        xa = cross_attention(h, mem, p['xa_w_q'][layer], p['xa_w_k'][layer], p['xa_w_v'][layer], p['xa_w_o'][layer])
        h = residual_layer_norm(h, xa, p['ln_xa_g'][layer], p['ln_xa_b'][layer], DEEPNORM_ALPHA)
        ff = swiglu(h, p['ffn_w_gate'][layer], p['ffn_w_up'][layer], p['ffn_w_down'][layer])
        h = residual_layer_norm(h, ff, p['ln_ffn_g'][layer], p['ln_ffn_b'][layer], DEEPNORM_ALPHA)
    return h


def _flat_pack(arrays, length, dtype):
    flat = jnp.concatenate([a.reshape(-1).astype(dtype) for a in arrays])
    return jnp.pad(flat, (0, length - flat.shape[0]))


def _unpack(flat, shapes):
    out, at = [], 0
    for shp in shapes:
        n = 1
        for d in shp:
            n *= d
        out.append(flat[at:at + n].reshape(shp))
        at += n
    return out


def _unpack_halves(halves, shapes):
    first, second = halves
    cut = first.shape[0]
    out, at = [], 0
    for shp in shapes:
        n = 1
        for d in shp:
            n *= d
        if at + n <= cut:
            flat = first[at:at + n]
        elif at >= cut:
            flat = second[at - cut:at - cut + n]
        else:
            flat = jnp.concatenate([first[at:], second[:at + n - cut]])
        out.append(flat.reshape(shp))
        at += n
    return out


def _shard_of(full, axis, s):
    n = full.shape[axis] // 4
    return lax.slice_in_dim(full, s * n, (s + 1) * n, axis=axis)


def kernel(x, mem, ev_w_in, ev_mla_q_norm, ev_mla_w_uq, ev_mla_kv_norm, ev_mla_w_ukv, ev_gla_w_gate2, ev_gla_b_gate, ev_gla_norm_g, ev_gla_norm_b, ev_w_out, od_w_in, od_rwkv_mu, od_rwkv_w0, od_rwkv_w_decay2, od_rwkv_a0, od_rwkv_w_a2, od_rwkv_w_gate2, od_rwkv_k_k, od_rwkv_k_a, od_rwkv_r_k, od_rwkv_gn_g, od_rwkv_gn_b, od_w_out, ln_mix_g, ln_mix_b, xa_w_q, xa_w_k, xa_w_v, xa_w_o, ln_xa_g, ln_xa_b, ffn_w_gate, ffn_w_up, ffn_w_down, ln_ffn_g, ln_ffn_b, loss_target, m_ev_w_in, m_ev_mla_q_norm, m_ev_mla_w_uq, m_ev_mla_kv_norm, m_ev_mla_w_ukv, m_ev_gla_w_gate2, m_ev_gla_b_gate, m_ev_gla_norm_g, m_ev_gla_norm_b, m_ev_w_out, m_od_w_in, m_od_rwkv_mu, m_od_rwkv_w0, m_od_rwkv_w_decay2, m_od_rwkv_a0, m_od_rwkv_w_a2, m_od_rwkv_w_gate2, m_od_rwkv_k_k, m_od_rwkv_k_a, m_od_rwkv_r_k, m_od_rwkv_gn_g, m_od_rwkv_gn_b, m_od_w_out, m_ln_mix_g, m_ln_mix_b, m_xa_w_q, m_xa_w_k, m_xa_w_v, m_xa_w_o, m_ln_xa_g, m_ln_xa_b, m_ffn_w_gate, m_ffn_w_up, m_ffn_w_down, m_ln_ffn_g, m_ln_ffn_b, v_ev_w_in, v_ev_mla_q_norm, v_ev_mla_w_uq, v_ev_mla_kv_norm, v_ev_mla_w_ukv, v_ev_gla_w_gate2, v_ev_gla_b_gate, v_ev_gla_norm_g, v_ev_gla_norm_b, v_ev_w_out, v_od_w_in, v_od_rwkv_mu, v_od_rwkv_w0, v_od_rwkv_w_decay2, v_od_rwkv_a0, v_od_rwkv_w_a2, v_od_rwkv_w_gate2, v_od_rwkv_k_k, v_od_rwkv_k_a, v_od_rwkv_r_k, v_od_rwkv_gn_g, v_od_rwkv_gn_b, v_od_w_out, v_ln_mix_g, v_ln_mix_b, v_xa_w_q, v_xa_w_k, v_xa_w_v, v_xa_w_o, v_ln_xa_g, v_ln_xa_b, v_ffn_w_gate, v_ffn_w_up, v_ffn_w_down, v_ln_ffn_g, v_ln_ffn_b):
    given = dict(locals())
    W = {n: given[n] for n in WEIGHT_NAMES}
    Mo = {n: given['m_' + n] for n in WEIGHT_NAMES}
    Vo = {n: given['v_' + n] for n in WEIGHT_NAMES}
    def count(shapes):
        total = 0
        for shp in shapes:
            n = 1
            for d in shp:
                n *= d
            total += n
        return total

    big_names = [n for n in WEIGHT_NAMES if n in MATRICES and count([given[n].shape]) >= NATIVE_MIN_ELEMENTS]
    mat_names = [n for n in WEIGHT_NAMES if n in MATRICES and n not in big_names]
    vec_names = [n for n in WEIGHT_NAMES if n in SHARDED_VECTORS]
    rep_names = list(REPLICATED)
    mat_shapes = [W[n].shape for n in mat_names]
    vec_shapes = [W[n].shape for n in vec_names]
    rep_shapes = [W[n].shape for n in rep_names]

    def halves_view(a):
        return a.reshape(2, count([a.shape[:-1]]) // 2, a.shape[-1])

    rc = -(-count(mat_shapes) // (2 * PACK_COLS * PACK_ROW_ALIGN)) * PACK_ROW_ALIGN
    mat_len = 2 * rc * PACK_COLS
    rv = -(-count(vec_shapes) // (2 * SMALL_COLS * 8)) * 8
    vec_len = 2 * rv * SMALL_COLS
    rr = -(-count(rep_shapes) // (SMALL_COLS * 8)) * 8
    rep_len = rr * SMALL_COLS

    wmat = _flat_pack([W[n] for n in mat_names], mat_len, BF16).reshape(2, rc, PACK_COLS)
    gathered = _all_gather_call(wmat)
    gvec = _all_gather_call(_flat_pack([W[n] for n in vec_names], vec_len, F32).reshape(2, rv, SMALL_COLS))
    full = {}
    mat_parts = [_unpack(gathered[s].reshape(-1), mat_shapes) for s in range(4)]
    for i, n in enumerate(mat_names):
        full[n] = jnp.concatenate([mat_parts[s][i] for s in range(4)], axis=MATRICES[n])
    for n in big_names:
        g4 = _all_gather_call(halves_view(W[n].astype(BF16))).reshape((4,) + W[n].shape)
        full[n] = jnp.concatenate([g4[s] for s in range(4)], axis=MATRICES[n])
    vec_parts =[_unpack(gvec[s].reshape(-1), vec_shapes) for s in range(4)]
    for i, n in enumerate(vec_names):
        full[n] = jnp.concatenate([vec_parts[s][i] for s in range(4)], axis=SHARDED_VECTORS[n])
    for n in rep_names:
        full[n] = W[n]

    B, S, D = x.shape
    y, vjp = jax.vjp(lambda p, xx: forward(p, xx, mem), full, x)
    dy, part = _loss_call(y.reshape(B * S, D), loss_target.reshape(B * S, D))
    loss = lax.psum(0.5 * part, ('x', 'y', 'c'))
    gfull, grad_x = vjp(dy.reshape(B, S, D))

    gmat = jnp.stack([_flat_pack([_shard_of(gfull[n], MATRICES[n], s) for n in mat_names], mat_len, BF16)
                      for s in range(4)]).reshape(4, 2, rc, PACK_COLS)
    half = _sum_parts_call(_scatter_call(gmat))
    other = _sibling_exchange_call(half)
    south = lax.axis_index('c') == 0
    ghalves = (jnp.where(south, half, other).reshape(-1), jnp.where(south, other, half).reshape(-1))
    grep = _flat_pack([gfull[n] for n in rep_names], rep_len, F32)
    gsmall = jnp.stack([jnp.concatenate([
        _flat_pack([_shard_of(gfull[n], SHARDED_VECTORS[n], s) for n in vec_names], vec_len, F32), grep])
        for s in range(4)]).reshape(4, 1, 2 * rv + rr, SMALL_COLS)
    gsmall = _sum_parts_call(_scatter_call(jnp.concatenate([gsmall, gsmall], axis=1)))

    def small_pack(src):
        return jnp.concatenate([_flat_pack([src[n] for n in vec_names], vec_len, F32),
                                _flat_pack([src[n] for n in rep_names], rep_len, F32)]).reshape(-1, SMALL_COLS)

    groups = [{}, {}, {}, {}]
    grads = dict(zip(mat_names, _unpack_halves(ghalves, mat_shapes)))
    for n in big_names:
        parts = jnp.stack([_shard_of(gfull[n], MATRICES[n], s) for s in range(4)])
        mine = _sum_parts_call(_scatter_call(parts.reshape((4,) + halves_view(W[n]).shape)))
        theirs = _sibling_exchange_call(mine)
        grads[n] = jnp.concatenate([jnp.where(south, mine, theirs), jnp.where(south, theirs, mine)]).reshape(W[n].shape)
    for n in big_names + mat_names:
        g = grads[n]
        rows = (-1, g.shape[-1])
        outs = _adamw_call(W[n].reshape(rows), g.reshape(rows), Mo[n].reshape(rows), Vo[n].reshape(rows))
        for grp, val in zip(groups, (g,) + outs):
            grp[n] = val.reshape(g.shape)
    small = (gsmall,) + _adamw_call(small_pack(W), gsmall, small_pack(Mo), small_pack(Vo))
    for grp, sm in zip(groups, small):
        sm = sm.reshape(-1)
        grp.update(zip(vec_names, _unpack(sm[:vec_len], vec_shapes)))
        grp.update(zip(rep_names, _unpack(sm[vec_len:], rep_shapes)))
    return (loss, grad_x, *[grp[n] for grp in groups for n in WEIGHT_NAMES])
```

```python
import functools

import jax
import jax.numpy as jnp
from jax import lax
from jax.experimental import pallas as pl
from jax.experimental.pallas import tpu as pltpu

F32 = jnp.float32
BF16 = jnp.bfloat16
MESH = pl.DeviceIdType.MESH

ROPE_THETA = 10000.0
LN_EPS = 1e-5
RMS_EPS = 1e-6
DEPTH = 2
DEEPNORM_ALPHA = (2.0 * DEPTH) ** 0.25
MLA_HEADS, MLA_NOPE, MLA_ROPE, MLA_V, MLA_Q_RANK, MLA_KV_RANK = 8, 128, 64, 128, 512, 256
GLA_HEADS, GLA_DK, GLA_DV, GLA_GATE_RANK, GLA_TAU, GLA_CHUNK = 4, 128, 256, 16, 16.0, 64
DIL_HEADS, DIL_HEAD_DIM = 8, 128
DIL_BRANCHES = ((128, 1), (512, 4), (2048, 16))
RWKV_HEADS, RWKV_HEAD_DIM = 16, 64
RWKV_DECAY_RANK, RWKV_A_RANK, RWKV_GATE_RANK = 96, 96, 256
RWKV_GN_EPS = 64e-5
XA_HEADS = 4
DIL_WIDTH = DIL_HEADS * DIL_HEAD_DIM
RWKV_WIDTH = RWKV_HEADS * RWKV_HEAD_DIM
EVEN_IN_WIDTHS = (MLA_Q_RANK, MLA_KV_RANK, MLA_ROPE, GLA_HEADS * GLA_DK, GLA_HEADS * GLA_DK,
                  GLA_HEADS * GLA_DV, GLA_HEADS * GLA_DV, GLA_GATE_RANK)
RWKV_IN_WIDTHS = (RWKV_WIDTH, RWKV_WIDTH, RWKV_WIDTH, RWKV_DECAY_RANK, RWKV_A_RANK, RWKV_GATE_RANK)
ADAM_LR, ADAM_B1, ADAM_B2, ADAM_EPS, ADAM_WD, ADAM_STEP = 0.001, 0.9, 0.999, 1e-08, 0.01, 10

LANES = 128
SUBLANES = 8
VMEM_LIMIT_BYTES = 48 * 1024 * 1024
NEG_BIG = -1e30

PACK_COLS = 1024
NATIVE_MIN_ELEMENTS = 1 << 20
SMALL_COLS = 128

MATRICES = {
    'ev_w_in': 2, 'ev_mla_w_uq': 2, 'ev_mla_w_ukv': 2, 'ev_gla_w_gate2': 2, 'ev_w_out': 1, 'od_w_in': 2,
    'od_rwkv_w_decay2': 2, 'od_rwkv_w_a2': 2, 'od_rwkv_w_gate2': 2, 'od_w_out': 1,
    'xa_w_q': 1, 'xa_w_k': 1, 'xa_w_v': 1, 'xa_w_o': 1, 'ffn_w_gate': 2, 'ffn_w_up': 2, 'ffn_w_down': 1,
}
SHARDED_VECTORS = {
    'od_rwkv_mu': 1, 'od_rwkv_w0': 1, 'od_rwkv_a0': 1, 'od_rwkv_k_k': 1, 'od_rwkv_k_a': 1,
    'od_rwkv_gn_g': 1, 'od_rwkv_gn_b': 1,
}
REPLICATED = ('ev_mla_q_norm', 'ev_mla_kv_norm', 'ev_gla_b_gate', 'ev_gla_norm_g', 'ev_gla_norm_b', 'od_rwkv_r_k',
              'ln_mix_g', 'ln_mix_b', 'ln_xa_g', 'ln_xa_b', 'ln_ffn_g', 'ln_ffn_b')
WEIGHT_NAMES = ('ev_w_in', 'ev_mla_q_norm', 'ev_mla_w_uq', 'ev_mla_kv_norm', 'ev_mla_w_ukv', 'ev_gla_w_gate2',
                'ev_gla_b_gate', 'ev_gla_norm_g', 'ev_gla_norm_b', 'ev_w_out', 'od_w_in', 'od_rwkv_mu', 'od_rwkv_w0',
                'od_rwkv_w_decay2', 'od_rwkv_a0', 'od_rwkv_w_a2', 'od_rwkv_w_gate2', 'od_rwkv_k_k', 'od_rwkv_k_a',
                'od_rwkv_r_k', 'od_rwkv_gn_g', 'od_rwkv_gn_b', 'od_w_out', 'ln_mix_g', 'ln_mix_b', 'xa_w_q', 'xa_w_k',
                'xa_w_v', 'xa_w_o', 'ln_xa_g', 'ln_xa_b', 'ffn_w_gate', 'ffn_w_up', 'ffn_w_down', 'ln_ffn_g', 'ln_ffn_b')


def _pick(n, cap, mult):
    d = (min(cap, n) // mult) * mult
    while d >= mult:
        if n % d == 0:
            return d
        d -= mult
    return n


def _params(semantics):
    return pltpu.CompilerParams(dimension_semantics=semantics, vmem_limit_bytes=VMEM_LIMIT_BYTES)


_DIMS = {(False, False): (((1,), (0,)), ((), ())), (False, True): (((1,), (1,)), ((), ())),
         (True, False): (((0,), (0,)), ((), ()))}


def _bmm(a, b, ta, tb, out_dtype=F32):
    G = a.shape[0]
    K, M = (a.shape[1], a.shape[2]) if ta else (a.shape[2], a.shape[1])
    N = b.shape[1] if tb else b.shape[2]
    assert (b.shape[2] if tb else b.shape[1]) == K and b.shape[0] == G
    tm, tn = _pick(M, 1024, LANES), _pick(N, 1024, LANES)
    tk = K if K <= 2048 else _pick(K, 2048, LANES)
    nk = K // tk
    gb = 1
    if tm == M and tn == N and nk == 1:
        per = 4 * (M * K + K * N + M * N)
        gb = _pick(G, max(1, min(8, (2 << 20) // per)), 1)
    dims = _DIMS[(ta, tb)]

    def body(a_ref, b_ref, o_ref, *scratch):
        def prod(i):
            return lax.dot_general(a_ref[i].astype(BF16), b_ref[i].astype(BF16), dims, preferred_element_type=F32)

        if nk == 1:
            for i in range(gb):
                o_ref[i] = prod(i).astype(o_ref.dtype)
        else:
            acc_ref, = scratch
            k = pl.program_id(3)

            @pl.when(k == 0)
            def _():
                acc_ref[...] = jnp.zeros_like(acc_ref)

            for i in range(gb):
                acc_ref[i] += prod(i)

            @pl.when(k == nk - 1)
            def _():
                o_ref[...] = acc_ref[...].astype(o_ref.dtype)

    a_spec = (pl.BlockSpec((gb, tk, tm), lambda g, i, j, k: (g, k, i)) if ta
              else pl.BlockSpec((gb, tm, tk), lambda g, i, j, k: (g, i, k)))
    b_spec = (pl.BlockSpec((gb, tn, tk), lambda g, i, j, k: (g, j, k)) if tb
              else pl.BlockSpec((gb, tk, tn), lambda g, i, j, k: (g, k, j)))
    return pl.pallas_call(
        body, name='bmm_' + ('t' if ta else 'n') + ('t' if tb else 'n'),
        out_shape=jax.ShapeDtypeStruct((G, M, N), out_dtype),
        grid=(G // gb, M // tm, N // tn, nk),
        in_specs=[a_spec, b_spec],
        out_specs=pl.BlockSpec((gb, tm, tn), lambda g, i, j, k: (g, i, j)),
        scratch_shapes=[] if nk == 1 else [pltpu.VMEM((gb, tm, tn), F32)],
        compiler_params=_params(('parallel', 'parallel', 'parallel', 'arbitrary')),
    )(a, b)


def _like(x):
    return jnp.zeros((), x.dtype)


@jax.custom_vjp
def bmm_nn(a, b):
    return _bmm(a, b, False, False)


def _bmm_nn_fwd(a, b):
    ab, bb = a.astype(BF16), b.astype(BF16)
    return _bmm(ab, bb, False, False), (ab, bb, _like(a), _like(b))


def _bmm_nn_bwd(res, g):
    a, b, la, lb = res
    g = g.astype(BF16)
    return _bmm(g, b, False, True, la.dtype), _bmm(a, g, True, False, lb.dtype)


bmm_nn.defvjp(_bmm_nn_fwd, _bmm_nn_bwd)


@jax.custom_vjp
def bmm_nt(a, b):
    return _bmm(a, b, False, True)


def _bmm_nt_fwd(a, b):
    ab, bb = a.astype(BF16), b.astype(BF16)
    return _bmm(ab, bb, False, True), (ab, bb, _like(a), _like(b))


def _bmm_nt_bwd(res, g):
    a, b, la, lb = res
    g = g.astype(BF16)
    return _bmm(g, b, False, False, la.dtype), _bmm(g, a, True, False, lb.dtype)


bmm_nt.defvjp(_bmm_nt_fwd, _bmm_nt_bwd)


def mm(x, w):
    lead = x.shape[:-1]
    out = bmm_nn(x.reshape(1, -1, x.shape[-1]), w[None])
    return out.reshape(lead + (w.shape[1],))


def _norm_stats(x, center, eps):
    if center:
        xc = x - jnp.mean(x, axis=-1, keepdims=True)
    else:
        xc = x
    rstd = lax.rsqrt(jnp.mean(xc * xc, axis=-1, keepdims=True) + eps)
    return xc * rstd, rstd


def _norm_fwd_call(x, g, b, center, eps):
    R, C = x.shape
    tr = _pick(R, max(8, (1 << 19) // C), 8)

    def body(x_ref, g_ref, b_ref, y_ref):
        xhat, _ = _norm_stats(x_ref[...], center, eps)
        y_ref[...] = xhat * g_ref[...] + b_ref[...]

    row = pl.BlockSpec((tr, C), lambda i: (i, 0))
    vec = pl.BlockSpec((1, C), lambda i: (0, 0))
    return pl.pallas_call(
        body, name='norm_fwd', out_shape=jax.ShapeDtypeStruct((R, C), F32), grid=(R // tr,),
        in_specs=[row, vec, vec], out_specs=row, compiler_params=_params(('parallel',)),
    )(x, g, b)


def _norm_bwd_call(x, g, dy, center, eps):
    R, C = x.shape
    tr = _pick(R, max(8, (1 << 19) // C), 8)

    def body(x_ref, g_ref, dy_ref, dx_ref, dg_ref, db_ref):
        @pl.when(pl.program_id(0) == 0)
        def _():
            dg_ref[...] = jnp.zeros_like(dg_ref)
            db_ref[...] = jnp.zeros_like(db_ref)

        xhat, rstd = _norm_stats(x_ref[...], center, eps)
        dy = dy_ref[...]
        dxh = dy * g_ref[...]
        proj = xhat * jnp.mean(dxh * xhat, axis=-1, keepdims=True)
        if center:
            dx_ref[...] = rstd * (dxh - jnp.mean(dxh, axis=-1, keepdims=True) - proj)
        else:
            dx_ref[...] = rstd * (dxh - proj)
        dg_ref[...] += jnp.sum(dy * xhat, axis=0, keepdims=True)
        db_ref[...] += jnp.sum(dy, axis=0, keepdims=True)

    row = pl.BlockSpec((tr, C), lambda i: (i, 0))
    vec = pl.BlockSpec((1, C), lambda i: (0, 0))
    return pl.pallas_call(
        body, name='norm_bwd',
        out_shape=(jax.ShapeDtypeStruct((R, C), F32), jax.ShapeDtypeStruct((1, C), F32), jax.ShapeDtypeStruct((1, C), F32)),
        grid=(R // tr,), in_specs=[row, vec, row], out_specs=(row, vec, vec), compiler_params=_params(('arbitrary',)),
    )(x, g, dy)


@functools.partial(jax.custom_vjp, nondiff_argnums=(3, 4))
def _norm2d(x, g, b, center, eps):
    return _norm_fwd_call(x, g, b, center, eps)


def _norm2d_fwd(x, g, b, center, eps):
    return _norm_fwd_call(x, g, b, center, eps), (x, g)


def _norm2d_bwd(center, eps, res, dy):
    x, g = res
    return _norm_bwd_call(x, g, dy, center, eps)


_norm2d.defvjp(_norm2d_fwd, _norm2d_bwd)


def _resnorm_fwd_call(h, s, g, b, alpha, eps):
    R, C = h.shape
    tr = _pick(R, max(8, (1 << 19) // C), 8)

    def body(h_ref, s_ref, g_ref, b_ref, y_ref):
        xhat, _ = _norm_stats(alpha * h_ref[...] + s_ref[...], True, eps)
        y_ref[...] = xhat * g_ref[...] + b_ref[...]

    row = pl.BlockSpec((tr, C), lambda i: (i, 0))
    vec = pl.BlockSpec((1, C), lambda i: (0, 0))
    return pl.pallas_call(
        body, name='resnorm_fwd', out_shape=jax.ShapeDtypeStruct((R, C), F32), grid=(R // tr,),
        in_specs=[row, row, vec, vec], out_specs=row, compiler_params=_params(('parallel',)),
    )(h, s, g, b)


def _resnorm_bwd_call(h, s, g, dy, alpha, eps):
    R, C = h.shape
    tr = _pick(R, max(8, (1 << 19) // C), 8)

    def body(h_ref, s_ref, g_ref, dy_ref, dx_ref, dg_ref, db_ref):
        @pl.when(pl.program_id(0) == 0)
        def _():
            dg_ref[...] = jnp.zeros_like(dg_ref)
            db_ref[...] = jnp.zeros_like(db_ref)

        xhat, rstd = _norm_stats(alpha * h_ref[...] + s_ref[...], True, eps)
        dy = dy_ref[...]
        dxh = dy * g_ref[...]
        proj = xhat * jnp.mean(dxh * xhat, axis=-1, keepdims=True)
        dx_ref[...] = rstd * (dxh - jnp.mean(dxh, axis=-1, keepdims=True) - proj)
        dg_ref[...] += jnp.sum(dy * xhat, axis=0, keepdims=True)
        db_ref[...] += jnp.sum(dy, axis=0, keepdims=True)

    row = pl.BlockSpec((tr, C), lambda i: (i, 0))
    vec = pl.BlockSpec((1, C), lambda i: (0, 0))
    return pl.pallas_call(
        body, name='resnorm_bwd',
        out_shape=(jax.ShapeDtypeStruct((R, C), F32), jax.ShapeDtypeStruct((1, C), F32), jax.ShapeDtypeStruct((1, C), F32)),
        grid=(R // tr,), in_specs=[row, row, vec, row], out_specs=(row, vec, vec), compiler_params=_params(('arbitrary',)),
    )(h, s, g, dy)


@functools.partial(jax.custom_vjp, nondiff_argnums=(4, 5))
def _resnorm2d(h, s, g, b, alpha, eps):
    return _resnorm_fwd_call(h, s, g, b, alpha, eps)


def _resnorm2d_fwd(h, s, g, b, alpha, eps):
    return _resnorm_fwd_call(h, s, g, b, alpha, eps), (h, s, g)


def _resnorm2d_bwd(alpha, eps, res, dy):
    h, s, g = res
    dx, dg, db = _resnorm_bwd_call(h, s, g, dy, alpha, eps)
    return alpha * dx, dx, dg, db


_resnorm2d.defvjp(_resnorm2d_fwd, _resnorm2d_bwd)


def residual_layer_norm(h, s, g, b, alpha):
    C = h.shape[-1]
    return _resnorm2d(h.reshape(-1, C), s.reshape(-1, C), g.reshape(1, C), b.reshape(1, C), alpha, LN_EPS).reshape(h.shape)


def layer_norm(x, g, b, eps=LN_EPS):
    C = x.shape[-1]
    return _norm2d(x.reshape(-1, C), g.reshape(1, C), b.reshape(1, C), True, eps).reshape(x.shape)


def rms_norm(x, g):
    C = x.shape[-1]
    return _norm2d(x.reshape(-1, C), g.reshape(1, C), jnp.zeros((1, C), F32), False, RMS_EPS).reshape(x.shape)


def _softmax_fwd_call(s, bias, scale):
    G1, G2, R, C = s.shape
    tr = _pick(R, max(8, (1 << 19) // C), 8)

    def body(s_ref, bias_ref, p_ref, lse_ref):
        z = s_ref[0, 0] * scale + bias_ref[0]
        m = jnp.max(z, axis=-1, keepdims=True)
        e = jnp.exp(z - m)
        den = jnp.sum(e, axis=-1, keepdims=True)
        p_ref[0, 0] = (e / den).astype(BF16)
        lse_ref[0, 0] = m + jnp.log(den)

    blk = pl.BlockSpec((1, 1, tr, C), lambda a, b, r: (a, b, r, 0))
    col = pl.BlockSpec((1, 1, tr, 1), lambda a, b, r: (a, b, r, 0))
    return pl.pallas_call(
        body, name='softmax_fwd',
        out_shape=(jax.ShapeDtypeStruct(s.shape, BF16), jax.ShapeDtypeStruct((G1, G2, R, 1), F32)),
        grid=(G1, G2, R // tr), in_specs=[blk, pl.BlockSpec((1, tr, C), lambda a, b, r: (b, r, 0))],
        out_specs=(blk, col), compiler_params=_params(('parallel', 'parallel', 'parallel')),
    )(s, bias)


def _softmax_bwd_call(p, dp, dlse, scale):
    G1, G2, R, C = p.shape
    tr = _pick(R, max(8, (1 << 19) // C), 8)

    def body(p_ref, dp_ref, dlse_ref, ds_ref):
        p = p_ref[0, 0].astype(F32)
        dp = dp_ref[0, 0].astype(F32)
        inner = jnp.sum(dp * p, axis=-1, keepdims=True)
        ds_ref[0, 0] = (p * (dp - inner + dlse_ref[0, 0])) * scale

    blk = pl.BlockSpec((1, 1, tr, C), lambda a, b, r: (a, b, r, 0))
    col = pl.BlockSpec((1, 1, tr, 1), lambda a, b, r: (a, b, r, 0))
    return pl.pallas_call(
        body, name='softmax_bwd', out_shape=jax.ShapeDtypeStruct(p.shape, F32),
        grid=(G1, G2, R // tr), in_specs=[blk, blk, col], out_specs=blk,
        compiler_params=_params(('parallel', 'parallel', 'parallel')),
    )(p, dp, dlse)


@functools.partial(jax.custom_vjp, nondiff_argnums=(2,))
def softmax_lse(s, bias, scale):
    return _softmax_fwd_call(s, bias, scale)


def _softmax_lse_fwd(s, bias, scale):
    p, lse = _softmax_fwd_call(s, bias, scale)
    return (p, lse), (p, bias)


def _softmax_lse_bwd(scale, res, cts):
    p, bias = res
    dp, dlse = cts
    return _softmax_bwd_call(p, dp, dlse, scale), jnp.zeros_like(bias)


softmax_lse.defvjp(_softmax_lse_fwd, _softmax_lse_bwd)


def _dot(a, b, dims):
    return lax.dot_general(a.astype(BF16), b.astype(BF16), dims, preferred_element_type=F32)


_NN, _NT, _TN = _DIMS[(False, False)], _DIMS[(False, True)], _DIMS[(True, False)]


def _gla_fwd_call(q, k, v, dec):
    G, nc, C, dk = q.shape
    dv = v.shape[-1]

    def body(q_ref, k_ref, v_ref, dec_ref, o_ref, st_ref, state):
        @pl.when(pl.program_id(1) == 0)
        def _():
            state[...] = jnp.zeros_like(state)

        s = state[...]
        st_ref[0, 0] = s
        o_ref[0, 0] = _dot(q_ref[0, 0], s, _NN)
        state[...] = s * dec_ref[0, 0] + _dot(k_ref[0, 0], v_ref[0, 0], _TN)

    def spec(r, c):
        return pl.BlockSpec((1, 1, r, c), lambda g, t: (g, t, 0, 0))

    return pl.pallas_call(
        body, name='gla_scan_fwd',
        out_shape=(jax.ShapeDtypeStruct((G, nc, C, dv), F32), jax.ShapeDtypeStruct((G, nc, dk, dv), F32)),
        grid=(G, nc), in_specs=[spec(C, dk), spec(C, dk), spec(C, dv), spec(dk, 1)],
        out_specs=(spec(C, dv), spec(dk, dv)), scratch_shapes=[pltpu.VMEM((dk, dv), F32)],
        compiler_params=_params(('parallel', 'arbitrary')),
    )(q, k, v, dec)


def _gla_bwd_call(q, k, v, dec, states, do):
    G, nc, C, dk = q.shape
    dv = v.shape[-1]

    def body(q_ref, k_ref, v_ref, dec_ref, st_ref, do_ref, dq_ref, dk_ref, dv_ref, ddec_ref, dstate):
        @pl.when(pl.program_id(1) == 0)
        def _():
            dstate[...] = jnp.zeros_like(dstate)

        s = st_ref[0, 0]
        d = dstate[...]
        do = do_ref[0, 0]
        dq_ref[0, 0] = _dot(do, s, _NT)
        dk_ref[0, 0] = _dot(v_ref[0, 0], d, _NT)
        dv_ref[0, 0] = _dot(k_ref[0, 0], d, _NN)
        ddec_ref[0, 0] = jnp.sum(s * d, axis=1, keepdims=True)
        dstate[...] = d * dec_ref[0, 0] + _dot(q_ref[0, 0], do, _TN)

    def spec(r, c):
        return pl.BlockSpec((1, 1, r, c), lambda g, t: (g, nc - 1 - t, 0, 0))

    return pl.pallas_call(
        body, name='gla_scan_bwd',
        out_shape=(jax.ShapeDtypeStruct(q.shape, F32), jax.ShapeDtypeStruct(k.shape, F32),
                   jax.ShapeDtypeStruct(v.shape, F32), jax.ShapeDtypeStruct(dec.shape, F32)),
        grid=(G, nc), in_specs=[spec(C, dk), spec(C, dk), spec(C, dv), spec(dk, 1), spec(dk, dv), spec(C, dv)],
        out_specs=(spec(C, dk), spec(C, dk), spec(C, dv), spec(dk, 1)), scratch_shapes=[pltpu.VMEM((dk, dv), F32)],
        compiler_params=_params(('parallel', 'arbitrary')),
    )(q, k, v, dec, states, do)


@jax.custom_vjp
def gla_scan(q, k, v, dec):
    return _gla_fwd_call(q, k, v, dec)[0]


def _gla_scan_fwd(q, k, v, dec):
    o, states = _gla_fwd_call(q, k, v, dec)
    return o, (q, k, v, dec, states)


def _gla_scan_bwd(res, do):
    return _gla_bwd_call(*res, do)


gla_scan.defvjp(_gla_scan_fwd, _gla_scan_bwd)


RWKV_PAIRS_PER_STEP = 8
RWKV_TIME_BLOCK = 32
RN = RWKV_HEAD_DIM


def _rwkv_consts():
    row = lax.broadcasted_iota(jnp.int32, (RN, LANES), 0)
    lane = lax.broadcasted_iota(jnp.int32, (RN, LANES), 1)
    diag = (lane % RN == row).astype(F32)
    r2 = lax.broadcasted_iota(jnp.int32, (LANES, LANES), 0)
    l2 = lax.broadcasted_iota(jnp.int32, (LANES, LANES), 1)
    seg = (r2 // RN == l2 // RN).astype(BF16)
    return diag, seg


def _stage(lhs_ref, slot, p):
    hi = p.astype(BF16)
    lhs_ref[pl.ds(slot * LANES, RN), :] = hi
    lhs_ref[pl.ds(slot * LANES + RN, RN), :] = (p - hi.astype(F32)).astype(BF16)


def _seg_sums(lhs_ref, nslots, seg):
    res = jnp.dot(lhs_ref[pl.ds(0, nslots * LANES), :], seg, preferred_element_type=F32)
    return [res[i * LANES:i * LANES + RN] + res[i * LANES + RN:(i + 1) * LANES] for i in range(nslots)]


def _rwkv_blocks(B, S, C):
    npairs = C // LANES
    pp = RWKV_PAIRS_PER_STEP if npairs % RWKV_PAIRS_PER_STEP == 0 else 1
    T = _pick(S, RWKV_TIME_BLOCK, 8)
    return npairs, pp, T


def _rwkv_fwd_call(r, w, k, v, kk, b):
    B, S, C = r.shape
    npairs, pp, T = _rwkv_blocks(B, S, C)
    G = SUBLANES

    def body(r_ref, w_ref, k_ref, v_ref, kk_ref, b_ref, y_ref, sall_ref, state, step_lhs, v_lhs, y_lhs):
        @pl.when(pl.program_id(2) == 0)
        def _():
            state[...] = jnp.zeros_like(state)

        diag, seg = _rwkv_consts()
        rowid = lax.broadcasted_iota(jnp.int32, (SUBLANES, LANES), 0)

        def group(t8, carry):
            rows = pl.ds(pl.multiple_of(t8 * G, G), G)
            sls = [slice(p * LANES, (p + 1) * LANES) for p in range(pp)]
            ops = [[ref[0, rows, sl] for ref in (r_ref, w_ref, k_ref, v_ref, kk_ref, b_ref)] for sl in sls]
            for j in range(G):
                for p in range(pp):
                    _stage(v_lhs, j * pp + p, diag * ops[p][3][j:j + 1])
            vcols = _seg_sums(v_lhs, G * pp, seg)
            s = list(carry)
            for j in range(G):
                for p in range(pp):
                    sall_ref[0, p, t8 * G + j] = s[p]
                    _stage(step_lhs, p, s[p] * ops[p][4][j:j + 1])
                sas = _seg_sums(step_lhs, pp, seg)
                for p in range(pp):
                    rt, wt, kt, _, _, bt = ops[p]
                    s[p] = s[p] * wt[j:j + 1] - sas[p] * bt[j:j + 1] + vcols[j * pp + p] * kt[j:j + 1]
                    _stage(y_lhs, j * pp + p, s[p] * rt[j:j + 1])
            ycols = _seg_sums(y_lhs, G * pp, seg)
            for p in range(pp):
                ytile = jnp.zeros((SUBLANES, LANES), F32)
                for j in range(G):
                    ytile = jnp.where(rowid == j, jnp.sum(diag * ycols[j * pp + p], axis=0, keepdims=True), ytile)
                y_ref[0, rows, sls[p]] = ytile
            return tuple(s)

        final = lax.fori_loop(0, T // G, group, tuple(state[p] for p in range(pp)))
        for p in range(pp):
            state[p] = final[p]

    seq = pl.BlockSpec((1, T, pp * LANES), lambda bi, g, t: (bi, t, g))
    return pl.pallas_call(
        body, name='rwkv_scan_fwd',
        out_shape=(jax.ShapeDtypeStruct((B, S, C), F32), jax.ShapeDtypeStruct((B, npairs, S, RN, LANES), F32)),
        grid=(B, npairs // pp, S // T), in_specs=[seq] * 6,
        out_specs=(seq, pl.BlockSpec((1, pp, T, RN, LANES), lambda bi, g, t: (bi, g, t, 0, 0))),
        scratch_shapes=[pltpu.VMEM((pp, RN, LANES), F32), pltpu.VMEM((pp * LANES, LANES), BF16),
                        pltpu.VMEM((G * pp * LANES, LANES), BF16), pltpu.VMEM((G * pp * LANES, LANES), BF16)],
        compiler_params=_params(('parallel', 'parallel', 'arbitrary')),
    )(r, w, k, v, kk, b)


def _rwkv_bwd_call(r, w, k, v, kk, b, sall, dy):
    B, S, C = r.shape
    npairs, pp, T = _rwkv_blocks(B, S, C)
    nt = S // T
    G = SUBLANES

    def body(r_ref, w_ref, k_ref, v_ref, kk_ref, b_ref, sall_ref, dy_ref,
             dr_ref, dw_ref, dk_ref, dv_ref, dkk_ref, db_ref, dstate, step_lhs, pre_lhs, dv_lhs):
        @pl.when(pl.program_id(2) == 0)
        def _():
            dstate[...] = jnp.zeros_like(dstate)

        diag, seg = _rwkv_consts()
        rowid = lax.broadcasted_iota(jnp.int32, (SUBLANES, LANES), 0)

        def colsum(z):
            return jnp.sum(z, axis=0, keepdims=True)

        def group(i, carry):
            t8 = T // G - 1 - i
            rows = pl.ds(pl.multiple_of(t8 * G, G), G)
            sls = [slice(p * LANES, (p + 1) * LANES) for p in range(pp)]
            ops = [[ref[0, rows, sl] for ref in (r_ref, w_ref, k_ref, v_ref, kk_ref, b_ref, dy_ref)] for sl in sls]
            for j in range(G):
                for p in range(pp):
                    _stage(pre_lhs, j * pp + p, sall_ref[0, p, t8 * G + j] * ops[p][4][j:j + 1])
                    _stage(pre_lhs, (G + j) * pp + p, diag * ops[p][3][j:j + 1])
                    _stage(pre_lhs, (2 * G + j) * pp + p, diag * ops[p][6][j:j + 1])
            pre = _seg_sums(pre_lhs, 3 * G * pp, seg)
            ds = list(carry)
            tiles = [[jnp.zeros((SUBLANES, LANES), F32) for _ in range(5)] for _ in range(pp)]
            for j in reversed(range(G)):
                d = []
                for p in range(pp):
                    rt, _, kt, _, _, bt, _ = ops[p]
                    d.append(ds[p] + pre[(2 * G + j) * pp + p] * rt[j:j + 1])
                    _stage(step_lhs, p, d[p] * bt[j:j + 1])
                    _stage(dv_lhs, j * pp + p, d[p] * kt[j:j + 1])
                dsas = _seg_sums(step_lhs, pp, seg)
                for p in range(pp):
                    rt, wt, kt, _, kkt, bt, _ = ops[p]
                    s = sall_ref[0, p, t8 * G + j]
                    sa, vcol, dycol = -pre[j * pp + p], pre[(G + j) * pp + p], pre[(2 * G + j) * pp + p]
                    if j < G - 1:
                        s2 = sall_ref[0, p, t8 * G + j + 1]
                    else:
                        s2 = s * wt[j:j + 1] + sa * bt[j:j + 1] + vcol * kt[j:j + 1]
                    vals = (colsum(s2 * dycol), colsum(d[p] * s), colsum(d[p] * vcol), -colsum(s * dsas[p]),
                            colsum(d[p] * sa))
                    tiles[p] = [jnp.where(rowid == j, val, tile) for val, tile in zip(vals, tiles[p])]
                    ds[p] = d[p] * wt[j:j + 1] - dsas[p] * kkt[j:j + 1]
            dvcols = _seg_sums(dv_lhs, G * pp, seg)
            for p in range(pp):
                dvt = jnp.zeros((SUBLANES, LANES), F32)
                for j in range(G):
                    dvt = jnp.where(rowid == j, colsum(diag * dvcols[j * pp + p]), dvt)
                dv_ref[0, rows, sls[p]] = dvt
                for ref, tile in zip((dr_ref, dw_ref, dk_ref, dkk_ref, db_ref), tiles[p]):
                    ref[0, rows, sls[p]] = tile
            return tuple(ds)

        final = lax.fori_loop(0, T // G, group, tuple(dstate[p] for p in range(pp)))
        for p in range(pp):
            dstate[p] = final[p]

    seq = pl.BlockSpec((1, T, pp * LANES), lambda bi, g, t: (bi, nt - 1 - t, g))
    sds = jax.ShapeDtypeStruct((B, S, C), F32)
    return pl.pallas_call(
        body, name='rwkv_scan_bwd', out_shape=(sds,) * 6,
        grid=(B, npairs // pp, nt),
        in_specs=[seq] * 6 + [pl.BlockSpec((1, pp, T, RN, LANES), lambda bi, g, t: (bi, g, nt - 1 - t, 0, 0)), seq],
        out_specs=(seq,) * 6,
        scratch_shapes=[pltpu.VMEM((pp, RN, LANES), F32), pltpu.VMEM((pp * LANES, LANES), BF16),
                        pltpu.VMEM((3 * G * pp * LANES, LANES), BF16), pltpu.VMEM((G * pp * LANES, LANES), BF16)],
        compiler_params=_params(('parallel', 'parallel', 'arbitrary')),
    )(r, w, k, v, kk, b, sall, dy)


@jax.custom_vjp
def rwkv_scan(r, w, k, v, kk, b):
    return _rwkv_fwd_call(r, w, k, v, kk, b)[0]


def _rwkv_scan_fwd(r, w, k, v, kk, b):
    y, sall = _rwkv_fwd_call(r, w, k, v, kk, b)
    return y, (r, w, k, v, kk, b, sall)


def _rwkv_scan_bwd(res, dy):
    return _rwkv_bwd_call(*res, dy)


rwkv_scan.defvjp(_rwkv_scan_fwd, _rwkv_scan_bwd)


def _loss_call(y, target):
    R, D = y.shape
    tr = _pick(R, max(8, (1 << 19) // D), 8)

    def body(y_ref, t_ref, dy_ref, part_ref):
        @pl.when(pl.program_id(0) == 0)
        def _():
            part_ref[...] = jnp.zeros_like(part_ref)

        diff = y_ref[...] - t_ref[...]
        dy_ref[...] = diff / D
        part_ref[...] += jnp.sum(jnp.mean(diff * diff, axis=-1, keepdims=True), axis=0, keepdims=True)

    row = pl.BlockSpec((tr, D), lambda i: (i, 0))
    dy, part = pl.pallas_call(
        body, name='loss_head',
        out_shape=(jax.ShapeDtypeStruct((R, D), F32), jax.ShapeDtypeStruct((1, 1), F32)),
        grid=(R // tr,), in_specs=[row, row], out_specs=(row, pl.BlockSpec((1, 1), lambda i: (0, 0))),
        compiler_params=_params(('arbitrary',)),
    )(y, target)
    return dy, part[0, 0]


def _sum_parts_call(parts):
    P, R, C = parts.shape
    tr = _pick(R, max(BF16_TILE_ROWS, (1 << 18) // C), BF16_TILE_ROWS)

    def body(p_ref, o_ref):
        acc = p_ref[0].astype(F32)
        for i in range(1, P):
            acc = acc + p_ref[i].astype(F32)
        o_ref[...] = acc

    return pl.pallas_call(
        body, name='sum_parts', out_shape=jax.ShapeDtypeStruct((R, C), F32), grid=(R // tr,),
        in_specs=[pl.BlockSpec((P, tr, C), lambda i: (0, i, 0))], out_specs=pl.BlockSpec((tr, C), lambda i: (i, 0)),
        compiler_params=_params(('parallel',)),
    )(parts)


def _adamw_call(w, g, m, v):
    R, C = w.shape
    tr = _pick(R, max(8, (1 << 18) // C), 8)

    def body(w_ref, g_ref, m_ref, v_ref, d_ref, nm_ref, nv_ref):
        g = g_ref[...]
        m = ADAM_B1 * m_ref[...] + (1.0 - ADAM_B1) * g
        v = ADAM_B2 * v_ref[...] + (1.0 - ADAM_B2) * (g * g)
        m_hat = m / (1.0 - ADAM_B1 ** ADAM_STEP)
        v_hat = v / (1.0 - ADAM_B2 ** ADAM_STEP)
        d_ref[...] = -ADAM_LR * (m_hat / (jnp.sqrt(v_hat) + ADAM_EPS) + ADAM_WD * w_ref[...])
        nm_ref[...] = m
        nv_ref[...] = v

    row = pl.BlockSpec((tr, C), lambda i: (i, 0))
    sds = jax.ShapeDtypeStruct((R, C), F32)
    return pl.pallas_call(
        body, name='adamw', out_shape=(sds, sds, sds), grid=(R // tr,),
        in_specs=[row] * 4, out_specs=(row,) * 3, compiler_params=_params(('parallel',)),
    )(w, g, m, v)


ANY = pl.BlockSpec(memory_space=pl.ANY)


def _place():
    return lax.axis_index('x'), lax.axis_index('y'), lax.axis_index('c')


COPY_SPLIT = 8
BF16_TILE_ROWS = 16
PACK_ROW_ALIGN = COPY_SPLIT * BF16_TILE_ROWS


def _row_split(rows):
    if rows % PACK_ROW_ALIGN == 0:
        return COPY_SPLIT, rows // COPY_SPLIT
    return 1, rows


def _all_gather_call(pack):
    _, R, C = pack.shape
    ns, rs = _row_split(R)

    def body(pk_ref, out_ref, send_sems, recv_sems):
        x, y, c = _place()
        chips = [(1 - x, y), (x, 1 - y), (1 - x, 1 - y)]
        me = 2 * x + y

        def copy(k, i, src, dst, to):
            rows = pl.ds(i * rs, rs)
            return pltpu.make_async_remote_copy(src_ref=src.at[rows], dst_ref=dst.at[rows], send_sem=send_sems.at[k * ns + i],
                                                recv_sem=recv_sems.at[k * ns + i], device_id=to, device_id_type=MESH)

        first = [copy(j, i, pk_ref.at[c], out_ref.at[me, c], (px, py, c))
                 for j, (px, py) in enumerate(chips) for i in range(ns)]
        for cp in first:
            cp.start()
        passed = []
        for i in range(ns):
            for j, (px, py) in enumerate(chips):
                landed = out_ref.at[2 * px + py, c]
                copy(j, i, landed, landed, (px, py, c)).wait_recv()
                fwd = copy(3 + j, i, landed, landed, (x, y, 1 - c))
                fwd.start()
                passed.append(fwd)
        for i in range(ns):
            for j, (px, py) in enumerate(chips):
                other = out_ref.at[2 * px + py, 1 - c]
                copy(3 + j, i, other, other, (x, y, 1 - c)).wait_recv()
        for cp in first + passed:
            cp.wait_send()

    others = pl.pallas_call(
        body, name='all_gather', out_shape=jax.ShapeDtypeStruct((4, 2, R, C), pack.dtype),
        in_specs=[ANY], out_specs=ANY,
        scratch_shapes=[pltpu.SemaphoreType.DMA((6 * ns,)), pltpu.SemaphoreType.DMA((6 * ns,))],
    )(pack)
    x, y, _ = _place()
    return lax.dynamic_update_slice(others, pack[None], (2 * x + y, 0, 0, 0))


def _scatter_call(src):
    _, _, R, C = src.shape
    ns, rs = _row_split(R)

    def body(src_ref, out_ref, send_sems, recv_sems):
        x, y, c = _place()
        me = 4 * x + 2 * y + c
        peers = []
        for rel in range(1, 8):
            px = 1 - x if rel & 4 else x
            py = 1 - y if rel & 2 else y
            pc = 1 - c if rel & 1 else c
            peers.append((rel - 1, px, py, pc))

        def copy(k, i, src, dst, to):
            rows = pl.ds(i * rs, rs)
            return pltpu.make_async_remote_copy(src_ref=src.at[rows], dst_ref=dst.at[rows], send_sem=send_sems.at[k * ns + i],
                                                recv_sem=recv_sems.at[k * ns + i], device_id=to, device_id_type=MESH)

        sends = [copy(k, i, src_ref.at[2 * px + py, pc], out_ref.at[me], (px, py, pc))
                 for i in range(ns) for k, px, py, pc in peers]
        for cp in sends:
            cp.start()
        for i in range(ns):
            for k, px, py, pc in peers:
                slot = out_ref.at[4 * px + 2 * py + pc]
                copy(k, i, slot, slot, (px, py, pc)).wait_recv()
        for cp in sends:
            cp.wait_send()

    others = pl.pallas_call(
        body, name='scatter_parts', out_shape=jax.ShapeDtypeStruct((8, R, C), src.dtype),
        in_specs=[ANY], out_specs=ANY,
        scratch_shapes=[pltpu.SemaphoreType.DMA((7 * ns,)), pltpu.SemaphoreType.DMA((7 * ns,))],
    )(src)
    x, y, c = _place()
    own = lax.dynamic_slice(src, (2 * x + y, c, 0, 0), (1, 1, R, C)).reshape(1, R, C)
    return lax.dynamic_update_slice(others, own, (4 * x + 2 * y + c, 0, 0))


def _sibling_exchange_call(half):
    R, C = half.shape
    ns, rs = _row_split(R)

    def body(h_ref, other_ref, send_sems, recv_sems):
        x, y, c = _place()

        def copy(i):
            rows = pl.ds(i * rs, rs)
            return pltpu.make_async_remote_copy(src_ref=h_ref.at[rows], dst_ref=other_ref.at[rows], send_sem=send_sems.at[i],
                                                recv_sem=recv_sems.at[i], device_id=(x, y, 1 - c), device_id_type=MESH)

        sends = [copy(i) for i in range(ns)]
        for cp in sends:
            cp.start()
        for cp in sends:
            cp.wait_recv()
        for cp in sends:
            cp.wait_send()

    return pl.pallas_call(
        body, name='sibling_exchange', out_shape=jax.ShapeDtypeStruct((R, C), half.dtype),
        in_specs=[ANY], out_specs=ANY,
        scratch_shapes=[pltpu.SemaphoreType.DMA((ns,)), pltpu.SemaphoreType.DMA((ns,))],
    )(half)


def rope_tables(seq_len, dim):
    inv = ROPE_THETA ** (-jnp.arange(0, dim, 2, dtype=F32) / dim)
    ang = jnp.arange(seq_len, dtype=F32)[:, None] * inv[None, :]
    return jnp.cos(ang), jnp.sin(ang)


def apply_rope(x, cos, sin):
    x1, x2 = jnp.split(x, 2, axis=-1)
    return jnp.concatenate([x1 * cos - x2 * sin, x1 * sin + x2 * cos], axis=-1)


def _pad_to(n):
    return -(-n // LANES) * LANES


def _pad_cols(w, widths):
    parts, at = [], 0
    for n in widths:
        parts.append(jnp.pad(w[..., at:at + n], [(0, 0)] * (w.ndim - 1) + [(0, _pad_to(n) - n)]))
        at += n
    return jnp.concatenate(parts, axis=-1)


def _split_padded(t, widths):
    out, at = [], 0
    for n in widths:
        out.append(t[..., at:at + n])
        at += _pad_to(n)
    return out


def _pad_rows(w, rows):
    return jnp.pad(w, ((0, rows - w.shape[0]), (0, 0)))


def _heads_attention(q, k, v, bias, scale):
    s = bmm_nt(q, k)
    p, _ = softmax_lse(s[:, None], bias[None], scale)
    return bmm_nn(p[:, 0], v)


def gla(q, k, v, r, gate_lr, w_gate2, b_gate, norm_g, norm_b):
    B, S, _ = q.shape
    H, dk, dv, C = GLA_HEADS, GLA_DK, GLA_DV, GLA_CHUNK
    nc = S // C
    log_a = jax.nn.log_sigmoid(mm(gate_lr, w_gate2) + b_gate) / GLA_TAU

    def chunks(t, d):
        return t.reshape(B, nc, C, H, d).transpose(0, 3, 1, 2, 4)

    qc = chunks(q, dk) * (dk ** -0.5)
    kc = chunks(k, dk)
    vc = chunks(v, dv)
    b = jnp.cumsum(chunks(log_a, dk), axis=3)
    b_last = b[:, :, :, -1:, :]
    q_dec = qc * jnp.exp(b)
    k_inv = kc * jnp.exp(-b)
    k_end = kc * jnp.exp(b_last - b)
    causal = jnp.tril(jnp.ones((C, C), dtype=bool))
    G = B * H
    att = bmm_nt(q_dec.reshape(G * nc, C, dk), k_inv.reshape(G * nc, C, dk))
    att = jnp.where(causal, att, 0.0)
    o_intra = bmm_nn(att, vc.reshape(G * nc, C, dv)).reshape(B, H, nc, C, dv)
    dec = jnp.exp(b_last[:, :, :, 0, :]).reshape(G, nc, dk, 1)
    o_inter = gla_scan(q_dec.reshape(G, nc, C, dk), k_end.reshape(G, nc, C, dk), vc.reshape(G, nc, C, dv), dec)
    o = o_intra + o_inter.reshape(B, H, nc, C, dv)
    o = o.transpose(0, 2, 3, 1, 4).reshape(B, S, H, dv)
    o = layer_norm(o, norm_g, norm_b).reshape(B, S, H * dv)
    return o * jax.nn.silu(r)


def even_mixer(x, p):
    B, S, _ = x.shape
    H = MLA_HEADS
    cos, sin = rope_tables(S, MLA_ROPE)
    z = mm(x, _pad_cols(p['ev_w_in'][0], EVEN_IN_WIDTHS))
    c_q, c_kv, k_pe, q_g, k_g, v_g, r_g, _ = _split_padded(z, EVEN_IN_WIDTHS)
    lr_at = sum(_pad_to(n) for n in EVEN_IN_WIDTHS[:-1])
    lr_g = z[..., lr_at:]
    q = mm(rms_norm(c_q, p['ev_mla_q_norm'][0]), p['ev_mla_w_uq'][0])
    q = q.reshape(B, S, H, MLA_NOPE + MLA_ROPE).transpose(0, 2, 1, 3)
    kv = mm(rms_norm(c_kv, p['ev_mla_kv_norm'][0]), p['ev_mla_w_ukv'][0])
    kv = kv.reshape(B, S, H, MLA_NOPE + MLA_V).transpose(0, 2, 1, 3)
    q_pe = apply_rope(q[..., MLA_NOPE:], cos, sin)
    k_pe = jnp.broadcast_to(apply_rope(k_pe[:, None], cos, sin), (B, H, S, MLA_ROPE))
    qf = jnp.concatenate([q[..., :MLA_NOPE], q_pe], axis=-1)
    kf = jnp.concatenate([kv[..., :MLA_NOPE], k_pe], axis=-1)
    pos = jnp.arange(S)
    bias = jnp.where(pos[None, :] <= pos[:, None], 0.0, NEG_BIG).astype(F32)
    a_out = _heads_attention(qf.reshape(B * H, S, -1), kf.reshape(B * H, S, -1),
                             kv[..., MLA_NOPE:].reshape(B * H, S, MLA_V), bias, (MLA_NOPE + MLA_ROPE) ** -0.5)
    a_out = a_out.reshape(B, H, S, MLA_V).transpose(0, 2, 1, 3).reshape(B, S, H * MLA_V)
    w_gate2 = _pad_rows(p['ev_gla_w_gate2'][0], lr_g.shape[-1])
    b_out = gla(q_g, k_g, v_g, r_g, lr_g, w_gate2, p['ev_gla_b_gate'][0], p['ev_gla_norm_g'][0], p['ev_gla_norm_b'][0])
    return mm(jnp.concatenate([a_out, b_out], axis=-1), p['ev_w_out'][0])


def dilated_branch(q, k, v, window, dil):
    B, H, S, dh = q.shape
    span = window // dil
    L = S // dil
    nb = -(-L // span)
    Lp = nb * span

    def residues(t):
        t = t.reshape(B, H, L, dil, dh).transpose(0, 1, 3, 2, 4)
        t = jnp.pad(t, ((0, 0), (0, 0), (0, 0), (0, Lp - L), (0, 0)))
        return t.reshape(B, H, dil, nb, span, dh)

    def with_prev(t):
        prev = jnp.pad(t, ((0, 0), (0, 0), (0, 0), (1, 0), (0, 0), (0, 0)))[:, :, :, :-1]
        return jnp.concatenate([prev, t], axis=4)

    qb = residues(q)
    kw, vw = with_prev(residues(k)), with_prev(residues(v))
    G = B * H * dil * nb
    s = bmm_nt(qb.reshape(G, span, dh), kw.reshape(G, 2 * span, dh))
    qi = jnp.arange(span)[:, None] + span
    kj = jnp.arange(2 * span)[None, :]
    dist = qi - kj
    in_band = (dist >= 0) & (dist <= span)
    has_prev = (jnp.arange(nb) > 0)[:, None, None] | (kj >= span)[None]
    valid = in_band[None] & has_prev
    bias = jnp.where(valid, 0.0, NEG_BIG).astype(F32)
    p, lse = softmax_lse(s.reshape(B * H * dil, nb, span, 2 * span), bias, dh ** -0.5)
    o = bmm_nn(p.reshape(G, span, 2 * span), vw.reshape(G, 2 * span, dh)).reshape(B, H, dil, nb, span, dh)
    lse = lse.reshape(B, H, dil, nb, span)

    def back(t):
        t = t.reshape((B, H, dil, Lp) + t.shape[5:])[:, :, :, :L]
        return jnp.moveaxis(t, 2, 3).reshape((B, H, S) + t.shape[4:])

    return back(o), back(lse)


def dilated_mixture(q, k, v):
    outs, lses = [], []
    for window, dil in DIL_BRANCHES:
        o, lse = dilated_branch(q, k, v, window, dil)
        outs.append(o)
        lses.append(lse)
    wts = jax.nn.softmax(jnp.stack(lses, axis=0), axis=0)
    return jnp.sum(wts[..., None] * jnp.stack(outs, axis=0), axis=0)


def token_shift(t, mu):
    prev = jnp.pad(t, ((0, 0), (1, 0), (0, 0)))[:, :-1]
    return t + (prev - t) * mu


def rwkv7(r, k, v, w_lr, a_lr, g_lr, w0, w_decay2, a0, w_a2, w_gate2, k_k, k_a, r_k, gn_g, gn_b):
    B, S, _ = r.shape
    H, n = RWKV_HEADS, RWKV_HEAD_DIM
    w = -jax.nn.softplus(-(w0 + mm(jnp.tanh(w_lr), w_decay2))) - 0.5
    decay = jnp.exp(-jnp.exp(w))
    a = jax.nn.sigmoid(a0 + mm(a_lr, w_a2))
    g = mm(jax.nn.sigmoid(g_lr), w_gate2)
    kk = (k * k_k).reshape(B, S, H, n)
    kk = kk / jnp.maximum(jnp.sqrt(jnp.sum(kk * kk, axis=-1, keepdims=True)), 1e-12)
    kk = kk.reshape(B, S, H * n)
    kh = k * (1.0 + (a - 1.0) * k_a)
    y = rwkv_scan(r, decay, kh, v, kk, kk * a).reshape(B, S, H, n)
    y = layer_norm(y, jnp.ones((n,), F32), jnp.zeros((n,), F32), RWKV_GN_EPS).reshape(B, S, H * n) * gn_g + gn_b
    bonus = jnp.sum((r * kh).reshape(B, S, H, n) * r_k, axis=-1, keepdims=True) * v.reshape(B, S, H, n)
    y = y + bonus.reshape(B, S, H * n)
    return y * g


def odd_mixer(x, p):
    B, S, _ = x.shape
    cos, sin = rope_tables(S, DIL_HEAD_DIM)
    widths = (3 * DIL_WIDTH,) + RWKV_IN_WIDTHS
    h = mm(x, _pad_cols(p['od_w_in'][0], widths))
    c_in = h[..., :3 * DIL_WIDTH]
    d_in = h[..., 3 * DIL_WIDTH:]
    q, k, v = [t.reshape(B, S, DIL_HEADS, DIL_HEAD_DIM).transpose(0, 2, 1, 3) for t in jnp.split(c_in, 3, axis=-1)]
    q, k = apply_rope(q, cos, sin), apply_rope(k, cos, sin)
    c_out = dilated_mixture(q, k, v).transpose(0, 2, 1, 3).reshape(B, S, DIL_WIDTH)
    mu = _pad_cols(p['od_rwkv_mu'][0], RWKV_IN_WIDTHS)
    sh = token_shift(d_in, mu)
    at = [0]
    for n in RWKV_IN_WIDTHS:
        at.append(at[-1] + _pad_to(n))
    r, kd, vd = [sh[..., at[i]:at[i + 1]] for i in range(3)]
    w_lr, a_lr, g_lr = [sh[..., at[i]:at[i + 1]] for i in range(3, 6)]
    d_out = rwkv7(r, kd, vd, w_lr, a_lr, g_lr, p['od_rwkv_w0'][0], _pad_rows(p['od_rwkv_w_decay2'][0], w_lr.shape[-1]),
                  p['od_rwkv_a0'][0], _pad_rows(p['od_rwkv_w_a2'][0], a_lr.shape[-1]), p['od_rwkv_w_gate2'][0],
                  p['od_rwkv_k_k'][0], p['od_rwkv_k_a'][0], p['od_rwkv_r_k'][0], p['od_rwkv_gn_g'][0], p['od_rwkv_gn_b'][0])
    return mm(jnp.concatenate([c_out, d_out], axis=-1), p['od_w_out'][0])


def cross_attention(x, mem, w_q, w_k, w_v, w_o):
    B, S, D = x.shape
    M = mem.shape[1]
    hd = D // XA_HEADS

    def heads(t, n):
        return t.reshape(B, n, XA_HEADS, hd).transpose(0, 2, 1, 3).reshape(B * XA_HEADS, n, hd)

    q, k, v = heads(mm(x, w_q), S), heads(mm(mem, w_k), M), heads(mm(mem, w_v), M)
    o = _heads_attention(q, k, v, jnp.zeros((S, M), F32), hd ** -0.5)
    o = o.reshape(B, XA_HEADS, S, hd).transpose(0, 2, 1, 3).reshape(B, S, D)
    return mm(o, w_o)


def swiglu(x, w_gate, w_up, w_down):
    return mm(jax.nn.silu(mm(x, w_gate)) * mm(x, w_up), w_down)


def forward(p, x, mem):
    h = x
    for layer in range(DEPTH):
        mix = even_mixer(h, p) if layer % 2 == 0 else odd_mixer(h, p)
        h = residual_layer_norm(h, mix, p['ln_mix_g'][layer], p['ln_mix_b'][layer], DEEPNORM_ALPHA)
        xa = cross_attention(h, mem, p['xa_w_q'][layer], p['xa_w_k'][layer], p['xa_w_v'][layer], p['xa_w_o'][layer])
        h = residual_layer_norm(h, xa, p['ln_xa_g'][layer], p['ln_xa_b'][layer], DEEPNORM_ALPHA)
        ff = swiglu(h, p['ffn_w_gate'][layer], p['ffn_w_up'][layer], p['ffn_w_down'][layer])
        h = residual_layer_norm(h, ff, p['ln_ffn_g'][layer], p['ln_ffn_b'][layer], DEEPNORM_ALPHA)
    return h


def _flat_pack(arrays, length, dtype):
    flat = jnp.concatenate([a.reshape(-1).astype(dtype) for a in arrays])
    return jnp.pad(flat, (0, length - flat.shape[0]))


def _unpack(flat, shapes):
    out, at = [], 0
    for shp in shapes:
        n = 1
        for d in shp:
            n *= d
        out.append(flat[at:at + n].reshape(shp))
        at += n
    return out


def _unpack_halves(halves, shapes):
    first, second = halves
    cut = first.shape[0]
    out, at = [], 0
    for shp in shapes:
        n = 1
        for d in shp:
            n *= d
        if at + n <= cut:
            flat = first[at:at + n]
        elif at >= cut:
            flat = second[at - cut:at - cut + n]
        else:
            flat = jnp.concatenate([first[at:], second[:at + n - cut]])
        out.append(flat.reshape(shp))
        at += n
    return out


def _shard_of(full, axis, s):
    n = full.shape[axis] // 4
    return lax.slice_in_dim(full, s * n, (s + 1) * n, axis=axis)


def kernel(x, mem, ev_w_in, ev_mla_q_norm, ev_mla_w_uq, ev_mla_kv_norm, ev_mla_w_ukv, ev_gla_w_gate2, ev_gla_b_gate, ev_gla_norm_g, ev_gla_norm_b, ev_w_out, od_w_in, od_rwkv_mu, od_rwkv_w0, od_rwkv_w_decay2, od_rwkv_a0, od_rwkv_w_a2, od_rwkv_w_gate2, od_rwkv_k_k, od_rwkv_k_a, od_rwkv_r_k, od_rwkv_gn_g, od_rwkv_gn_b, od_w_out, ln_mix_g, ln_mix_b, xa_w_q, xa_w_k, xa_w_v, xa_w_o, ln_xa_g, ln_xa_b, ffn_w_gate, ffn_w_up, ffn_w_down, ln_ffn_g, ln_ffn_b, loss_target, m_ev_w_in, m_ev_mla_q_norm, m_ev_mla_w_uq, m_ev_mla_kv_norm, m_ev_mla_w_ukv, m_ev_gla_w_gate2, m_ev_gla_b_gate, m_ev_gla_norm_g, m_ev_gla_norm_b, m_ev_w_out, m_od_w_in, m_od_rwkv_mu, m_od_rwkv_w0, m_od_rwkv_w_decay2, m_od_rwkv_a0, m_od_rwkv_w_a2, m_od_rwkv_w_gate2, m_od_rwkv_k_k, m_od_rwkv_k_a, m_od_rwkv_r_k, m_od_rwkv_gn_g, m_od_rwkv_gn_b, m_od_w_out, m_ln_mix_g, m_ln_mix_b, m_xa_w_q, m_xa_w_k, m_xa_w_v, m_xa_w_o, m_ln_xa_g, m_ln_xa_b, m_ffn_w_gate, m_ffn_w_up, m_ffn_w_down, m_ln_ffn_g, m_ln_ffn_b, v_ev_w_in, v_ev_mla_q_norm, v_ev_mla_w_uq, v_ev_mla_kv_norm, v_ev_mla_w_ukv, v_ev_gla_w_gate2, v_ev_gla_b_gate, v_ev_gla_norm_g, v_ev_gla_norm_b, v_ev_w_out, v_od_w_in, v_od_rwkv_mu, v_od_rwkv_w0, v_od_rwkv_w_decay2, v_od_rwkv_a0, v_od_rwkv_w_a2, v_od_rwkv_w_gate2, v_od_rwkv_k_k, v_od_rwkv_k_a, v_od_rwkv_r_k, v_od_rwkv_gn_g, v_od_rwkv_gn_b, v_od_w_out, v_ln_mix_g, v_ln_mix_b, v_xa_w_q, v_xa_w_k, v_xa_w_v, v_xa_w_o, v_ln_xa_g, v_ln_xa_b, v_ffn_w_gate, v_ffn_w_up, v_ffn_w_down, v_ln_ffn_g, v_ln_ffn_b):
    given = dict(locals())
    W = {n: given[n] for n in WEIGHT_NAMES}
    Mo = {n: given['m_' + n] for n in WEIGHT_NAMES}
    Vo = {n: given['v_' + n] for n in WEIGHT_NAMES}
    def count(shapes):
        total = 0
        for shp in shapes:
            n = 1
            for d in shp:
                n *= d
            total += n
        return total

    big_names = [n for n in WEIGHT_NAMES if n in MATRICES and count([given[n].shape]) >= NATIVE_MIN_ELEMENTS]
    mat_names = [n for n in WEIGHT_NAMES if n in MATRICES and n not in big_names]
    vec_names = [n for n in WEIGHT_NAMES if n in SHARDED_VECTORS]
    rep_names = list(REPLICATED)
    mat_shapes = [W[n].shape for n in mat_names]
    vec_shapes = [W[n].shape for n in vec_names]
    rep_shapes = [W[n].shape for n in rep_names]

    def halves_view(a):
        return a.reshape(2, count([a.shape[:-1]]) // 2, a.shape[-1])

    rc = -(-count(mat_shapes) // (2 * PACK_COLS * PACK_ROW_ALIGN)) * PACK_ROW_ALIGN
    mat_len = 2 * rc * PACK_COLS
    rv = -(-count(vec_shapes) // (2 * SMALL_COLS * 8)) * 8
    vec_len = 2 * rv * SMALL_COLS
    rr = -(-count(rep_shapes) // (SMALL_COLS * 8)) * 8
    rep_len = rr * SMALL_COLS

    wmat = _flat_pack([W[n] for n in mat_names], mat_len, BF16).reshape(2, rc, PACK_COLS)
    gathered = _all_gather_call(wmat)
    gvec = _all_gather_call(_flat_pack([W[n] for n in vec_names], vec_len, F32).reshape(2, rv, SMALL_COLS))
    full = {}
    mat_parts = [_unpack(gathered[s].reshape(-1), mat_shapes) for s in range(4)]
    for i, n in enumerate(mat_names):
        full[n] = jnp.concatenate([mat_parts[s][i] for s in range(4)], axis=MATRICES[n])
    for n in big_names:
        g4 = _all_gather_call(halves_view(W[n].astype(BF16))).reshape((4,) + W[n].shape)
        full[n] = jnp.concatenate([g4[s] for s in range(4)], axis=MATRICES[n])
    vec_parts =[_unpack(gvec[s].reshape(-1), vec_shapes) for s in range(4)]
    for i, n in enumerate(vec_names):
        full[n] = jnp.concatenate([vec_parts[s][i] for s in range(4)], axis=SHARDED_VECTORS[n])
    for n in rep_names:
        full[n] = W[n]

    B, S, D = x.shape
    y, vjp = jax.vjp(lambda p, xx: forward(p, xx, mem), full, x)
    dy, part = _loss_call(y.reshape(B * S, D), loss_target.reshape(B * S, D))
    loss = lax.psum(0.5 * part, ('x', 'y', 'c'))
    gfull, grad_x = vjp(dy.reshape(B, S, D))

    gmat = jnp.stack([_flat_pack([_shard_of(gfull[n], MATRICES[n], s) for n in mat_names], mat_len, BF16)
                      for s in range(4)]).reshape(4, 2, rc, PACK_COLS)
    half = _sum_parts_call(_scatter_call(gmat))
    other = _sibling_exchange_call(half)
    south = lax.axis_index('c') == 0
    ghalves = (jnp.where(south, half, other).reshape(-1), jnp.where(south, other, half).reshape(-1))
    grep = _flat_pack([gfull[n] for n in rep_names], rep_len, F32)
    gsmall = jnp.stack([jnp.concatenate([
        _flat_pack([_shard_of(gfull[n], SHARDED_VECTORS[n], s) for n in vec_names], vec_len, F32), grep])
        for s in range(4)]).reshape(4, 1, 2 * rv + rr, SMALL_COLS)
    gsmall = _sum_parts_call(_scatter_call(jnp.concatenate([gsmall, gsmall], axis=1)))

    def small_pack(src):
        return jnp.concatenate([_flat_pack([src[n] for n in vec_names], vec_len, F32),
                                _flat_pack([src[n] for n in rep_names], rep_len, F32)]).reshape(-1, SMALL_COLS)

    groups = [{}, {}, {}, {}]
    grads = dict(zip(mat_names, _unpack_halves(ghalves, mat_shapes)))
    for n in big_names:
        parts = jnp.stack([_shard_of(gfull[n], MATRICES[n], s) for s in range(4)])
        mine = _sum_parts_call(_scatter_call(parts.reshape((4,) + halves_view(W[n]).shape)))
        theirs = _sibling_exchange_call(mine)
        grads[n] = jnp.concatenate([jnp.where(south, mine, theirs), jnp.where(south, theirs, mine)]).reshape(W[n].shape)
    for n in big_names + mat_names:
        g = grads[n]
        rows = (-1, g.shape[-1])
        outs = _adamw_call(W[n].reshape(rows), g.reshape(rows), Mo[n].reshape(rows), Vo[n].reshape(rows))
        for grp, val in zip(groups, (g,) + outs):
            grp[n] = val.reshape(g.shape)
    small = (gsmall,) + _adamw_call(small_pack(W), gsmall, small_pack(Mo), small_pack(Vo))
    for grp, sm in zip(groups, small):
        sm = sm.reshape(-1)
        grp.update(zip(vec_names, _unpack(sm[:vec_len], vec_shapes)))
        grp.update(zip(rep_names, _unpack(sm[vec_len:], rep_shapes)))
    return (loss, grad_x, *[grp[n] for grp in groups for n in WEIGHT_NAMES])
```
